```python
import math
import jax, jax.numpy as jnp
from jax import lax
import numpy as np

D_MODEL = 1024
BATCH = 8
SEQ = 2048
DEPTH = 4

N_MIXERS = 3
EPS = 1e-6
D_FF = -(-8 * D_MODEL // (3 * 256)) * 256
FOURIER_GROUP = 128
N_FOURIER_GROUPS = D_MODEL // FOURIER_GROUP
S5_GROUP = 16
S5_GROUPS = D_MODEL // S5_GROUP
S5_STATE = 64
S5_DT_MIN = 1e-3
S5_DT_MAX = 1e-1
HEAD_DIM = 64
HEADS_PER_GROUP = D_MODEL // HEAD_DIM
DILATED_GROUPS = ((128, 1), (512, 4), (2048, 16))
N_ATTN_GROUPS = len(DILATED_GROUPS)
N_ATTN_HEADS = N_ATTN_GROUPS * HEADS_PER_GROUP
QKV_WIDTH = N_ATTN_GROUPS * 3 * HEADS_PER_GROUP * HEAD_DIM
ATTN_OUT = HEADS_PER_GROUP * HEAD_DIM
NUM_BUCKETS = 32
MAX_DISTANCE = 1024
N_LAYERS_A = (DEPTH + 2) // 3
N_LAYERS_B = (DEPTH + 1) // 3
N_LAYERS_C = DEPTH // 3

kernel_name = "hybrid_fnet_s5_dilated_encoder"


def rms_norm(x, g):
    xf = x.astype(jnp.float32)
    y = xf * lax.rsqrt(jnp.mean(xf * xf, axis=-1, keepdims=True) + EPS)
    return (y * g.astype(jnp.float32)).astype(x.dtype)


def swiglu(h, w_gate_up, w_down):
    gate, up = jnp.split(h @ w_gate_up, 2, axis=-1)
    return (jax.nn.silu(gate) * up) @ w_down


def fourier_mixer(h, w_out):
    b, s, d = h.shape
    hg = h.astype(jnp.float32).reshape(b, s, N_FOURIER_GROUPS, FOURIER_GROUP)
    f = jnp.fft.fft2(hg, axes=(1, 3), norm="ortho").real
    return f.reshape(b, s, d).astype(h.dtype) @ w_out


def _s5_direction(u, lam_re, lam_im, log_dt, b_re, b_im, c_re, c_im, reverse):
    f32 = jnp.float32
    lam = lax.complex(lam_re.astype(f32), lam_im.astype(f32))
    dt = jnp.exp(log_dt.astype(f32))[:, None]
    lam_bar = jnp.exp(lam * dt)
    b_bar = ((lam_bar - 1.0) / lam)[..., None] * lax.complex(b_re.astype(f32), b_im.astype(f32))
    bu = lax.complex(jnp.einsum('bsgp,gnp->bsgn', u, b_bar.real),
                     jnp.einsum('bsgp,gnp->bsgn', u, b_bar.imag))
    a = jnp.broadcast_to(lam_bar, bu.shape)

    def combine(e1, e2):
        a1, x1 = e1
        a2, x2 = e2
        return a1 * a2, a2 * x1 + x2

    _, hs = lax.associative_scan(combine, (a, bu), axis=1, reverse=reverse)
    c = lax.complex(c_re.astype(f32), c_im.astype(f32))
    return jnp.einsum('gpn,bsgn->bsgp', c, hs).real


def s5_mixer(h, lam_re, lam_im, log_dt, b_re, b_im, c_re, c_im, d_skip, w_glu):
    b, s, d = h.shape
    hf = h.astype(jnp.float32)
    u = hf.reshape(b, s, S5_GROUPS, S5_GROUP)
    y_fwd = _s5_direction(u, lam_re[0], lam_im[0], log_dt[0], b_re[0], b_im[0], c_re[0], c_im[0], False)
    y_bwd = _s5_direction(u, lam_re[1], lam_im[1], log_dt[1], b_re[1], b_im[1], c_re[1], c_im[1], True)
    y = (y_fwd + y_bwd).reshape(b, s, d) + d_skip.astype(jnp.float32) * hf
    g = jax.nn.gelu(y).astype(h.dtype)
    val, gate = jnp.split(g @ w_glu, 2, axis=-1)
    return val * jax.nn.sigmoid(gate)


def t5_bucket(rel):
    half = NUM_BUCKETS // 2
    max_exact = half // 2
    n = np.abs(rel)
    sign = (rel > 0).astype(np.int32) * half
    large = max_exact + (np.log(np.maximum(n, 1) / max_exact) / math.log(MAX_DISTANCE / max_exact)
                         * (half - max_exact)).astype(np.int32)
    large = np.minimum(large, half - 1)
    return (sign + np.where(n < max_exact, n, large)).astype(np.int32)


def head_rms_norm(t, g):
    tf = t.astype(jnp.float32)
    y = tf * lax.rsqrt(jnp.mean(tf * tf, axis=-1, keepdims=True) + EPS)
    return (y * g.astype(jnp.float32)).astype(t.dtype)


def dilated_group_attention(q, k, v, bias_g, dil, n_side):
    bsz, s, nh, e = q.shape
    seg = s // dil
    blk = n_side
    nb = -(-seg // blk)
    segp = nb * blk

    def res(t):
        return t.reshape(bsz, seg, dil, nh, e).transpose(0, 2, 1, 3, 4)

    qb = jnp.pad(res(q), ((0, 0), (0, 0), (0, segp - seg), (0, 0), (0, 0))).reshape(bsz, dil, nb, blk, nh, e)

    def kv_blocks(t):
        tp = jnp.pad(res(t), ((0, 0), (0, 0), (blk, segp - seg + blk), (0, 0), (0, 0)))
        tp = tp.reshape(bsz, dil, nb + 2, blk, nh, e)
        return jnp.concatenate([tp[:, :, :-2], tp[:, :, 1:-1], tp[:, :, 2:]], axis=3)

    kb = kv_blocks(k)
    vb = kv_blocks(v)
    qi = np.arange(blk)[:, None]
    ki = np.arange(3 * blk)[None, :]
    off = ki - blk - qi
    band = np.abs(off) <= n_side
    key_idx = np.arange(nb)[:, None] * blk + np.arange(3 * blk)[None, :] - blk
    in_range = (key_idx >= 0) & (key_idx < seg)
    allowed = band[None] & in_range[:, None, :]
    buckets = t5_bucket(off * dil)
    bias = jnp.transpose(bias_g.astype(jnp.float32)[buckets], (2, 0, 1))

    sc = jnp.einsum('brnqhe,brnkhe->brnhqk', qb, kb, preferred_element_type=jnp.float32) * (e ** -0.5)
    sc = jnp.where(allowed[None, None, :, None], sc + bias, -1e30)
    m = jnp.max(sc, axis=-1, keepdims=True)
    p = jnp.exp(sc - m)
    den = jnp.sum(p, axis=-1, keepdims=True)
    o = jnp.einsum('brnhqk,brnkhe->brnqhe', (p / den).astype(v.dtype), vb)
    lse = (m + jnp.log(den))[..., 0]
    o = o.reshape(bsz, dil, segp, nh, e)[:, :, :seg].transpose(0, 2, 1, 3, 4).reshape(bsz, s, nh, e)
    lse = lse.transpose(0, 1, 2, 4, 3).reshape(bsz, dil, segp, nh)[:, :, :seg]
    lse = lse.transpose(0, 2, 1, 3).reshape(bsz, s, nh)
    return o, lse


def dilated_attention_mixer(h, w_qkv, q_gain, k_gain, w_o, rel_bias):
    b, s, d = h.shape
    qkv = (h @ w_qkv).reshape(b, s, N_ATTN_GROUPS, 3, HEADS_PER_GROUP, HEAD_DIM)
    outs = []
    lses = []
    for g, (window, dil) in enumerate(DILATED_GROUPS):
        q = head_rms_norm(qkv[:, :, g, 0], q_gain[g])
        k = head_rms_norm(qkv[:, :, g, 1], k_gain[g])
        v = qkv[:, :, g, 2]
        bias_g = rel_bias[:, g * HEADS_PER_GROUP:(g + 1) * HEADS_PER_GROUP]
        o, l = dilated_group_attention(q, k, v, bias_g, dil, (window // 2) // dil)
        outs.append(o)
        lses.append(l)
    wts = jax.nn.softmax(jnp.stack(lses, axis=0), axis=0)
    o = jnp.einsum('gbsh,gbshe->bshe', wts, jnp.stack(outs, axis=0).astype(jnp.float32))
    return o.reshape(b, s, ATTN_OUT).astype(h.dtype) @ w_o


def setup_inputs(seed: int = 0) -> dict:
    key = jax.random.key(seed)
    ks = jax.random.split(key, 24)
    f32 = jnp.float32
    d = D_MODEL

    def nrm(k, shape, scale):
        return jax.random.normal(k, shape, f32) * scale

    s5_shape = (N_LAYERS_B, 2, S5_GROUPS, S5_STATE)
    n_idx = jnp.arange(S5_STATE, dtype=f32)
    return {
        "x": nrm(ks[0], (BATCH, SEQ, d), 1.0),
        "norm_mix_g": 1.0 + nrm(ks[1], (DEPTH, d), 0.1),
        "norm_ffn_g": 1.0 + nrm(ks[2], (DEPTH, d), 0.1),
        "fnet_w_out": nrm(ks[3], (N_LAYERS_A, d, d), d ** -0.5),
        "s5_lambda_re": -0.5 + nrm(ks[4], s5_shape, 0.01),
        "s5_lambda_im": math.pi * n_idx + nrm(ks[5], s5_shape, 0.01),
        "s5_log_dt": jax.random.uniform(ks[6], (N_LAYERS_B, 2, S5_GROUPS), f32,
                                        math.log(S5_DT_MIN), math.log(S5_DT_MAX)),
        "s5_b_re": nrm(ks[7], (N_LAYERS_B, 2, S5_GROUPS, S5_STATE, S5_GROUP), (2 * S5_GROUP) ** -0.5),
        "s5_b_im": nrm(ks[8], (N_LAYERS_B, 2, S5_GROUPS, S5_STATE, S5_GROUP), (2 * S5_GROUP) ** -0.5),
        "s5_c_re": nrm(ks[9], (N_LAYERS_B, 2, S5_GROUPS, S5_GROUP, S5_STATE), S5_STATE ** -0.5),
        "s5_c_im": nrm(ks[10], (N_LAYERS_B, 2, S5_GROUPS, S5_GROUP, S5_STATE), S5_STATE ** -0.5),
        "s5_d": nrm(ks[11], (N_LAYERS_B, d), 1.0),
        "s5_w_glu": nrm(ks[12], (N_LAYERS_B, d, 2 * d), d ** -0.5),
        "attn_w_qkv": nrm(ks[13], (N_LAYERS_C, d, QKV_WIDTH), d ** -0.5),
        "attn_q_gain": 1.0 + nrm(ks[14], (N_LAYERS_C, N_ATTN_GROUPS, HEAD_DIM), 0.1),
        "attn_k_gain": 1.0 + nrm(ks[15], (N_LAYERS_C, N_ATTN_GROUPS, HEAD_DIM), 0.1),
        "attn_w_o": nrm(ks[16], (N_LAYERS_C, ATTN_OUT, d), ATTN_OUT ** -0.5),
        "rel_bias": nrm(ks[17], (NUM_BUCKETS, N_ATTN_HEADS), 0.5),
        "ffn_w_gate_up": nrm(ks[18], (DEPTH, d, 2 * D_FF), d ** -0.5),
        "ffn_w_down": nrm(ks[19], (DEPTH, D_FF, d), D_FF ** -0.5),
    }


def reference(x, norm_mix_g, norm_ffn_g, fnet_w_out, s5_lambda_re, s5_lambda_im, s5_log_dt,
              s5_b_re, s5_b_im, s5_c_re, s5_c_im, s5_d, s5_w_glu, attn_w_qkv, attn_q_gain,
              attn_k_gain, attn_w_o, rel_bias, ffn_w_gate_up, ffn_w_down):
    counts = [0, 0, 0]
    for i in range(DEPTH):
        kind = i % N_MIXERS
        j = counts[kind]
        counts[kind] += 1
        h = rms_norm(x, norm_mix_g[i])
        if kind == 0:
            mix = fourier_mixer(h, fnet_w_out[j])
        elif kind == 1:
            mix = s5_mixer(h, s5_lambda_re[j], s5_lambda_im[j], s5_log_dt[j], s5_b_re[j], s5_b_im[j],
                           s5_c_re[j], s5_c_im[j], s5_d[j], s5_w_glu[j])
        else:
            mix = dilated_attention_mixer(h, attn_w_qkv[j], attn_q_gain[j], attn_k_gain[j],
                                          attn_w_o[j], rel_bias)
        x = x + mix.astype(x.dtype)
        h = rms_norm(x, norm_ffn_g[i])
        x = x + swiglu(h, ffn_w_gate_up[i], ffn_w_down[i]).astype(x.dtype)
    return x
```

```python
import functools
import math

import numpy as np
import jax
import jax.numpy as jnp
from jax import lax
from jax.experimental import pallas as pl
from jax.experimental.pallas import tpu as pltpu

F32 = jnp.float32
BF16 = jnp.bfloat16

D_MODEL = 1024
BATCH = 8
SEQ = 2048
DEPTH = 4
N_TOKENS = BATCH * SEQ
EPS = 1e-6
D_FF = 2816
FOURIER_GROUP = 128
S5_GROUP = 16
S5_GROUPS = 64
S5_STATE = 64
HEAD_DIM = 64
HEADS_PER_GROUP = 16
DILATED_GROUPS = ((128, 1), (512, 4), (2048, 16))
N_ATTN_GROUPS = 3
NUM_BUCKETS = 32
MAX_DISTANCE = 1024
ATTN_SIDE = 64

LANES = 128
VMEM_LIMIT_CAP = 60 * 1024 * 1024

S5_CHUNK = 16
S5_CHUNKS = SEQ // S5_CHUNK
S5_ROWS = BATCH * S5_CHUNKS
S5_TILE_GROUPS = LANES // S5_GROUP
S5_CK = S5_CHUNK * S5_GROUP

ATTN_BQ = 128
ATTN_BK = ATTN_BQ + 2 * ATTN_SIDE


def _params(sem, vmem_bytes):
    return pltpu.CompilerParams(dimension_semantics=sem,
                                vmem_limit_bytes=int(min(VMEM_LIMIT_CAP, vmem_bytes)))


def _rms(x, g):
    ms = jnp.mean(x * x, axis=-1, keepdims=True)
    return x * lax.rsqrt(ms + EPS) * g


def _sigmoid(x):
    return 1.0 / (1.0 + jnp.exp(-x))


def _resident(shape):
    nd = len(shape)
    return pl.BlockSpec(shape, lambda *_: (0,) * nd, pipeline_mode=pl.Buffered(1))


def _ffn_body(x_ref, g_ref, wgu_ref, wd_ref, o_ref):
    x = x_ref[...]
    h = _rms(x, g_ref[...]).astype(BF16)
    gu = jnp.dot(h, wgu_ref[...], preferred_element_type=F32)
    gate = gu[:, :D_FF]
    up = gu[:, D_FF:]
    a = (gate * _sigmoid(gate) * up).astype(BF16)
    o_ref[...] = x + jnp.dot(a, wd_ref[...], preferred_element_type=F32)


def _ffn(x2, g, wgu, wd, tm=256):
    m = x2.shape[0]
    vmem = (wgu.size + wd.size) * 2 + 4 * tm * D_MODEL * 4 + 4 * tm * 2 * D_FF * 4 + (4 << 20)
    return pl.pallas_call(
        _ffn_body,
        grid=(m // tm,),
        in_specs=[pl.BlockSpec((tm, D_MODEL), lambda i: (i, 0)),
                  _resident((1, D_MODEL)),
                  _resident((D_MODEL, 2 * D_FF)),
                  _resident((D_FF, D_MODEL))],
        out_specs=pl.BlockSpec((tm, D_MODEL), lambda i: (i, 0)),
        out_shape=jax.ShapeDtypeStruct((m, D_MODEL), F32),
        compiler_params=_params(("parallel",), vmem),
        name="ffn",
    )(x2, g.reshape(1, D_MODEL), wgu, wd)


def _norm_matmul_body(x_ref, g_ref, w_ref, o_ref):
    h = _rms(x_ref[...], g_ref[...]).astype(BF16)
    o_ref[...] = jnp.dot(h, w_ref[...], preferred_element_type=F32).astype(o_ref.dtype)


def _norm_matmul(x2, g, w, tn, tm=512, out_dtype=BF16):
    m = x2.shape[0]
    n = w.shape[1]
    vmem = 2 * D_MODEL * tn * 2 + 2 * tm * D_MODEL * 4 + 2 * tm * tn * 2 + 2 * tm * tn * 4 + (4 << 20)
    return pl.pallas_call(
        _norm_matmul_body,
        grid=(n // tn, m // tm),
        in_specs=[pl.BlockSpec((tm, D_MODEL), lambda j, i: (i, 0)),
                  _resident((1, D_MODEL)),
                  pl.BlockSpec((D_MODEL, tn), lambda j, i: (0, j))],
        out_specs=pl.BlockSpec((tm, tn), lambda j, i: (i, j)),
        out_shape=jax.ShapeDtypeStruct((m, n), out_dtype),
        compiler_params=_params(("parallel", "parallel"), vmem),
        name="norm_matmul",
    )(x2, g.reshape(1, D_MODEL), w)


def _fnet_weight_body(cc_ref, sc_ref, w_ref, o_ref):
    w = w_ref[...]
    o_ref[:, :D_MODEL] = jnp.dot(cc_ref[...], w, preferred_element_type=F32,
                                 precision=lax.Precision.HIGHEST).astype(BF16)
    o_ref[:, D_MODEL:] = jnp.dot(sc_ref[...], w, preferred_element_type=F32,
                                 precision=lax.Precision.HIGHEST).astype(BF16)


def _fnet_weights(w_out):
    n = np.arange(FOURIER_GROUP)
    ang = 2.0 * np.pi * ((n[:, None] * n[None, :]) % FOURIER_GROUP) / FOURIER_GROUP
    cc = jnp.asarray(np.cos(ang) / math.sqrt(FOURIER_GROUP), F32)
    sc = jnp.asarray(np.sin(ang) / math.sqrt(FOURIER_GROUP), F32)
    ng = D_MODEL // FOURIER_GROUP
    return pl.pallas_call(
        _fnet_weight_body,
        grid=(ng,),
        in_specs=[_resident((FOURIER_GROUP, FOURIER_GROUP)),
                  _resident((FOURIER_GROUP, FOURIER_GROUP)),
                  pl.BlockSpec((FOURIER_GROUP, D_MODEL), lambda i: (i, 0))],
        out_specs=pl.BlockSpec((FOURIER_GROUP, 2 * D_MODEL), lambda i: (i, 0)),
        out_shape=jax.ShapeDtypeStruct((D_MODEL, 2 * D_MODEL), BF16),
        compiler_params=_params(("parallel",), 16 << 20),
        name="fnet_weights",
    )(cc, sc, w_out)


def _seq_dft_tables():
    k = lax.broadcasted_iota(jnp.int32, (SEQ, SEQ), 0)
    n = lax.broadcasted_iota(jnp.int32, (SEQ, SEQ), 1)
    ang = ((k * n) % SEQ).astype(F32) * (2.0 * math.pi / SEQ)
    scale = 1.0 / math.sqrt(SEQ)
    return (jnp.cos(ang) * scale).astype(BF16), (jnp.sin(ang) * scale).astype(BF16)


def _fnet_seq_body(x_ref, cs_ref, ss_ref, y_ref, o_ref):
    yc = y_ref[:, :D_MODEL]
    ys = y_ref[:, D_MODEL:]
    acc = jnp.dot(cs_ref[...], yc, preferred_element_type=F32)
    acc = acc - jnp.dot(ss_ref[...], ys, preferred_element_type=F32)
    o_ref[...] = x_ref[...] + acc


def _fnet_layer(x, g, w_out, tm=512):
    wcs = _fnet_weights(w_out)
    y = _norm_matmul(x.reshape(N_TOKENS, D_MODEL), g, wcs, tn=2 * D_MODEL)
    y = y.reshape(BATCH, SEQ, 2 * D_MODEL)
    cs, ss = _seq_dft_tables()
    vmem = 2 * SEQ * 2 * D_MODEL * 2 + 4 * tm * SEQ * 2 + 4 * tm * D_MODEL * 4 + (8 << 20)
    return pl.pallas_call(
        _fnet_seq_body,
        grid=(BATCH, SEQ // tm),
        in_specs=[pl.BlockSpec((None, tm, D_MODEL), lambda b, i: (b, i, 0)),
                  pl.BlockSpec((tm, SEQ), lambda b, i: (i, 0)),
                  pl.BlockSpec((tm, SEQ), lambda b, i: (i, 0)),
                  pl.BlockSpec((None, SEQ, 2 * D_MODEL), lambda b, i: (b, 0, 0))],
        out_specs=pl.BlockSpec((None, tm, D_MODEL), lambda b, i: (b, i, 0)),
        out_shape=jax.ShapeDtypeStruct((BATCH, SEQ, D_MODEL), F32),
        compiler_params=_params(("parallel", "parallel"), vmem),
        name="fnet_seq",
    )(x, cs, ss, y)


def _s5_matrices(lam_re, lam_im, log_dt, b_re, b_im, c_re, c_im):
    lam = lax.complex(lam_re.astype(F32), lam_im.astype(F32))
    dt = jnp.exp(log_dt.astype(F32))[..., None]
    lam_dt = lam * dt
    lam_bar = jnp.exp(lam_dt)
    b_bar = ((lam_bar - 1.0) / lam)[..., None] * lax.complex(b_re.astype(F32), b_im.astype(F32))
    c = lax.complex(c_re.astype(F32), c_im.astype(F32))
    L = S5_CHUNK
    taus = jnp.arange(L + 1, dtype=F32)
    pw = jnp.exp(lam_dt[..., None, :] * taus[:, None])
    hi = lax.Precision.HIGHEST
    kern = jnp.einsum('dgpn,dgtn,dgnq->dgtpq', c, pw[:, :, :L], b_bar, precision=hi).real
    tt = np.arange(L)
    lag = tt[None, :] - tt[:, None]
    kf = jnp.where((lag >= 0)[None, :, :, None, None], kern[0][:, np.clip(lag, 0, L - 1)], 0.0)
    kb = jnp.where((lag <= 0)[None, :, :, None, None], kern[1][:, np.clip(-lag, 0, L - 1)], 0.0)
    toep = (kf + kb).transpose(0, 1, 4, 2, 3).reshape(S5_GROUPS, S5_CK, S5_CK)
    sf = pw[0][:, ::-1][:, 1:][..., None] * b_bar[0][:, None]
    sb = pw[1][:, :L][..., None] * b_bar[1][:, None]
    def rows(z):
        return z.transpose(0, 1, 3, 2).reshape(S5_GROUPS, S5_CK, S5_STATE)
    s_in = jnp.concatenate([rows(sf.real), rows(sb.real), rows(sf.imag), rows(sb.imag)], axis=-1)
    wx = jnp.concatenate([toep, s_in], axis=-1).astype(BF16)
    zf = c[0][:, None] * pw[0][:, 1:][:, :, None, :]
    zb = c[1][:, None] * pw[1][:, ::-1][:, :L][:, :, None, :]
    def cols(z):
        return z.transpose(0, 3, 1, 2).reshape(S5_GROUPS, S5_STATE, S5_CK)
    wc = jnp.concatenate([cols(zf.real), cols(zb.real), -cols(zf.imag), -cols(zb.imag)], axis=1).astype(BF16)
    al = pw[:, :, L]
    coef = jnp.stack([jnp.concatenate([al[0].real, al[1].real], -1),
                      jnp.concatenate([al[0].imag, al[1].imag], -1)], axis=1)
    return wx, wc, coef.astype(F32)


def _s5_norm_body(x_ref, g_ref, o_ref):
    h = _rms(x_ref[...], g_ref[...])
    o_ref[...] = h.reshape(o_ref.shape)


def _s5_norm(x, g, tm=512):
    cr = tm // S5_CHUNK
    return pl.pallas_call(
        _s5_norm_body,
        grid=(BATCH, SEQ // tm),
        in_specs=[pl.BlockSpec((None, tm, D_MODEL), lambda b, i: (b, i, 0)),
                  _resident((1, D_MODEL))],
        out_specs=pl.BlockSpec((cr, None, S5_CHUNK, D_MODEL), lambda b, i: (i, b, 0, 0)),
        out_shape=jax.ShapeDtypeStruct((S5_CHUNKS, BATCH, S5_CHUNK, D_MODEL), F32),
        compiler_params=_params(("parallel", "parallel"), 32 << 20),
        name="s5_norm",
    )(x, g.reshape(1, D_MODEL))


def _block_transpose8(a, lane_block):
    a = list(a)
    for d in (4, 2, 1):
        take_lo = (lane_block & d) == 0
        nxt = list(a)
        for i in range(8):
            if i & d:
                continue
            lo, hi = a[i], a[i + d]
            nxt[i] = jnp.where(take_lo, lo, pltpu.roll(hi, S5_GROUP * d, 1))
            nxt[i + d] = jnp.where(take_lo, pltpu.roll(lo, LANES - S5_GROUP * d, 1), hi)
        a = nxt
    return a


def _gelu_tanh(y):
    return 0.5 * y * (1.0 + jnp.tanh(math.sqrt(2.0 / math.pi) * (y + 0.044715 * (y * y * y))))


S5_RB = 64


def _s5_body(h_ref, wx_ref, wc_ref, coef_ref, d_ref, o_ref, x_sc, y_sc, s_sc, hs_sc):
    lane = lax.broadcasted_iota(jnp.int32, (1, LANES), 1)
    lane_block = lane // S5_GROUP
    fwd_lanes = lane < S5_STATE
    n_rb = S5_ROWS // S5_RB
    tok_rb = S5_RB * S5_CHUNK

    def relayout_in(i, carry):
        base = pl.multiple_of(i * tok_rb, tok_rb)
        rows = pl.multiple_of(i * S5_RB, S5_RB)
        for half in range(2):
            a = [h_ref[pl.ds(base + half * 8 + t, S5_RB, stride=S5_CHUNK), :] for t in range(8)]
            xt = _block_transpose8(a, lane_block)
            for gi in range(S5_TILE_GROUPS):
                x_sc[gi, pl.ds(rows, S5_RB), half * LANES:(half + 1) * LANES] = xt[gi].astype(BF16)
        return carry

    lax.fori_loop(0, n_rb, relayout_in, 0)

    for gi in range(S5_TILE_GROUPS):
        z = jnp.dot(x_sc[gi], wx_ref[gi], preferred_element_type=F32)
        y_sc[gi] = z[:, :S5_CK]
        s_sc[0] = z[:, S5_CK:S5_CK + LANES]
        s_sc[1] = z[:, S5_CK + LANES:]
        a_re = coef_ref[gi, 0:1, :]
        a_im = coef_ref[gi, 1:2, :]

        def scan_step(k, st):
            st_re, st_im = st
            rf = pl.multiple_of(k * BATCH, BATCH)
            rb = pl.multiple_of((S5_CHUNKS - 1 - k) * BATCH, BATCH)
            hs_sc[0, pl.ds(rf, BATCH), :S5_STATE] = st_re[:, :S5_STATE]
            hs_sc[1, pl.ds(rf, BATCH), :S5_STATE] = st_im[:, :S5_STATE]
            hs_sc[0, pl.ds(rb, BATCH), S5_STATE:] = st_re[:, S5_STATE:]
            hs_sc[1, pl.ds(rb, BATCH), S5_STATE:] = st_im[:, S5_STATE:]
            s_re = jnp.where(fwd_lanes, s_sc[0, pl.ds(rf, BATCH), :], s_sc[0, pl.ds(rb, BATCH), :])
            s_im = jnp.where(fwd_lanes, s_sc[1, pl.ds(rf, BATCH), :], s_sc[1, pl.ds(rb, BATCH), :])
            n_re = a_re * st_re - a_im * st_im + s_re
            n_im = a_re * st_im + a_im * st_re + s_im
            return n_re, n_im

        zero = jnp.zeros((BATCH, LANES), F32)
        lax.fori_loop(0, S5_CHUNKS, scan_step, (zero, zero))
        hcat = jnp.concatenate([hs_sc[0], hs_sc[1]], axis=1).astype(BF16)
        y_sc[gi] = y_sc[gi] + jnp.dot(hcat, wc_ref[gi], preferred_element_type=F32)

    d_skip = d_ref[...]

    def relayout_out(i, carry):
        base = pl.multiple_of(i * tok_rb, tok_rb)
        rows = pl.multiple_of(i * S5_RB, S5_RB)
        for half in range(2):
            yg = [y_sc[gi, pl.ds(rows, S5_RB), half * LANES:(half + 1) * LANES] for gi in range(S5_TILE_GROUPS)]
            yt = _block_transpose8(yg, lane_block)
            for t in range(8):
                tok = pl.ds(base + half * 8 + t, S5_RB, stride=S5_CHUNK)
                o_ref[tok, :] = _gelu_tanh(yt[t] + d_skip * h_ref[tok, :])
        return carry

    lax.fori_loop(0, n_rb, relayout_out, 0)


def _s5_core(h2, wx, wc, coef, d_skip):
    n_tiles = S5_GROUPS // S5_TILE_GROUPS
    tok_block = pl.BlockSpec((N_TOKENS, LANES), lambda j: (0, j), pipeline_mode=pl.Buffered(1))
    vmem = (2 * N_TOKENS * LANES * 4 + S5_TILE_GROUPS * S5_ROWS * S5_CK * (2 + 4)
            + 4 * S5_ROWS * LANES * 4 + 2 * S5_TILE_GROUPS * S5_CK * 3 * S5_CK * 2 + (12 << 20))
    return pl.pallas_call(
        _s5_body,
        grid=(n_tiles,),
        in_specs=[tok_block,
                  pl.BlockSpec((S5_TILE_GROUPS, S5_CK, 2 * S5_CK), lambda j: (j, 0, 0)),
                  pl.BlockSpec((S5_TILE_GROUPS, S5_CK, S5_CK), lambda j: (j, 0, 0)),
                  pl.BlockSpec((S5_TILE_GROUPS, 2, LANES), lambda j: (j, 0, 0)),
                  pl.BlockSpec((1, LANES), lambda j: (0, j))],
        out_specs=tok_block,
        out_shape=jax.ShapeDtypeStruct((N_TOKENS, D_MODEL), F32),
        scratch_shapes=[pltpu.VMEM((S5_TILE_GROUPS, S5_ROWS, S5_CK), BF16),
                        pltpu.VMEM((S5_TILE_GROUPS, S5_ROWS, S5_CK), F32),
                        pltpu.VMEM((2, S5_ROWS, LANES), F32),
                        pltpu.VMEM((2, S5_ROWS, LANES), F32)],
        compiler_params=_params(("arbitrary",), vmem),
        name="s5_core",
    )(h2, wx, wc, coef, d_skip.reshape(1, D_MODEL))


def _s5_glu_body(x_ref, a_ref, w_ref, o_ref):
    a = a_ref[...].reshape(x_ref.shape).astype(BF16)
    vg = jnp.dot(a, w_ref[...], preferred_element_type=F32)
    o_ref[...] = x_ref[...] + vg[:, :D_MODEL] * _sigmoid(vg[:, D_MODEL:])


def _s5_glu(x, act, w_glu, tm=512):
    cr = tm // S5_CHUNK
    vmem = w_glu.size * 2 + 6 * tm * D_MODEL * 4 + 2 * tm * 2 * D_MODEL * 4 + (4 << 20)
    return pl.pallas_call(
        _s5_glu_body,
        grid=(BATCH, SEQ // tm),
        in_specs=[pl.BlockSpec((None, tm, D_MODEL), lambda b, i: (b, i, 0)),
                  pl.BlockSpec((cr, None, S5_CHUNK, D_MODEL), lambda b, i: (i, b, 0, 0)),
                  _resident((D_MODEL, 2 * D_MODEL))],
        out_specs=pl.BlockSpec((None, tm, D_MODEL), lambda b, i: (b, i, 0)),
        out_shape=jax.ShapeDtypeStruct((BATCH, SEQ, D_MODEL), F32),
        compiler_params=_params(("parallel", "parallel"), vmem),
        name="s5_glu",
    )(x, act, w_glu)


def _s5_layer(x, g, lam_re, lam_im, log_dt, b_re, b_im, c_re, c_im, d_skip, w_glu):
    wx, wc, coef = _s5_matrices(lam_re, lam_im, log_dt, b_re, b_im, c_re, c_im)
    h = _s5_norm(x, g)
    act = _s5_core(h.reshape(N_TOKENS, D_MODEL), wx, wc, coef, d_skip)
    act = act.reshape(S5_CHUNKS, BATCH, S5_CHUNK, D_MODEL)
    return _s5_glu(x, act, w_glu.astype(BF16))


def _t5_bucket(rel):
    half = NUM_BUCKETS // 2
    max_exact = half // 2
    n = np.abs(rel)
    sign = (rel > 0).astype(np.int32) * half
    large = max_exact + (np.log(np.maximum(n, 1) / max_exact) / math.log(MAX_DISTANCE / max_exact)
                         * (half - max_exact)).astype(np.int32)
    large = np.minimum(large, half - 1)
    return (sign + np.where(n < max_exact, n, large)).astype(np.int32)


def _attn_bias_table(rel_bias_g, dil):
    qi = np.arange(ATTN_BQ)[:, None]
    kj = np.arange(ATTN_BK)[None, :]
    off = kj - ATTN_SIDE - qi
    band = np.abs(off) <= ATTN_SIDE
    buckets = _t5_bucket(off * dil)
    tab = jnp.transpose(rel_bias_g.astype(F32)[buckets], (2, 0, 1))
    return jnp.where(band[None], tab, -1e30)


def _attn_group_body(q_ref, k_ref, v_ref, qg_ref, kg_ref, bias_ref, o_ref, l_ref,
                     qs_sc, ks_sc, kp_sc, vp_sc, *, dil):
    seg = SEQ // dil
    nb = seg // ATTN_BQ
    lane = lax.broadcasted_iota(jnp.int32, (1, LANES), 1)
    head0 = lane < HEAD_DIM

    def head_norm(t, gain):
        sq = t * t
        s0 = jnp.sum(jnp.where(head0, sq, 0.0), axis=-1, keepdims=True)
        s1 = jnp.sum(jnp.where(head0, 0.0, sq), axis=-1, keepdims=True)
        ms = jnp.where(head0, s0, s1) * (1.0 / HEAD_DIM)
        return t * lax.rsqrt(ms + EPS) * gain

    qs_sc[...] = head_norm(q_ref[...].astype(F32), qg_ref[...]) * (HEAD_DIM ** -0.5)
    ks_sc[0] = head_norm(k_ref[...].astype(F32), kg_ref[...])
    ks_sc[1] = v_ref[...].astype(F32)
    zpad = jnp.zeros((ATTN_SIDE, LANES), BF16)
    for r in range(dil):
        kp_sc[r, :ATTN_SIDE] = zpad
        kp_sc[r, ATTN_SIDE + seg:] = zpad
        vp_sc[r, :ATTN_SIDE] = zpad
        vp_sc[r, ATTN_SIDE + seg:] = zpad
        kp_sc[r, ATTN_SIDE:ATTN_SIDE + seg] = ks_sc[0, pl.ds(r, seg, stride=dil), :].astype(BF16)
        vp_sc[r, ATTN_SIDE:ATTN_SIDE + seg] = ks_sc[1, pl.ds(r, seg, stride=dil), :].astype(BF16)

    kcol = lax.broadcasted_iota(jnp.int32, (1, ATTN_BK), 1)
    for r in range(dil):
        for i in range(nb):
            rows = pl.ds(r + i * ATTN_BQ * dil, ATTN_BQ, stride=dil)
            qb = qs_sc[rows, :]
            kb = kp_sc[r, i * ATTN_BQ:i * ATTN_BQ + ATTN_BK]
            vb = vp_sc[r, i * ATTN_BQ:i * ATTN_BQ + ATTN_BK]
            key = kcol + (i * ATTN_BQ - ATTN_SIDE)
            edge = (i == 0) or (i == nb - 1)
            in_range = (key >= 0) & (key < seg)
            o_acc = None
            lse = []
            for a in range(2):
                sel = head0 if a == 0 else jnp.logical_not(head0)
                qa = jnp.where(sel, qb, 0.0).astype(BF16)
                s = lax.dot_general(qa, kb, (((1,), (1,)), ((), ())), preferred_element_type=F32)
                s = s + bias_ref[a]
                if edge:
                    s = jnp.where(in_range, s, -1e30)
                m = jnp.max(s, axis=-1, keepdims=True)
                p = jnp.exp(s - m)
                den = jnp.sum(p, axis=-1, keepdims=True)
                pn = (p / den).astype(BF16)
                va = jnp.where(sel, vb, jnp.zeros_like(vb))
                oa = jnp.dot(pn, va, preferred_element_type=F32)
                o_acc = oa if o_acc is None else o_acc + oa
                lse.append(m + jnp.log(den))
            o_ref[rows, :] = o_acc
            l_ref[rows, :] = jnp.where(head0, lse[0], lse[1])


def _attn_group(qkv3, g, dil, q_gain, k_gain, bias_tab):
    seg = SEQ // dil
    n_pairs = HEADS_PER_GROUP // 2
    blk = lambda which: pl.BlockSpec((None, SEQ, LANES),
                                     lambda b, p: (b, 0, (g * 3 + which) * n_pairs + p))
    out_blk = pl.BlockSpec((None, SEQ, LANES), lambda b, p: (b, 0, p))
    gain2 = lambda gn: jnp.tile(gn.astype(F32), 2).reshape(1, LANES)
    shape = jax.ShapeDtypeStruct((BATCH, SEQ, D_MODEL), F32)
    return pl.pallas_call(
        functools.partial(_attn_group_body, dil=dil),
        grid=(BATCH, n_pairs),
        in_specs=[blk(0), blk(1), blk(2), _resident((1, LANES)), _resident((1, LANES)),
                  pl.BlockSpec((2, ATTN_BQ, ATTN_BK), lambda b, p: (p, 0, 0))],
        out_specs=[out_blk, out_blk],
        out_shape=[shape, shape],
        scratch_shapes=[pltpu.VMEM((SEQ, LANES), F32),
                        pltpu.VMEM((2, SEQ, LANES), F32),
                        pltpu.VMEM((dil, seg + 2 * ATTN_SIDE, LANES), BF16),
                        pltpu.VMEM((dil, seg + 2 * ATTN_SIDE, LANES), BF16)],
        compiler_params=_params(("parallel", "parallel"), 32 << 20),
        name=f"attn_dil{dil}",
    )(qkv3, qkv3, qkv3, gain2(q_gain), gain2(k_gain), bias_tab)


def _attn_merge_body(x_ref, o0, o1, o2, l0, l1, l2, w_ref, out_ref):
    la, lb, lc = l0[...], l1[...], l2[...]
    m = jnp.maximum(jnp.maximum(la, lb), lc)
    wa, wb, wc = jnp.exp(la - m), jnp.exp(lb - m), jnp.exp(lc - m)
    o = (wa * o0[...] + wb * o1[...] + wc * o2[...]) / (wa + wb + wc)
    out_ref[...] = x_ref[...] + jnp.dot(o.astype(BF16), w_ref[...], preferred_element_type=F32)


def _attn_merge(x2, outs, lses, w_o, tm=512):
    tile = pl.BlockSpec((tm, D_MODEL), lambda i: (i, 0))
    return pl.pallas_call(
        _attn_merge_body,
        grid=(N_TOKENS // tm,),
        in_specs=[tile] * 7 + [_resident((D_MODEL, D_MODEL))],
        out_specs=tile,
        out_shape=jax.ShapeDtypeStruct((N_TOKENS, D_MODEL), F32),
        compiler_params=_params(("parallel",), 48 << 20),
        name="attn_merge",
    )(x2, *outs, *lses, w_o)


def _attn_layer(x, g, w_qkv, q_gain, k_gain, w_o, rel_bias):
    x2 = x.reshape(N_TOKENS, D_MODEL)
    qkv = _norm_matmul(x2, g, w_qkv.astype(BF16), tn=3 * D_MODEL)
    qkv3 = qkv.reshape(BATCH, SEQ, -1)
    outs, lses = [], []
    for gi, (window, dil) in enumerate(DILATED_GROUPS):
        assert (window // 2) // dil == ATTN_SIDE
        tab = _attn_bias_table(rel_bias[:, gi * HEADS_PER_GROUP:(gi + 1) * HEADS_PER_GROUP], dil)
        o, l = _attn_group(qkv3, gi, dil, q_gain[gi], k_gain[gi], tab)
        outs.append(o.reshape(N_TOKENS, D_MODEL))
        lses.append(l.reshape(N_TOKENS, D_MODEL))
    return _attn_merge(x2, outs, lses, w_o.astype(BF16)).reshape(BATCH, SEQ, D_MODEL)


def kernel(x, norm_mix_g, norm_ffn_g, fnet_w_out, s5_lambda_re, s5_lambda_im, s5_log_dt, s5_b_re, s5_b_im, s5_c_re, s5_c_im, s5_d, s5_w_glu, attn_w_qkv, attn_q_gain, attn_k_gain, attn_w_o, rel_bias, ffn_w_gate_up, ffn_w_down):
    counts = [0, 0, 0]
    for i in range(DEPTH):
        kind = i % 3
        j = counts[kind]
        counts[kind] += 1
        if kind == 0:
            x = _fnet_layer(x, norm_mix_g[i], fnet_w_out[j])
        elif kind == 1:
            x = _s5_layer(x, norm_mix_g[i], s5_lambda_re[j], s5_lambda_im[j], s5_log_dt[j], s5_b_re[j],
                          s5_b_im[j], s5_c_re[j], s5_c_im[j], s5_d[j], s5_w_glu[j])
        else:
            x = _attn_layer(x, norm_mix_g[i], attn_w_qkv[j], attn_q_gain[j], attn_k_gain[j], attn_w_o[j],
                            rel_bias)
        x = _ffn(x.reshape(N_TOKENS, D_MODEL), norm_ffn_g[i], ffn_w_gate_up[i].astype(BF16),
                 ffn_w_down[i].astype(BF16)).reshape(BATCH, SEQ, D_MODEL)
    return x
```

```python
import functools
import math

import numpy as np
import jax
import jax.numpy as jnp
from jax import lax
from jax.experimental import pallas as pl
from jax.experimental.pallas import tpu as pltpu

F32 = jnp.float32
BF16 = jnp.bfloat16

D_MODEL = 1024
BATCH = 8
SEQ = 2048
DEPTH = 4
N_TOKENS = BATCH * SEQ
EPS = 1e-6
D_FF = 2816
FOURIER_GROUP = 128
S5_GROUP = 16
S5_GROUPS = 64
S5_STATE = 64
HEAD_DIM = 64
HEADS_PER_GROUP = 16
DILATED_GROUPS = ((128, 1), (512, 4), (2048, 16))
N_ATTN_GROUPS = 3
NUM_BUCKETS = 32
MAX_DISTANCE = 1024
ATTN_SIDE = 64

LANES = 128
VMEM_LIMIT_CAP = 60 * 1024 * 1024

S5_CHUNK = 16
S5_CHUNKS = SEQ // S5_CHUNK
S5_ROWS = BATCH * S5_CHUNKS
S5_TILE_GROUPS = LANES // S5_GROUP
S5_CK = S5_CHUNK * S5_GROUP

ATTN_BQ = 128
ATTN_BK = ATTN_BQ + 2 * ATTN_SIDE


def _params(sem, vmem_bytes):
    return pltpu.CompilerParams(dimension_semantics=sem,
                                vmem_limit_bytes=int(min(VMEM_LIMIT_CAP, vmem_bytes)))


def _rms(x, g):
    ms = jnp.mean(x * x, axis=-1, keepdims=True)
    return x * lax.rsqrt(ms + EPS) * g


def _sigmoid(x):
    return 1.0 / (1.0 + jnp.exp(-x))


def _resident(shape):
    nd = len(shape)
    return pl.BlockSpec(shape, lambda *_: (0,) * nd, pipeline_mode=pl.Buffered(1))


def _ffn_body(x_ref, g_ref, wgu_ref, wd_ref, o_ref):
    x = x_ref[...]
    h = _rms(x, g_ref[...]).astype(BF16)
    gu = jnp.dot(h, wgu_ref[...], preferred_element_type=F32)
    gate = gu[:, :D_FF]
    up = gu[:, D_FF:]
    a = (gate * _sigmoid(gate) * up).astype(BF16)
    o_ref[...] = x + jnp.dot(a, wd_ref[...], preferred_element_type=F32)


def _ffn(x2, g, wgu, wd, tm=256):
    m = x2.shape[0]
    vmem = (wgu.size + wd.size) * 2 + 4 * tm * D_MODEL * 4 + 4 * tm * 2 * D_FF * 4 + (4 << 20)
    return pl.pallas_call(
        _ffn_body,
        grid=(m // tm,),
        in_specs=[pl.BlockSpec((tm, D_MODEL), lambda i: (i, 0)),
                  _resident((1, D_MODEL)),
                  _resident((D_MODEL, 2 * D_FF)),
                  _resident((D_FF, D_MODEL))],
        out_specs=pl.BlockSpec((tm, D_MODEL), lambda i: (i, 0)),
        out_shape=jax.ShapeDtypeStruct((m, D_MODEL), F32),
        compiler_params=_params(("parallel",), vmem),
        name="ffn",
    )(x2, g.reshape(1, D_MODEL), wgu, wd)


def _norm_matmul_body(x_ref, g_ref, w_ref, o_ref):
    h = _rms(x_ref[...], g_ref[...]).astype(BF16)
    o_ref[...] = jnp.dot(h, w_ref[...], preferred_element_type=F32).astype(o_ref.dtype)


def _norm_matmul(x2, g, w, tn, tm=512, out_dtype=BF16):
    m = x2.shape[0]
    n = w.shape[1]
    vmem = 2 * D_MODEL * tn * 2 + 2 * tm * D_MODEL * 4 + 2 * tm * tn * 2 + 2 * tm * tn * 4 + (4 << 20)
    return pl.pallas_call(
        _norm_matmul_body,
        grid=(n // tn, m // tm),
        in_specs=[pl.BlockSpec((tm, D_MODEL), lambda j, i: (i, 0)),
                  _resident((1, D_MODEL)),
                  pl.BlockSpec((D_MODEL, tn), lambda j, i: (0, j))],
        out_specs=pl.BlockSpec((tm, tn), lambda j, i: (i, j)),
        out_shape=jax.ShapeDtypeStruct((m, n), out_dtype),
        compiler_params=_params(("parallel", "parallel"), vmem),
        name="norm_matmul",
    )(x2, g.reshape(1, D_MODEL), w)


def _fnet_weight_body(cc_ref, sc_ref, w_ref, o_ref):
    w = w_ref[...]
    o_ref[:, :D_MODEL] = jnp.dot(cc_ref[...], w, preferred_element_type=F32,
                                 precision=lax.Precision.HIGHEST).astype(BF16)
    o_ref[:, D_MODEL:] = jnp.dot(sc_ref[...], w, preferred_element_type=F32,
                                 precision=lax.Precision.HIGHEST).astype(BF16)


def _fnet_weights(w_out):
    n = np.arange(FOURIER_GROUP)
    ang = 2.0 * np.pi * ((n[:, None] * n[None, :]) % FOURIER_GROUP) / FOURIER_GROUP
    cc = jnp.asarray(np.cos(ang) / math.sqrt(FOURIER_GROUP), F32)
    sc = jnp.asarray(np.sin(ang) / math.sqrt(FOURIER_GROUP), F32)
    ng = D_MODEL // FOURIER_GROUP
    return pl.pallas_call(
        _fnet_weight_body,
        grid=(ng,),
        in_specs=[_resident((FOURIER_GROUP, FOURIER_GROUP)),
                  _resident((FOURIER_GROUP, FOURIER_GROUP)),
                  pl.BlockSpec((FOURIER_GROUP, D_MODEL), lambda i: (i, 0))],
        out_specs=pl.BlockSpec((FOURIER_GROUP, 2 * D_MODEL), lambda i: (i, 0)),
        out_shape=jax.ShapeDtypeStruct((D_MODEL, 2 * D_MODEL), BF16),
        compiler_params=_params(("parallel",), 16 << 20),
        name="fnet_weights",
    )(cc, sc, w_out)


def _seq_dft_tables():
    k = lax.broadcasted_iota(jnp.int32, (SEQ, SEQ), 0)
    n = lax.broadcasted_iota(jnp.int32, (SEQ, SEQ), 1)
    ang = ((k * n) % SEQ).astype(F32) * (2.0 * math.pi / SEQ)
    scale = 1.0 / math.sqrt(SEQ)
    return (jnp.cos(ang) * scale).astype(BF16), (jnp.sin(ang) * scale).astype(BF16)


def _fnet_seq_body(x_ref, cs_ref, ss_ref, y_ref, o_ref):
    yc = y_ref[:, :D_MODEL]
    ys = y_ref[:, D_MODEL:]
    acc = jnp.dot(cs_ref[...], yc, preferred_element_type=F32)
    acc = acc - jnp.dot(ss_ref[...], ys, preferred_element_type=F32)
    o_ref[...] = x_ref[...] + acc


def _fnet_layer(x, g, w_out, tm=512):
    wcs = _fnet_weights(w_out)
    y = _norm_matmul(x.reshape(N_TOKENS, D_MODEL), g, wcs, tn=2 * D_MODEL)
    y = y.reshape(BATCH, SEQ, 2 * D_MODEL)
    cs, ss = _seq_dft_tables()
    vmem = 2 * SEQ * 2 * D_MODEL * 2 + 4 * tm * SEQ * 2 + 4 * tm * D_MODEL * 4 + (8 << 20)
    return pl.pallas_call(
        _fnet_seq_body,
        grid=(BATCH, SEQ // tm),
        in_specs=[pl.BlockSpec((None, tm, D_MODEL), lambda b, i: (b, i, 0)),
                  pl.BlockSpec((tm, SEQ), lambda b, i: (i, 0)),
                  pl.BlockSpec((tm, SEQ), lambda b, i: (i, 0)),
                  pl.BlockSpec((None, SEQ, 2 * D_MODEL), lambda b, i: (b, 0, 0))],
        out_specs=pl.BlockSpec((None, tm, D_MODEL), lambda b, i: (b, i, 0)),
        out_shape=jax.ShapeDtypeStruct((BATCH, SEQ, D_MODEL), F32),
        compiler_params=_params(("parallel", "parallel"), vmem),
        name="fnet_seq",
    )(x, cs, ss, y)


def _s5_matrices(lam_re, lam_im, log_dt, b_re, b_im, c_re, c_im):
    lam = lax.complex(lam_re.astype(F32), lam_im.astype(F32))
    dt = jnp.exp(log_dt.astype(F32))[..., None]
    lam_dt = lam * dt
    lam_bar = jnp.exp(lam_dt)
    b_bar = ((lam_bar - 1.0) / lam)[..., None] * lax.complex(b_re.astype(F32), b_im.astype(F32))
    c = lax.complex(c_re.astype(F32), c_im.astype(F32))
    L = S5_CHUNK
    taus = jnp.arange(L + 1, dtype=F32)
    pw = jnp.exp(lam_dt[..., None, :] * taus[:, None])
    hi = lax.Precision.HIGHEST
    kern = jnp.einsum('dgpn,dgtn,dgnq->dgtpq', c, pw[:, :, :L], b_bar, precision=hi).real
    tt = np.arange(L)
    lag = tt[None, :] - tt[:, None]
    sel_f = jnp.asarray(lag[:, :, None] == tt, F32)
    sel_b = jnp.asarray(-lag[:, :, None] == tt, F32)
    toep = (jnp.einsum('abt,gtpq->gaqbp', sel_f, kern[0], precision=hi)
            + jnp.einsum('abt,gtpq->gaqbp', sel_b, kern[1], precision=hi))
    toep = toep.reshape(S5_GROUPS, S5_CK, S5_CK)
    sf = pw[0][:, ::-1][:, 1:][..., None] * b_bar[0][:, None]
    sb = pw[1][:, :L][..., None] * b_bar[1][:, None]
    def rows(z):
        return z.transpose(0, 1, 3, 2).reshape(S5_GROUPS, S5_CK, S5_STATE)
    s_in = jnp.concatenate([rows(sf.real), rows(sb.real), rows(sf.imag), rows(sb.imag)], axis=-1)
    wx = jnp.concatenate([toep, s_in], axis=-1).astype(BF16)
    zf = c[0][:, None] * pw[0][:, 1:][:, :, None, :]
    zb = c[1][:, None] * pw[1][:, ::-1][:, :L][:, :, None, :]
    def cols(z):
        return z.transpose(0, 3, 1, 2).reshape(S5_GROUPS, S5_STATE, S5_CK)
    wc = jnp.concatenate([cols(zf.real), cols(zb.real), -cols(zf.imag), -cols(zb.imag)], axis=1).astype(BF16)
    al = pw[:, :, L]
    coef = jnp.stack([jnp.concatenate([al[0].real, al[1].real], -1),
                      jnp.concatenate([al[0].imag, al[1].imag], -1)], axis=1)
    return wx, wc, coef.astype(F32)


def _s5_norm_body(x_ref, g_ref, o_ref):
    h = _rms(x_ref[...], g_ref[...])
    o_ref[...] = h.reshape(o_ref.shape)


def _s5_norm(x, g, tm=512):
    cr = tm // S5_CHUNK
    return pl.pallas_call(
        _s5_norm_body,
        grid=(BATCH, SEQ // tm),
        in_specs=[pl.BlockSpec((None, tm, D_MODEL), lambda b, i: (b, i, 0)),
                  _resident((1, D_MODEL))],
        out_specs=pl.BlockSpec((cr, None, S5_CHUNK, D_MODEL), lambda b, i: (i, b, 0, 0)),
        out_shape=jax.ShapeDtypeStruct((S5_CHUNKS, BATCH, S5_CHUNK, D_MODEL), F32),
        compiler_params=_params(("parallel", "parallel"), 32 << 20),
        name="s5_norm",
    )(x, g.reshape(1, D_MODEL))


def _block_transpose8(a, lane_block):
    a = list(a)
    for d in (4, 2, 1):
        take_lo = (lane_block & d) == 0
        nxt = list(a)
        for i in range(8):
            if i & d:
                continue
            lo, hi = a[i], a[i + d]
            nxt[i] = jnp.where(take_lo, lo, pltpu.roll(hi, S5_GROUP * d, 1))
            nxt[i + d] = jnp.where(take_lo, pltpu.roll(lo, LANES - S5_GROUP * d, 1), hi)
        a = nxt
    return a


def _gelu_tanh(y):
    return 0.5 * y * (1.0 + jnp.tanh(math.sqrt(2.0 / math.pi) * (y + 0.044715 * (y * y * y))))


S5_RB = 64


def _s5_body(h_ref, wx_ref, wc_ref, coef_ref, d_ref, o_ref, x_sc, y_sc, s_sc, hs_sc):
    lane = lax.broadcasted_iota(jnp.int32, (1, LANES), 1)
    lane_block = lane // S5_GROUP
    fwd_lanes = lane < S5_STATE
    n_rb = S5_ROWS // S5_RB
    tok_rb = S5_RB * S5_CHUNK

    def relayout_in(i, carry):
        base = pl.multiple_of(i * tok_rb, tok_rb)
        rows = pl.multiple_of(i * S5_RB, S5_RB)
        for half in range(2):
            a = [h_ref[pl.ds(base + half * 8 + t, S5_RB, stride=S5_CHUNK), :] for t in range(8)]
            xt = _block_transpose8(a, lane_block)
            for gi in range(S5_TILE_GROUPS):
                x_sc[gi, pl.ds(rows, S5_RB), half * LANES:(half + 1) * LANES] = xt[gi].astype(BF16)
        return carry

    lax.fori_loop(0, n_rb, relayout_in, 0)

    for gi in range(S5_TILE_GROUPS):
        z = jnp.dot(x_sc[gi], wx_ref[gi], preferred_element_type=F32)
        y_sc[gi] = z[:, :S5_CK]
        s_sc[0] = z[:, S5_CK:S5_CK + LANES]
        s_sc[1] = z[:, S5_CK + LANES:]
        a_re = coef_ref[gi, 0:1, :]
        a_im = coef_ref[gi, 1:2, :]

        def scan_step(k, st):
            st_re, st_im = st
            rf = pl.multiple_of(k * BATCH, BATCH)
            rb = pl.multiple_of((S5_CHUNKS - 1 - k) * BATCH, BATCH)
            hs_sc[0, pl.ds(rf, BATCH), :S5_STATE] = st_re[:, :S5_STATE]
            hs_sc[1, pl.ds(rf, BATCH), :S5_STATE] = st_im[:, :S5_STATE]
            hs_sc[0, pl.ds(rb, BATCH), S5_STATE:] = st_re[:, S5_STATE:]
            hs_sc[1, pl.ds(rb, BATCH), S5_STATE:] = st_im[:, S5_STATE:]
            s_re = jnp.where(fwd_lanes, s_sc[0, pl.ds(rf, BATCH), :], s_sc[0, pl.ds(rb, BATCH), :])
            s_im = jnp.where(fwd_lanes, s_sc[1, pl.ds(rf, BATCH), :], s_sc[1, pl.ds(rb, BATCH), :])
            n_re = a_re * st_re - a_im * st_im + s_re
            n_im = a_re * st_im + a_im * st_re + s_im
            return n_re, n_im

        zero = jnp.zeros((BATCH, LANES), F32)
        lax.fori_loop(0, S5_CHUNKS, scan_step, (zero, zero))
        hcat = jnp.concatenate([hs_sc[0], hs_sc[1]], axis=1).astype(BF16)
        y_sc[gi] = y_sc[gi] + jnp.dot(hcat, wc_ref[gi], preferred_element_type=F32)

    d_skip = d_ref[...]

    def relayout_out(i, carry):
        base = pl.multiple_of(i * tok_rb, tok_rb)
        rows = pl.multiple_of(i * S5_RB, S5_RB)
        for half in range(2):
            yg = [y_sc[gi, pl.ds(rows, S5_RB), half * LANES:(half + 1) * LANES] for gi in range(S5_TILE_GROUPS)]
            yt = _block_transpose8(yg, lane_block)
            for t in range(8):
                tok = pl.ds(base + half * 8 + t, S5_RB, stride=S5_CHUNK)
                o_ref[tok, :] = _gelu_tanh(yt[t] + d_skip * h_ref[tok, :])
        return carry

    lax.fori_loop(0, n_rb, relayout_out, 0)


def _s5_core(h2, wx, wc, coef, d_skip):
    n_tiles = S5_GROUPS // S5_TILE_GROUPS
    tok_block = pl.BlockSpec((N_TOKENS, LANES), lambda j: (0, j), pipeline_mode=pl.Buffered(1))
    vmem = (2 * N_TOKENS * LANES * 4 + S5_TILE_GROUPS * S5_ROWS * S5_CK * (2 + 4)
            + 4 * S5_ROWS * LANES * 4 + 2 * S5_TILE_GROUPS * S5_CK * 3 * S5_CK * 2 + (12 << 20))
    return pl.pallas_call(
        _s5_body,
        grid=(n_tiles,),
        in_specs=[tok_block,
                  pl.BlockSpec((S5_TILE_GROUPS, S5_CK, 2 * S5_CK), lambda j: (j, 0, 0)),
                  pl.BlockSpec((S5_TILE_GROUPS, S5_CK, S5_CK), lambda j: (j, 0, 0)),
                  pl.BlockSpec((S5_TILE_GROUPS, 2, LANES), lambda j: (j, 0, 0)),
                  pl.BlockSpec((1, LANES), lambda j: (0, j))],
        out_specs=tok_block,
        out_shape=jax.ShapeDtypeStruct((N_TOKENS, D_MODEL), F32),
        scratch_shapes=[pltpu.VMEM((S5_TILE_GROUPS, S5_ROWS, S5_CK), BF16),
                        pltpu.VMEM((S5_TILE_GROUPS, S5_ROWS, S5_CK), F32),
                        pltpu.VMEM((2, S5_ROWS, LANES), F32),
                        pltpu.VMEM((2, S5_ROWS, LANES), F32)],
        compiler_params=_params(("arbitrary",), vmem),
        name="s5_core",
    )(h2, wx, wc, coef, d_skip.reshape(1, D_MODEL))


def _s5_glu_body(x_ref, a_ref, w_ref, o_ref):
    a = a_ref[...].reshape(x_ref.shape).astype(BF16)
    vg = jnp.dot(a, w_ref[...], preferred_element_type=F32)
    o_ref[...] = x_ref[...] + vg[:, :D_MODEL] * _sigmoid(vg[:, D_MODEL:])


def _s5_glu(x, act, w_glu, tm=512):
    cr = tm // S5_CHUNK
    vmem = w_glu.size * 2 + 6 * tm * D_MODEL * 4 + 2 * tm * 2 * D_MODEL * 4 + (4 << 20)
    return pl.pallas_call(
        _s5_glu_body,
        grid=(BATCH, SEQ // tm),
        in_specs=[pl.BlockSpec((None, tm, D_MODEL), lambda b, i: (b, i, 0)),
                  pl.BlockSpec((cr, None, S5_CHUNK, D_MODEL), lambda b, i: (i, b, 0, 0)),
                  _resident((D_MODEL, 2 * D_MODEL))],
        out_specs=pl.BlockSpec((None, tm, D_MODEL), lambda b, i: (b, i, 0)),
        out_shape=jax.ShapeDtypeStruct((BATCH, SEQ, D_MODEL), F32),
        compiler_params=_params(("parallel", "parallel"), vmem),
        name="s5_glu",
    )(x, act, w_glu)


def _s5_layer(x, g, lam_re, lam_im, log_dt, b_re, b_im, c_re, c_im, d_skip, w_glu):
    wx, wc, coef = _s5_matrices(lam_re, lam_im, log_dt, b_re, b_im, c_re, c_im)
    h = _s5_norm(x, g)
    act = _s5_core(h.reshape(N_TOKENS, D_MODEL), wx, wc, coef, d_skip)
    act = act.reshape(S5_CHUNKS, BATCH, S5_CHUNK, D_MODEL)
    return _s5_glu(x, act, w_glu.astype(BF16))


def _t5_bucket(rel):
    half = NUM_BUCKETS // 2
    max_exact = half // 2
    n = np.abs(rel)
    sign = (rel > 0).astype(np.int32) * half
    large = max_exact + (np.log(np.maximum(n, 1) / max_exact) / math.log(MAX_DISTANCE / max_exact)
                         * (half - max_exact)).astype(np.int32)
    large = np.minimum(large, half - 1)
    return (sign + np.where(n < max_exact, n, large)).astype(np.int32)


LOG2E = 1.4426950408889634
N_PAIRS = HEADS_PER_GROUP // 2
ATTN_PAD_ROWS = SEQ + ATTN_BQ * max(d for _, d in DILATED_GROUPS)
QKV_GROUP_WIDTH = 3 * D_MODEL
QKV_SLAB = 2 * LANES
QKV_TILE_ROWS = 512


def _attn_bias_tables(rel_bias):
    n_off = ATTN_BK + ATTN_BQ - 1
    offs = np.arange(n_off) - (ATTN_BQ - 1) - ATTN_SIDE
    k = np.arange(ATTN_BK)[:, None]
    q = np.arange(ATTN_BQ)[None, :]
    band = np.abs(k - ATTN_SIDE - q) <= ATTN_SIDE
    tabs = []
    for gi, (_, dil) in enumerate(DILATED_GROUPS):
        onehot = jnp.asarray(_t5_bucket(offs * dil)[:, None] == np.arange(NUM_BUCKETS), F32)
        f = jnp.dot(onehot, rel_bias[:, gi * HEADS_PER_GROUP:(gi + 1) * HEADS_PER_GROUP].astype(F32),
                    precision=lax.Precision.HIGHEST).T
        fr = jnp.pad(f[:, ::-1], ((0, 0), (0, 1)))
        skew = jnp.tile(fr, (1, ATTN_BK))[:, ATTN_BK - 1:ATTN_BK - 1 + ATTN_BK * n_off]
        t = skew.reshape(HEADS_PER_GROUP, ATTN_BK, n_off)[:, :, :ATTN_BQ]
        t = jnp.where(band[None], t * LOG2E, -1e30)
        t = t.reshape(N_PAIRS, 2, ATTN_BK, ATTN_BQ).transpose(0, 2, 1, 3).reshape(N_PAIRS, ATTN_BK, 2 * ATTN_BQ)
        tabs.append(t)
    return jnp.stack(tabs)


def _qkv_body(x_ref, g_ref, w_ref, qg_ref, kg_ref, q_out, k_out, v_out, *, n_res):
    rows = x_ref.shape[0]
    h = jnp.concatenate(
        [_rms(x_ref[:, r * D_MODEL:(r + 1) * D_MODEL], g_ref[...]).astype(BF16) for r in range(n_res)], axis=0)
    lane = lax.broadcasted_iota(jnp.int32, (1, LANES), 1)
    head0 = lane < HEAD_DIM

    def head_norm(t, gain):
        sq = t * t
        s0 = jnp.sum(jnp.where(head0, sq, 0.0), axis=-1, keepdims=True)
        s1 = jnp.sum(jnp.where(head0, 0.0, sq), axis=-1, keepdims=True)
        ms = jnp.where(head0, s0, s1) * (1.0 / HEAD_DIM)
        return t * lax.rsqrt(ms + EPS) * gain

    for c in range(QKV_GROUP_WIDTH // QKV_SLAB):
        z = jnp.dot(h, w_ref[:, c * QKV_SLAB:(c + 1) * QKV_SLAB], preferred_element_type=F32)
        for half in range(QKV_SLAB // LANES):
            section, lo = divmod(c * QKV_SLAB + half * LANES, D_MODEL)
            zz = z[:, half * LANES:(half + 1) * LANES]
            if section == 0:
                zz = head_norm(zz, qg_ref[...]) * (HEAD_DIM ** -0.5 * LOG2E)
                qa = jnp.where(head0, zz, 0.0).astype(BF16)
                qb = jnp.where(head0, 0.0, zz).astype(BF16)
            elif section == 1:
                zz = head_norm(zz, kg_ref[...]).astype(BF16)
            else:
                zz = zz.astype(BF16)
            for r in range(n_res):
                piece = slice(r * rows, (r + 1) * rows)
                if section == 0:
                    q_out[r, 0, :, lo:lo + LANES] = qa[piece]
                    q_out[r, 1, :, lo:lo + LANES] = qb[piece]
                elif section == 1:
                    k_out[r, :, lo:lo + LANES] = zz[piece]
                else:
                    v_out[r, :, lo:lo + LANES] = zz[piece]


def _qkv_group(x, g, w_qkv, gi, dil, q_gain, k_gain):
    seg = SEQ // dil
    rows = min(seg, QKV_TILE_ROWS)
    n_res = QKV_TILE_ROWS // rows
    assert dil % n_res == 0 and seg % rows == 0
    gain2 = lambda gn: jnp.tile(gn.astype(F32), 2).reshape(1, LANES)
    kv_spec = pl.BlockSpec((None, n_res, rows, D_MODEL), lambda b, r, i: (b, r, i, 0))
    kv_shape = jax.ShapeDtypeStruct((BATCH, dil, seg, D_MODEL), BF16)
    vmem = (D_MODEL * QKV_GROUP_WIDTH * 2 + 2 * QKV_TILE_ROWS * D_MODEL * 4
            + 2 * 4 * QKV_TILE_ROWS * D_MODEL * 2 + (12 << 20))
    return pl.pallas_call(
        functools.partial(_qkv_body, n_res=n_res),
        grid=(BATCH, dil // n_res, seg // rows),
        in_specs=[pl.BlockSpec((None, rows, n_res * D_MODEL), lambda b, r, i: (b, i, r)),
                  _resident((1, D_MODEL)),
                  pl.BlockSpec((D_MODEL, QKV_GROUP_WIDTH), lambda b, r, i: (0, gi), pipeline_mode=pl.Buffered(1)),
                  _resident((1, LANES)), _resident((1, LANES))],
        out_specs=[pl.BlockSpec((None, n_res, 2, rows, D_MODEL), lambda b, r, i: (b, r, 0, i, 0)),
                   kv_spec, kv_spec],
        out_shape=[jax.ShapeDtypeStruct((BATCH, dil, 2, seg, D_MODEL), BF16), kv_shape, kv_shape],
        compiler_params=_params(("parallel", "parallel", "parallel"), vmem),
        name=f"qkv_dil{dil}",
    )(x.reshape(BATCH, seg, dil * D_MODEL), g.reshape(1, D_MODEL), w_qkv, gain2(q_gain), gain2(k_gain))


def _attn_body(q0, k0, v0, q1, k1, v1, q2, k2, v2, bias_ref, o_ref, kp_sc, vp_sc, lt_sc, og_sc, lg_sc):
    lane = lax.broadcasted_iota(jnp.int32, (1, LANES), 1)
    head0 = lane < HEAD_DIM
    krow = lax.broadcasted_iota(jnp.int32, (ATTN_BK, 1), 0)
    zpad = jnp.zeros((ATTN_SIDE, LANES), BF16)

    qkv_refs = ((q0, k0, v0), (q1, k1, v1), (q2, k2, v2))
    for g, (_, dil) in enumerate(DILATED_GROUPS):
        q_ref, k_ref, v_ref = qkv_refs[g]
        seg = SEQ // dil
        nb = seg // ATTN_BQ
        pad_seg = seg + 2 * ATTN_SIDE
        lt_sc[...] = jnp.zeros(lt_sc.shape, F32)
        for r in range(dil):
            base = r * pad_seg
            for src, dst in ((k_ref, kp_sc), (v_ref, vp_sc)):
                dst[base:base + ATTN_SIDE] = zpad
                dst[base + ATTN_SIDE + seg:base + pad_seg] = zpad
                dst[base + ATTN_SIDE:base + ATTN_SIDE + seg] = src[r]

        for r in range(dil):
            base = r * pad_seg
            for i in range(nb):
                blk = r * nb + i
                win = slice(base + i * ATTN_BQ, base + i * ATTN_BQ + ATTN_BK)
                qrows = slice(i * ATTN_BQ, (i + 1) * ATTN_BQ)
                qm = jnp.concatenate([q_ref[r, 0, qrows, :], q_ref[r, 1, qrows, :]], axis=0)
                s = lax.dot_general(kp_sc[win], qm, (((1,), (1,)), ((), ())), preferred_element_type=F32)
                s = s + bias_ref[g]
                if i == 0 or i == nb - 1:
                    ok = None
                    if i == 0:
                        ok = krow >= ATTN_SIDE
                    if i == nb - 1:
                        ok_hi = krow < ATTN_BK - ATTN_SIDE
                        ok = ok_hi if ok is None else jnp.logical_and(ok, ok_hi)
                    s = jnp.where(ok, s, -1e30)
                vb = vp_sc[win]
                o_heads, lse_heads = [], []
                for a in range(2):
                    sa = s[:, a * ATTN_BQ:(a + 1) * ATTN_BQ]
                    m = jnp.max(sa, axis=0, keepdims=True)
                    p = jnp.exp2(sa - m)
                    den = jnp.sum(p, axis=0, keepdims=True)
                    pt = (p * (1.0 / den)).T.astype(BF16)
                    o_heads.append(jnp.dot(pt, vb, preferred_element_type=F32))
                    lse_heads.append(m + jnp.log2(den))
                rows = pl.ds(r + i * ATTN_BQ * dil, ATTN_BQ, stride=dil)
                og_sc[g, rows, :] = jnp.where(head0, o_heads[0], o_heads[1])
                lt_sc[8 * blk:8 * blk + 1, :] = lse_heads[0]
                lt_sc[8 * blk + 1:8 * blk + 2, :] = lse_heads[1]

        ltt = lt_sc[...].T
        for r in range(dil):
            for i in range(nb):
                c = 8 * (r * nb + i)
                rows = pl.ds(r + i * ATTN_BQ * dil, ATTN_BQ, stride=dil)
                lg_sc[g, rows, :] = jnp.where(head0, ltt[:, c:c + 1], ltt[:, c + 1:c + 2])

    l0, l1, l2 = lg_sc[0], lg_sc[1], lg_sc[2]
    m = jnp.maximum(jnp.maximum(l0, l1), l2)
    w0, w1, w2 = jnp.exp2(l0 - m), jnp.exp2(l1 - m), jnp.exp2(l2 - m)
    o = (w0 * og_sc[0] + w1 * og_sc[1] + w2 * og_sc[2]) / (w0 + w1 + w2)
    o_ref[...] = o.astype(o_ref.dtype)


def _attn_core(qkv, bias_tabs):
    in_specs = []
    for (_, dil) in DILATED_GROUPS:
        seg = SEQ // dil
        in_specs.append(pl.BlockSpec((None, dil, 2, seg, LANES), lambda b, p: (b, 0, 0, 0, p)))
        in_specs.append(pl.BlockSpec((None, dil, seg, LANES), lambda b, p: (b, 0, 0, p)))
        in_specs.append(pl.BlockSpec((None, dil, seg, LANES), lambda b, p: (b, 0, 0, p)))
    in_specs.append(pl.BlockSpec((N_ATTN_GROUPS, None, ATTN_BK, 2 * ATTN_BQ), lambda b, p: (0, p, 0, 0)))
    return pl.pallas_call(
        _attn_body,
        grid=(BATCH, N_PAIRS),
        in_specs=in_specs,
        out_specs=pl.BlockSpec((None, SEQ, LANES), lambda b, p: (b, 0, p)),
        out_shape=jax.ShapeDtypeStruct((BATCH, SEQ, D_MODEL), BF16),
        scratch_shapes=[pltpu.VMEM((ATTN_PAD_ROWS, LANES), BF16),
                        pltpu.VMEM((ATTN_PAD_ROWS, LANES), BF16),
                        pltpu.VMEM((LANES, LANES), F32),
                        pltpu.VMEM((N_ATTN_GROUPS, SEQ, LANES), F32),
                        pltpu.VMEM((N_ATTN_GROUPS, SEQ, LANES), F32)],
        compiler_params=_params(("parallel", "parallel"), 40 << 20),
        name="attn_core",
    )(*qkv, bias_tabs)


def _proj_residual_body(x_ref, a_ref, w_ref, o_ref):
    o_ref[...] = x_ref[...] + jnp.dot(a_ref[...], w_ref[...], preferred_element_type=F32)


def _proj_residual(x2, a2, w, tm=512):
    tile = pl.BlockSpec((tm, D_MODEL), lambda i: (i, 0))
    return pl.pallas_call(
        _proj_residual_body,
        grid=(x2.shape[0] // tm,),
        in_specs=[tile, tile, _resident((D_MODEL, D_MODEL))],
        out_specs=tile,
        out_shape=jax.ShapeDtypeStruct(x2.shape, F32),
        compiler_params=_params(("parallel",), 32 << 20),
        name="proj_residual",
    )(x2, a2, w)


def _attn_layer(x, g, w_qkv, q_gain, k_gain, w_o, rel_bias):
    w_bf = w_qkv.astype(BF16)
    qkv = []
    for gi, (window, dil) in enumerate(DILATED_GROUPS):
        assert (window // 2) // dil == ATTN_SIDE and (SEQ // dil) % ATTN_BQ == 0
        qkv.extend(_qkv_group(x, g, w_bf, gi, dil, q_gain[gi], k_gain[gi]))
    o = _attn_core(qkv, _attn_bias_tables(rel_bias))
    x2 = x.reshape(N_TOKENS, D_MODEL)
    return _proj_residual(x2, o.reshape(N_TOKENS, D_MODEL), w_o.astype(BF16)).reshape(BATCH, SEQ, D_MODEL)


def kernel(x, norm_mix_g, norm_ffn_g, fnet_w_out, s5_lambda_re, s5_lambda_im, s5_log_dt, s5_b_re, s5_b_im, s5_c_re, s5_c_im, s5_d, s5_w_glu, attn_w_qkv, attn_q_gain, attn_k_gain, attn_w_o, rel_bias, ffn_w_gate_up, ffn_w_down):
    counts = [0, 0, 0]
    for i in range(DEPTH):
        kind = i % 3
        j = counts[kind]
        counts[kind] += 1
        if kind == 0:
            x = _fnet_layer(x, norm_mix_g[i], fnet_w_out[j])
        elif kind == 1:
            x = _s5_layer(x, norm_mix_g[i], s5_lambda_re[j], s5_lambda_im[j], s5_log_dt[j], s5_b_re[j],
                          s5_b_im[j], s5_c_re[j], s5_c_im[j], s5_d[j], s5_w_glu[j])
        else:
            x = _attn_layer(x, norm_mix_g[i], attn_w_qkv[j], attn_q_gain[j], attn_k_gain[j], attn_w_o[j],
                            rel_bias)
        x = _ffn(x.reshape(N_TOKENS, D_MODEL), norm_ffn_g[i], ffn_w_gate_up[i].astype(BF16),
                 ffn_w_down[i].astype(BF16)).reshape(BATCH, SEQ, D_MODEL)
    return x
```

```python
import functools
import math

import numpy as np
import jax
import jax.numpy as jnp
from jax import lax
from jax.experimental import pallas as pl
from jax.experimental.pallas import tpu as pltpu

F32 = jnp.float32
BF16 = jnp.bfloat16

D_MODEL = 1024
BATCH = 8
SEQ = 2048
DEPTH = 4
N_TOKENS = BATCH * SEQ
EPS = 1e-6
D_FF = 2816
FOURIER_GROUP = 128
S5_GROUP = 16
S5_GROUPS = 64
S5_STATE = 64
HEAD_DIM = 64
HEADS_PER_GROUP = 16
DILATED_GROUPS = ((128, 1), (512, 4), (2048, 16))
N_ATTN_GROUPS = 3
NUM_BUCKETS = 32
MAX_DISTANCE = 1024
ATTN_SIDE = 64

LANES = 128
VMEM_LIMIT_CAP = 60 * 1024 * 1024

S5_CHUNK = 16
S5_CHUNKS = SEQ // S5_CHUNK
S5_ROWS = BATCH * S5_CHUNKS
S5_TILE_GROUPS = LANES // S5_GROUP
S5_CK = S5_CHUNK * S5_GROUP

ATTN_BQ = 128
ATTN_BK = ATTN_BQ + 2 * ATTN_SIDE


def _params(sem, vmem_bytes):
    return pltpu.CompilerParams(dimension_semantics=sem,
                                vmem_limit_bytes=int(min(VMEM_LIMIT_CAP, vmem_bytes)))


def _rms(x, g):
    ms = jnp.mean(x * x, axis=-1, keepdims=True)
    return x * lax.rsqrt(ms + EPS) * g


def _sigmoid(x):
    return 1.0 / (1.0 + jnp.exp(-x))


def _resident(shape):
    nd = len(shape)
    return pl.BlockSpec(shape, lambda *_: (0,) * nd, pipeline_mode=pl.Buffered(1))


def _ffn_body(x_ref, g_ref, wgu_ref, wd_ref, o_ref):
    x = x_ref[...]
    h = _rms(x, g_ref[...]).astype(BF16)
    gu = jnp.dot(h, wgu_ref[...], preferred_element_type=F32)
    gate = gu[:, :D_FF]
    up = gu[:, D_FF:]
    a = (gate * _sigmoid(gate) * up).astype(BF16)
    o_ref[...] = x + jnp.dot(a, wd_ref[...], preferred_element_type=F32)


def _ffn(x2, g, wgu, wd, tm=256):
    m = x2.shape[0]
    vmem = (wgu.size + wd.size) * 2 + 4 * tm * D_MODEL * 4 + 4 * tm * 2 * D_FF * 4 + (4 << 20)
    return pl.pallas_call(
        _ffn_body,
        grid=(m // tm,),
        in_specs=[pl.BlockSpec((tm, D_MODEL), lambda i: (i, 0)),
                  _resident((1, D_MODEL)),
                  _resident((D_MODEL, 2 * D_FF)),
                  _resident((D_FF, D_MODEL))],
        out_specs=pl.BlockSpec((tm, D_MODEL), lambda i: (i, 0)),
        out_shape=jax.ShapeDtypeStruct((m, D_MODEL), F32),
        compiler_params=_params(("parallel",), vmem),
        name="ffn",
    )(x2, g.reshape(1, D_MODEL), wgu, wd)


def _norm_matmul_body(x_ref, g_ref, w_ref, o_ref):
    h = _rms(x_ref[...], g_ref[...]).astype(BF16)
    o_ref[...] = jnp.dot(h, w_ref[...], preferred_element_type=F32).astype(o_ref.dtype)


def _norm_matmul(x2, g, w, tn, tm=512, out_dtype=BF16):
    m = x2.shape[0]
    n = w.shape[1]
    vmem = 2 * D_MODEL * tn * 2 + 2 * tm * D_MODEL * 4 + 2 * tm * tn * 2 + 2 * tm * tn * 4 + (4 << 20)
    return pl.pallas_call(
        _norm_matmul_body,
        grid=(n // tn, m // tm),
        in_specs=[pl.BlockSpec((tm, D_MODEL), lambda j, i: (i, 0)),
                  _resident((1, D_MODEL)),
                  pl.BlockSpec((D_MODEL, tn), lambda j, i: (0, j))],
        out_specs=pl.BlockSpec((tm, tn), lambda j, i: (i, j)),
        out_shape=jax.ShapeDtypeStruct((m, n), out_dtype),
        compiler_params=_params(("parallel", "parallel"), vmem),
        name="norm_matmul",
    )(x2, g.reshape(1, D_MODEL), w)


def _fnet_weight_body(cc_ref, sc_ref, w_ref, o_ref):
    w = w_ref[...]
    o_ref[:, :D_MODEL] = jnp.dot(cc_ref[...], w, preferred_element_type=F32,
                                 precision=lax.Precision.HIGHEST).astype(BF16)
    o_ref[:, D_MODEL:] = jnp.dot(sc_ref[...], w, preferred_element_type=F32,
                                 precision=lax.Precision.HIGHEST).astype(BF16)


def _fnet_weights(w_out):
    n = np.arange(FOURIER_GROUP)
    ang = 2.0 * np.pi * ((n[:, None] * n[None, :]) % FOURIER_GROUP) / FOURIER_GROUP
    cc = jnp.asarray(np.cos(ang) / math.sqrt(FOURIER_GROUP), F32)
    sc = jnp.asarray(np.sin(ang) / math.sqrt(FOURIER_GROUP), F32)
    ng = D_MODEL // FOURIER_GROUP
    return pl.pallas_call(
        _fnet_weight_body,
        grid=(ng,),
        in_specs=[_resident((FOURIER_GROUP, FOURIER_GROUP)),
                  _resident((FOURIER_GROUP, FOURIER_GROUP)),
                  pl.BlockSpec((FOURIER_GROUP, D_MODEL), lambda i: (i, 0))],
        out_specs=pl.BlockSpec((FOURIER_GROUP, 2 * D_MODEL), lambda i: (i, 0)),
        out_shape=jax.ShapeDtypeStruct((D_MODEL, 2 * D_MODEL), BF16),
        compiler_params=_params(("parallel",), 16 << 20),
        name="fnet_weights",
    )(cc, sc, w_out)


FN_HALF = SEQ // 2
FN_BLK = 256
FN_NB = FN_HALF // FN_BLK


def _fnet_tables():
    k = np.arange(FN_HALF)[:, None]
    n = np.arange(FN_HALF)[None, :]
    ang = 2.0 * np.pi * ((k * n) % SEQ) / SEQ
    scale = 1.0 / math.sqrt(SEQ)
    i = np.arange(FN_BLK)
    rev = (i[None, :] == FN_BLK - i[:, None]).astype(np.float32)
    return (jnp.asarray(np.cos(ang) * scale, BF16), jnp.asarray(np.sin(ang) * scale, BF16),
            jnp.asarray(rev, BF16))


def _fnet_body(x_ref, g_ref, cs_ref, ss_ref, w_ref, rev_ref, o_ref, e_sc, d_sc, zc_sc, zs_sc):
    g = g_ref[...]
    scale = 1.0 / math.sqrt(SEQ)
    row = lax.broadcasted_iota(jnp.int32, (FN_BLK, 1), 0)
    first = row == 0
    sign = jnp.where((row & 1) == 0, 1.0, -1.0)
    rev = rev_ref[...]
    blk = lambda a: pl.ds(a * FN_BLK, FN_BLK)
    mirror = lambda a: pl.ds(SEQ - (a + 1) * FN_BLK, FN_BLK)

    alt = jnp.zeros((1, D_MODEL), F32)
    carry = jnp.zeros((1, D_MODEL), F32)
    for a in range(FN_NB):
        lo = _rms(x_ref[blk(a), :], g)
        hi = _rms(x_ref[mirror(a), :], g)
        alt = alt + jnp.sum(sign * lo, axis=0, keepdims=True) + jnp.sum(sign * hi, axis=0, keepdims=True)
        r = jnp.dot(rev, hi.astype(BF16), preferred_element_type=F32)
        r = jnp.where(first, carry, r)
        e_sc[blk(a), :] = (lo + r).astype(BF16)
        d_sc[blk(a), :] = (lo - r).astype(BF16)
        carry = hi[0:1, :]
    h_nyq = carry * scale

    for c in range(FN_NB):
        zc = jnp.dot(cs_ref[blk(c), :], e_sc[...], preferred_element_type=F32) + sign * h_nyq
        zs = jnp.dot(ss_ref[blk(c), :], d_sc[...], preferred_element_type=F32)
        zc_sc[blk(c), :] = zc.astype(BF16)
        zs_sc[blk(c), :] = zs.astype(BF16)

    wc = w_ref[:, :D_MODEL]
    ws = w_ref[:, D_MODEL:]
    z_nyq = jnp.broadcast_to(alt * scale, (8, D_MODEL)).astype(BF16)
    carry = jnp.dot(z_nyq, wc, preferred_element_type=F32)[0:1, :]
    for c in reversed(range(FN_NB)):
        p = jnp.dot(zc_sc[blk(c), :], wc, preferred_element_type=F32)
        q = jnp.dot(zs_sc[blk(c), :], ws, preferred_element_type=F32)
        o_ref[blk(c), :] = x_ref[blk(c), :] + (p - q)
        m = p + q
        m_hi = m.astype(BF16)
        m_lo = (m - m_hi.astype(F32)).astype(BF16)
        r = (jnp.dot(rev, m_hi, preferred_element_type=F32) + jnp.dot(rev, m_lo, preferred_element_type=F32))
        r = jnp.where(first, carry, r)
        o_ref[mirror(c), :] = x_ref[mirror(c), :] + r
        carry = m[0:1, :]


def _fnet_layer(x, g, w_out):
    wcs = _fnet_weights(w_out)
    cs, ss, rev = _fnet_tables()
    seq_block = pl.BlockSpec((None, SEQ, D_MODEL), lambda b: (b, 0, 0))
    half = pltpu.VMEM((FN_HALF, D_MODEL), BF16)
    vmem = 4 * SEQ * D_MODEL * 4 + 4 * FN_HALF * D_MODEL * 2 + 4 * FN_HALF * D_MODEL * 2 + (12 << 20)
    return pl.pallas_call(
        _fnet_body,
        grid=(BATCH,),
        in_specs=[seq_block, _resident((1, D_MODEL)), _resident((FN_HALF, FN_HALF)),
                  _resident((FN_HALF, FN_HALF)), _resident((D_MODEL, 2 * D_MODEL)),
                  _resident((FN_BLK, FN_BLK))],
        out_specs=seq_block,
        out_shape=jax.ShapeDtypeStruct((BATCH, SEQ, D_MODEL), F32),
        scratch_shapes=[half, half, half, half],
        compiler_params=_params(("parallel",), vmem),
        name="fnet_mix",
    )(x, g.reshape(1, D_MODEL), cs, ss, wcs, rev)


def _s5_matrices(lam_re, lam_im, log_dt, b_re, b_im, c_re, c_im):
    lam = lax.complex(lam_re.astype(F32), lam_im.astype(F32))
    dt = jnp.exp(log_dt.astype(F32))[..., None]
    lam_dt = lam * dt
    lam_bar = jnp.exp(lam_dt)
    b_bar = ((lam_bar - 1.0) / lam)[..., None] * lax.complex(b_re.astype(F32), b_im.astype(F32))
    c = lax.complex(c_re.astype(F32), c_im.astype(F32))
    L = S5_CHUNK
    taus = jnp.arange(L + 1, dtype=F32)
    pw = jnp.exp(lam_dt[..., None, :] * taus[:, None])
    hi = lax.Precision.HIGHEST
    kern = jnp.einsum('dgpn,dgtn,dgnq->dgtpq', c, pw[:, :, :L], b_bar, precision=hi).real
    tt = np.arange(L)
    lag = tt[None, :] - tt[:, None]
    sel_f = jnp.asarray(lag[:, :, None] == tt, F32)
    sel_b = jnp.asarray(-lag[:, :, None] == tt, F32)
    toep = (jnp.einsum('abt,gtpq->gaqbp', sel_f, kern[0], precision=hi)
            + jnp.einsum('abt,gtpq->gaqbp', sel_b, kern[1], precision=hi))
    toep = toep.reshape(S5_GROUPS, S5_CK, S5_CK)
    sf = pw[0][:, ::-1][:, 1:][..., None] * b_bar[0][:, None]
    sb = pw[1][:, :L][..., None] * b_bar[1][:, None]
    def rows(z):
        return z.transpose(0, 1, 3, 2).reshape(S5_GROUPS, S5_CK, S5_STATE)
    s_in = jnp.concatenate([rows(sf.real), rows(sb.real), rows(sf.imag), rows(sb.imag)], axis=-1)
    wx = jnp.concatenate([toep, s_in], axis=-1).astype(BF16)
    zf = c[0][:, None] * pw[0][:, 1:][:, :, None, :]
    zb = c[1][:, None] * pw[1][:, ::-1][:, :L][:, :, None, :]
    def cols(z):
        return z.transpose(0, 3, 1, 2).reshape(S5_GROUPS, S5_STATE, S5_CK)
    wc = jnp.concatenate([cols(zf.real), cols(zb.real), -cols(zf.imag), -cols(zb.imag)], axis=1).astype(BF16)
    al = pw[:, :, L]
    coef = jnp.stack([jnp.concatenate([al[0].real, al[1].real], -1),
                      jnp.concatenate([al[0].imag, al[1].imag], -1)], axis=1)
    return wx, wc, coef.astype(F32)


def _s5_norm_body(x_ref, g_ref, o_ref):
    h = _rms(x_ref[...], g_ref[...])
    o_ref[...] = h.reshape(o_ref.shape)


def _s5_norm(x, g, tm=512):
    cr = tm // S5_CHUNK
    return pl.pallas_call(
        _s5_norm_body,
        grid=(BATCH, SEQ // tm),
        in_specs=[pl.BlockSpec((None, tm, D_MODEL), lambda b, i: (b, i, 0)),
                  _resident((1, D_MODEL))],
        out_specs=pl.BlockSpec((cr, None, S5_CHUNK, D_MODEL), lambda b, i: (i, b, 0, 0)),
        out_shape=jax.ShapeDtypeStruct((S5_CHUNKS, BATCH, S5_CHUNK, D_MODEL), F32),
        compiler_params=_params(("parallel", "parallel"), 32 << 20),
        name="s5_norm",
    )(x, g.reshape(1, D_MODEL))


def _block_transpose8(a, lane_block):
    a = list(a)
    for d in (4, 2, 1):
        take_lo = (lane_block & d) == 0
        nxt = list(a)
        for i in range(8):
            if i & d:
                continue
            lo, hi = a[i], a[i + d]
            nxt[i] = jnp.where(take_lo, lo, pltpu.roll(hi, S5_GROUP * d, 1))
            nxt[i + d] = jnp.where(take_lo, pltpu.roll(lo, LANES - S5_GROUP * d, 1), hi)
        a = nxt
    return a


def _gelu_tanh(y):
    return 0.5 * y * (1.0 + jnp.tanh(math.sqrt(2.0 / math.pi) * (y + 0.044715 * (y * y * y))))


S5_RB = 64


def _s5_body(h_ref, wx_ref, wc_ref, coef_ref, d_ref, o_ref, x_sc, y_sc, s_sc, hs_sc):
    lane = lax.broadcasted_iota(jnp.int32, (1, LANES), 1)
    lane_block = lane // S5_GROUP
    fwd_lanes = lane < S5_STATE
    n_rb = S5_ROWS // S5_RB
    tok_rb = S5_RB * S5_CHUNK

    def relayout_in(i, carry):
        base = pl.multiple_of(i * tok_rb, tok_rb)
        rows = pl.multiple_of(i * S5_RB, S5_RB)
        for half in range(2):
            a = [h_ref[pl.ds(base + half * 8 + t, S5_RB, stride=S5_CHUNK), :] for t in range(8)]
            xt = _block_transpose8(a, lane_block)
            for gi in range(S5_TILE_GROUPS):
                x_sc[gi, pl.ds(rows, S5_RB), half * LANES:(half + 1) * LANES] = xt[gi].astype(BF16)
        return carry

    lax.fori_loop(0, n_rb, relayout_in, 0)

    for gi in range(S5_TILE_GROUPS):
        z = jnp.dot(x_sc[gi], wx_ref[gi], preferred_element_type=F32)
        y_sc[gi] = z[:, :S5_CK]
        s_sc[0] = z[:, S5_CK:S5_CK + LANES]
        s_sc[1] = z[:, S5_CK + LANES:]
        a_re = coef_ref[gi, 0:1, :]
        a_im = coef_ref[gi, 1:2, :]

        def scan_step(k, st):
            st_re, st_im = st
            rf = pl.multiple_of(k * BATCH, BATCH)
            rb = pl.multiple_of((S5_CHUNKS - 1 - k) * BATCH, BATCH)
            hs_sc[0, pl.ds(rf, BATCH), :S5_STATE] = st_re[:, :S5_STATE]
            hs_sc[1, pl.ds(rf, BATCH), :S5_STATE] = st_im[:, :S5_STATE]
            hs_sc[0, pl.ds(rb, BATCH), S5_STATE:] = st_re[:, S5_STATE:]
            hs_sc[1, pl.ds(rb, BATCH), S5_STATE:] = st_im[:, S5_STATE:]
            s_re = jnp.where(fwd_lanes, s_sc[0, pl.ds(rf, BATCH), :], s_sc[0, pl.ds(rb, BATCH), :])
            s_im = jnp.where(fwd_lanes, s_sc[1, pl.ds(rf, BATCH), :], s_sc[1, pl.ds(rb, BATCH), :])
            n_re = a_re * st_re - a_im * st_im + s_re
            n_im = a_re * st_im + a_im * st_re + s_im
            return n_re, n_im

        zero = jnp.zeros((BATCH, LANES), F32)
        lax.fori_loop(0, S5_CHUNKS, scan_step, (zero, zero))
        hcat = jnp.concatenate([hs_sc[0], hs_sc[1]], axis=1).astype(BF16)
        y_sc[gi] = y_sc[gi] + jnp.dot(hcat, wc_ref[gi], preferred_element_type=F32)

    d_skip = d_ref[...]

    def relayout_out(i, carry):
        base = pl.multiple_of(i * tok_rb, tok_rb)
        rows = pl.multiple_of(i * S5_RB, S5_RB)
        for half in range(2):
            yg = [y_sc[gi, pl.ds(rows, S5_RB), half * LANES:(half + 1) * LANES] for gi in range(S5_TILE_GROUPS)]
            yt = _block_transpose8(yg, lane_block)
            for t in range(8):
                tok = pl.ds(base + half * 8 + t, S5_RB, stride=S5_CHUNK)
                o_ref[tok, :] = _gelu_tanh(yt[t] + d_skip * h_ref[tok, :])
        return carry

    lax.fori_loop(0, n_rb, relayout_out, 0)


def _s5_core(h2, wx, wc, coef, d_skip):
    n_tiles = S5_GROUPS // S5_TILE_GROUPS
    tok_block = pl.BlockSpec((N_TOKENS, LANES), lambda j: (0, j), pipeline_mode=pl.Buffered(1))
    vmem = (2 * N_TOKENS * LANES * 4 + S5_TILE_GROUPS * S5_ROWS * S5_CK * (2 + 4)
            + 4 * S5_ROWS * LANES * 4 + 2 * S5_TILE_GROUPS * S5_CK * 3 * S5_CK * 2 + (12 << 20))
    return pl.pallas_call(
        _s5_body,
        grid=(n_tiles,),
        in_specs=[tok_block,
                  pl.BlockSpec((S5_TILE_GROUPS, S5_CK, 2 * S5_CK), lambda j: (j, 0, 0)),
                  pl.BlockSpec((S5_TILE_GROUPS, S5_CK, S5_CK), lambda j: (j, 0, 0)),
                  pl.BlockSpec((S5_TILE_GROUPS, 2, LANES), lambda j: (j, 0, 0)),
                  pl.BlockSpec((1, LANES), lambda j: (0, j))],
        out_specs=tok_block,
        out_shape=jax.ShapeDtypeStruct((N_TOKENS, D_MODEL), F32),
        scratch_shapes=[pltpu.VMEM((S5_TILE_GROUPS, S5_ROWS, S5_CK), BF16),
                        pltpu.VMEM((S5_TILE_GROUPS, S5_ROWS, S5_CK), F32),
                        pltpu.VMEM((2, S5_ROWS, LANES), F32),
                        pltpu.VMEM((2, S5_ROWS, LANES), F32)],
        compiler_params=_params(("arbitrary",), vmem),
        name="s5_core",
    )(h2, wx, wc, coef, d_skip.reshape(1, D_MODEL))


def _s5_glu_body(x_ref, a_ref, w_ref, o_ref):
    a = a_ref[...].reshape(x_ref.shape).astype(BF16)
    vg = jnp.dot(a, w_ref[...], preferred_element_type=F32)
    o_ref[...] = x_ref[...] + vg[:, :D_MODEL] * _sigmoid(vg[:, D_MODEL:])


def _s5_glu(x, act, w_glu, tm=512):
    cr = tm // S5_CHUNK
    vmem = w_glu.size * 2 + 6 * tm * D_MODEL * 4 + 2 * tm * 2 * D_MODEL * 4 + (4 << 20)
    return pl.pallas_call(
        _s5_glu_body,
        grid=(BATCH, SEQ // tm),
        in_specs=[pl.BlockSpec((None, tm, D_MODEL), lambda b, i: (b, i, 0)),
                  pl.BlockSpec((cr, None, S5_CHUNK, D_MODEL), lambda b, i: (i, b, 0, 0)),
                  _resident((D_MODEL, 2 * D_MODEL))],
        out_specs=pl.BlockSpec((None, tm, D_MODEL), lambda b, i: (b, i, 0)),
        out_shape=jax.ShapeDtypeStruct((BATCH, SEQ, D_MODEL), F32),
        compiler_params=_params(("parallel", "parallel"), vmem),
        name="s5_glu",
    )(x, act, w_glu)


def _s5_layer(x, g, lam_re, lam_im, log_dt, b_re, b_im, c_re, c_im, d_skip, w_glu):
    wx, wc, coef = _s5_matrices(lam_re, lam_im, log_dt, b_re, b_im, c_re, c_im)
    h = _s5_norm(x, g)
    act = _s5_core(h.reshape(N_TOKENS, D_MODEL), wx, wc, coef, d_skip)
    act = act.reshape(S5_CHUNKS, BATCH, S5_CHUNK, D_MODEL)
    return _s5_glu(x, act, w_glu.astype(BF16))


def _t5_bucket(rel):
    half = NUM_BUCKETS // 2
    max_exact = half // 2
    n = np.abs(rel)
    sign = (rel > 0).astype(np.int32) * half
    large = max_exact + (np.log(np.maximum(n, 1) / max_exact) / math.log(MAX_DISTANCE / max_exact)
                         * (half - max_exact)).astype(np.int32)
    large = np.minimum(large, half - 1)
    return (sign + np.where(n < max_exact, n, large)).astype(np.int32)


LOG2E = 1.4426950408889634
N_PAIRS = HEADS_PER_GROUP // 2
ATTN_PAD_ROWS = SEQ + ATTN_BQ * max(d for _, d in DILATED_GROUPS)
QKV_GROUP_WIDTH = 3 * D_MODEL
QKV_SLAB = 2 * LANES
QKV_TILE_ROWS = 512


def _attn_bias_tables(rel_bias):
    n_off = ATTN_BK + ATTN_BQ - 1
    offs = np.arange(n_off) - (ATTN_BQ - 1) - ATTN_SIDE
    k = np.arange(ATTN_BK)[:, None]
    q = np.arange(ATTN_BQ)[None, :]
    band = np.abs(k - ATTN_SIDE - q) <= ATTN_SIDE
    tabs = []
    for gi, (_, dil) in enumerate(DILATED_GROUPS):
        onehot = jnp.asarray(_t5_bucket(offs * dil)[:, None] == np.arange(NUM_BUCKETS), F32)
        f = jnp.dot(onehot, rel_bias[:, gi * HEADS_PER_GROUP:(gi + 1) * HEADS_PER_GROUP].astype(F32),
                    precision=lax.Precision.HIGHEST).T
        fr = jnp.pad(f[:, ::-1], ((0, 0), (0, 1)))
        skew = jnp.tile(fr, (1, ATTN_BK))[:, ATTN_BK - 1:ATTN_BK - 1 + ATTN_BK * n_off]
        t = skew.reshape(HEADS_PER_GROUP, ATTN_BK, n_off)[:, :, :ATTN_BQ]
        t = jnp.where(band[None], t * LOG2E, -1e30)
        t = t.reshape(N_PAIRS, 2, ATTN_BK, ATTN_BQ).transpose(0, 2, 1, 3).reshape(N_PAIRS, ATTN_BK, 2 * ATTN_BQ)
        tabs.append(t)
    return jnp.stack(tabs)


def _qkv_body(x_ref, g_ref, w_ref, qg_ref, kg_ref, perm_ref, q_out, k_out, v_out, *, n_res):
    rows = x_ref.shape[0] // n_res
    h = _rms(x_ref[...], g_ref[...]).astype(BF16)
    if n_res > 1:
        h = jnp.dot(perm_ref[...], h, preferred_element_type=F32).astype(BF16)
    lane = lax.broadcasted_iota(jnp.int32, (1, LANES), 1)
    head0 = lane < HEAD_DIM

    def head_norm(t, gain):
        sq = t * t
        s0 = jnp.sum(jnp.where(head0, sq, 0.0), axis=-1, keepdims=True)
        s1 = jnp.sum(jnp.where(head0, 0.0, sq), axis=-1, keepdims=True)
        ms = jnp.where(head0, s0, s1) * (1.0 / HEAD_DIM)
        return t * lax.rsqrt(ms + EPS) * gain

    for c in range(QKV_GROUP_WIDTH // QKV_SLAB):
        z = jnp.dot(h, w_ref[:, c * QKV_SLAB:(c + 1) * QKV_SLAB], preferred_element_type=F32)
        for half in range(QKV_SLAB // LANES):
            section, lo = divmod(c * QKV_SLAB + half * LANES, D_MODEL)
            zz = z[:, half * LANES:(half + 1) * LANES]
            if section == 0:
                zz = head_norm(zz, qg_ref[...]) * (HEAD_DIM ** -0.5 * LOG2E)
                qa = jnp.where(head0, zz, 0.0).astype(BF16)
                qb = jnp.where(head0, 0.0, zz).astype(BF16)
            elif section == 1:
                zz = head_norm(zz, kg_ref[...]).astype(BF16)
            else:
                zz = zz.astype(BF16)
            for r in range(n_res):
                piece = slice(r * rows, (r + 1) * rows)
                if section == 0:
                    q_out[r, 0, :, lo:lo + LANES] = qa[piece]
                    q_out[r, 1, :, lo:lo + LANES] = qb[piece]
                elif section == 1:
                    k_out[r, :, lo:lo + LANES] = zz[piece]
                else:
                    v_out[r, :, lo:lo + LANES] = zz[piece]


def _qkv_group(x, g, w_qkv, gi, dil, q_gain, k_gain):
    seg = SEQ // dil
    tm = QKV_TILE_ROWS
    rows = tm // dil
    assert tm % dil == 0 and rows % 16 == 0
    gain2 = lambda gn: jnp.tile(gn.astype(F32), 2).reshape(1, LANES)
    src = np.arange(tm).reshape(rows, dil).T.reshape(tm)
    perm = jnp.asarray(src[:, None] == np.arange(tm)[None, :], BF16)
    kv_spec = pl.BlockSpec((None, dil, rows, D_MODEL), lambda b, i: (b, 0, i, 0))
    kv_shape = jax.ShapeDtypeStruct((BATCH, dil, seg, D_MODEL), BF16)
    vmem = (D_MODEL * QKV_GROUP_WIDTH * 2 + 2 * tm * D_MODEL * 4 + 2 * 4 * tm * D_MODEL * 2 + (12 << 20))
    return pl.pallas_call(
        functools.partial(_qkv_body, n_res=dil),
        grid=(BATCH, SEQ // tm),
        in_specs=[pl.BlockSpec((None, tm, D_MODEL), lambda b, i: (b, i, 0)),
                  _resident((1, D_MODEL)),
                  pl.BlockSpec((D_MODEL, QKV_GROUP_WIDTH), lambda b, i: (0, gi), pipeline_mode=pl.Buffered(1)),
                  _resident((1, LANES)), _resident((1, LANES)), _resident((tm, tm))],
        out_specs=[pl.BlockSpec((None, dil, 2, rows, D_MODEL), lambda b, i: (b, 0, 0, i, 0)),
                   kv_spec, kv_spec],
        out_shape=[jax.ShapeDtypeStruct((BATCH, dil, 2, seg, D_MODEL), BF16), kv_shape, kv_shape],
        compiler_params=_params(("parallel", "parallel"), vmem),
        name=f"qkv_dil{dil}",
    )(x, g.reshape(1, D_MODEL), w_qkv, gain2(q_gain), gain2(k_gain), perm)


def _attn_body(q0, k0, v0, q1, k1, v1, q2, k2, v2, bias_ref, o_ref, kp_sc, vp_sc, lt_sc, og_sc, lg_sc):
    lane = lax.broadcasted_iota(jnp.int32, (1, LANES), 1)
    head0 = lane < HEAD_DIM
    krow = lax.broadcasted_iota(jnp.int32, (ATTN_BK, 1), 0)
    zpad = jnp.zeros((ATTN_SIDE, LANES), BF16)

    qkv_refs = ((q0, k0, v0), (q1, k1, v1), (q2, k2, v2))
    for g, (_, dil) in enumerate(DILATED_GROUPS):
        q_ref, k_ref, v_ref = qkv_refs[g]
        seg = SEQ // dil
        nb = seg // ATTN_BQ
        pad_seg = seg + 2 * ATTN_SIDE
        lt_sc[...] = jnp.zeros(lt_sc.shape, F32)
        for r in range(dil):
            base = r * pad_seg
            for src, dst in ((k_ref, kp_sc), (v_ref, vp_sc)):
                dst[base:base + ATTN_SIDE] = zpad
                dst[base + ATTN_SIDE + seg:base + pad_seg] = zpad
                dst[base + ATTN_SIDE:base + ATTN_SIDE + seg] = src[r]

        for r in range(dil):
            base = r * pad_seg
            for i in range(nb):
                blk = r * nb + i
                win = slice(base + i * ATTN_BQ, base + i * ATTN_BQ + ATTN_BK)
                qrows = slice(i * ATTN_BQ, (i + 1) * ATTN_BQ)
                qm = jnp.concatenate([q_ref[r, 0, qrows, :], q_ref[r, 1, qrows, :]], axis=0)
                s = lax.dot_general(kp_sc[win], qm, (((1,), (1,)), ((), ())), preferred_element_type=F32)
                s = s + bias_ref[g]
                if i == 0 or i == nb - 1:
                    ok = None
                    if i == 0:
                        ok = krow >= ATTN_SIDE
                    if i == nb - 1:
                        ok_hi = krow < ATTN_BK - ATTN_SIDE
                        ok = ok_hi if ok is None else jnp.logical_and(ok, ok_hi)
                    s = jnp.where(ok, s, -1e30)
                vb = vp_sc[win]
                o_heads, lse_heads = [], []
                for a in range(2):
                    sa = s[:, a * ATTN_BQ:(a + 1) * ATTN_BQ]
                    m = jnp.max(sa, axis=0, keepdims=True)
                    p = jnp.exp2(sa - m)
                    den = jnp.sum(p, axis=0, keepdims=True)
                    pt = (p * (1.0 / den)).T.astype(BF16)
                    o_heads.append(jnp.dot(pt, vb, preferred_element_type=F32))
                    lse_heads.append(m + jnp.log2(den))
                rows = pl.ds(r + i * ATTN_BQ * dil, ATTN_BQ, stride=dil)
                og_sc[g, rows, :] = jnp.where(head0, o_heads[0], o_heads[1])
                lt_sc[8 * blk:8 * blk + 1, :] = lse_heads[0]
                lt_sc[8 * blk + 1:8 * blk + 2, :] = lse_heads[1]

        ltt = lt_sc[...].T
        for r in range(dil):
            for i in range(nb):
                c = 8 * (r * nb + i)
                rows = pl.ds(r + i * ATTN_BQ * dil, ATTN_BQ, stride=dil)
                lg_sc[g, rows, :] = jnp.where(head0, ltt[:, c:c + 1], ltt[:, c + 1:c + 2])

    l0, l1, l2 = lg_sc[0], lg_sc[1], lg_sc[2]
    m = jnp.maximum(jnp.maximum(l0, l1), l2)
    w0, w1, w2 = jnp.exp2(l0 - m), jnp.exp2(l1 - m), jnp.exp2(l2 - m)
    o = (w0 * og_sc[0] + w1 * og_sc[1] + w2 * og_sc[2]) / (w0 + w1 + w2)
    o_ref[...] = o.astype(o_ref.dtype)


def _attn_core(qkv, bias_tabs):
    in_specs = []
    for (_, dil) in DILATED_GROUPS:
        seg = SEQ // dil
        in_specs.append(pl.BlockSpec((None, dil, 2, seg, LANES), lambda b, p: (b, 0, 0, 0, p)))
        in_specs.append(pl.BlockSpec((None, dil, seg, LANES), lambda b, p: (b, 0, 0, p)))
        in_specs.append(pl.BlockSpec((None, dil, seg, LANES), lambda b, p: (b, 0, 0, p)))
    in_specs.append(pl.BlockSpec((N_ATTN_GROUPS, None, ATTN_BK, 2 * ATTN_BQ), lambda b, p: (0, p, 0, 0)))
    return pl.pallas_call(
        _attn_body,
        grid=(BATCH, N_PAIRS),
        in_specs=in_specs,
        out_specs=pl.BlockSpec((None, SEQ, LANES), lambda b, p: (b, 0, p)),
        out_shape=jax.ShapeDtypeStruct((BATCH, SEQ, D_MODEL), BF16),
        scratch_shapes=[pltpu.VMEM((ATTN_PAD_ROWS, LANES), BF16),
                        pltpu.VMEM((ATTN_PAD_ROWS, LANES), BF16),
                        pltpu.VMEM((LANES, LANES), F32),
                        pltpu.VMEM((N_ATTN_GROUPS, SEQ, LANES), F32),
                        pltpu.VMEM((N_ATTN_GROUPS, SEQ, LANES), F32)],
        compiler_params=_params(("parallel", "parallel"), 40 << 20),
        name="attn_core",
    )(*qkv, bias_tabs)


def _proj_residual_body(x_ref, a_ref, w_ref, o_ref):
    o_ref[...] = x_ref[...] + jnp.dot(a_ref[...], w_ref[...], preferred_element_type=F32)


def _proj_residual(x2, a2, w, tm=512):
    tile = pl.BlockSpec((tm, D_MODEL), lambda i: (i, 0))
    return pl.pallas_call(
        _proj_residual_body,
        grid=(x2.shape[0] // tm,),
        in_specs=[tile, tile, _resident((D_MODEL, D_MODEL))],
        out_specs=tile,
        out_shape=jax.ShapeDtypeStruct(x2.shape, F32),
        compiler_params=_params(("parallel",), 32 << 20),
        name="proj_residual",
    )(x2, a2, w)


def _attn_layer(x, g, w_qkv, q_gain, k_gain, w_o, rel_bias):
    w_bf = w_qkv.astype(BF16)
    qkv = []
    for gi, (window, dil) in enumerate(DILATED_GROUPS):
        assert (window // 2) // dil == ATTN_SIDE and (SEQ // dil) % ATTN_BQ == 0
        qkv.extend(_qkv_group(x, g, w_bf, gi, dil, q_gain[gi], k_gain[gi]))
    o = _attn_core(qkv, _attn_bias_tables(rel_bias))
    x2 = x.reshape(N_TOKENS, D_MODEL)
    return _proj_residual(x2, o.reshape(N_TOKENS, D_MODEL), w_o.astype(BF16)).reshape(BATCH, SEQ, D_MODEL)


def kernel(x, norm_mix_g, norm_ffn_g, fnet_w_out, s5_lambda_re, s5_lambda_im, s5_log_dt, s5_b_re, s5_b_im, s5_c_re, s5_c_im, s5_d, s5_w_glu, attn_w_qkv, attn_q_gain, attn_k_gain, attn_w_o, rel_bias, ffn_w_gate_up, ffn_w_down):
    counts = [0, 0, 0]
    for i in range(DEPTH):
        kind = i % 3
        j = counts[kind]
        counts[kind] += 1
        if kind == 0:
            x = _fnet_layer(x, norm_mix_g[i], fnet_w_out[j])
        elif kind == 1:
            x = _s5_layer(x, norm_mix_g[i], s5_lambda_re[j], s5_lambda_im[j], s5_log_dt[j], s5_b_re[j],
                          s5_b_im[j], s5_c_re[j], s5_c_im[j], s5_d[j], s5_w_glu[j])
        else:
            x = _attn_layer(x, norm_mix_g[i], attn_w_qkv[j], attn_q_gain[j], attn_k_gain[j], attn_w_o[j],
                            rel_bias)
        x = _ffn(x.reshape(N_TOKENS, D_MODEL), norm_ffn_g[i], ffn_w_gate_up[i].astype(BF16),
                 ffn_w_down[i].astype(BF16)).reshape(BATCH, SEQ, D_MODEL)
    return x
```

```python
import functools
import math

import numpy as np
import jax
import jax.numpy as jnp
from jax import lax
from jax.experimental import pallas as pl
from jax.experimental.pallas import tpu as pltpu

F32 = jnp.float32
BF16 = jnp.bfloat16

D_MODEL = 1024
BATCH = 8
SEQ = 2048
DEPTH = 4
N_TOKENS = BATCH * SEQ
EPS = 1e-6
D_FF = 2816
FOURIER_GROUP = 128
S5_GROUP = 16
S5_GROUPS = 64
S5_STATE = 64
HEAD_DIM = 64
HEADS_PER_GROUP = 16
DILATED_GROUPS = ((128, 1), (512, 4), (2048, 16))
N_ATTN_GROUPS = 3
NUM_BUCKETS = 32
MAX_DISTANCE = 1024
ATTN_SIDE = 64

LANES = 128
VMEM_LIMIT_CAP = 60 * 1024 * 1024

S5_CHUNK = 16
S5_CHUNKS = SEQ // S5_CHUNK
S5_ROWS = BATCH * S5_CHUNKS
S5_TILE_GROUPS = LANES // S5_GROUP
S5_CK = S5_CHUNK * S5_GROUP

ATTN_BQ = 128
ATTN_BK = ATTN_BQ + 2 * ATTN_SIDE


def _params(sem, vmem_bytes):
    return pltpu.CompilerParams(dimension_semantics=sem,
                                vmem_limit_bytes=int(min(VMEM_LIMIT_CAP, vmem_bytes)))


def _rms(x, g):
    ms = jnp.mean(x * x, axis=-1, keepdims=True)
    return x * lax.rsqrt(ms + EPS) * g


def _sigmoid(x):
    return 1.0 / (1.0 + jnp.exp(-x))


def _resident(shape):
    nd = len(shape)
    return pl.BlockSpec(shape, lambda *_: (0,) * nd, pipeline_mode=pl.Buffered(1))


def _cast_body(w_ref, o_ref):
    o_ref[...] = w_ref[...].astype(BF16)


def _cast_bf16(w, rows=256):
    n_layers, n_rows, n_cols = w.shape
    block = pl.BlockSpec((None, rows, n_cols), lambda l, i: (l, i, 0))
    return pl.pallas_call(
        _cast_body,
        grid=(n_layers, n_rows // rows),
        in_specs=[block],
        out_specs=block,
        out_shape=jax.ShapeDtypeStruct(w.shape, BF16),
        compiler_params=_params(("parallel", "parallel"), 4 * rows * n_cols * 6 + (4 << 20)),
        name="cast_bf16",
    )(w)


def _ffn_body(x_ref, g_ref, wgu_ref, wd_ref, o_ref):
    x = x_ref[...]
    h = _rms(x, g_ref[...]).astype(BF16)
    gu = jnp.dot(h, wgu_ref[...], preferred_element_type=F32)
    gate = gu[:, :D_FF]
    up = gu[:, D_FF:]
    a = (gate * _sigmoid(gate) * up).astype(BF16)
    o_ref[...] = x + jnp.dot(a, wd_ref[...], preferred_element_type=F32)


def _layer_block(shape, layer):
    return pl.BlockSpec((None,) + shape, lambda *_: (layer,) + (0,) * len(shape), pipeline_mode=pl.Buffered(1))


def _ffn(x2, g, wgu_all, wd_all, layer, tm=256):
    m = x2.shape[0]
    vmem = 3 * D_MODEL * D_FF * 2 + 4 * tm * D_MODEL * 4 + 4 * tm * 2 * D_FF * 4 + (4 << 20)
    return pl.pallas_call(
        _ffn_body,
        grid=(m // tm,),
        in_specs=[pl.BlockSpec((tm, D_MODEL), lambda i: (i, 0)),
                  _resident((1, D_MODEL)),
                  _layer_block((D_MODEL, 2 * D_FF), layer),
                  _layer_block((D_FF, D_MODEL), layer)],
        out_specs=pl.BlockSpec((tm, D_MODEL), lambda i: (i, 0)),
        out_shape=jax.ShapeDtypeStruct((m, D_MODEL), F32),
        compiler_params=_params(("parallel",), vmem),
        name="ffn",
    )(x2, g.reshape(1, D_MODEL), wgu_all, wd_all)


def _fnet_weight_body(cc_ref, sc_ref, w_ref, o_ref):
    w = w_ref[...]
    o_ref[:, :D_MODEL] = jnp.dot(cc_ref[...], w, preferred_element_type=F32,
                                 precision=lax.Precision.HIGHEST).astype(BF16)
    o_ref[:, D_MODEL:] = jnp.dot(sc_ref[...], w, preferred_element_type=F32,
                                 precision=lax.Precision.HIGHEST).astype(BF16)


def _fnet_weights(w_out):
    n = np.arange(FOURIER_GROUP)
    ang = 2.0 * np.pi * ((n[:, None] * n[None, :]) % FOURIER_GROUP) / FOURIER_GROUP
    cc = jnp.asarray(np.cos(ang) / math.sqrt(FOURIER_GROUP), F32)
    sc = jnp.asarray(np.sin(ang) / math.sqrt(FOURIER_GROUP), F32)
    ng = D_MODEL // FOURIER_GROUP
    return pl.pallas_call(
        _fnet_weight_body,
        grid=(ng,),
        in_specs=[_resident((FOURIER_GROUP, FOURIER_GROUP)),
                  _resident((FOURIER_GROUP, FOURIER_GROUP)),
                  pl.BlockSpec((FOURIER_GROUP, D_MODEL), lambda i: (i, 0))],
        out_specs=pl.BlockSpec((FOURIER_GROUP, 2 * D_MODEL), lambda i: (i, 0)),
        out_shape=jax.ShapeDtypeStruct((D_MODEL, 2 * D_MODEL), BF16),
        compiler_params=_params(("parallel",), 16 << 20),
        name="fnet_weights",
    )(cc, sc, w_out)


FN_HALF = SEQ // 2
FN_BLK = 256
FN_NB = FN_HALF // FN_BLK


def _fnet_tables():
    k = np.arange(FN_HALF)[:, None]
    n = np.arange(FN_HALF)[None, :]
    ang = 2.0 * np.pi * ((k * n) % SEQ) / SEQ
    scale = 1.0 / math.sqrt(SEQ)
    i = np.arange(FN_BLK)
    rev = (i[None, :] == FN_BLK - i[:, None]).astype(np.float32)
    return (jnp.asarray(np.cos(ang) * scale, BF16), jnp.asarray(np.sin(ang) * scale, BF16),
            jnp.asarray(rev, BF16))


def _fnet_body(x_ref, g_ref, cs_ref, ss_ref, w_ref, rev_ref, o_ref, e_sc, d_sc, zc_sc, zs_sc):
    g = g_ref[...]
    scale = 1.0 / math.sqrt(SEQ)
    row = lax.broadcasted_iota(jnp.int32, (FN_BLK, 1), 0)
    first = row == 0
    sign = jnp.where((row & 1) == 0, 1.0, -1.0)
    rev = rev_ref[...]
    blk = lambda a: pl.ds(a * FN_BLK, FN_BLK)
    mirror = lambda a: pl.ds(SEQ - (a + 1) * FN_BLK, FN_BLK)

    alt = jnp.zeros((1, D_MODEL), F32)
    carry = jnp.zeros((1, D_MODEL), F32)
    for a in range(FN_NB):
        lo = _rms(x_ref[blk(a), :], g)
        hi = _rms(x_ref[mirror(a), :], g)
        alt = alt + jnp.sum(sign * lo, axis=0, keepdims=True) + jnp.sum(sign * hi, axis=0, keepdims=True)
        r = jnp.dot(rev, hi.astype(BF16), preferred_element_type=F32)
        r = jnp.where(first, carry, r)
        e_sc[blk(a), :] = (lo + r).astype(BF16)
        d_sc[blk(a), :] = (lo - r).astype(BF16)
        carry = hi[0:1, :]
    h_nyq = carry * scale

    for c in range(FN_NB):
        zc = jnp.dot(cs_ref[blk(c), :], e_sc[...], preferred_element_type=F32) + sign * h_nyq
        zs = jnp.dot(ss_ref[blk(c), :], d_sc[...], preferred_element_type=F32)
        zc_sc[blk(c), :] = zc.astype(BF16)
        zs_sc[blk(c), :] = zs.astype(BF16)

    wc = w_ref[:, :D_MODEL]
    ws = w_ref[:, D_MODEL:]
    z_nyq = jnp.broadcast_to(alt * scale, (8, D_MODEL)).astype(BF16)
    carry = jnp.dot(z_nyq, wc, preferred_element_type=F32)[0:1, :]
    for c in reversed(range(FN_NB)):
        p = jnp.dot(zc_sc[blk(c), :], wc, preferred_element_type=F32)
        q = jnp.dot(zs_sc[blk(c), :], ws, preferred_element_type=F32)
        o_ref[blk(c), :] = x_ref[blk(c), :] + (p - q)
        m = p + q
        m_hi = m.astype(BF16)
        m_lo = (m - m_hi.astype(F32)).astype(BF16)
        r = (jnp.dot(rev, m_hi, preferred_element_type=F32) + jnp.dot(rev, m_lo, preferred_element_type=F32))
        r = jnp.where(first, carry, r)
        o_ref[mirror(c), :] = x_ref[mirror(c), :] + r
        carry = m[0:1, :]


def _fnet_layer(x, g, w_out):
    wcs = _fnet_weights(w_out)
    cs, ss, rev = _fnet_tables()
    seq_block = pl.BlockSpec((None, SEQ, D_MODEL), lambda b: (b, 0, 0))
    half = pltpu.VMEM((FN_HALF, D_MODEL), BF16)
    vmem = 4 * SEQ * D_MODEL * 4 + 4 * FN_HALF * D_MODEL * 2 + 4 * FN_HALF * D_MODEL * 2 + (12 << 20)
    return pl.pallas_call(
        _fnet_body,
        grid=(BATCH,),
        in_specs=[seq_block, _resident((1, D_MODEL)), _resident((FN_HALF, FN_HALF)),
                  _resident((FN_HALF, FN_HALF)), _resident((D_MODEL, 2 * D_MODEL)),
                  _resident((FN_BLK, FN_BLK))],
        out_specs=seq_block,
        out_shape=jax.ShapeDtypeStruct((BATCH, SEQ, D_MODEL), F32),
        scratch_shapes=[half, half, half, half],
        compiler_params=_params(("parallel",), vmem),
        name="fnet_mix",
    )(x, g.reshape(1, D_MODEL), cs, ss, wcs, rev)


def _s5_matrices(lam_re, lam_im, log_dt, b_re, b_im, c_re, c_im):
    lam = lax.complex(lam_re.astype(F32), lam_im.astype(F32))
    dt = jnp.exp(log_dt.astype(F32))[..., None]
    lam_dt = lam * dt
    lam_bar = jnp.exp(lam_dt)
    b_bar = ((lam_bar - 1.0) / lam)[..., None] * lax.complex(b_re.astype(F32), b_im.astype(F32))
    c = lax.complex(c_re.astype(F32), c_im.astype(F32))
    L = S5_CHUNK
    taus = jnp.arange(L + 1, dtype=F32)
    pw = jnp.exp(lam_dt[..., None, :] * taus[:, None])
    hi = lax.Precision.HIGHEST
    kern = jnp.einsum('dgpn,dgtn,dgnq->dgtpq', c, pw[:, :, :L], b_bar, precision=hi).real
    kt = kern.transpose(0, 1, 4, 2, 3)
    strip = jnp.concatenate([kt[1][:, :, :0:-1], kt[0][:, :, :1] + kt[1][:, :, :1], kt[0][:, :, 1:]], axis=2)
    strip = strip.reshape(S5_GROUPS, S5_GROUP, (2 * L - 1) * S5_GROUP)
    toep = jnp.stack([strip[:, :, (L - 1 - t) * S5_GROUP:(L - 1 - t) * S5_GROUP + S5_CK] for t in range(L)],
                     axis=1).reshape(S5_GROUPS, S5_CK, S5_CK)
    sf = pw[0][:, ::-1][:, 1:][..., None] * b_bar[0][:, None]
    sb = pw[1][:, :L][..., None] * b_bar[1][:, None]
    def rows(z):
        return z.transpose(0, 1, 3, 2).reshape(S5_GROUPS, S5_CK, S5_STATE)
    s_in = jnp.concatenate([rows(sf.real), rows(sb.real), rows(sf.imag), rows(sb.imag)], axis=-1)
    zf = c[0][:, None] * pw[0][:, 1:][:, :, None, :]
    zb = c[1][:, None] * pw[1][:, ::-1][:, :L][:, :, None, :]
    def cols(z):
        return z.transpose(0, 3, 1, 2).reshape(S5_GROUPS, S5_STATE, S5_CK)
    wc = jnp.concatenate([cols(zf.real), cols(zb.real), -cols(zf.imag), -cols(zb.imag)], axis=1).astype(BF16)
    al = pw[:, :, L]
    coef = jnp.stack([jnp.concatenate([al[0].real, al[1].real], -1),
                      jnp.concatenate([al[0].imag, al[1].imag], -1)], axis=1)
    return toep.astype(BF16), s_in.astype(BF16), wc, coef.astype(F32)


def _s5_norm_body(x_ref, g_ref, o_ref):
    h = _rms(x_ref[...], g_ref[...])
    o_ref[...] = h.reshape(o_ref.shape)


def _s5_norm(x, g, tm=512):
    cr = tm // S5_CHUNK
    return pl.pallas_call(
        _s5_norm_body,
        grid=(BATCH, SEQ // tm),
        in_specs=[pl.BlockSpec((None, tm, D_MODEL), lambda b, i: (b, i, 0)),
                  _resident((1, D_MODEL))],
        out_specs=pl.BlockSpec((cr, None, S5_CHUNK, D_MODEL), lambda b, i: (i, b, 0, 0)),
        out_shape=jax.ShapeDtypeStruct((S5_CHUNKS, BATCH, S5_CHUNK, D_MODEL), F32),
        compiler_params=_params(("parallel", "parallel"), 32 << 20),
        name="s5_norm",
    )(x, g.reshape(1, D_MODEL))


def _block_transpose8(a, lane_block):
    a = list(a)
    for d in (4, 2, 1):
        take_lo = (lane_block & d) == 0
        nxt = list(a)
        for i in range(8):
            if i & d:
                continue
            lo, hi = a[i], a[i + d]
            nxt[i] = jnp.where(take_lo, lo, pltpu.roll(hi, S5_GROUP * d, 1))
            nxt[i + d] = jnp.where(take_lo, pltpu.roll(lo, LANES - S5_GROUP * d, 1), hi)
        a = nxt
    return a


def _gelu_tanh(y):
    return 0.5 * y * (1.0 + jnp.tanh(math.sqrt(2.0 / math.pi) * (y + 0.044715 * (y * y * y))))


S5_RB = 64


def _s5_body(h_ref, wt_ref, ws_ref, wc_ref, coef_ref, d_ref, o_ref, x_sc, y_sc, s_sc, hs_sc):
    lane = lax.broadcasted_iota(jnp.int32, (1, LANES), 1)
    lane_block = lane // S5_GROUP
    fwd_lanes = lane < S5_STATE
    n_rb = S5_ROWS // S5_RB
    tok_rb = S5_RB * S5_CHUNK

    def relayout_in(i, carry):
        base = pl.multiple_of(i * tok_rb, tok_rb)
        rows = pl.multiple_of(i * S5_RB, S5_RB)
        for half in range(2):
            a = [h_ref[pl.ds(base + half * 8 + t, S5_RB, stride=S5_CHUNK), :] for t in range(8)]
            xt = _block_transpose8(a, lane_block)
            for gi in range(S5_TILE_GROUPS):
                x_sc[gi, pl.ds(rows, S5_RB), half * LANES:(half + 1) * LANES] = xt[gi].astype(BF16)
        return carry

    lax.fori_loop(0, n_rb, relayout_in, 0)

    for gi in range(S5_TILE_GROUPS):
        xg = x_sc[gi]
        y_sc[gi] = jnp.dot(xg, wt_ref[gi], preferred_element_type=F32)
        z = jnp.dot(xg, ws_ref[gi], preferred_element_type=F32)
        s_sc[0] = z[:, :LANES]
        s_sc[1] = z[:, LANES:]
        a_re = coef_ref[gi, 0:1, :]
        a_im = coef_ref[gi, 1:2, :]

        def scan_step(k, st):
            st_re, st_im = st
            rf = pl.multiple_of(k * BATCH, BATCH)
            rb = pl.multiple_of((S5_CHUNKS - 1 - k) * BATCH, BATCH)
            hs_sc[0, pl.ds(rf, BATCH), :S5_STATE] = st_re[:, :S5_STATE]
            hs_sc[1, pl.ds(rf, BATCH), :S5_STATE] = st_im[:, :S5_STATE]
            hs_sc[0, pl.ds(rb, BATCH), S5_STATE:] = st_re[:, S5_STATE:]
            hs_sc[1, pl.ds(rb, BATCH), S5_STATE:] = st_im[:, S5_STATE:]
            s_re = jnp.where(fwd_lanes, s_sc[0, pl.ds(rf, BATCH), :], s_sc[0, pl.ds(rb, BATCH), :])
            s_im = jnp.where(fwd_lanes, s_sc[1, pl.ds(rf, BATCH), :], s_sc[1, pl.ds(rb, BATCH), :])
            n_re = a_re * st_re - a_im * st_im + s_re
            n_im = a_re * st_im + a_im * st_re + s_im
            return n_re, n_im

        zero = jnp.zeros((BATCH, LANES), F32)
        lax.fori_loop(0, S5_CHUNKS, scan_step, (zero, zero))
        hcat = jnp.concatenate([hs_sc[0], hs_sc[1]], axis=1).astype(BF16)
        y_sc[gi] = y_sc[gi] + jnp.dot(hcat, wc_ref[gi], preferred_element_type=F32)

    d_skip = d_ref[...]

    def relayout_out(i, carry):
        base = pl.multiple_of(i * tok_rb, tok_rb)
        rows = pl.multiple_of(i * S5_RB, S5_RB)
        for half in range(2):
            yg = [y_sc[gi, pl.ds(rows, S5_RB), half * LANES:(half + 1) * LANES] for gi in range(S5_TILE_GROUPS)]
            yt = _block_transpose8(yg, lane_block)
            for t in range(8):
                tok = pl.ds(base + half * 8 + t, S5_RB, stride=S5_CHUNK)
                o_ref[tok, :] = _gelu_tanh(yt[t] + d_skip * h_ref[tok, :])
        return carry

    lax.fori_loop(0, n_rb, relayout_out, 0)


def _s5_core(h2, toep, s_in, wc, coef, d_skip):
    n_tiles = S5_GROUPS // S5_TILE_GROUPS
    tok_block = pl.BlockSpec((N_TOKENS, LANES), lambda j: (0, j), pipeline_mode=pl.Buffered(1))
    vmem = (2 * N_TOKENS * LANES * 4 + S5_TILE_GROUPS * S5_ROWS * S5_CK * (2 + 4)
            + 4 * S5_ROWS * LANES * 4 + 2 * S5_TILE_GROUPS * S5_CK * 3 * S5_CK * 2 + (12 << 20))
    return pl.pallas_call(
        _s5_body,
        grid=(n_tiles,),
        in_specs=[tok_block,
                  pl.BlockSpec((S5_TILE_GROUPS, S5_CK, S5_CK), lambda j: (j, 0, 0)),
                  pl.BlockSpec((S5_TILE_GROUPS, S5_CK, S5_CK), lambda j: (j, 0, 0)),
                  pl.BlockSpec((S5_TILE_GROUPS, S5_CK, S5_CK), lambda j: (j, 0, 0)),
                  pl.BlockSpec((S5_TILE_GROUPS, 2, LANES), lambda j: (j, 0, 0)),
                  pl.BlockSpec((1, LANES), lambda j: (0, j))],
        out_specs=tok_block,
        out_shape=jax.ShapeDtypeStruct((N_TOKENS, D_MODEL), F32),
        scratch_shapes=[pltpu.VMEM((S5_TILE_GROUPS, S5_ROWS, S5_CK), BF16),
                        pltpu.VMEM((S5_TILE_GROUPS, S5_ROWS, S5_CK), F32),
                        pltpu.VMEM((2, S5_ROWS, LANES), F32),
                        pltpu.VMEM((2, S5_ROWS, LANES), F32)],
        compiler_params=_params(("arbitrary",), vmem),
        name="s5_core",
    )(h2, toep, s_in, wc, coef, d_skip.reshape(1, D_MODEL))


def _s5_glu_body(x_ref, a_ref, w_ref, o_ref):
    a = a_ref[...].reshape(x_ref.shape).astype(BF16)
    vg = jnp.dot(a, w_ref[...], preferred_element_type=F32)
    o_ref[...] = x_ref[...] + vg[:, :D_MODEL] * _sigmoid(vg[:, D_MODEL:])


def _s5_glu(x, act, w_glu_all, layer, tm=512):
    cr = tm // S5_CHUNK
    vmem = 2 * D_MODEL * D_MODEL * 2 + 6 * tm * D_MODEL * 4 + 2 * tm * 2 * D_MODEL * 4 + (4 << 20)
    return pl.pallas_call(
        _s5_glu_body,
        grid=(BATCH, SEQ // tm),
        in_specs=[pl.BlockSpec((None, tm, D_MODEL), lambda b, i: (b, i, 0)),
                  pl.BlockSpec((cr, None, S5_CHUNK, D_MODEL), lambda b, i: (i, b, 0, 0)),
                  _layer_block((D_MODEL, 2 * D_MODEL), layer)],
        out_specs=pl.BlockSpec((None, tm, D_MODEL), lambda b, i: (b, i, 0)),
        out_shape=jax.ShapeDtypeStruct((BATCH, SEQ, D_MODEL), F32),
        compiler_params=_params(("parallel", "parallel"), vmem),
        name="s5_glu",
    )(x, act, w_glu_all)


def _s5_layer(x, g, lam_re, lam_im, log_dt, b_re, b_im, c_re, c_im, d_skip, w_glu_all, layer):
    toep, s_in, wc, coef = _s5_matrices(lam_re, lam_im, log_dt, b_re, b_im, c_re, c_im)
    h = _s5_norm(x, g)
    act = _s5_core(h.reshape(N_TOKENS, D_MODEL), toep, s_in, wc, coef, d_skip)
    act = act.reshape(S5_CHUNKS, BATCH, S5_CHUNK, D_MODEL)
    return _s5_glu(x, act, w_glu_all, layer)


def _t5_bucket(rel):
    half = NUM_BUCKETS // 2
    max_exact = half // 2
    n = np.abs(rel)
    sign = (rel > 0).astype(np.int32) * half
    large = max_exact + (np.log(np.maximum(n, 1) / max_exact) / math.log(MAX_DISTANCE / max_exact)
                         * (half - max_exact)).astype(np.int32)
    large = np.minimum(large, half - 1)
    return (sign + np.where(n < max_exact, n, large)).astype(np.int32)


LOG2E = 1.4426950408889634
N_PAIRS = HEADS_PER_GROUP // 2
ATTN_PAD_ROWS = SEQ + ATTN_BQ * max(d for _, d in DILATED_GROUPS)
QKV_GROUP_WIDTH = 3 * D_MODEL
QKV_SLAB = 2 * LANES
QKV_TILE_ROWS = 512


def _attn_bias_tables(rel_bias):
    n_off = ATTN_BK + ATTN_BQ - 1
    offs = np.arange(n_off) - (ATTN_BQ - 1) - ATTN_SIDE
    k = np.arange(ATTN_BK)[:, None]
    q = np.arange(ATTN_BQ)[None, :]
    band = np.abs(k - ATTN_SIDE - q) <= ATTN_SIDE
    tabs = []
    for gi, (_, dil) in enumerate(DILATED_GROUPS):
        onehot = jnp.asarray(_t5_bucket(offs * dil)[:, None] == np.arange(NUM_BUCKETS), F32)
        f = jnp.dot(onehot, rel_bias[:, gi * HEADS_PER_GROUP:(gi + 1) * HEADS_PER_GROUP].astype(F32),
                    precision=lax.Precision.HIGHEST).T
        fr = jnp.pad(f[:, ::-1], ((0, 0), (0, 1)))
        skew = jnp.tile(fr, (1, ATTN_BK))[:, ATTN_BK - 1:ATTN_BK - 1 + ATTN_BK * n_off]
        t = skew.reshape(HEADS_PER_GROUP, ATTN_BK, n_off)[:, :, :ATTN_BQ]
        t = jnp.where(band[None], t * LOG2E, -1e30)
        t = t.reshape(N_PAIRS, 2, ATTN_BK, ATTN_BQ).transpose(0, 2, 1, 3).reshape(N_PAIRS, ATTN_BK, 2 * ATTN_BQ)
        tabs.append(t)
    return jnp.stack(tabs)


def _qkv_body(x_ref, g_ref, w_ref, qg_ref, kg_ref, perm_ref, q_out, k_out, v_out, *, n_res):
    rows = x_ref.shape[0] // n_res
    h = _rms(x_ref[...], g_ref[...]).astype(BF16)
    if n_res > 1:
        h = jnp.dot(perm_ref[...], h, preferred_element_type=F32).astype(BF16)
    lane = lax.broadcasted_iota(jnp.int32, (1, LANES), 1)
    head0 = lane < HEAD_DIM

    def head_norm(t, gain):
        sq = t * t
        s0 = jnp.sum(jnp.where(head0, sq, 0.0), axis=-1, keepdims=True)
        s1 = jnp.sum(jnp.where(head0, 0.0, sq), axis=-1, keepdims=True)
        ms = jnp.where(head0, s0, s1) * (1.0 / HEAD_DIM)
        return t * lax.rsqrt(ms + EPS) * gain

    for c in range(QKV_GROUP_WIDTH // QKV_SLAB):
        z = jnp.dot(h, w_ref[:, c * QKV_SLAB:(c + 1) * QKV_SLAB], preferred_element_type=F32)
        for half in range(QKV_SLAB // LANES):
            section, lo = divmod(c * QKV_SLAB + half * LANES, D_MODEL)
            zz = z[:, half * LANES:(half + 1) * LANES]
            if section == 0:
                zz = head_norm(zz, qg_ref[...]) * (HEAD_DIM ** -0.5 * LOG2E)
                qa = jnp.where(head0, zz, 0.0).astype(BF16)
                qb = jnp.where(head0, 0.0, zz).astype(BF16)
            elif section == 1:
                zz = head_norm(zz, kg_ref[...]).astype(BF16)
            else:
                zz = zz.astype(BF16)
            for r in range(n_res):
                piece = slice(r * rows, (r + 1) * rows)
                if section == 0:
                    q_out[r, 0, :, lo:lo + LANES] = qa[piece]
                    q_out[r, 1, :, lo:lo + LANES] = qb[piece]
                elif section == 1:
                    k_out[r, :, lo:lo + LANES] = zz[piece]
                else:
                    v_out[r, :, lo:lo + LANES] = zz[piece]


def _qkv_group(x, g, w_qkv_all, layer, gi, dil, q_gain, k_gain):
    seg = SEQ // dil
    tm = QKV_TILE_ROWS
    rows = tm // dil
    assert tm % dil == 0 and rows % 16 == 0
    gain2 = lambda gn: jnp.tile(gn.astype(F32), 2).reshape(1, LANES)
    src = np.arange(tm).reshape(rows, dil).T.reshape(tm)
    perm = jnp.asarray(src[:, None] == np.arange(tm)[None, :], BF16)
    kv_spec = pl.BlockSpec((None, dil, rows, D_MODEL), lambda b, i: (b, 0, i, 0))
    kv_shape = jax.ShapeDtypeStruct((BATCH, dil, seg, D_MODEL), BF16)
    vmem = (D_MODEL * QKV_GROUP_WIDTH * 2 + 2 * tm * D_MODEL * 4 + 2 * 4 * tm * D_MODEL * 2 + (12 << 20))
    return pl.pallas_call(
        functools.partial(_qkv_body, n_res=dil),
        grid=(BATCH, SEQ // tm),
        in_specs=[pl.BlockSpec((None, tm, D_MODEL), lambda b, i: (b, i, 0)),
                  _resident((1, D_MODEL)),
                  pl.BlockSpec((None, D_MODEL, QKV_GROUP_WIDTH), lambda b, i: (layer, 0, gi),
                               pipeline_mode=pl.Buffered(1)),
                  _resident((1, LANES)), _resident((1, LANES)), _resident((tm, tm))],
        out_specs=[pl.BlockSpec((None, dil, 2, rows, D_MODEL), lambda b, i: (b, 0, 0, i, 0)),
                   kv_spec, kv_spec],
        out_shape=[jax.ShapeDtypeStruct((BATCH, dil, 2, seg, D_MODEL), BF16), kv_shape, kv_shape],
        compiler_params=_params(("parallel", "parallel"), vmem),
        name=f"qkv_dil{dil}",
    )(x, g.reshape(1, D_MODEL), w_qkv_all, gain2(q_gain), gain2(k_gain), perm)


def _attn_body(q0, k0, v0, q1, k1, v1, q2, k2, v2, bias_ref, o_ref, kp_sc, vp_sc, lt_sc, og_sc, lg_sc):
    lane = lax.broadcasted_iota(jnp.int32, (1, LANES), 1)
    head0 = lane < HEAD_DIM
    krow = lax.broadcasted_iota(jnp.int32, (ATTN_BK, 1), 0)
    zpad = jnp.zeros((ATTN_SIDE, LANES), BF16)

    qkv_refs = ((q0, k0, v0), (q1, k1, v1), (q2, k2, v2))
    for g, (_, dil) in enumerate(DILATED_GROUPS):
        q_ref, k_ref, v_ref = qkv_refs[g]
        seg = SEQ // dil
        nb = seg // ATTN_BQ
        pad_seg = seg + 2 * ATTN_SIDE
        lt_sc[...] = jnp.zeros(lt_sc.shape, F32)
        for r in range(dil):
            base = r * pad_seg
            for src, dst in ((k_ref, kp_sc), (v_ref, vp_sc)):
                dst[base:base + ATTN_SIDE] = zpad
                dst[base + ATTN_SIDE + seg:base + pad_seg] = zpad
                dst[base + ATTN_SIDE:base + ATTN_SIDE + seg] = src[r]

        for r in range(dil):
            base = r * pad_seg
            for i in range(nb):
                blk = r * nb + i
                win = slice(base + i * ATTN_BQ, base + i * ATTN_BQ + ATTN_BK)
                qrows = slice(i * ATTN_BQ, (i + 1) * ATTN_BQ)
                qm = jnp.concatenate([q_ref[r, 0, qrows, :], q_ref[r, 1, qrows, :]], axis=0)
                s = lax.dot_general(kp_sc[win], qm, (((1,), (1,)), ((), ())), preferred_element_type=F32)
                s = s + bias_ref[g]
                if i == 0 or i == nb - 1:
                    ok = None
                    if i == 0:
                        ok = krow >= ATTN_SIDE
                    if i == nb - 1:
                        ok_hi = krow < ATTN_BK - ATTN_SIDE
                        ok = ok_hi if ok is None else jnp.logical_and(ok, ok_hi)
                    s = jnp.where(ok, s, -1e30)
                vb = vp_sc[win]
                o_heads, lse_heads = [], []
                for a in range(2):
                    sa = s[:, a * ATTN_BQ:(a + 1) * ATTN_BQ]
                    m = jnp.max(sa, axis=0, keepdims=True)
                    p = jnp.exp2(sa - m)
                    den = jnp.sum(p, axis=0, keepdims=True)
                    pt = (p * (1.0 / den)).T.astype(BF16)
                    o_heads.append(jnp.dot(pt, vb, preferred_element_type=F32))
                    lse_heads.append(m + jnp.log2(den))
                rows = pl.ds(r + i * ATTN_BQ * dil, ATTN_BQ, stride=dil)
                og_sc[g, rows, :] = jnp.where(head0, o_heads[0], o_heads[1])
                lt_sc[8 * blk:8 * blk + 1, :] = lse_heads[0]
                lt_sc[8 * blk + 1:8 * blk + 2, :] = lse_heads[1]

        ltt = lt_sc[...].T
        for r in range(dil):
            for i in range(nb):
                c = 8 * (r * nb + i)
                rows = pl.ds(r + i * ATTN_BQ * dil, ATTN_BQ, stride=dil)
                lg_sc[g, rows, :] = jnp.where(head0, ltt[:, c:c + 1], ltt[:, c + 1:c + 2])

    l0, l1, l2 = lg_sc[0], lg_sc[1], lg_sc[2]
    m = jnp.maximum(jnp.maximum(l0, l1), l2)
    w0, w1, w2 = jnp.exp2(l0 - m), jnp.exp2(l1 - m), jnp.exp2(l2 - m)
    o = (w0 * og_sc[0] + w1 * og_sc[1] + w2 * og_sc[2]) / (w0 + w1 + w2)
    o_ref[...] = o.astype(o_ref.dtype)


def _attn_core(qkv, bias_tabs):
    in_specs = []
    for (_, dil) in DILATED_GROUPS:
        seg = SEQ // dil
        in_specs.append(pl.BlockSpec((None, dil, 2, seg, LANES), lambda b, p: (b, 0, 0, 0, p)))
        in_specs.append(pl.BlockSpec((None, dil, seg, LANES), lambda b, p: (b, 0, 0, p)))
        in_specs.append(pl.BlockSpec((None, dil, seg, LANES), lambda b, p: (b, 0, 0, p)))
    in_specs.append(pl.BlockSpec((N_ATTN_GROUPS, None, ATTN_BK, 2 * ATTN_BQ), lambda b, p: (0, p, 0, 0)))
    return pl.pallas_call(
        _attn_body,
        grid=(BATCH, N_PAIRS),
        in_specs=in_specs,
        out_specs=pl.BlockSpec((None, SEQ, LANES), lambda b, p: (b, 0, p)),
        out_shape=jax.ShapeDtypeStruct((BATCH, SEQ, D_MODEL), BF16),
        scratch_shapes=[pltpu.VMEM((ATTN_PAD_ROWS, LANES), BF16),
                        pltpu.VMEM((ATTN_PAD_ROWS, LANES), BF16),
                        pltpu.VMEM((LANES, LANES), F32),
                        pltpu.VMEM((N_ATTN_GROUPS, SEQ, LANES), F32),
                        pltpu.VMEM((N_ATTN_GROUPS, SEQ, LANES), F32)],
        compiler_params=_params(("parallel", "parallel"), 40 << 20),
        name="attn_core",
    )(*qkv, bias_tabs)


def _proj_residual_body(x_ref, a_ref, w_ref, o_ref):
    o_ref[...] = x_ref[...] + jnp.dot(a_ref[...], w_ref[...], preferred_element_type=F32)


def _proj_residual(x2, a2, w_all, layer, tm=512):
    tile = pl.BlockSpec((tm, D_MODEL), lambda i: (i, 0))
    return pl.pallas_call(
        _proj_residual_body,
        grid=(x2.shape[0] // tm,),
        in_specs=[tile, tile, _layer_block((D_MODEL, D_MODEL), layer)],
        out_specs=tile,
        out_shape=jax.ShapeDtypeStruct(x2.shape, F32),
        compiler_params=_params(("parallel",), 32 << 20),
        name="proj_residual",
    )(x2, a2, w_all)


def _attn_layer(x, g, w_qkv_all, q_gain, k_gain, w_o_all, rel_bias, layer):
    qkv = []
    for gi, (window, dil) in enumerate(DILATED_GROUPS):
        assert (window // 2) // dil == ATTN_SIDE and (SEQ // dil) % ATTN_BQ == 0
        qkv.extend(_qkv_group(x, g, w_qkv_all, layer, gi, dil, q_gain[gi], k_gain[gi]))
    o = _attn_core(qkv, _attn_bias_tables(rel_bias))
    x2 = x.reshape(N_TOKENS, D_MODEL)
    return _proj_residual(x2, o.reshape(N_TOKENS, D_MODEL), w_o_all, layer).reshape(BATCH, SEQ, D_MODEL)


def kernel(x, norm_mix_g, norm_ffn_g, fnet_w_out, s5_lambda_re, s5_lambda_im, s5_log_dt, s5_b_re, s5_b_im, s5_c_re, s5_c_im, s5_d, s5_w_glu, attn_w_qkv, attn_q_gain, attn_k_gain, attn_w_o, rel_bias, ffn_w_gate_up, ffn_w_down):
    w_gate_up, w_down = _cast_bf16(ffn_w_gate_up), _cast_bf16(ffn_w_down)
    w_glu, w_qkv, w_o = _cast_bf16(s5_w_glu), _cast_bf16(attn_w_qkv), _cast_bf16(attn_w_o)
    counts = [0, 0, 0]
    for i in range(DEPTH):
        kind = i % 3
        j = counts[kind]
        counts[kind] += 1
        if kind == 0:
            x = _fnet_layer(x, norm_mix_g[i], fnet_w_out[j])
        elif kind == 1:
            x = _s5_layer(x, norm_mix_g[i], s5_lambda_re[j], s5_lambda_im[j], s5_log_dt[j], s5_b_re[j],
                          s5_b_im[j], s5_c_re[j], s5_c_im[j], s5_d[j], w_glu, j)
        else:
            x = _attn_layer(x, norm_mix_g[i], w_qkv, attn_q_gain[j], attn_k_gain[j], w_o, rel_bias, j)
        x = _ffn(x.reshape(N_TOKENS, D_MODEL), norm_ffn_g[i], w_gate_up, w_down, i).reshape(BATCH, SEQ, D_MODEL)
    return x
```

```python
import functools
import math

import numpy as np
import jax
import jax.numpy as jnp
from jax import lax
from jax.experimental import pallas as pl
from jax.experimental.pallas import tpu as pltpu

F32 = jnp.float32
BF16 = jnp.bfloat16

D_MODEL = 1024
BATCH = 8
SEQ = 2048
DEPTH = 4
N_TOKENS = BATCH * SEQ
EPS = 1e-6
D_FF = 2816
FOURIER_GROUP = 128
S5_GROUP = 16
S5_GROUPS = 64
S5_STATE = 64
HEAD_DIM = 64
HEADS_PER_GROUP = 16
DILATED_GROUPS = ((128, 1), (512, 4), (2048, 16))
N_ATTN_GROUPS = 3
NUM_BUCKETS = 32
MAX_DISTANCE = 1024
ATTN_SIDE = 64

LANES = 128
VMEM_LIMIT_CAP = 60 * 1024 * 1024

S5_CHUNK = 16
S5_CHUNKS = SEQ // S5_CHUNK
S5_ROWS = BATCH * S5_CHUNKS
S5_TILE_GROUPS = LANES // S5_GROUP
S5_CK = S5_CHUNK * S5_GROUP

ATTN_BQ = 128
ATTN_BK = ATTN_BQ + 2 * ATTN_SIDE


def _params(sem, vmem_bytes):
    return pltpu.CompilerParams(dimension_semantics=sem,
                                vmem_limit_bytes=int(min(VMEM_LIMIT_CAP, vmem_bytes)))


def _rms(x, g):
    ms = jnp.mean(x * x, axis=-1, keepdims=True)
    return x * lax.rsqrt(ms + EPS) * g


def _sigmoid(x):
    return 1.0 / (1.0 + jnp.exp(-x))


def _resident(shape):
    nd = len(shape)
    return pl.BlockSpec(shape, lambda *_: (0,) * nd, pipeline_mode=pl.Buffered(1))


def _cast_body(w_ref, o_ref):
    o_ref[...] = w_ref[...].astype(BF16)


def _cast_bf16(w, rows=256):
    n_layers, n_rows, n_cols = w.shape
    block = pl.BlockSpec((None, rows, n_cols), lambda l, i: (l, i, 0))
    return pl.pallas_call(
        _cast_body,
        grid=(n_layers, n_rows // rows),
        in_specs=[block],
        out_specs=block,
        out_shape=jax.ShapeDtypeStruct(w.shape, BF16),
        compiler_params=_params(("parallel", "parallel"), 4 * rows * n_cols * 6 + (4 << 20)),
        name="cast_bf16",
    )(w)


def _ffn_body(x_ref, g_ref, wgu_ref, wd_ref, o_ref):
    x = x_ref[...]
    h = _rms(x, g_ref[...]).astype(BF16)
    gu = jnp.dot(h, wgu_ref[...], preferred_element_type=F32)
    gate = gu[:, :D_FF]
    up = gu[:, D_FF:]
    a = (gate * _sigmoid(gate) * up).astype(BF16)
    o_ref[...] = x + jnp.dot(a, wd_ref[...], preferred_element_type=F32)


def _layer_block(shape, layer):
    return pl.BlockSpec((None,) + shape, lambda *_: (layer,) + (0,) * len(shape), pipeline_mode=pl.Buffered(1))


def _ffn(x2, g, wgu_all, wd_all, layer, tm=256):
    m = x2.shape[0]
    vmem = 3 * D_MODEL * D_FF * 2 + 4 * tm * D_MODEL * 4 + 4 * tm * 2 * D_FF * 4 + (4 << 20)
    return pl.pallas_call(
        _ffn_body,
        grid=(m // tm,),
        in_specs=[pl.BlockSpec((tm, D_MODEL), lambda i: (i, 0)),
                  _resident((1, D_MODEL)),
                  _layer_block((D_MODEL, 2 * D_FF), layer),
                  _layer_block((D_FF, D_MODEL), layer)],
        out_specs=pl.BlockSpec((tm, D_MODEL), lambda i: (i, 0)),
        out_shape=jax.ShapeDtypeStruct((m, D_MODEL), F32),
        compiler_params=_params(("parallel",), vmem),
        name="ffn",
    )(x2, g.reshape(1, D_MODEL), wgu_all, wd_all)


def _fnet_weight_body(cc_ref, sc_ref, w_ref, o_ref):
    w = w_ref[...]
    o_ref[:, :D_MODEL] = jnp.dot(cc_ref[...], w, preferred_element_type=F32,
                                 precision=lax.Precision.HIGHEST).astype(BF16)
    o_ref[:, D_MODEL:] = jnp.dot(sc_ref[...], w, preferred_element_type=F32,
                                 precision=lax.Precision.HIGHEST).astype(BF16)


def _fnet_weights(w_out):
    n = np.arange(FOURIER_GROUP)
    ang = 2.0 * np.pi * ((n[:, None] * n[None, :]) % FOURIER_GROUP) / FOURIER_GROUP
    cc = jnp.asarray(np.cos(ang) / math.sqrt(FOURIER_GROUP), F32)
    sc = jnp.asarray(np.sin(ang) / math.sqrt(FOURIER_GROUP), F32)
    ng = D_MODEL // FOURIER_GROUP
    return pl.pallas_call(
        _fnet_weight_body,
        grid=(ng,),
        in_specs=[_resident((FOURIER_GROUP, FOURIER_GROUP)),
                  _resident((FOURIER_GROUP, FOURIER_GROUP)),
                  pl.BlockSpec((FOURIER_GROUP, D_MODEL), lambda i: (i, 0))],
        out_specs=pl.BlockSpec((FOURIER_GROUP, 2 * D_MODEL), lambda i: (i, 0)),
        out_shape=jax.ShapeDtypeStruct((D_MODEL, 2 * D_MODEL), BF16),
        compiler_params=_params(("parallel",), 16 << 20),
        name="fnet_weights",
    )(cc, sc, w_out)


FN_HALF = SEQ // 2
FN_BLK = 256
FN_NB = FN_HALF // FN_BLK


def _fnet_tables():
    k = np.arange(FN_HALF)[:, None]
    n = np.arange(FN_HALF)[None, :]
    ang = 2.0 * np.pi * ((k * n) % SEQ) / SEQ
    scale = 1.0 / math.sqrt(SEQ)
    i = np.arange(FN_BLK)
    rev = (i[None, :] == FN_BLK - i[:, None]).astype(np.float32)
    return (jnp.asarray(np.cos(ang) * scale, BF16), jnp.asarray(np.sin(ang) * scale, BF16),
            jnp.asarray(rev, BF16))


def _fnet_body(x_ref, g_ref, cs_ref, ss_ref, w_ref, rev_ref, o_ref, e_sc, d_sc, zc_sc, zs_sc):
    g = g_ref[...]
    scale = 1.0 / math.sqrt(SEQ)
    row = lax.broadcasted_iota(jnp.int32, (FN_BLK, 1), 0)
    first = row == 0
    sign = jnp.where((row & 1) == 0, 1.0, -1.0)
    rev = rev_ref[...]
    blk = lambda a: pl.ds(a * FN_BLK, FN_BLK)
    mirror = lambda a: pl.ds(SEQ - (a + 1) * FN_BLK, FN_BLK)

    alt = jnp.zeros((1, D_MODEL), F32)
    carry = jnp.zeros((1, D_MODEL), F32)
    for a in range(FN_NB):
        lo = _rms(x_ref[blk(a), :], g)
        hi = _rms(x_ref[mirror(a), :], g)
        alt = alt + jnp.sum(sign * lo, axis=0, keepdims=True) + jnp.sum(sign * hi, axis=0, keepdims=True)
        r = jnp.dot(rev, hi.astype(BF16), preferred_element_type=F32)
        r = jnp.where(first, carry, r)
        e_sc[blk(a), :] = (lo + r).astype(BF16)
        d_sc[blk(a), :] = (lo - r).astype(BF16)
        carry = hi[0:1, :]
    h_nyq = carry * scale

    for c in range(FN_NB):
        zc = jnp.dot(cs_ref[blk(c), :], e_sc[...], preferred_element_type=F32) + sign * h_nyq
        zs = jnp.dot(ss_ref[blk(c), :], d_sc[...], preferred_element_type=F32)
        zc_sc[blk(c), :] = zc.astype(BF16)
        zs_sc[blk(c), :] = zs.astype(BF16)

    wc = w_ref[:, :D_MODEL]
    ws = w_ref[:, D_MODEL:]
    z_nyq = jnp.broadcast_to(alt * scale, (8, D_MODEL)).astype(BF16)
    carry = jnp.dot(z_nyq, wc, preferred_element_type=F32)[0:1, :]
    for c in reversed(range(FN_NB)):
        p = jnp.dot(zc_sc[blk(c), :], wc, preferred_element_type=F32)
        q = jnp.dot(zs_sc[blk(c), :], ws, preferred_element_type=F32)
        o_ref[blk(c), :] = x_ref[blk(c), :] + (p - q)
        m = p + q
        m_hi = m.astype(BF16)
        m_lo = (m - m_hi.astype(F32)).astype(BF16)
        r = (jnp.dot(rev, m_hi, preferred_element_type=F32) + jnp.dot(rev, m_lo, preferred_element_type=F32))
        r = jnp.where(first, carry, r)
        o_ref[mirror(c), :] = x_ref[mirror(c), :] + r
        carry = m[0:1, :]


def _fnet_layer(x, g, w_out):
    wcs = _fnet_weights(w_out)
    cs, ss, rev = _fnet_tables()
    seq_block = pl.BlockSpec((None, SEQ, D_MODEL), lambda b: (b, 0, 0))
    half = pltpu.VMEM((FN_HALF, D_MODEL), BF16)
    vmem = 4 * SEQ * D_MODEL * 4 + 4 * FN_HALF * D_MODEL * 2 + 4 * FN_HALF * D_MODEL * 2 + (12 << 20)
    return pl.pallas_call(
        _fnet_body,
        grid=(BATCH,),
        in_specs=[seq_block, _resident((1, D_MODEL)), _resident((FN_HALF, FN_HALF)),
                  _resident((FN_HALF, FN_HALF)), _resident((D_MODEL, 2 * D_MODEL)),
                  _resident((FN_BLK, FN_BLK))],
        out_specs=seq_block,
        out_shape=jax.ShapeDtypeStruct((BATCH, SEQ, D_MODEL), F32),
        scratch_shapes=[half, half, half, half],
        compiler_params=_params(("parallel",), vmem),
        name="fnet_mix",
    )(x, g.reshape(1, D_MODEL), cs, ss, wcs, rev)


def _s5_matrices(lam_re, lam_im, log_dt, b_re, b_im, c_re, c_im):
    lam = lax.complex(lam_re.astype(F32), lam_im.astype(F32))
    dt = jnp.exp(log_dt.astype(F32))[..., None]
    lam_dt = lam * dt
    lam_bar = jnp.exp(lam_dt)
    b_bar = ((lam_bar - 1.0) / lam)[..., None] * lax.complex(b_re.astype(F32), b_im.astype(F32))
    c = lax.complex(c_re.astype(F32), c_im.astype(F32))
    L = S5_CHUNK
    taus = jnp.arange(L + 1, dtype=F32)
    pw = jnp.exp(lam_dt[..., None, :] * taus[:, None])
    hi = lax.Precision.HIGHEST
    kern = jnp.einsum('dgpn,dgtn,dgnq->dgtpq', c, pw[:, :, :L], b_bar, precision=hi).real
    kt = kern.transpose(0, 1, 4, 2, 3)
    strip = jnp.concatenate([kt[1][:, :, :0:-1], kt[0][:, :, :1] + kt[1][:, :, :1], kt[0][:, :, 1:]], axis=2)
    strip = strip.reshape(S5_GROUPS, S5_GROUP, (2 * L - 1) * S5_GROUP)
    strip = jnp.pad(strip, ((0, 0), (0, 0), (0, S5_GROUP)))
    b_t = b_bar.transpose(0, 1, 3, 2)
    sf = pw[0][:, ::-1][:, 1:][:, :, None, :] * b_t[0][:, None]
    sb = pw[1][:, :L][:, :, None, :] * b_t[1][:, None]
    s_in = jnp.concatenate([sf.real, sb.real, sf.imag, sb.imag], axis=-1).reshape(S5_GROUPS, S5_CK, S5_CK)
    zf = c[0][:, None] * pw[0][:, 1:][:, :, None, :]
    zb = c[1][:, None] * pw[1][:, ::-1][:, :L][:, :, None, :]
    wc_t = jnp.concatenate([zf.real, zb.real, -zf.imag, -zb.imag], axis=-1).reshape(S5_GROUPS, S5_CK, S5_CK)
    al = pw[:, :, L]
    coef = jnp.stack([jnp.concatenate([al[0].real, al[1].real], -1),
                      jnp.concatenate([al[0].imag, al[1].imag], -1)], axis=1)
    return strip.astype(F32), s_in.astype(BF16), wc_t.astype(BF16), coef.astype(F32)


def _s5_norm_body(x_ref, g_ref, o_ref):
    h = _rms(x_ref[...], g_ref[...])
    o_ref[...] = h.reshape(o_ref.shape)


def _s5_norm(x, g, tm=512):
    cr = tm // S5_CHUNK
    return pl.pallas_call(
        _s5_norm_body,
        grid=(BATCH, SEQ // tm),
        in_specs=[pl.BlockSpec((None, tm, D_MODEL), lambda b, i: (b, i, 0)),
                  _resident((1, D_MODEL))],
        out_specs=pl.BlockSpec((cr, None, S5_CHUNK, D_MODEL), lambda b, i: (i, b, 0, 0)),
        out_shape=jax.ShapeDtypeStruct((S5_CHUNKS, BATCH, S5_CHUNK, D_MODEL), F32),
        compiler_params=_params(("parallel", "parallel"), 32 << 20),
        name="s5_norm",
    )(x, g.reshape(1, D_MODEL))


def _block_transpose8(a, lane_block):
    a = list(a)
    for d in (4, 2, 1):
        take_lo = (lane_block & d) == 0
        nxt = list(a)
        for i in range(8):
            if i & d:
                continue
            lo, hi = a[i], a[i + d]
            nxt[i] = jnp.where(take_lo, lo, pltpu.roll(hi, S5_GROUP * d, 1))
            nxt[i + d] = jnp.where(take_lo, pltpu.roll(lo, LANES - S5_GROUP * d, 1), hi)
        a = nxt
    return a


def _gelu_tanh(y):
    return 0.5 * y * (1.0 + jnp.tanh(math.sqrt(2.0 / math.pi) * (y + 0.044715 * (y * y * y))))


S5_RB = 64


def _s5_body(h_ref, strip_ref, ws_ref, wc_ref, coef_ref, d_ref, o_ref, x_sc, y_sc, s_sc, hs_sc, toep_sc):
    lane = lax.broadcasted_iota(jnp.int32, (1, LANES), 1)
    lane_block = lane // S5_GROUP
    fwd_lanes = lane < S5_STATE
    n_rb = S5_ROWS // S5_RB
    tok_rb = S5_RB * S5_CHUNK

    def relayout_in(i, carry):
        base = pl.multiple_of(i * tok_rb, tok_rb)
        rows = pl.multiple_of(i * S5_RB, S5_RB)
        for half in range(2):
            a = [h_ref[pl.ds(base + half * 8 + t, S5_RB, stride=S5_CHUNK), :] for t in range(8)]
            xt = _block_transpose8(a, lane_block)
            for gi in range(S5_TILE_GROUPS):
                x_sc[gi, pl.ds(rows, S5_RB), half * LANES:(half + 1) * LANES] = xt[gi].astype(BF16)
        return carry

    lax.fori_loop(0, n_rb, relayout_in, 0)

    nt = (((1,), (1,)), ((), ()))
    for gi in range(S5_TILE_GROUPS):
        strip = strip_ref[gi]
        for t in range(S5_CHUNK):
            off = (S5_CHUNK - 1 - t) * S5_GROUP
            win = strip if off == 0 else pltpu.roll(strip, strip.shape[1] - off, 1)
            toep_sc[t * S5_GROUP:(t + 1) * S5_GROUP, :] = win[:, :S5_CK].astype(BF16)
        xg = x_sc[gi]
        y_sc[gi] = jnp.dot(xg, toep_sc[...], preferred_element_type=F32)
        z = jnp.dot(xg, ws_ref[gi], preferred_element_type=F32)
        s_sc[0] = z[:, :LANES]
        s_sc[1] = z[:, LANES:]
        a_re = coef_ref[gi, 0:1, :]
        a_im = coef_ref[gi, 1:2, :]

        def scan_step(k, st):
            st_re, st_im = st
            rf = pl.multiple_of(k * BATCH, BATCH)
            rb = pl.multiple_of((S5_CHUNKS - 1 - k) * BATCH, BATCH)
            hs_sc[0, pl.ds(rf, BATCH), :S5_STATE] = st_re[:, :S5_STATE]
            hs_sc[1, pl.ds(rf, BATCH), :S5_STATE] = st_im[:, :S5_STATE]
            hs_sc[0, pl.ds(rb, BATCH), S5_STATE:] = st_re[:, S5_STATE:]
            hs_sc[1, pl.ds(rb, BATCH), S5_STATE:] = st_im[:, S5_STATE:]
            s_re = jnp.where(fwd_lanes, s_sc[0, pl.ds(rf, BATCH), :], s_sc[0, pl.ds(rb, BATCH), :])
            s_im = jnp.where(fwd_lanes, s_sc[1, pl.ds(rf, BATCH), :], s_sc[1, pl.ds(rb, BATCH), :])
            n_re = a_re * st_re - a_im * st_im + s_re
            n_im = a_re * st_im + a_im * st_re + s_im
            return n_re, n_im

        zero = jnp.zeros((BATCH, LANES), F32)
        lax.fori_loop(0, S5_CHUNKS, scan_step, (zero, zero))
        hcat = jnp.concatenate([hs_sc[0], hs_sc[1]], axis=1).astype(BF16)
        y_sc[gi] = y_sc[gi] + lax.dot_general(hcat, wc_ref[gi], nt, preferred_element_type=F32)

    d_skip = d_ref[...]

    def relayout_out(i, carry):
        base = pl.multiple_of(i * tok_rb, tok_rb)
        rows = pl.multiple_of(i * S5_RB, S5_RB)
        for half in range(2):
            yg = [y_sc[gi, pl.ds(rows, S5_RB), half * LANES:(half + 1) * LANES] for gi in range(S5_TILE_GROUPS)]
            yt = _block_transpose8(yg, lane_block)
            for t in range(8):
                tok = pl.ds(base + half * 8 + t, S5_RB, stride=S5_CHUNK)
                o_ref[tok, :] = _gelu_tanh(yt[t] + d_skip * h_ref[tok, :])
        return carry

    lax.fori_loop(0, n_rb, relayout_out, 0)


def _s5_core(h2, strip, s_in, wc_t, coef, d_skip):
    n_tiles = S5_GROUPS // S5_TILE_GROUPS
    tok_block = pl.BlockSpec((N_TOKENS, LANES), lambda j: (0, j))
    vmem = (4 * N_TOKENS * LANES * 4 + S5_TILE_GROUPS * S5_ROWS * S5_CK * (2 + 4)
            + 4 * S5_ROWS * LANES * 4 + 2 * S5_TILE_GROUPS * S5_CK * 3 * S5_CK * 2 + (12 << 20))
    return pl.pallas_call(
        _s5_body,
        grid=(n_tiles,),
        in_specs=[tok_block,
                  pl.BlockSpec((S5_TILE_GROUPS, S5_GROUP, 2 * S5_CK), lambda j: (j, 0, 0)),
                  pl.BlockSpec((S5_TILE_GROUPS, S5_CK, S5_CK), lambda j: (j, 0, 0)),
                  pl.BlockSpec((S5_TILE_GROUPS, S5_CK, S5_CK), lambda j: (j, 0, 0)),
                  pl.BlockSpec((S5_TILE_GROUPS, 2, LANES), lambda j: (j, 0, 0)),
                  pl.BlockSpec((1, LANES), lambda j: (0, j))],
        out_specs=tok_block,
        out_shape=jax.ShapeDtypeStruct((N_TOKENS, D_MODEL), F32),
        scratch_shapes=[pltpu.VMEM((S5_TILE_GROUPS, S5_ROWS, S5_CK), BF16),
                        pltpu.VMEM((S5_TILE_GROUPS, S5_ROWS, S5_CK), F32),
                        pltpu.VMEM((2, S5_ROWS, LANES), F32),
                        pltpu.VMEM((2, S5_ROWS, LANES), F32),
                        pltpu.VMEM((S5_CK, S5_CK), BF16)],
        compiler_params=_params(("arbitrary",), vmem),
        name="s5_core",
    )(h2, strip, s_in, wc_t, coef, d_skip.reshape(1, D_MODEL))


def _s5_glu_body(x_ref, a_ref, w_ref, o_ref):
    a = a_ref[...].reshape(x_ref.shape).astype(BF16)
    vg = jnp.dot(a, w_ref[...], preferred_element_type=F32)
    o_ref[...] = x_ref[...] + vg[:, :D_MODEL] * _sigmoid(vg[:, D_MODEL:])


def _s5_glu(x, act, w_glu_all, layer, tm=512):
    cr = tm // S5_CHUNK
    vmem = 2 * D_MODEL * D_MODEL * 2 + 6 * tm * D_MODEL * 4 + 2 * tm * 2 * D_MODEL * 4 + (4 << 20)
    return pl.pallas_call(
        _s5_glu_body,
        grid=(BATCH, SEQ // tm),
        in_specs=[pl.BlockSpec((None, tm, D_MODEL), lambda b, i: (b, i, 0)),
                  pl.BlockSpec((cr, None, S5_CHUNK, D_MODEL), lambda b, i: (i, b, 0, 0)),
                  _layer_block((D_MODEL, 2 * D_MODEL), layer)],
        out_specs=pl.BlockSpec((None, tm, D_MODEL), lambda b, i: (b, i, 0)),
        out_shape=jax.ShapeDtypeStruct((BATCH, SEQ, D_MODEL), F32),
        compiler_params=_params(("parallel", "parallel"), vmem),
        name="s5_glu",
    )(x, act, w_glu_all)


def _s5_layer(x, g, lam_re, lam_im, log_dt, b_re, b_im, c_re, c_im, d_skip, w_glu_all, layer):
    strip, s_in, wc_t, coef = _s5_matrices(lam_re, lam_im, log_dt, b_re, b_im, c_re, c_im)
    h = _s5_norm(x, g)
    act = _s5_core(h.reshape(N_TOKENS, D_MODEL), strip, s_in, wc_t, coef, d_skip)
    act = act.reshape(S5_CHUNKS, BATCH, S5_CHUNK, D_MODEL)
    return _s5_glu(x, act, w_glu_all, layer)


def _t5_bucket(rel):
    half = NUM_BUCKETS // 2
    max_exact = half // 2
    n = np.abs(rel)
    sign = (rel > 0).astype(np.int32) * half
    large = max_exact + (np.log(np.maximum(n, 1) / max_exact) / math.log(MAX_DISTANCE / max_exact)
                         * (half - max_exact)).astype(np.int32)
    large = np.minimum(large, half - 1)
    return (sign + np.where(n < max_exact, n, large)).astype(np.int32)


LOG2E = 1.4426950408889634
N_PAIRS = HEADS_PER_GROUP // 2
ATTN_PAD_ROWS = SEQ + ATTN_BQ * max(d for _, d in DILATED_GROUPS)
QKV_GROUP_WIDTH = 3 * D_MODEL
QKV_SLAB = 2 * LANES
QKV_TILE_ROWS = 512


def _attn_bias_tables(rel_bias):
    n_off = ATTN_BK + ATTN_BQ - 1
    offs = np.arange(n_off) - (ATTN_BQ - 1) - ATTN_SIDE
    k = np.arange(ATTN_BK)[:, None]
    q = np.arange(ATTN_BQ)[None, :]
    band = np.abs(k - ATTN_SIDE - q) <= ATTN_SIDE
    tabs = []
    for gi, (_, dil) in enumerate(DILATED_GROUPS):
        onehot = jnp.asarray(_t5_bucket(offs * dil)[:, None] == np.arange(NUM_BUCKETS), F32)
        f = jnp.dot(onehot, rel_bias[:, gi * HEADS_PER_GROUP:(gi + 1) * HEADS_PER_GROUP].astype(F32),
                    precision=lax.Precision.HIGHEST).T
        fr = jnp.pad(f[:, ::-1], ((0, 0), (0, 1)))
        skew = jnp.tile(fr, (1, ATTN_BK))[:, ATTN_BK - 1:ATTN_BK - 1 + ATTN_BK * n_off]
        t = skew.reshape(HEADS_PER_GROUP, ATTN_BK, n_off)[:, :, :ATTN_BQ]
        t = jnp.where(band[None], t * LOG2E, -1e30)
        t = t.reshape(N_PAIRS, 2, ATTN_BK, ATTN_BQ).transpose(0, 2, 1, 3).reshape(N_PAIRS, ATTN_BK, 2 * ATTN_BQ)
        tabs.append(t)
    return jnp.stack(tabs)


def _qkv_body(x_ref, g_ref, w_ref, qg_ref, kg_ref, perm_ref, q_out, k_out, v_out, *, n_res):
    rows = x_ref.shape[0] // n_res
    h = _rms(x_ref[...], g_ref[...]).astype(BF16)
    if n_res > 1:
        h = jnp.dot(perm_ref[...], h, preferred_element_type=F32).astype(BF16)
    lane = lax.broadcasted_iota(jnp.int32, (1, LANES), 1)
    head0 = lane < HEAD_DIM

    def head_norm(t, gain):
        sq = t * t
        s0 = jnp.sum(jnp.where(head0, sq, 0.0), axis=-1, keepdims=True)
        s1 = jnp.sum(jnp.where(head0, 0.0, sq), axis=-1, keepdims=True)
        ms = jnp.where(head0, s0, s1) * (1.0 / HEAD_DIM)
        return t * lax.rsqrt(ms + EPS) * gain

    for c in range(QKV_GROUP_WIDTH // QKV_SLAB):
        z = jnp.dot(h, w_ref[:, c * QKV_SLAB:(c + 1) * QKV_SLAB], preferred_element_type=F32)
        for half in range(QKV_SLAB // LANES):
            section, lo = divmod(c * QKV_SLAB + half * LANES, D_MODEL)
            zz = z[:, half * LANES:(half + 1) * LANES]
            if section == 0:
                zz = head_norm(zz, qg_ref[...]) * (HEAD_DIM ** -0.5 * LOG2E)
                qa = jnp.where(head0, zz, 0.0).astype(BF16)
                qb = jnp.where(head0, 0.0, zz).astype(BF16)
            elif section == 1:
                zz = head_norm(zz, kg_ref[...]).astype(BF16)
            else:
                zz = zz.astype(BF16)
            for r in range(n_res):
                piece = slice(r * rows, (r + 1) * rows)
                if section == 0:
                    q_out[r, 0, :, lo:lo + LANES] = qa[piece]
                    q_out[r, 1, :, lo:lo + LANES] = qb[piece]
                elif section == 1:
                    k_out[r, :, lo:lo + LANES] = zz[piece]
                else:
                    v_out[r, :, lo:lo + LANES] = zz[piece]


def _qkv_group(x, g, w_qkv_all, layer, gi, dil, q_gain, k_gain):
    seg = SEQ // dil
    tm = QKV_TILE_ROWS
    rows = tm // dil
    assert tm % dil == 0 and rows % 16 == 0
    gain2 = lambda gn: jnp.tile(gn.astype(F32), 2).reshape(1, LANES)
    src = np.arange(tm).reshape(rows, dil).T.reshape(tm)
    perm = jnp.asarray(src[:, None] == np.arange(tm)[None, :], BF16)
    kv_spec = pl.BlockSpec((None, dil, rows, D_MODEL), lambda b, i: (b, 0, i, 0))
    kv_shape = jax.ShapeDtypeStruct((BATCH, dil, seg, D_MODEL), BF16)
    vmem = (D_MODEL * QKV_GROUP_WIDTH * 2 + 2 * tm * D_MODEL * 4 + 2 * 4 * tm * D_MODEL * 2 + (12 << 20))
    return pl.pallas_call(
        functools.partial(_qkv_body, n_res=dil),
        grid=(BATCH, SEQ // tm),
        in_specs=[pl.BlockSpec((None, tm, D_MODEL), lambda b, i: (b, i, 0)),
                  _resident((1, D_MODEL)),
                  pl.BlockSpec((None, D_MODEL, QKV_GROUP_WIDTH), lambda b, i: (layer, 0, gi),
                               pipeline_mode=pl.Buffered(1)),
                  _resident((1, LANES)), _resident((1, LANES)), _resident((tm, tm))],
        out_specs=[pl.BlockSpec((None, dil, 2, rows, D_MODEL), lambda b, i: (b, 0, 0, i, 0)),
                   kv_spec, kv_spec],
        out_shape=[jax.ShapeDtypeStruct((BATCH, dil, 2, seg, D_MODEL), BF16), kv_shape, kv_shape],
        compiler_params=_params(("parallel", "parallel"), vmem),
        name=f"qkv_dil{dil}",
    )(x, g.reshape(1, D_MODEL), w_qkv_all, gain2(q_gain), gain2(k_gain), perm)


def _attn_body(q0, k0, v0, q1, k1, v1, q2, k2, v2, bias_ref, o_ref, kp_sc, vp_sc, lt_sc, og_sc, lg_sc):
    lane = lax.broadcasted_iota(jnp.int32, (1, LANES), 1)
    head0 = lane < HEAD_DIM
    krow = lax.broadcasted_iota(jnp.int32, (ATTN_BK, 1), 0)
    zpad = jnp.zeros((ATTN_SIDE, LANES), BF16)

    qkv_refs = ((q0, k0, v0), (q1, k1, v1), (q2, k2, v2))
    for g, (_, dil) in enumerate(DILATED_GROUPS):
        q_ref, k_ref, v_ref = qkv_refs[g]
        seg = SEQ // dil
        nb = seg // ATTN_BQ
        pad_seg = seg + 2 * ATTN_SIDE
        lt_sc[...] = jnp.zeros(lt_sc.shape, F32)
        for r in range(dil):
            base = r * pad_seg
            for src, dst in ((k_ref, kp_sc), (v_ref, vp_sc)):
                dst[base:base + ATTN_SIDE] = zpad
                dst[base + ATTN_SIDE + seg:base + pad_seg] = zpad
                dst[base + ATTN_SIDE:base + ATTN_SIDE + seg] = src[r]

        for r in range(dil):
            base = r * pad_seg
            for i in range(nb):
                blk = r * nb + i
                win = slice(base + i * ATTN_BQ, base + i * ATTN_BQ + ATTN_BK)
                qrows = slice(i * ATTN_BQ, (i + 1) * ATTN_BQ)
                qm = jnp.concatenate([q_ref[r, 0, qrows, :], q_ref[r, 1, qrows, :]], axis=0)
                s = lax.dot_general(kp_sc[win], qm, (((1,), (1,)), ((), ())), preferred_element_type=F32)
                s = s + bias_ref[g]
                if i == 0 or i == nb - 1:
                    ok = None
                    if i == 0:
                        ok = krow >= ATTN_SIDE
                    if i == nb - 1:
                        ok_hi = krow < ATTN_BK - ATTN_SIDE
                        ok = ok_hi if ok is None else jnp.logical_and(ok, ok_hi)
                    s = jnp.where(ok, s, -1e30)
                vb = vp_sc[win]
                o_heads, lse_heads = [], []
                for a in range(2):
                    sa = s[:, a * ATTN_BQ:(a + 1) * ATTN_BQ]
                    m = jnp.max(sa, axis=0, keepdims=True)
                    p = jnp.exp2(sa - m)
                    den = jnp.sum(p, axis=0, keepdims=True)
                    pt = (p * (1.0 / den)).T.astype(BF16)
                    o_heads.append(jnp.dot(pt, vb, preferred_element_type=F32))
                    lse_heads.append(m + jnp.log2(den))
                rows = pl.ds(r + i * ATTN_BQ * dil, ATTN_BQ, stride=dil)
                og_sc[g, rows, :] = jnp.where(head0, o_heads[0], o_heads[1])
                lt_sc[8 * blk:8 * blk + 1, :] = lse_heads[0]
                lt_sc[8 * blk + 1:8 * blk + 2, :] = lse_heads[1]

        ltt = lt_sc[...].T
        for r in range(dil):
            for i in range(nb):
                c = 8 * (r * nb + i)
                rows = pl.ds(r + i * ATTN_BQ * dil, ATTN_BQ, stride=dil)
                lg_sc[g, rows, :] = jnp.where(head0, ltt[:, c:c + 1], ltt[:, c + 1:c + 2])

    l0, l1, l2 = lg_sc[0], lg_sc[1], lg_sc[2]
    m = jnp.maximum(jnp.maximum(l0, l1), l2)
    w0, w1, w2 = jnp.exp2(l0 - m), jnp.exp2(l1 - m), jnp.exp2(l2 - m)
    o = (w0 * og_sc[0] + w1 * og_sc[1] + w2 * og_sc[2]) / (w0 + w1 + w2)
    o_ref[...] = o.astype(o_ref.dtype)


def _attn_core(qkv, bias_tabs):
    in_specs = []
    for (_, dil) in DILATED_GROUPS:
        seg = SEQ // dil
        in_specs.append(pl.BlockSpec((None, dil, 2, seg, LANES), lambda b, p: (b, 0, 0, 0, p)))
        in_specs.append(pl.BlockSpec((None, dil, seg, LANES), lambda b, p: (b, 0, 0, p)))
        in_specs.append(pl.BlockSpec((None, dil, seg, LANES), lambda b, p: (b, 0, 0, p)))
    in_specs.append(pl.BlockSpec((N_ATTN_GROUPS, None, ATTN_BK, 2 * ATTN_BQ), lambda b, p: (0, p, 0, 0)))
    return pl.pallas_call(
        _attn_body,
        grid=(BATCH, N_PAIRS),
        in_specs=in_specs,
        out_specs=pl.BlockSpec((None, SEQ, LANES), lambda b, p: (b, 0, p)),
        out_shape=jax.ShapeDtypeStruct((BATCH, SEQ, D_MODEL), BF16),
        scratch_shapes=[pltpu.VMEM((ATTN_PAD_ROWS, LANES), BF16),
                        pltpu.VMEM((ATTN_PAD_ROWS, LANES), BF16),
                        pltpu.VMEM((LANES, LANES), F32),
                        pltpu.VMEM((N_ATTN_GROUPS, SEQ, LANES), F32),
                        pltpu.VMEM((N_ATTN_GROUPS, SEQ, LANES), F32)],
        compiler_params=_params(("parallel", "parallel"), 40 << 20),
        name="attn_core",
    )(*qkv, bias_tabs)


def _proj_residual_body(x_ref, a_ref, w_ref, o_ref):
    o_ref[...] = x_ref[...] + jnp.dot(a_ref[...], w_ref[...], preferred_element_type=F32)


def _proj_residual(x2, a2, w_all, layer, tm=512):
    tile = pl.BlockSpec((tm, D_MODEL), lambda i: (i, 0))
    return pl.pallas_call(
        _proj_residual_body,
        grid=(x2.shape[0] // tm,),
        in_specs=[tile, tile, _layer_block((D_MODEL, D_MODEL), layer)],
        out_specs=tile,
        out_shape=jax.ShapeDtypeStruct(x2.shape, F32),
        compiler_params=_params(("parallel",), 32 << 20),
        name="proj_residual",
    )(x2, a2, w_all)


def _attn_layer(x, g, w_qkv_all, q_gain, k_gain, w_o_all, rel_bias, layer):
    qkv = []
    for gi, (window, dil) in enumerate(DILATED_GROUPS):
        assert (window // 2) // dil == ATTN_SIDE and (SEQ // dil) % ATTN_BQ == 0
        qkv.extend(_qkv_group(x, g, w_qkv_all, layer, gi, dil, q_gain[gi], k_gain[gi]))
    o = _attn_core(qkv, _attn_bias_tables(rel_bias))
    x2 = x.reshape(N_TOKENS, D_MODEL)
    return _proj_residual(x2, o.reshape(N_TOKENS, D_MODEL), w_o_all, layer).reshape(BATCH, SEQ, D_MODEL)


def kernel(x, norm_mix_g, norm_ffn_g, fnet_w_out, s5_lambda_re, s5_lambda_im, s5_log_dt, s5_b_re, s5_b_im, s5_c_re, s5_c_im, s5_d, s5_w_glu, attn_w_qkv, attn_q_gain, attn_k_gain, attn_w_o, rel_bias, ffn_w_gate_up, ffn_w_down):
    w_gate_up, w_down = _cast_bf16(ffn_w_gate_up), _cast_bf16(ffn_w_down)
    w_glu, w_qkv, w_o = _cast_bf16(s5_w_glu), _cast_bf16(attn_w_qkv), _cast_bf16(attn_w_o)
    counts = [0, 0, 0]
    for i in range(DEPTH):
        kind = i % 3
        j = counts[kind]
        counts[kind] += 1
        if kind == 0:
            x = _fnet_layer(x, norm_mix_g[i], fnet_w_out[j])
        elif kind == 1:
            x = _s5_layer(x, norm_mix_g[i], s5_lambda_re[j], s5_lambda_im[j], s5_log_dt[j], s5_b_re[j],
                          s5_b_im[j], s5_c_re[j], s5_c_im[j], s5_d[j], w_glu, j)
        else:
            x = _attn_layer(x, norm_mix_g[i], w_qkv, attn_q_gain[j], attn_k_gain[j], w_o, rel_bias, j)
        x = _ffn(x.reshape(N_TOKENS, D_MODEL), norm_ffn_g[i], w_gate_up, w_down, i).reshape(BATCH, SEQ, D_MODEL)
    return x
```

```python
import functools
import math

import numpy as np
import jax
import jax.numpy as jnp
from jax import lax
from jax.experimental import pallas as pl
from jax.experimental.pallas import tpu as pltpu

F32 = jnp.float32
BF16 = jnp.bfloat16

D_MODEL = 1024
BATCH = 8
SEQ = 2048
DEPTH = 4
N_TOKENS = BATCH * SEQ
EPS = 1e-6
D_FF = 2816
FOURIER_GROUP = 128
S5_GROUP = 16
S5_GROUPS = 64
S5_STATE = 64
HEAD_DIM = 64
HEADS_PER_GROUP = 16
DILATED_GROUPS = ((128, 1), (512, 4), (2048, 16))
N_ATTN_GROUPS = 3
NUM_BUCKETS = 32
MAX_DISTANCE = 1024
ATTN_SIDE = 64

LANES = 128
VMEM_LIMIT_CAP = 60 * 1024 * 1024

S5_CHUNK = 16
S5_CHUNKS = SEQ // S5_CHUNK
S5_ROWS = BATCH * S5_CHUNKS
S5_TILE_GROUPS = LANES // S5_GROUP
S5_CK = S5_CHUNK * S5_GROUP

ATTN_BQ = 128
ATTN_BK = ATTN_BQ + 2 * ATTN_SIDE


def _params(sem, vmem_bytes):
    return pltpu.CompilerParams(dimension_semantics=sem,
                                vmem_limit_bytes=int(min(VMEM_LIMIT_CAP, vmem_bytes)))


def _rms(x, g):
    ms = jnp.mean(x * x, axis=-1, keepdims=True)
    return x * lax.rsqrt(ms + EPS) * g


def _sigmoid(x):
    return 1.0 / (1.0 + jnp.exp(-x))


def _resident(shape):
    nd = len(shape)
    return pl.BlockSpec(shape, lambda *_: (0,) * nd, pipeline_mode=pl.Buffered(1))


def _cast_body(w_ref, o_ref):
    o_ref[...] = w_ref[...].astype(BF16)


def _cast_bf16(w, rows=256):
    n_layers, n_rows, n_cols = w.shape
    block = pl.BlockSpec((None, rows, n_cols), lambda l, i: (l, i, 0))
    return pl.pallas_call(
        _cast_body,
        grid=(n_layers, n_rows // rows),
        in_specs=[block],
        out_specs=block,
        out_shape=jax.ShapeDtypeStruct(w.shape, BF16),
        compiler_params=_params(("parallel", "parallel"), 4 * rows * n_cols * 6 + (4 << 20)),
        name="cast_bf16",
    )(w)


def _ffn_body(x_ref, g_ref, wgu_ref, wd_ref, o_ref):
    x = x_ref[...]
    h = _rms(x, g_ref[...]).astype(BF16)
    gu = jnp.dot(h, wgu_ref[...], preferred_element_type=F32)
    gate = gu[:, :D_FF]
    up = gu[:, D_FF:]
    a = (gate * _sigmoid(gate) * up).astype(BF16)
    o_ref[...] = x + jnp.dot(a, wd_ref[...], preferred_element_type=F32)


def _layer_block(shape, layer):
    return pl.BlockSpec((None,) + shape, lambda *_: (layer,) + (0,) * len(shape), pipeline_mode=pl.Buffered(1))


def _ffn(x2, g, wgu_all, wd_all, layer, tm=256):
    m = x2.shape[0]
    vmem = 3 * D_MODEL * D_FF * 2 + 4 * tm * D_MODEL * 4 + 4 * tm * 2 * D_FF * 4 + (4 << 20)
    return pl.pallas_call(
        _ffn_body,
        grid=(m // tm,),
        in_specs=[pl.BlockSpec((tm, D_MODEL), lambda i: (i, 0)),
                  _resident((1, D_MODEL)),
                  _layer_block((D_MODEL, 2 * D_FF), layer),
                  _layer_block((D_FF, D_MODEL), layer)],
        out_specs=pl.BlockSpec((tm, D_MODEL), lambda i: (i, 0)),
        out_shape=jax.ShapeDtypeStruct((m, D_MODEL), F32),
        compiler_params=_params(("parallel",), vmem),
        name="ffn",
    )(x2, g.reshape(1, D_MODEL), wgu_all, wd_all)


def _fnet_weight_body(cc_ref, sc_ref, w_ref, o_ref):
    w = w_ref[...]
    o_ref[:, :D_MODEL] = jnp.dot(cc_ref[...], w, preferred_element_type=F32,
                                 precision=lax.Precision.HIGHEST).astype(BF16)
    o_ref[:, D_MODEL:] = jnp.dot(sc_ref[...], w, preferred_element_type=F32,
                                 precision=lax.Precision.HIGHEST).astype(BF16)


def _fnet_weights(w_out):
    n = np.arange(FOURIER_GROUP)
    ang = 2.0 * np.pi * ((n[:, None] * n[None, :]) % FOURIER_GROUP) / FOURIER_GROUP
    cc = jnp.asarray(np.cos(ang) / math.sqrt(FOURIER_GROUP), F32)
    sc = jnp.asarray(np.sin(ang) / math.sqrt(FOURIER_GROUP), F32)
    ng = D_MODEL // FOURIER_GROUP
    return pl.pallas_call(
        _fnet_weight_body,
        grid=(ng,),
        in_specs=[_resident((FOURIER_GROUP, FOURIER_GROUP)),
                  _resident((FOURIER_GROUP, FOURIER_GROUP)),
                  pl.BlockSpec((FOURIER_GROUP, D_MODEL), lambda i: (i, 0))],
        out_specs=pl.BlockSpec((FOURIER_GROUP, 2 * D_MODEL), lambda i: (i, 0)),
        out_shape=jax.ShapeDtypeStruct((D_MODEL, 2 * D_MODEL), BF16),
        compiler_params=_params(("parallel",), 16 << 20),
        name="fnet_weights",
    )(cc, sc, w_out)


FN_HALF = SEQ // 2
FN_BLK = 256
FN_NB = FN_HALF // FN_BLK


def _fnet_tables():
    k = np.arange(FN_HALF)[:, None]
    n = np.arange(FN_HALF)[None, :]
    ang = 2.0 * np.pi * ((k * n) % SEQ) / SEQ
    scale = 1.0 / math.sqrt(SEQ)
    i = np.arange(FN_BLK)
    rev = (i[None, :] == FN_BLK - i[:, None]).astype(np.float32)
    return (jnp.asarray(np.cos(ang) * scale, BF16), jnp.asarray(np.sin(ang) * scale, BF16),
            jnp.asarray(rev, BF16))


def _fnet_body(x_ref, g_ref, cs_ref, ss_ref, w_ref, rev_ref, o_ref, e_sc, d_sc, zc_sc, zs_sc):
    g = g_ref[...]
    scale = 1.0 / math.sqrt(SEQ)
    row = lax.broadcasted_iota(jnp.int32, (FN_BLK, 1), 0)
    first = row == 0
    sign = jnp.where((row & 1) == 0, 1.0, -1.0)
    rev = rev_ref[...]
    blk = lambda a: pl.ds(a * FN_BLK, FN_BLK)
    mirror = lambda a: pl.ds(SEQ - (a + 1) * FN_BLK, FN_BLK)

    alt = jnp.zeros((1, D_MODEL), F32)
    carry = jnp.zeros((1, D_MODEL), F32)
    for a in range(FN_NB):
        lo = _rms(x_ref[blk(a), :], g)
        hi = _rms(x_ref[mirror(a), :], g)
        alt = alt + jnp.sum(sign * lo, axis=0, keepdims=True) + jnp.sum(sign * hi, axis=0, keepdims=True)
        r = jnp.dot(rev, hi.astype(BF16), preferred_element_type=F32)
        r = jnp.where(first, carry, r)
        e_sc[blk(a), :] = (lo + r).astype(BF16)
        d_sc[blk(a), :] = (lo - r).astype(BF16)
        carry = hi[0:1, :]
    h_nyq = carry * scale

    for c in range(FN_NB):
        zc = jnp.dot(cs_ref[blk(c), :], e_sc[...], preferred_element_type=F32) + sign * h_nyq
        zs = jnp.dot(ss_ref[blk(c), :], d_sc[...], preferred_element_type=F32)
        zc_sc[blk(c), :] = zc.astype(BF16)
        zs_sc[blk(c), :] = zs.astype(BF16)

    wc = w_ref[:, :D_MODEL]
    ws = w_ref[:, D_MODEL:]
    z_nyq = jnp.broadcast_to(alt * scale, (8, D_MODEL)).astype(BF16)
    carry = jnp.dot(z_nyq, wc, preferred_element_type=F32)[0:1, :]
    for c in reversed(range(FN_NB)):
        p = jnp.dot(zc_sc[blk(c), :], wc, preferred_element_type=F32)
        q = jnp.dot(zs_sc[blk(c), :], ws, preferred_element_type=F32)
        o_ref[blk(c), :] = x_ref[blk(c), :] + (p - q)
        m = p + q
        m_hi = m.astype(BF16)
        m_lo = (m - m_hi.astype(F32)).astype(BF16)
        r = (jnp.dot(rev, m_hi, preferred_element_type=F32) + jnp.dot(rev, m_lo, preferred_element_type=F32))
        r = jnp.where(first, carry, r)
        o_ref[mirror(c), :] = x_ref[mirror(c), :] + r
        carry = m[0:1, :]


def _fnet_layer(x, g, w_out):
    wcs = _fnet_weights(w_out)
    cs, ss, rev = _fnet_tables()
    seq_block = pl.BlockSpec((None, SEQ, D_MODEL), lambda b: (b, 0, 0))
    half = pltpu.VMEM((FN_HALF, D_MODEL), BF16)
    vmem = 4 * SEQ * D_MODEL * 4 + 4 * FN_HALF * D_MODEL * 2 + 4 * FN_HALF * D_MODEL * 2 + (12 << 20)
    return pl.pallas_call(
        _fnet_body,
        grid=(BATCH,),
        in_specs=[seq_block, _resident((1, D_MODEL)), _resident((FN_HALF, FN_HALF)),
                  _resident((FN_HALF, FN_HALF)), _resident((D_MODEL, 2 * D_MODEL)),
                  _resident((FN_BLK, FN_BLK))],
        out_specs=seq_block,
        out_shape=jax.ShapeDtypeStruct((BATCH, SEQ, D_MODEL), F32),
        scratch_shapes=[half, half, half, half],
        compiler_params=_params(("parallel",), vmem),
        name="fnet_mix",
    )(x, g.reshape(1, D_MODEL), cs, ss, wcs, rev)


def _s5_matrices(lam_re, lam_im, log_dt, b_re, b_im, c_re, c_im):
    lam = lax.complex(lam_re.astype(F32), lam_im.astype(F32))
    dt = jnp.exp(log_dt.astype(F32))[..., None]
    lam_dt = lam * dt
    lam_bar = jnp.exp(lam_dt)
    b_bar = ((lam_bar - 1.0) / lam)[..., None] * lax.complex(b_re.astype(F32), b_im.astype(F32))
    c = lax.complex(c_re.astype(F32), c_im.astype(F32))
    L = S5_CHUNK
    taus = jnp.arange(L + 1, dtype=F32)
    pw = jnp.exp(lam_dt[..., None, :] * taus[:, None])
    hi = lax.Precision.HIGHEST
    kern = jnp.einsum('dgpn,dgtn,dgnq->dgtpq', c, pw[:, :, :L], b_bar, precision=hi).real
    kt = kern.transpose(0, 1, 4, 2, 3)
    strip = jnp.concatenate([kt[1][:, :, :0:-1], kt[0][:, :, :1] + kt[1][:, :, :1], kt[0][:, :, 1:]], axis=2)
    strip = strip.reshape(S5_GROUPS, S5_GROUP, (2 * L - 1) * S5_GROUP)
    strip = jnp.pad(strip, ((0, 0), (0, 0), (0, S5_GROUP)))
    def both(fwd, bwd):
        z = jnp.concatenate([fwd, bwd], axis=-1)
        return z.real[:, :, None, :], z.imag[:, :, None, :]
    def outer(pw_pair, mat_pair, sign):
        (ar, ai), (br, bi) = pw_pair, mat_pair
        re = ar * br - ai * bi
        im = ar * bi + ai * br
        return jnp.concatenate([re, sign * im], axis=-1).reshape(S5_GROUPS, S5_CK, S5_CK)
    b_t = b_bar.transpose(0, 1, 3, 2)
    mat = lambda m: tuple(z[:, None, :, 0, :] for z in both(m[0], m[1]))
    s_in = outer(both(pw[0][:, ::-1][:, 1:], pw[1][:, :L]), mat(b_t), 1.0)
    wc_t = outer(both(pw[0][:, 1:], pw[1][:, ::-1][:, :L]), mat(c), -1.0)
    al = pw[:, :, L]
    coef = jnp.stack([jnp.concatenate([al[0].real, al[1].real], -1),
                      jnp.concatenate([al[0].imag, al[1].imag], -1)], axis=1)
    return strip.astype(F32), s_in.astype(BF16), wc_t.astype(BF16), coef.astype(F32)


def _s5_norm_body(x_ref, g_ref, o_ref):
    h = _rms(x_ref[...], g_ref[...])
    o_ref[...] = h.reshape(o_ref.shape)


def _s5_norm(x, g, tm=512):
    cr = tm // S5_CHUNK
    return pl.pallas_call(
        _s5_norm_body,
        grid=(BATCH, SEQ // tm),
        in_specs=[pl.BlockSpec((None, tm, D_MODEL), lambda b, i: (b, i, 0)),
                  _resident((1, D_MODEL))],
        out_specs=pl.BlockSpec((cr, None, S5_CHUNK, D_MODEL), lambda b, i: (i, b, 0, 0)),
        out_shape=jax.ShapeDtypeStruct((S5_CHUNKS, BATCH, S5_CHUNK, D_MODEL), F32),
        compiler_params=_params(("parallel", "parallel"), 32 << 20),
        name="s5_norm",
    )(x, g.reshape(1, D_MODEL))


def _block_transpose8(a, lane_block):
    a = list(a)
    for d in (4, 2, 1):
        take_lo = (lane_block & d) == 0
        nxt = list(a)
        for i in range(8):
            if i & d:
                continue
            lo, hi = a[i], a[i + d]
            nxt[i] = jnp.where(take_lo, lo, pltpu.roll(hi, S5_GROUP * d, 1))
            nxt[i + d] = jnp.where(take_lo, pltpu.roll(lo, LANES - S5_GROUP * d, 1), hi)
        a = nxt
    return a


def _gelu_tanh(y):
    return 0.5 * y * (1.0 + jnp.tanh(math.sqrt(2.0 / math.pi) * (y + 0.044715 * (y * y * y))))


S5_RB = 64


def _s5_body(h_ref, strip_ref, ws_ref, wc_ref, coef_ref, d_ref, o_ref, x_sc, y_sc, s_sc, hs_sc, toep_sc):
    lane = lax.broadcasted_iota(jnp.int32, (1, LANES), 1)
    lane_block = lane // S5_GROUP
    fwd_lanes = lane < S5_STATE
    n_rb = S5_ROWS // S5_RB
    tok_rb = S5_RB * S5_CHUNK

    def relayout_in(i, carry):
        base = pl.multiple_of(i * tok_rb, tok_rb)
        rows = pl.multiple_of(i * S5_RB, S5_RB)
        for half in range(2):
            a = [h_ref[pl.ds(base + half * 8 + t, S5_RB, stride=S5_CHUNK), :] for t in range(8)]
            xt = _block_transpose8(a, lane_block)
            for gi in range(S5_TILE_GROUPS):
                x_sc[gi, pl.ds(rows, S5_RB), half * LANES:(half + 1) * LANES] = xt[gi].astype(BF16)
        return carry

    lax.fori_loop(0, n_rb, relayout_in, 0)

    nt = (((1,), (1,)), ((), ()))
    for gi in range(S5_TILE_GROUPS):
        strip = strip_ref[gi]
        for t in range(S5_CHUNK):
            off = (S5_CHUNK - 1 - t) * S5_GROUP
            win = strip if off == 0 else pltpu.roll(strip, strip.shape[1] - off, 1)
            toep_sc[t * S5_GROUP:(t + 1) * S5_GROUP, :] = win[:, :S5_CK].astype(BF16)
        xg = x_sc[gi]
        y_sc[gi] = jnp.dot(xg, toep_sc[...], preferred_element_type=F32)
        z = jnp.dot(xg, ws_ref[gi], preferred_element_type=F32)
        s_sc[0] = z[:, :LANES]
        s_sc[1] = z[:, LANES:]
        a_re = coef_ref[gi, 0:1, :]
        a_im = coef_ref[gi, 1:2, :]

        def scan_step(k, st):
            st_re, st_im = st
            rf = pl.multiple_of(k * BATCH, BATCH)
            rb = pl.multiple_of((S5_CHUNKS - 1 - k) * BATCH, BATCH)
            hs_sc[0, pl.ds(rf, BATCH), :S5_STATE] = st_re[:, :S5_STATE]
            hs_sc[1, pl.ds(rf, BATCH), :S5_STATE] = st_im[:, :S5_STATE]
            hs_sc[0, pl.ds(rb, BATCH), S5_STATE:] = st_re[:, S5_STATE:]
            hs_sc[1, pl.ds(rb, BATCH), S5_STATE:] = st_im[:, S5_STATE:]
            s_re = jnp.where(fwd_lanes, s_sc[0, pl.ds(rf, BATCH), :], s_sc[0, pl.ds(rb, BATCH), :])
            s_im = jnp.where(fwd_lanes, s_sc[1, pl.ds(rf, BATCH), :], s_sc[1, pl.ds(rb, BATCH), :])
            n_re = a_re * st_re - a_im * st_im + s_re
            n_im = a_re * st_im + a_im * st_re + s_im
            return n_re, n_im

        zero = jnp.zeros((BATCH, LANES), F32)
        lax.fori_loop(0, S5_CHUNKS, scan_step, (zero, zero))
        hcat = jnp.concatenate([hs_sc[0], hs_sc[1]], axis=1).astype(BF16)
        y_sc[gi] = y_sc[gi] + lax.dot_general(hcat, wc_ref[gi], nt, preferred_element_type=F32)

    d_skip = d_ref[...]

    def relayout_out(i, carry):
        base = pl.multiple_of(i * tok_rb, tok_rb)
        rows = pl.multiple_of(i * S5_RB, S5_RB)
        for half in range(2):
            yg = [y_sc[gi, pl.ds(rows, S5_RB), half * LANES:(half + 1) * LANES] for gi in range(S5_TILE_GROUPS)]
            yt = _block_transpose8(yg, lane_block)
            for t in range(8):
                tok = pl.ds(base + half * 8 + t, S5_RB, stride=S5_CHUNK)
                o_ref[tok, :] = _gelu_tanh(yt[t] + d_skip * h_ref[tok, :])
        return carry

    lax.fori_loop(0, n_rb, relayout_out, 0)


def _s5_core(h2, strip, s_in, wc_t, coef, d_skip):
    n_tiles = S5_GROUPS // S5_TILE_GROUPS
    tok_block = pl.BlockSpec((N_TOKENS, LANES), lambda j: (0, j))
    vmem = (4 * N_TOKENS * LANES * 4 + S5_TILE_GROUPS * S5_ROWS * S5_CK * (2 + 4)
            + 4 * S5_ROWS * LANES * 4 + 2 * S5_TILE_GROUPS * S5_CK * 3 * S5_CK * 2 + (12 << 20))
    return pl.pallas_call(
        _s5_body,
        grid=(n_tiles,),
        in_specs=[tok_block,
                  pl.BlockSpec((S5_TILE_GROUPS, S5_GROUP, 2 * S5_CK), lambda j: (j, 0, 0)),
                  pl.BlockSpec((S5_TILE_GROUPS, S5_CK, S5_CK), lambda j: (j, 0, 0)),
                  pl.BlockSpec((S5_TILE_GROUPS, S5_CK, S5_CK), lambda j: (j, 0, 0)),
                  pl.BlockSpec((S5_TILE_GROUPS, 2, LANES), lambda j: (j, 0, 0)),
                  pl.BlockSpec((1, LANES), lambda j: (0, j))],
        out_specs=tok_block,
        out_shape=jax.ShapeDtypeStruct((N_TOKENS, D_MODEL), F32),
        scratch_shapes=[pltpu.VMEM((S5_TILE_GROUPS, S5_ROWS, S5_CK), BF16),
                        pltpu.VMEM((S5_TILE_GROUPS, S5_ROWS, S5_CK), F32),
                        pltpu.VMEM((2, S5_ROWS, LANES), F32),
                        pltpu.VMEM((2, S5_ROWS, LANES), F32),
                        pltpu.VMEM((S5_CK, S5_CK), BF16)],
        compiler_params=_params(("arbitrary",), vmem),
        name="s5_core",
    )(h2, strip, s_in, wc_t, coef, d_skip.reshape(1, D_MODEL))


def _s5_glu_body(x_ref, a_ref, w_ref, o_ref):
    a = a_ref[...].reshape(x_ref.shape).astype(BF16)
    vg = jnp.dot(a, w_ref[...], preferred_element_type=F32)
    o_ref[...] = x_ref[...] + vg[:, :D_MODEL] * _sigmoid(vg[:, D_MODEL:])


def _s5_glu(x, act, w_glu_all, layer, tm=512):
    cr = tm // S5_CHUNK
    vmem = 2 * D_MODEL * D_MODEL * 2 + 6 * tm * D_MODEL * 4 + 2 * tm * 2 * D_MODEL * 4 + (4 << 20)
    return pl.pallas_call(
        _s5_glu_body,
        grid=(BATCH, SEQ // tm),
        in_specs=[pl.BlockSpec((None, tm, D_MODEL), lambda b, i: (b, i, 0)),
                  pl.BlockSpec((cr, None, S5_CHUNK, D_MODEL), lambda b, i: (i, b, 0, 0)),
                  _layer_block((D_MODEL, 2 * D_MODEL), layer)],
        out_specs=pl.BlockSpec((None, tm, D_MODEL), lambda b, i: (b, i, 0)),
        out_shape=jax.ShapeDtypeStruct((BATCH, SEQ, D_MODEL), F32),
        compiler_params=_params(("parallel", "parallel"), vmem),
        name="s5_glu",
    )(x, act, w_glu_all)


def _s5_layer(x, g, lam_re, lam_im, log_dt, b_re, b_im, c_re, c_im, d_skip, w_glu_all, layer):
    strip, s_in, wc_t, coef = _s5_matrices(lam_re, lam_im, log_dt, b_re, b_im, c_re, c_im)
    h = _s5_norm(x, g)
    act = _s5_core(h.reshape(N_TOKENS, D_MODEL), strip, s_in, wc_t, coef, d_skip)
    act = act.reshape(S5_CHUNKS, BATCH, S5_CHUNK, D_MODEL)
    return _s5_glu(x, act, w_glu_all, layer)


def _t5_bucket(rel):
    half = NUM_BUCKETS // 2
    max_exact = half // 2
    n = np.abs(rel)
    sign = (rel > 0).astype(np.int32) * half
    large = max_exact + (np.log(np.maximum(n, 1) / max_exact) / math.log(MAX_DISTANCE / max_exact)
                         * (half - max_exact)).astype(np.int32)
    large = np.minimum(large, half - 1)
    return (sign + np.where(n < max_exact, n, large)).astype(np.int32)


LOG2E = 1.4426950408889634
N_PAIRS = HEADS_PER_GROUP // 2
ATTN_PAD_ROWS = SEQ + ATTN_BQ * max(d for _, d in DILATED_GROUPS)
QKV_GROUP_WIDTH = 3 * D_MODEL
QKV_SLAB = 2 * LANES
QKV_TILE_ROWS = 512


def _attn_bias_tables(rel_bias):
    n_off = ATTN_BK + ATTN_BQ - 1
    offs = np.arange(n_off) - (ATTN_BQ - 1) - ATTN_SIDE
    strips = []
    for gi, (_, dil) in enumerate(DILATED_GROUPS):
        onehot = jnp.asarray(_t5_bucket(offs * dil)[:, None] == np.arange(NUM_BUCKETS), F32)
        f = jnp.dot(onehot, rel_bias[:, gi * HEADS_PER_GROUP:(gi + 1) * HEADS_PER_GROUP].astype(F32),
                    precision=lax.Precision.HIGHEST).T
        strips.append(jnp.pad(f[:, ::-1], ((0, 0), (0, BIAS_STRIP - n_off))))
    strips = jnp.stack(strips).reshape(N_ATTN_GROUPS, HEADS_PER_GROUP, 1, BIAS_STRIP)
    return pl.pallas_call(
        _attn_bias_body,
        grid=(N_ATTN_GROUPS, N_PAIRS),
        in_specs=[pl.BlockSpec((None, 2, 1, BIAS_STRIP), lambda g, p: (g, p, 0, 0))],
        out_specs=pl.BlockSpec((None, None, ATTN_BK, 2 * ATTN_BQ), lambda g, p: (g, p, 0, 0)),
        out_shape=jax.ShapeDtypeStruct((N_ATTN_GROUPS, N_PAIRS, ATTN_BK, 2 * ATTN_BQ), F32),
        compiler_params=_params(("parallel", "parallel"), 16 << 20),
        name="attn_bias",
    )(strips)


BIAS_STRIP = 512


def _attn_bias_body(f_ref, o_ref):
    krow = lax.broadcasted_iota(jnp.int32, (ATTN_BK, 1), 0)
    qcol = lax.broadcasted_iota(jnp.int32, (1, ATTN_BQ), 1)
    band = jnp.abs(krow - ATTN_SIDE - qcol) <= ATTN_SIDE
    for a in range(2):
        x = jnp.broadcast_to(f_ref[a], (ATTN_BK, BIAS_STRIP))
        t = pltpu.roll(x, BIAS_STRIP - (ATTN_BK - 1), 1, stride=1, stride_axis=0)[:, :ATTN_BQ]
        o_ref[:, a * ATTN_BQ:(a + 1) * ATTN_BQ] = jnp.where(band, t * LOG2E, -1e30)


def _qkv_body(x_ref, g_ref, w_ref, qg_ref, kg_ref, perm_ref, q_out, k_out, v_out, *, n_res):
    rows = x_ref.shape[0] // n_res
    h = _rms(x_ref[...], g_ref[...]).astype(BF16)
    if n_res > 1:
        h = jnp.dot(perm_ref[...], h, preferred_element_type=F32).astype(BF16)
    lane = lax.broadcasted_iota(jnp.int32, (1, LANES), 1)
    head0 = lane < HEAD_DIM

    def head_norm(t, gain):
        sq = t * t
        s0 = jnp.sum(jnp.where(head0, sq, 0.0), axis=-1, keepdims=True)
        s1 = jnp.sum(jnp.where(head0, 0.0, sq), axis=-1, keepdims=True)
        ms = jnp.where(head0, s0, s1) * (1.0 / HEAD_DIM)
        return t * lax.rsqrt(ms + EPS) * gain

    for c in range(QKV_GROUP_WIDTH // QKV_SLAB):
        z = jnp.dot(h, w_ref[:, c * QKV_SLAB:(c + 1) * QKV_SLAB], preferred_element_type=F32)
        for half in range(QKV_SLAB // LANES):
            section, lo = divmod(c * QKV_SLAB + half * LANES, D_MODEL)
            zz = z[:, half * LANES:(half + 1) * LANES]
            if section == 0:
                zz = head_norm(zz, qg_ref[...]) * (HEAD_DIM ** -0.5 * LOG2E)
                qa = jnp.where(head0, zz, 0.0).astype(BF16)
                qb = jnp.where(head0, 0.0, zz).astype(BF16)
            elif section == 1:
                zz = head_norm(zz, kg_ref[...]).astype(BF16)
            else:
                zz = zz.astype(BF16)
            for r in range(n_res):
                piece = slice(r * rows, (r + 1) * rows)
                if section == 0:
                    q_out[r, 0, :, lo:lo + LANES] = qa[piece]
                    q_out[r, 1, :, lo:lo + LANES] = qb[piece]
                elif section == 1:
                    k_out[r, :, lo:lo + LANES] = zz[piece]
                else:
                    v_out[r, :, lo:lo + LANES] = zz[piece]


def _qkv_group(x, g, w_qkv_all, layer, gi, dil, q_gain, k_gain):
    seg = SEQ // dil
    tm = QKV_TILE_ROWS
    rows = tm // dil
    assert tm % dil == 0 and rows % 16 == 0
    gain2 = lambda gn: jnp.tile(gn.astype(F32), 2).reshape(1, LANES)
    src = np.arange(tm).reshape(rows, dil).T.reshape(tm)
    perm = jnp.asarray(src[:, None] == np.arange(tm)[None, :], BF16)
    kv_spec = pl.BlockSpec((None, dil, rows, D_MODEL), lambda b, i: (b, 0, i, 0))
    kv_shape = jax.ShapeDtypeStruct((BATCH, dil, seg, D_MODEL), BF16)
    vmem = (D_MODEL * QKV_GROUP_WIDTH * 2 + 2 * tm * D_MODEL * 4 + 2 * 4 * tm * D_MODEL * 2 + (12 << 20))
    return pl.pallas_call(
        functools.partial(_qkv_body, n_res=dil),
        grid=(BATCH, SEQ // tm),
        in_specs=[pl.BlockSpec((None, tm, D_MODEL), lambda b, i: (b, i, 0)),
                  _resident((1, D_MODEL)),
                  pl.BlockSpec((None, D_MODEL, QKV_GROUP_WIDTH), lambda b, i: (layer, 0, gi),
                               pipeline_mode=pl.Buffered(1)),
                  _resident((1, LANES)), _resident((1, LANES)), _resident((tm, tm))],
        out_specs=[pl.BlockSpec((None, dil, 2, rows, D_MODEL), lambda b, i: (b, 0, 0, i, 0)),
                   kv_spec, kv_spec],
        out_shape=[jax.ShapeDtypeStruct((BATCH, dil, 2, seg, D_MODEL), BF16), kv_shape, kv_shape],
        compiler_params=_params(("parallel", "parallel"), vmem),
        name=f"qkv_dil{dil}",
    )(x, g.reshape(1, D_MODEL), w_qkv_all, gain2(q_gain), gain2(k_gain), perm)


def _attn_body(q0, k0, v0, q1, k1, v1, q2, k2, v2, bias_ref, o_ref, kp_sc, vp_sc, lt_sc, og_sc, lg_sc):
    lane = lax.broadcasted_iota(jnp.int32, (1, LANES), 1)
    head0 = lane < HEAD_DIM
    krow = lax.broadcasted_iota(jnp.int32, (ATTN_BK, 1), 0)
    zpad = jnp.zeros((ATTN_SIDE, LANES), BF16)

    qkv_refs = ((q0, k0, v0), (q1, k1, v1), (q2, k2, v2))
    for g, (_, dil) in enumerate(DILATED_GROUPS):
        q_ref, k_ref, v_ref = qkv_refs[g]
        seg = SEQ // dil
        nb = seg // ATTN_BQ
        pad_seg = seg + 2 * ATTN_SIDE
        lt_sc[...] = jnp.zeros(lt_sc.shape, F32)
        for r in range(dil):
            base = r * pad_seg
            for src, dst in ((k_ref, kp_sc), (v_ref, vp_sc)):
                dst[base:base + ATTN_SIDE] = zpad
                dst[base + ATTN_SIDE + seg:base + pad_seg] = zpad
                dst[base + ATTN_SIDE:base + ATTN_SIDE + seg] = src[r]

        for r in range(dil):
            base = r * pad_seg
            for i in range(nb):
                blk = r * nb + i
                win = slice(base + i * ATTN_BQ, base + i * ATTN_BQ + ATTN_BK)
                qrows = slice(i * ATTN_BQ, (i + 1) * ATTN_BQ)
                qm = jnp.concatenate([q_ref[r, 0, qrows, :], q_ref[r, 1, qrows, :]], axis=0)
                s = lax.dot_general(kp_sc[win], qm, (((1,), (1,)), ((), ())), preferred_element_type=F32)
                s = s + bias_ref[g]
                if i == 0 or i == nb - 1:
                    ok = None
                    if i == 0:
                        ok = krow >= ATTN_SIDE
                    if i == nb - 1:
                        ok_hi = krow < ATTN_BK - ATTN_SIDE
                        ok = ok_hi if ok is None else jnp.logical_and(ok, ok_hi)
                    s = jnp.where(ok, s, -1e30)
                vb = vp_sc[win]
                o_heads, lse_heads = [], []
                for a in range(2):
                    sa = s[:, a * ATTN_BQ:(a + 1) * ATTN_BQ]
                    m = jnp.max(sa, axis=0, keepdims=True)
                    p = jnp.exp2(sa - m)
                    den = jnp.sum(p, axis=0, keepdims=True)
                    pt = (p * (1.0 / den)).T.astype(BF16)
                    o_heads.append(jnp.dot(pt, vb, preferred_element_type=F32))
                    lse_heads.append(m + jnp.log2(den))
                rows = pl.ds(r + i * ATTN_BQ * dil, ATTN_BQ, stride=dil)
                og_sc[g, rows, :] = jnp.where(head0, o_heads[0], o_heads[1])
                lt_sc[8 * blk:8 * blk + 1, :] = lse_heads[0]
                lt_sc[8 * blk + 1:8 * blk + 2, :] = lse_heads[1]

        ltt = lt_sc[...].T
        for r in range(dil):
            for i in range(nb):
                c = 8 * (r * nb + i)
                rows = pl.ds(r + i * ATTN_BQ * dil, ATTN_BQ, stride=dil)
                lg_sc[g, rows, :] = jnp.where(head0, ltt[:, c:c + 1], ltt[:, c + 1:c + 2])

    l0, l1, l2 = lg_sc[0], lg_sc[1], lg_sc[2]
    m = jnp.maximum(jnp.maximum(l0, l1), l2)
    w0, w1, w2 = jnp.exp2(l0 - m), jnp.exp2(l1 - m), jnp.exp2(l2 - m)
    o = (w0 * og_sc[0] + w1 * og_sc[1] + w2 * og_sc[2]) / (w0 + w1 + w2)
    o_ref[...] = o.astype(o_ref.dtype)


def _attn_core(qkv, bias_tabs):
    in_specs = []
    for (_, dil) in DILATED_GROUPS:
        seg = SEQ // dil
        in_specs.append(pl.BlockSpec((None, dil, 2, seg, LANES), lambda b, p: (b, 0, 0, 0, p)))
        in_specs.append(pl.BlockSpec((None, dil, seg, LANES), lambda b, p: (b, 0, 0, p)))
        in_specs.append(pl.BlockSpec((None, dil, seg, LANES), lambda b, p: (b, 0, 0, p)))
    in_specs.append(pl.BlockSpec((N_ATTN_GROUPS, None, ATTN_BK, 2 * ATTN_BQ), lambda b, p: (0, p, 0, 0)))
    return pl.pallas_call(
        _attn_body,
        grid=(BATCH, N_PAIRS),
        in_specs=in_specs,
        out_specs=pl.BlockSpec((None, SEQ, LANES), lambda b, p: (b, 0, p)),
        out_shape=jax.ShapeDtypeStruct((BATCH, SEQ, D_MODEL), BF16),
        scratch_shapes=[pltpu.VMEM((ATTN_PAD_ROWS, LANES), BF16),
                        pltpu.VMEM((ATTN_PAD_ROWS, LANES), BF16),
                        pltpu.VMEM((LANES, LANES), F32),
                        pltpu.VMEM((N_ATTN_GROUPS, SEQ, LANES), F32),
                        pltpu.VMEM((N_ATTN_GROUPS, SEQ, LANES), F32)],
        compiler_params=_params(("parallel", "parallel"), 40 << 20),
        name="attn_core",
    )(*qkv, bias_tabs)


def _proj_residual_body(x_ref, a_ref, w_ref, o_ref):
    o_ref[...] = x_ref[...] + jnp.dot(a_ref[...], w_ref[...], preferred_element_type=F32)


def _proj_residual(x2, a2, w_all, layer, tm=512):
    tile = pl.BlockSpec((tm, D_MODEL), lambda i: (i, 0))
    return pl.pallas_call(
        _proj_residual_body,
        grid=(x2.shape[0] // tm,),
        in_specs=[tile, tile, _layer_block((D_MODEL, D_MODEL), layer)],
        out_specs=tile,
        out_shape=jax.ShapeDtypeStruct(x2.shape, F32),
        compiler_params=_params(("parallel",), 32 << 20),
        name="proj_residual",
    )(x2, a2, w_all)


def _attn_layer(x, g, w_qkv_all, q_gain, k_gain, w_o_all, rel_bias, layer):
    qkv = []
    for gi, (window, dil) in enumerate(DILATED_GROUPS):
        assert (window // 2) // dil == ATTN_SIDE and (SEQ // dil) % ATTN_BQ == 0
        qkv.extend(_qkv_group(x, g, w_qkv_all, layer, gi, dil, q_gain[gi], k_gain[gi]))
    o = _attn_core(qkv, _attn_bias_tables(rel_bias))
    x2 = x.reshape(N_TOKENS, D_MODEL)
    return _proj_residual(x2, o.reshape(N_TOKENS, D_MODEL), w_o_all, layer).reshape(BATCH, SEQ, D_MODEL)


def kernel(x, norm_mix_g, norm_ffn_g, fnet_w_out, s5_lambda_re, s5_lambda_im, s5_log_dt, s5_b_re, s5_b_im, s5_c_re, s5_c_im, s5_d, s5_w_glu, attn_w_qkv, attn_q_gain, attn_k_gain, attn_w_o, rel_bias, ffn_w_gate_up, ffn_w_down):
    w_gate_up, w_down = _cast_bf16(ffn_w_gate_up), _cast_bf16(ffn_w_down)
    w_glu, w_qkv, w_o = _cast_bf16(s5_w_glu), _cast_bf16(attn_w_qkv), _cast_bf16(attn_w_o)
    counts = [0, 0, 0]
    for i in range(DEPTH):
        kind = i % 3
        j = counts[kind]
        counts[kind] += 1
        if kind == 0:
            x = _fnet_layer(x, norm_mix_g[i], fnet_w_out[j])
        elif kind == 1:
            x = _s5_layer(x, norm_mix_g[i], s5_lambda_re[j], s5_lambda_im[j], s5_log_dt[j], s5_b_re[j],
                          s5_b_im[j], s5_c_re[j], s5_c_im[j], s5_d[j], w_glu, j)
        else:
            x = _attn_layer(x, norm_mix_g[i], w_qkv, attn_q_gain[j], attn_k_gain[j], w_o, rel_bias, j)
        x = _ffn(x.reshape(N_TOKENS, D_MODEL), norm_ffn_g[i], w_gate_up, w_down, i).reshape(BATCH, SEQ, D_MODEL)
    return x
```

```python
import functools
import math

import numpy as np
import jax
import jax.numpy as jnp
from jax import lax
from jax.experimental import pallas as pl
from jax.experimental.pallas import tpu as pltpu

F32 = jnp.float32
BF16 = jnp.bfloat16

D_MODEL = 1024
BATCH = 8
SEQ = 2048
DEPTH = 4
N_TOKENS = BATCH * SEQ
EPS = 1e-6
D_FF = 2816
FOURIER_GROUP = 128
S5_GROUP = 16
S5_GROUPS = 64
S5_STATE = 64
HEAD_DIM = 64
HEADS_PER_GROUP = 16
DILATED_GROUPS = ((128, 1), (512, 4), (2048, 16))
N_ATTN_GROUPS = 3
NUM_BUCKETS = 32
MAX_DISTANCE = 1024
ATTN_SIDE = 64

LANES = 128
VMEM_LIMIT_CAP = 60 * 1024 * 1024

S5_CHUNK = 16
S5_CHUNKS = SEQ // S5_CHUNK
S5_ROWS = BATCH * S5_CHUNKS
S5_TILE_GROUPS = LANES // S5_GROUP
S5_CK = S5_CHUNK * S5_GROUP

ATTN_BQ = 128
ATTN_BK = ATTN_BQ + 2 * ATTN_SIDE


def _params(sem, vmem_bytes):
    return pltpu.CompilerParams(dimension_semantics=sem,
                                vmem_limit_bytes=int(min(VMEM_LIMIT_CAP, vmem_bytes)))


def _rms(x, g):
    ms = jnp.mean(x * x, axis=-1, keepdims=True)
    return x * lax.rsqrt(ms + EPS) * g


def _sigmoid(x):
    return 1.0 / (1.0 + jnp.exp(-x))


def _resident(shape):
    nd = len(shape)
    return pl.BlockSpec(shape, lambda *_: (0,) * nd, pipeline_mode=pl.Buffered(1))


def _cast_body(w_ref, o_ref):
    o_ref[...] = w_ref[...].astype(BF16)


def _cast_bf16(w, rows=256):
    n_layers, n_rows, n_cols = w.shape
    block = pl.BlockSpec((None, rows, n_cols), lambda l, i: (l, i, 0))
    return pl.pallas_call(
        _cast_body,
        grid=(n_layers, n_rows // rows),
        in_specs=[block],
        out_specs=block,
        out_shape=jax.ShapeDtypeStruct(w.shape, BF16),
        compiler_params=_params(("parallel", "parallel"), 4 * rows * n_cols * 6 + (4 << 20)),
        name="cast_bf16",
    )(w)


def _ffn_body(x_ref, g_ref, wgu_ref, wd_ref, o_ref):
    x = x_ref[...]
    h = _rms(x, g_ref[...]).astype(BF16)
    gu = jnp.dot(h, wgu_ref[...], preferred_element_type=F32)
    gate = gu[:, :D_FF]
    up = gu[:, D_FF:]
    a = (gate * _sigmoid(gate) * up).astype(BF16)
    o_ref[...] = x + jnp.dot(a, wd_ref[...], preferred_element_type=F32)


def _layer_block(shape, layer):
    return pl.BlockSpec((None,) + shape, lambda *_: (layer,) + (0,) * len(shape), pipeline_mode=pl.Buffered(1))


def _ffn(x2, g, wgu_all, wd_all, layer, tm=512):
    m = x2.shape[0]
    vmem = 3 * D_MODEL * D_FF * 2 + 4 * tm * D_MODEL * 4 + 4 * tm * 2 * D_FF * 4 + (4 << 20)
    return pl.pallas_call(
        _ffn_body,
        grid=(m // tm,),
        in_specs=[pl.BlockSpec((tm, D_MODEL), lambda i: (i, 0)),
                  _resident((1, D_MODEL)),
                  _layer_block((D_MODEL, 2 * D_FF), layer),
                  _layer_block((D_FF, D_MODEL), layer)],
        out_specs=pl.BlockSpec((tm, D_MODEL), lambda i: (i, 0)),
        out_shape=jax.ShapeDtypeStruct((m, D_MODEL), F32),
        compiler_params=_params(("parallel",), vmem),
        name="ffn",
    )(x2, g.reshape(1, D_MODEL), wgu_all, wd_all)


def _fnet_weight_body(cc_ref, sc_ref, w_ref, o_ref):
    w = w_ref[...]
    o_ref[:, :D_MODEL] = jnp.dot(cc_ref[...], w, preferred_element_type=F32,
                                 precision=lax.Precision.HIGHEST).astype(BF16)
    o_ref[:, D_MODEL:] = jnp.dot(sc_ref[...], w, preferred_element_type=F32,
                                 precision=lax.Precision.HIGHEST).astype(BF16)


def _fnet_weights(w_out):
    n = np.arange(FOURIER_GROUP)
    ang = 2.0 * np.pi * ((n[:, None] * n[None, :]) % FOURIER_GROUP) / FOURIER_GROUP
    cc = jnp.asarray(np.cos(ang) / math.sqrt(FOURIER_GROUP), F32)
    sc = jnp.asarray(np.sin(ang) / math.sqrt(FOURIER_GROUP), F32)
    ng = D_MODEL // FOURIER_GROUP
    return pl.pallas_call(
        _fnet_weight_body,
        grid=(ng,),
        in_specs=[_resident((FOURIER_GROUP, FOURIER_GROUP)),
                  _resident((FOURIER_GROUP, FOURIER_GROUP)),
                  pl.BlockSpec((FOURIER_GROUP, D_MODEL), lambda i: (i, 0))],
        out_specs=pl.BlockSpec((FOURIER_GROUP, 2 * D_MODEL), lambda i: (i, 0)),
        out_shape=jax.ShapeDtypeStruct((D_MODEL, 2 * D_MODEL), BF16),
        compiler_params=_params(("parallel",), 16 << 20),
        name="fnet_weights",
    )(cc, sc, w_out)


FN_HALF = SEQ // 2
FN_BLK = 256
FN_NB = FN_HALF // FN_BLK


def _fnet_tables():
    k = np.arange(FN_HALF)[:, None]
    n = np.arange(FN_HALF)[None, :]
    ang = 2.0 * np.pi * ((k * n) % SEQ) / SEQ
    scale = 1.0 / math.sqrt(SEQ)
    i = np.arange(FN_BLK)
    rev = (i[None, :] == FN_BLK - i[:, None]).astype(np.float32)
    return (jnp.asarray(np.cos(ang) * scale, BF16), jnp.asarray(np.sin(ang) * scale, BF16),
            jnp.asarray(rev, BF16))


def _fnet_body(x_ref, g_ref, cs_ref, ss_ref, w_ref, rev_ref, o_ref, e_sc, d_sc, zc_sc, zs_sc):
    g = g_ref[...]
    scale = 1.0 / math.sqrt(SEQ)
    row = lax.broadcasted_iota(jnp.int32, (FN_BLK, 1), 0)
    first = row == 0
    sign = jnp.where((row & 1) == 0, 1.0, -1.0)
    rev = rev_ref[...]
    blk = lambda a: pl.ds(a * FN_BLK, FN_BLK)
    mirror = lambda a: pl.ds(SEQ - (a + 1) * FN_BLK, FN_BLK)

    alt = jnp.zeros((1, D_MODEL), F32)
    carry = jnp.zeros((1, D_MODEL), F32)
    for a in range(FN_NB):
        lo = _rms(x_ref[blk(a), :], g)
        hi = _rms(x_ref[mirror(a), :], g)
        alt = alt + jnp.sum(sign * lo, axis=0, keepdims=True) + jnp.sum(sign * hi, axis=0, keepdims=True)
        r = jnp.dot(rev, hi.astype(BF16), preferred_element_type=F32)
        r = jnp.where(first, carry, r)
        e_sc[blk(a), :] = (lo + r).astype(BF16)
        d_sc[blk(a), :] = (lo - r).astype(BF16)
        carry = hi[0:1, :]
    h_nyq = carry * scale

    for c in range(FN_NB):
        zc = jnp.dot(cs_ref[blk(c), :], e_sc[...], preferred_element_type=F32) + sign * h_nyq
        zs = jnp.dot(ss_ref[blk(c), :], d_sc[...], preferred_element_type=F32)
        zc_sc[blk(c), :] = zc.astype(BF16)
        zs_sc[blk(c), :] = zs.astype(BF16)

    wc = w_ref[:, :D_MODEL]
    ws = w_ref[:, D_MODEL:]
    z_nyq = jnp.broadcast_to(alt * scale, (8, D_MODEL)).astype(BF16)
    carry = jnp.dot(z_nyq, wc, preferred_element_type=F32)[0:1, :]
    for c in reversed(range(FN_NB)):
        p = jnp.dot(zc_sc[blk(c), :], wc, preferred_element_type=F32)
        q = jnp.dot(zs_sc[blk(c), :], ws, preferred_element_type=F32)
        o_ref[blk(c), :] = x_ref[blk(c), :] + (p - q)
        m = p + q
        m_hi = m.astype(BF16)
        m_lo = (m - m_hi.astype(F32)).astype(BF16)
        r = (jnp.dot(rev, m_hi, preferred_element_type=F32) + jnp.dot(rev, m_lo, preferred_element_type=F32))
        r = jnp.where(first, carry, r)
        o_ref[mirror(c), :] = x_ref[mirror(c), :] + r
        carry = m[0:1, :]


def _fnet_layer(x, g, w_out):
    wcs = _fnet_weights(w_out)
    cs, ss, rev = _fnet_tables()
    seq_block = pl.BlockSpec((None, SEQ, D_MODEL), lambda b: (b, 0, 0))
    half = pltpu.VMEM((FN_HALF, D_MODEL), BF16)
    vmem = 4 * SEQ * D_MODEL * 4 + 4 * FN_HALF * D_MODEL * 2 + 4 * FN_HALF * D_MODEL * 2 + (12 << 20)
    return pl.pallas_call(
        _fnet_body,
        grid=(BATCH,),
        in_specs=[seq_block, _resident((1, D_MODEL)), _resident((FN_HALF, FN_HALF)),
                  _resident((FN_HALF, FN_HALF)), _resident((D_MODEL, 2 * D_MODEL)),
                  _resident((FN_BLK, FN_BLK))],
        out_specs=seq_block,
        out_shape=jax.ShapeDtypeStruct((BATCH, SEQ, D_MODEL), F32),
        scratch_shapes=[half, half, half, half],
        compiler_params=_params(("parallel",), vmem),
        name="fnet_mix",
    )(x, g.reshape(1, D_MODEL), cs, ss, wcs, rev)


def _s5_matrices(lam_re, lam_im, log_dt, b_re, b_im, c_re, c_im):
    lam = lax.complex(lam_re.astype(F32), lam_im.astype(F32))
    dt = jnp.exp(log_dt.astype(F32))[..., None]
    lam_dt = lam * dt
    lam_bar = jnp.exp(lam_dt)
    b_bar = ((lam_bar - 1.0) / lam)[..., None] * lax.complex(b_re.astype(F32), b_im.astype(F32))
    c = lax.complex(c_re.astype(F32), c_im.astype(F32))
    L = S5_CHUNK
    taus = jnp.arange(L + 1, dtype=F32)
    pw = jnp.exp(lam_dt[..., None, :] * taus[:, None])
    def both(fwd, bwd):
        z = jnp.concatenate([fwd, bwd], axis=-1)
        return z.real[:, :, None, :], z.imag[:, :, None, :]
    def outer(pw_pair, mat_pair, sign):
        (ar, ai), (br, bi) = pw_pair, mat_pair
        re = ar * br - ai * bi
        im = ar * bi + ai * br
        return jnp.concatenate([re, sign * im], axis=-1).reshape(S5_GROUPS, -1, S5_CK)
    b_t = b_bar.transpose(0, 1, 3, 2)
    mat = lambda m: tuple(z[:, None, :, 0, :] for z in both(m[0], m[1]))
    s_in = outer(both(pw[0][:, ::-1][:, 1:], pw[1][:, :L]), mat(b_t), 1.0)
    wc_t = outer(both(pw[0][:, 1:], pw[1][:, ::-1][:, :L]), mat(c), -1.0)
    zeros = lambda n: jnp.zeros((S5_GROUPS, n, S5_STATE), pw.dtype)
    lag_f = jnp.concatenate([zeros(L - 1), pw[0][:, :L], zeros(1)], axis=1)
    lag_b = jnp.concatenate([pw[1][:, L - 1:0:-1], pw[1][:, :1], zeros(L)], axis=1)
    c_lag = outer(both(lag_f, lag_b), mat(c), -1.0)
    b_cat = jnp.concatenate(mat(b_t), axis=-1)[:, 0]
    strip = jnp.einsum('gqk,gjk->gqj', b_cat, c_lag, precision=lax.Precision.HIGHEST)
    al = pw[:, :, L]
    coef = jnp.stack([jnp.concatenate([al[0].real, al[1].real], -1),
                      jnp.concatenate([al[0].imag, al[1].imag], -1)], axis=1)
    return strip.astype(F32), s_in.astype(BF16), wc_t.astype(BF16), coef.astype(F32)


def _s5_norm_body(x_ref, g_ref, o_ref):
    h = _rms(x_ref[...], g_ref[...])
    o_ref[...] = h.reshape(o_ref.shape)


def _s5_norm(x, g, tm=512):
    cr = tm // S5_CHUNK
    return pl.pallas_call(
        _s5_norm_body,
        grid=(BATCH, SEQ // tm),
        in_specs=[pl.BlockSpec((None, tm, D_MODEL), lambda b, i: (b, i, 0)),
                  _resident((1, D_MODEL))],
        out_specs=pl.BlockSpec((cr, None, S5_CHUNK, D_MODEL), lambda b, i: (i, b, 0, 0)),
        out_shape=jax.ShapeDtypeStruct((S5_CHUNKS, BATCH, S5_CHUNK, D_MODEL), F32),
        compiler_params=_params(("parallel", "parallel"), 32 << 20),
        name="s5_norm",
    )(x, g.reshape(1, D_MODEL))


def _block_transpose8(a, lane_block):
    a = list(a)
    for d in (4, 2, 1):
        take_lo = (lane_block & d) == 0
        nxt = list(a)
        for i in range(8):
            if i & d:
                continue
            lo, hi = a[i], a[i + d]
            nxt[i] = jnp.where(take_lo, lo, pltpu.roll(hi, S5_GROUP * d, 1))
            nxt[i + d] = jnp.where(take_lo, pltpu.roll(lo, LANES - S5_GROUP * d, 1), hi)
        a = nxt
    return a


def _gelu_tanh(y):
    return 0.5 * y * (1.0 + jnp.tanh(math.sqrt(2.0 / math.pi) * (y + 0.044715 * (y * y * y))))


S5_RB = 64


def _s5_body(h_ref, strip_ref, ws_ref, wc_ref, coef_ref, d_ref, o_ref, x_sc, y_sc, s_sc, hs_sc, toep_sc):
    lane = lax.broadcasted_iota(jnp.int32, (1, LANES), 1)
    lane_block = lane // S5_GROUP
    fwd_lanes = lane < S5_STATE
    n_rb = S5_ROWS // S5_RB
    tok_rb = S5_RB * S5_CHUNK

    def relayout_in(i, carry):
        base = pl.multiple_of(i * tok_rb, tok_rb)
        rows = pl.multiple_of(i * S5_RB, S5_RB)
        for half in range(2):
            a = [h_ref[pl.ds(base + half * 8 + t, S5_RB, stride=S5_CHUNK), :] for t in range(8)]
            xt = _block_transpose8(a, lane_block)
            for gi in range(S5_TILE_GROUPS):
                x_sc[gi, pl.ds(rows, S5_RB), half * LANES:(half + 1) * LANES] = xt[gi].astype(BF16)
        return carry

    lax.fori_loop(0, n_rb, relayout_in, 0)

    nt = (((1,), (1,)), ((), ()))
    for gi in range(S5_TILE_GROUPS):
        strip = strip_ref[gi]
        for t in range(S5_CHUNK):
            off = (S5_CHUNK - 1 - t) * S5_GROUP
            win = strip if off == 0 else pltpu.roll(strip, strip.shape[1] - off, 1)
            toep_sc[t * S5_GROUP:(t + 1) * S5_GROUP, :] = win[:, :S5_CK].astype(BF16)
        xg = x_sc[gi]
        y_sc[gi] = jnp.dot(xg, toep_sc[...], preferred_element_type=F32)
        z = jnp.dot(xg, ws_ref[gi], preferred_element_type=F32)
        s_sc[0] = z[:, :LANES]
        s_sc[1] = z[:, LANES:]
        a_re = coef_ref[gi, 0:1, :]
        a_im = coef_ref[gi, 1:2, :]

        def scan_step(k, st):
            st_re, st_im = st
            rf = pl.multiple_of(k * BATCH, BATCH)
            rb = pl.multiple_of((S5_CHUNKS - 1 - k) * BATCH, BATCH)
            hs_sc[0, pl.ds(rf, BATCH), :S5_STATE] = st_re[:, :S5_STATE]
            hs_sc[1, pl.ds(rf, BATCH), :S5_STATE] = st_im[:, :S5_STATE]
            hs_sc[0, pl.ds(rb, BATCH), S5_STATE:] = st_re[:, S5_STATE:]
            hs_sc[1, pl.ds(rb, BATCH), S5_STATE:] = st_im[:, S5_STATE:]
            s_re = jnp.where(fwd_lanes, s_sc[0, pl.ds(rf, BATCH), :], s_sc[0, pl.ds(rb, BATCH), :])
            s_im = jnp.where(fwd_lanes, s_sc[1, pl.ds(rf, BATCH), :], s_sc[1, pl.ds(rb, BATCH), :])
            n_re = a_re * st_re - a_im * st_im + s_re
            n_im = a_re * st_im + a_im * st_re + s_im
            return n_re, n_im

        zero = jnp.zeros((BATCH, LANES), F32)
        lax.fori_loop(0, S5_CHUNKS, scan_step, (zero, zero))
        hcat = jnp.concatenate([hs_sc[0], hs_sc[1]], axis=1).astype(BF16)
        y_sc[gi] = y_sc[gi] + lax.dot_general(hcat, wc_ref[gi], nt, preferred_element_type=F32)

    d_skip = d_ref[...]

    def relayout_out(i, carry):
        base = pl.multiple_of(i * tok_rb, tok_rb)
        rows = pl.multiple_of(i * S5_RB, S5_RB)
        for half in range(2):
            yg = [y_sc[gi, pl.ds(rows, S5_RB), half * LANES:(half + 1) * LANES] for gi in range(S5_TILE_GROUPS)]
            yt = _block_transpose8(yg, lane_block)
            for t in range(8):
                tok = pl.ds(base + half * 8 + t, S5_RB, stride=S5_CHUNK)
                o_ref[tok, :] = _gelu_tanh(yt[t] + d_skip * h_ref[tok, :])
        return carry

    lax.fori_loop(0, n_rb, relayout_out, 0)


def _s5_core(h2, strip, s_in, wc_t, coef, d_skip):
    n_tiles = S5_GROUPS // S5_TILE_GROUPS
    tok_block = pl.BlockSpec((N_TOKENS, LANES), lambda j: (0, j))
    vmem = (4 * N_TOKENS * LANES * 4 + S5_TILE_GROUPS * S5_ROWS * S5_CK * (2 + 4)
            + 4 * S5_ROWS * LANES * 4 + 2 * S5_TILE_GROUPS * S5_CK * 3 * S5_CK * 2 + (12 << 20))
    return pl.pallas_call(
        _s5_body,
        grid=(n_tiles,),
        in_specs=[tok_block,
                  pl.BlockSpec((S5_TILE_GROUPS, S5_GROUP, 2 * S5_CK), lambda j: (j, 0, 0)),
                  pl.BlockSpec((S5_TILE_GROUPS, S5_CK, S5_CK), lambda j: (j, 0, 0)),
                  pl.BlockSpec((S5_TILE_GROUPS, S5_CK, S5_CK), lambda j: (j, 0, 0)),
                  pl.BlockSpec((S5_TILE_GROUPS, 2, LANES), lambda j: (j, 0, 0)),
                  pl.BlockSpec((1, LANES), lambda j: (0, j))],
        out_specs=tok_block,
        out_shape=jax.ShapeDtypeStruct((N_TOKENS, D_MODEL), F32),
        scratch_shapes=[pltpu.VMEM((S5_TILE_GROUPS, S5_ROWS, S5_CK), BF16),
                        pltpu.VMEM((S5_TILE_GROUPS, S5_ROWS, S5_CK), F32),
                        pltpu.VMEM((2, S5_ROWS, LANES), F32),
                        pltpu.VMEM((2, S5_ROWS, LANES), F32),
                        pltpu.VMEM((S5_CK, S5_CK), BF16)],
        compiler_params=_params(("arbitrary",), vmem),
        name="s5_core",
    )(h2, strip, s_in, wc_t, coef, d_skip.reshape(1, D_MODEL))


def _s5_glu_body(x_ref, a_ref, w_ref, o_ref):
    a = a_ref[...].reshape(x_ref.shape).astype(BF16)
    vg = jnp.dot(a, w_ref[...], preferred_element_type=F32)
    o_ref[...] = x_ref[...] + vg[:, :D_MODEL] * _sigmoid(vg[:, D_MODEL:])


def _s5_glu(x, act, w_glu_all, layer, tm=512):
    cr = tm // S5_CHUNK
    vmem = 2 * D_MODEL * D_MODEL * 2 + 6 * tm * D_MODEL * 4 + 2 * tm * 2 * D_MODEL * 4 + (4 << 20)
    return pl.pallas_call(
        _s5_glu_body,
        grid=(BATCH, SEQ // tm),
        in_specs=[pl.BlockSpec((None, tm, D_MODEL), lambda b, i: (b, i, 0)),
                  pl.BlockSpec((cr, None, S5_CHUNK, D_MODEL), lambda b, i: (i, b, 0, 0)),
                  _layer_block((D_MODEL, 2 * D_MODEL), layer)],
        out_specs=pl.BlockSpec((None, tm, D_MODEL), lambda b, i: (b, i, 0)),
        out_shape=jax.ShapeDtypeStruct((BATCH, SEQ, D_MODEL), F32),
        compiler_params=_params(("parallel", "parallel"), vmem),
        name="s5_glu",
    )(x, act, w_glu_all)


def _s5_layer(x, g, lam_re, lam_im, log_dt, b_re, b_im, c_re, c_im, d_skip, w_glu_all, layer):
    strip, s_in, wc_t, coef = _s5_matrices(lam_re, lam_im, log_dt, b_re, b_im, c_re, c_im)
    h = _s5_norm(x, g)
    act = _s5_core(h.reshape(N_TOKENS, D_MODEL), strip, s_in, wc_t, coef, d_skip)
    act = act.reshape(S5_CHUNKS, BATCH, S5_CHUNK, D_MODEL)
    return _s5_glu(x, act, w_glu_all, layer)


def _t5_bucket(rel):
    half = NUM_BUCKETS // 2
    max_exact = half // 2
    n = np.abs(rel)
    sign = (rel > 0).astype(np.int32) * half
    large = max_exact + (np.log(np.maximum(n, 1) / max_exact) / math.log(MAX_DISTANCE / max_exact)
                         * (half - max_exact)).astype(np.int32)
    large = np.minimum(large, half - 1)
    return (sign + np.where(n < max_exact, n, large)).astype(np.int32)


LOG2E = 1.4426950408889634
N_PAIRS = HEADS_PER_GROUP // 2
ATTN_PAD_ROWS = SEQ + ATTN_BQ * max(d for _, d in DILATED_GROUPS)
QKV_GROUP_WIDTH = 3 * D_MODEL
QKV_SLAB = 2 * LANES
QKV_TILE_ROWS = 512


def _attn_bias_tables(rel_bias):
    n_off = ATTN_BK + ATTN_BQ - 1
    offs = np.arange(n_off) - (ATTN_BQ - 1) - ATTN_SIDE
    strips = []
    for gi, (_, dil) in enumerate(DILATED_GROUPS):
        onehot = jnp.asarray(_t5_bucket(offs * dil)[:, None] == np.arange(NUM_BUCKETS), F32)
        f = jnp.dot(onehot, rel_bias[:, gi * HEADS_PER_GROUP:(gi + 1) * HEADS_PER_GROUP].astype(F32),
                    precision=lax.Precision.HIGHEST).T
        strips.append(jnp.pad(f[:, ::-1], ((0, 0), (0, BIAS_STRIP - n_off))))
    strips = jnp.stack(strips).reshape(N_ATTN_GROUPS, HEADS_PER_GROUP, 1, BIAS_STRIP)
    return pl.pallas_call(
        _attn_bias_body,
        grid=(N_ATTN_GROUPS, N_PAIRS),
        in_specs=[pl.BlockSpec((None, 2, 1, BIAS_STRIP), lambda g, p: (g, p, 0, 0))],
        out_specs=pl.BlockSpec((None, None, ATTN_BK, 2 * ATTN_BQ), lambda g, p: (g, p, 0, 0)),
        out_shape=jax.ShapeDtypeStruct((N_ATTN_GROUPS, N_PAIRS, ATTN_BK, 2 * ATTN_BQ), F32),
        compiler_params=_params(("parallel", "parallel"), 16 << 20),
        name="attn_bias",
    )(strips)


BIAS_STRIP = 512


def _attn_bias_body(f_ref, o_ref):
    krow = lax.broadcasted_iota(jnp.int32, (ATTN_BK, 1), 0)
    qcol = lax.broadcasted_iota(jnp.int32, (1, ATTN_BQ), 1)
    band = jnp.abs(krow - ATTN_SIDE - qcol) <= ATTN_SIDE
    for a in range(2):
        x = jnp.broadcast_to(f_ref[a], (ATTN_BK, BIAS_STRIP))
        t = pltpu.roll(x, BIAS_STRIP - (ATTN_BK - 1), 1, stride=1, stride_axis=0)[:, :ATTN_BQ]
        o_ref[:, a * ATTN_BQ:(a + 1) * ATTN_BQ] = jnp.where(band, t * LOG2E, -1e30)


def _qkv_body(x_ref, g_ref, w_ref, qg_ref, kg_ref, perm_ref, q_out, k_out, v_out, *, n_res):
    rows = x_ref.shape[0] // n_res
    h = _rms(x_ref[...], g_ref[...]).astype(BF16)
    if n_res > 1:
        h = jnp.dot(perm_ref[...], h, preferred_element_type=F32).astype(BF16)
    lane = lax.broadcasted_iota(jnp.int32, (1, LANES), 1)
    head0 = lane < HEAD_DIM

    def head_norm(t, gain):
        sq = t * t
        s0 = jnp.sum(jnp.where(head0, sq, 0.0), axis=-1, keepdims=True)
        s1 = jnp.sum(jnp.where(head0, 0.0, sq), axis=-1, keepdims=True)
        ms = jnp.where(head0, s0, s1) * (1.0 / HEAD_DIM)
        return t * lax.rsqrt(ms + EPS) * gain

    for c in range(QKV_GROUP_WIDTH // QKV_SLAB):
        z = jnp.dot(h, w_ref[:, c * QKV_SLAB:(c + 1) * QKV_SLAB], preferred_element_type=F32)
        for half in range(QKV_SLAB // LANES):
            section, lo = divmod(c * QKV_SLAB + half * LANES, D_MODEL)
            zz = z[:, half * LANES:(half + 1) * LANES]
            if section == 0:
                zz = head_norm(zz, qg_ref[...]) * (HEAD_DIM ** -0.5 * LOG2E)
                qa = jnp.where(head0, zz, 0.0).astype(BF16)
                qb = jnp.where(head0, 0.0, zz).astype(BF16)
            elif section == 1:
                zz = head_norm(zz, kg_ref[...]).astype(BF16)
            else:
                zz = zz.astype(BF16)
            for r in range(n_res):
                piece = slice(r * rows, (r + 1) * rows)
                if section == 0:
                    q_out[r, 0, :, lo:lo + LANES] = qa[piece]
                    q_out[r, 1, :, lo:lo + LANES] = qb[piece]
                elif section == 1:
                    k_out[r, :, lo:lo + LANES] = zz[piece]
                else:
                    v_out[r, :, lo:lo + LANES] = zz[piece]


def _qkv_group(x, g, w_qkv_all, layer, gi, dil, q_gain, k_gain):
    seg = SEQ // dil
    tm = QKV_TILE_ROWS
    rows = tm // dil
    assert tm % dil == 0 and rows % 16 == 0
    gain2 = lambda gn: jnp.tile(gn.astype(F32), 2).reshape(1, LANES)
    src = np.arange(tm).reshape(rows, dil).T.reshape(tm)
    perm = jnp.asarray(src[:, None] == np.arange(tm)[None, :], BF16)
    kv_spec = pl.BlockSpec((None, dil, rows, D_MODEL), lambda b, i: (b, 0, i, 0))
    kv_shape = jax.ShapeDtypeStruct((BATCH, dil, seg, D_MODEL), BF16)
    vmem = (D_MODEL * QKV_GROUP_WIDTH * 2 + 2 * tm * D_MODEL * 4 + 2 * 4 * tm * D_MODEL * 2 + (12 << 20))
    return pl.pallas_call(
        functools.partial(_qkv_body, n_res=dil),
        grid=(BATCH, SEQ // tm),
        in_specs=[pl.BlockSpec((None, tm, D_MODEL), lambda b, i: (b, i, 0)),
                  _resident((1, D_MODEL)),
                  pl.BlockSpec((None, D_MODEL, QKV_GROUP_WIDTH), lambda b, i: (layer, 0, gi),
                               pipeline_mode=pl.Buffered(1)),
                  _resident((1, LANES)), _resident((1, LANES)), _resident((tm, tm))],
        out_specs=[pl.BlockSpec((None, dil, 2, rows, D_MODEL), lambda b, i: (b, 0, 0, i, 0)),
                   kv_spec, kv_spec],
        out_shape=[jax.ShapeDtypeStruct((BATCH, dil, 2, seg, D_MODEL), BF16), kv_shape, kv_shape],
        compiler_params=_params(("parallel", "parallel"), vmem),
        name=f"qkv_dil{dil}",
    )(x, g.reshape(1, D_MODEL), w_qkv_all, gain2(q_gain), gain2(k_gain), perm)


def _attn_body(q0, k0, v0, q1, k1, v1, q2, k2, v2, bias_ref, o_ref, kp_sc, vp_sc, lt_sc, og_sc, lg_sc):
    lane = lax.broadcasted_iota(jnp.int32, (1, LANES), 1)
    head0 = lane < HEAD_DIM
    krow = lax.broadcasted_iota(jnp.int32, (ATTN_BK, 1), 0)
    zpad = jnp.zeros((ATTN_SIDE, LANES), BF16)

    qkv_refs = ((q0, k0, v0), (q1, k1, v1), (q2, k2, v2))
    for g, (_, dil) in enumerate(DILATED_GROUPS):
        q_ref, k_ref, v_ref = qkv_refs[g]
        seg = SEQ // dil
        nb = seg // ATTN_BQ
        pad_seg = seg + 2 * ATTN_SIDE
        lt_sc[...] = jnp.zeros(lt_sc.shape, F32)
        for r in range(dil):
            base = r * pad_seg
            for src, dst in ((k_ref, kp_sc), (v_ref, vp_sc)):
                dst[base:base + ATTN_SIDE] = zpad
                dst[base + ATTN_SIDE + seg:base + pad_seg] = zpad
                dst[base + ATTN_SIDE:base + ATTN_SIDE + seg] = src[r]

        for r in range(dil):
            base = r * pad_seg
            for i in range(nb):
                blk = r * nb + i
                win = slice(base + i * ATTN_BQ, base + i * ATTN_BQ + ATTN_BK)
                qrows = slice(i * ATTN_BQ, (i + 1) * ATTN_BQ)
                qm = jnp.concatenate([q_ref[r, 0, qrows, :], q_ref[r, 1, qrows, :]], axis=0)
                s = lax.dot_general(kp_sc[win], qm, (((1,), (1,)), ((), ())), preferred_element_type=F32)
                s = s + bias_ref[g]
                if i == 0 or i == nb - 1:
                    ok = None
                    if i == 0:
                        ok = krow >= ATTN_SIDE
                    if i == nb - 1:
                        ok_hi = krow < ATTN_BK - ATTN_SIDE
                        ok = ok_hi if ok is None else jnp.logical_and(ok, ok_hi)
                    s = jnp.where(ok, s, -1e30)
                vb = vp_sc[win]
                o_heads, lse_heads = [], []
                for a in range(2):
                    sa = s[:, a * ATTN_BQ:(a + 1) * ATTN_BQ]
                    m = jnp.max(sa, axis=0, keepdims=True)
                    p = jnp.exp2(sa - m)
                    den = jnp.sum(p, axis=0, keepdims=True)
                    pt = (p * (1.0 / den)).T.astype(BF16)
                    o_heads.append(jnp.dot(pt, vb, preferred_element_type=F32))
                    lse_heads.append(m + jnp.log2(den))
                rows = pl.ds(r + i * ATTN_BQ * dil, ATTN_BQ, stride=dil)
                og_sc[g, rows, :] = jnp.where(head0, o_heads[0], o_heads[1])
                lt_sc[8 * blk:8 * blk + 1, :] = lse_heads[0]
                lt_sc[8 * blk + 1:8 * blk + 2, :] = lse_heads[1]

        ltt = lt_sc[...].T
        for r in range(dil):
            for i in range(nb):
                c = 8 * (r * nb + i)
                rows = pl.ds(r + i * ATTN_BQ * dil, ATTN_BQ, stride=dil)
                lg_sc[g, rows, :] = jnp.where(head0, ltt[:, c:c + 1], ltt[:, c + 1:c + 2])

    l0, l1, l2 = lg_sc[0], lg_sc[1], lg_sc[2]
    m = jnp.maximum(jnp.maximum(l0, l1), l2)
    w0, w1, w2 = jnp.exp2(l0 - m), jnp.exp2(l1 - m), jnp.exp2(l2 - m)
    o = (w0 * og_sc[0] + w1 * og_sc[1] + w2 * og_sc[2]) / (w0 + w1 + w2)
    o_ref[...] = o.astype(o_ref.dtype)


def _attn_core(qkv, bias_tabs):
    in_specs = []
    for (_, dil) in DILATED_GROUPS:
        seg = SEQ // dil
        in_specs.append(pl.BlockSpec((None, dil, 2, seg, LANES), lambda b, p: (b, 0, 0, 0, p)))
        in_specs.append(pl.BlockSpec((None, dil, seg, LANES), lambda b, p: (b, 0, 0, p)))
        in_specs.append(pl.BlockSpec((None, dil, seg, LANES), lambda b, p: (b, 0, 0, p)))
    in_specs.append(pl.BlockSpec((N_ATTN_GROUPS, None, ATTN_BK, 2 * ATTN_BQ), lambda b, p: (0, p, 0, 0)))
    return pl.pallas_call(
        _attn_body,
        grid=(BATCH, N_PAIRS),
        in_specs=in_specs,
        out_specs=pl.BlockSpec((None, SEQ, LANES), lambda b, p: (b, 0, p)),
        out_shape=jax.ShapeDtypeStruct((BATCH, SEQ, D_MODEL), BF16),
        scratch_shapes=[pltpu.VMEM((ATTN_PAD_ROWS, LANES), BF16),
                        pltpu.VMEM((ATTN_PAD_ROWS, LANES), BF16),
                        pltpu.VMEM((LANES, LANES), F32),
                        pltpu.VMEM((N_ATTN_GROUPS, SEQ, LANES), F32),
                        pltpu.VMEM((N_ATTN_GROUPS, SEQ, LANES), F32)],
        compiler_params=_params(("parallel", "parallel"), 40 << 20),
        name="attn_core",
    )(*qkv, bias_tabs)


def _proj_residual_body(x_ref, a_ref, w_ref, o_ref):
    o_ref[...] = x_ref[...] + jnp.dot(a_ref[...], w_ref[...], preferred_element_type=F32)


def _proj_residual(x2, a2, w_all, layer, tm=512):
    tile = pl.BlockSpec((tm, D_MODEL), lambda i: (i, 0))
    return pl.pallas_call(
        _proj_residual_body,
        grid=(x2.shape[0] // tm,),
        in_specs=[tile, tile, _layer_block((D_MODEL, D_MODEL), layer)],
        out_specs=tile,
        out_shape=jax.ShapeDtypeStruct(x2.shape, F32),
        compiler_params=_params(("parallel",), 32 << 20),
        name="proj_residual",
    )(x2, a2, w_all)


def _attn_layer(x, g, w_qkv_all, q_gain, k_gain, w_o_all, rel_bias, layer):
    qkv = []
    for gi, (window, dil) in enumerate(DILATED_GROUPS):
        assert (window // 2) // dil == ATTN_SIDE and (SEQ // dil) % ATTN_BQ == 0
        qkv.extend(_qkv_group(x, g, w_qkv_all, layer, gi, dil, q_gain[gi], k_gain[gi]))
    o = _attn_core(qkv, _attn_bias_tables(rel_bias))
    x2 = x.reshape(N_TOKENS, D_MODEL)
    return _proj_residual(x2, o.reshape(N_TOKENS, D_MODEL), w_o_all, layer).reshape(BATCH, SEQ, D_MODEL)


def kernel(x, norm_mix_g, norm_ffn_g, fnet_w_out, s5_lambda_re, s5_lambda_im, s5_log_dt, s5_b_re, s5_b_im, s5_c_re, s5_c_im, s5_d, s5_w_glu, attn_w_qkv, attn_q_gain, attn_k_gain, attn_w_o, rel_bias, ffn_w_gate_up, ffn_w_down):
    w_gate_up, w_down = _cast_bf16(ffn_w_gate_up), _cast_bf16(ffn_w_down)
    w_glu, w_qkv, w_o = _cast_bf16(s5_w_glu), _cast_bf16(attn_w_qkv), _cast_bf16(attn_w_o)
    counts = [0, 0, 0]
    for i in range(DEPTH):
        kind = i % 3
        j = counts[kind]
        counts[kind] += 1
        if kind == 0:
            x = _fnet_layer(x, norm_mix_g[i], fnet_w_out[j])
        elif kind == 1:
            x = _s5_layer(x, norm_mix_g[i], s5_lambda_re[j], s5_lambda_im[j], s5_log_dt[j], s5_b_re[j],
                          s5_b_im[j], s5_c_re[j], s5_c_im[j], s5_d[j], w_glu, j)
        else:
            x = _attn_layer(x, norm_mix_g[i], w_qkv, attn_q_gain[j], attn_k_gain[j], w_o, rel_bias, j)
        x = _ffn(x.reshape(N_TOKENS, D_MODEL), norm_ffn_g[i], w_gate_up, w_down, i).reshape(BATCH, SEQ, D_MODEL)
    return x
```

```python
import functools
import math

import numpy as np
import jax
import jax.numpy as jnp
from jax import lax
from jax.experimental import pallas as pl
from jax.experimental.pallas import tpu as pltpu

F32 = jnp.float32
BF16 = jnp.bfloat16

D_MODEL = 1024
BATCH = 8
SEQ = 2048
DEPTH = 4
N_TOKENS = BATCH * SEQ
EPS = 1e-6
D_FF = 2816
FOURIER_GROUP = 128
S5_GROUP = 16
S5_GROUPS = 64
S5_STATE = 64
HEAD_DIM = 64
HEADS_PER_GROUP = 16
DILATED_GROUPS = ((128, 1), (512, 4), (2048, 16))
N_ATTN_GROUPS = 3
NUM_BUCKETS = 32
MAX_DISTANCE = 1024
ATTN_SIDE = 64

LANES = 128
VMEM_LIMIT_CAP = 60 * 1024 * 1024

S5_CHUNK = 16
S5_CHUNKS = SEQ // S5_CHUNK
S5_ROWS = BATCH * S5_CHUNKS
S5_TILE_GROUPS = LANES // S5_GROUP
S5_CK = S5_CHUNK * S5_GROUP

ATTN_BQ = 64
ATTN_BK = ATTN_BQ + 2 * ATTN_SIDE


def _params(sem, vmem_bytes):
    return pltpu.CompilerParams(dimension_semantics=sem,
                                vmem_limit_bytes=int(min(VMEM_LIMIT_CAP, vmem_bytes)))


def _rms(x, g):
    ms = jnp.mean(x * x, axis=-1, keepdims=True)
    return x * lax.rsqrt(ms + EPS) * g


def _sigmoid(x):
    return 1.0 / (1.0 + jnp.exp(-x))


def _resident(shape):
    nd = len(shape)
    return pl.BlockSpec(shape, lambda *_: (0,) * nd, pipeline_mode=pl.Buffered(1))


def _cast_body(w_ref, o_ref):
    o_ref[...] = w_ref[...].astype(BF16)


def _cast_bf16(w, rows=256):
    n_layers, n_rows, n_cols = w.shape
    block = pl.BlockSpec((None, rows, n_cols), lambda l, i: (l, i, 0))
    return pl.pallas_call(
        _cast_body,
        grid=(n_layers, n_rows // rows),
        in_specs=[block],
        out_specs=block,
        out_shape=jax.ShapeDtypeStruct(w.shape, BF16),
        compiler_params=_params(("parallel", "parallel"), 4 * rows * n_cols * 6 + (4 << 20)),
        name="cast_bf16",
    )(w)


def _ffn_body(x_ref, g_ref, wgu_ref, wd_ref, o_ref):
    x = x_ref[...]
    h = _rms(x, g_ref[...]).astype(BF16)
    gu = jnp.dot(h, wgu_ref[...], preferred_element_type=F32)
    gate = gu[:, :D_FF]
    up = gu[:, D_FF:]
    a = (gate * _sigmoid(gate) * up).astype(BF16)
    o_ref[...] = x + jnp.dot(a, wd_ref[...], preferred_element_type=F32)


def _layer_block(shape, layer):
    return pl.BlockSpec((None,) + shape, lambda *_: (layer,) + (0,) * len(shape), pipeline_mode=pl.Buffered(1))


def _ffn(x2, g, wgu_all, wd_all, layer, tm=512):
    m = x2.shape[0]
    vmem = 3 * D_MODEL * D_FF * 2 + 4 * tm * D_MODEL * 4 + 4 * tm * 2 * D_FF * 4 + (4 << 20)
    return pl.pallas_call(
        _ffn_body,
        grid=(m // tm,),
        in_specs=[pl.BlockSpec((tm, D_MODEL), lambda i: (i, 0)),
                  _resident((1, D_MODEL)),
                  _layer_block((D_MODEL, 2 * D_FF), layer),
                  _layer_block((D_FF, D_MODEL), layer)],
        out_specs=pl.BlockSpec((tm, D_MODEL), lambda i: (i, 0)),
        out_shape=jax.ShapeDtypeStruct((m, D_MODEL), F32),
        compiler_params=_params(("parallel",), vmem),
        name="ffn",
    )(x2, g.reshape(1, D_MODEL), wgu_all, wd_all)


def _fnet_weight_body(cc_ref, sc_ref, w_ref, o_ref):
    w = w_ref[...]
    o_ref[:, :D_MODEL] = jnp.dot(cc_ref[...], w, preferred_element_type=F32,
                                 precision=lax.Precision.HIGHEST).astype(BF16)
    o_ref[:, D_MODEL:] = jnp.dot(sc_ref[...], w, preferred_element_type=F32,
                                 precision=lax.Precision.HIGHEST).astype(BF16)


def _fnet_weights(w_out):
    n = np.arange(FOURIER_GROUP)
    ang = 2.0 * np.pi * ((n[:, None] * n[None, :]) % FOURIER_GROUP) / FOURIER_GROUP
    cc = jnp.asarray(np.cos(ang) / math.sqrt(FOURIER_GROUP), F32)
    sc = jnp.asarray(np.sin(ang) / math.sqrt(FOURIER_GROUP), F32)
    ng = D_MODEL // FOURIER_GROUP
    return pl.pallas_call(
        _fnet_weight_body,
        grid=(ng,),
        in_specs=[_resident((FOURIER_GROUP, FOURIER_GROUP)),
                  _resident((FOURIER_GROUP, FOURIER_GROUP)),
                  pl.BlockSpec((FOURIER_GROUP, D_MODEL), lambda i: (i, 0))],
        out_specs=pl.BlockSpec((FOURIER_GROUP, 2 * D_MODEL), lambda i: (i, 0)),
        out_shape=jax.ShapeDtypeStruct((D_MODEL, 2 * D_MODEL), BF16),
        compiler_params=_params(("parallel",), 16 << 20),
        name="fnet_weights",
    )(cc, sc, w_out)


FN_HALF = SEQ // 2
FN_BLK = 256
FN_NB = FN_HALF // FN_BLK


def _fnet_tables():
    k = np.arange(FN_HALF)[:, None]
    n = np.arange(FN_HALF)[None, :]
    ang = 2.0 * np.pi * ((k * n) % SEQ) / SEQ
    scale = 1.0 / math.sqrt(SEQ)
    i = np.arange(FN_BLK)
    rev = (i[None, :] == FN_BLK - i[:, None]).astype(np.float32)
    return (jnp.asarray(np.cos(ang) * scale, BF16), jnp.asarray(np.sin(ang) * scale, BF16),
            jnp.asarray(rev, BF16))


def _fnet_body(x_ref, g_ref, cs_ref, ss_ref, w_ref, rev_ref, o_ref, e_sc, d_sc, zc_sc, zs_sc):
    g = g_ref[...]
    scale = 1.0 / math.sqrt(SEQ)
    row = lax.broadcasted_iota(jnp.int32, (FN_BLK, 1), 0)
    first = row == 0
    sign = jnp.where((row & 1) == 0, 1.0, -1.0)
    rev = rev_ref[...]
    blk = lambda a: pl.ds(a * FN_BLK, FN_BLK)
    mirror = lambda a: pl.ds(SEQ - (a + 1) * FN_BLK, FN_BLK)

    alt = jnp.zeros((1, D_MODEL), F32)
    carry = jnp.zeros((1, D_MODEL), F32)
    for a in range(FN_NB):
        lo = _rms(x_ref[blk(a), :], g)
        hi = _rms(x_ref[mirror(a), :], g)
        alt = alt + jnp.sum(sign * lo, axis=0, keepdims=True) + jnp.sum(sign * hi, axis=0, keepdims=True)
        r = jnp.dot(rev, hi.astype(BF16), preferred_element_type=F32)
        r = jnp.where(first, carry, r)
        e_sc[blk(a), :] = (lo + r).astype(BF16)
        d_sc[blk(a), :] = (lo - r).astype(BF16)
        carry = hi[0:1, :]
    h_nyq = carry * scale

    for c in range(FN_NB):
        zc = jnp.dot(cs_ref[blk(c), :], e_sc[...], preferred_element_type=F32) + sign * h_nyq
        zs = jnp.dot(ss_ref[blk(c), :], d_sc[...], preferred_element_type=F32)
        zc_sc[blk(c), :] = zc.astype(BF16)
        zs_sc[blk(c), :] = zs.astype(BF16)

    wc = w_ref[:, :D_MODEL]
    ws = w_ref[:, D_MODEL:]
    z_nyq = jnp.broadcast_to(alt * scale, (8, D_MODEL)).astype(BF16)
    carry = jnp.dot(z_nyq, wc, preferred_element_type=F32)[0:1, :]
    for c in reversed(range(FN_NB)):
        p = jnp.dot(zc_sc[blk(c), :], wc, preferred_element_type=F32)
        q = jnp.dot(zs_sc[blk(c), :], ws, preferred_element_type=F32)
        o_ref[blk(c), :] = x_ref[blk(c), :] + (p - q)
        m = p + q
        m_hi = m.astype(BF16)
        m_lo = (m - m_hi.astype(F32)).astype(BF16)
        r = (jnp.dot(rev, m_hi, preferred_element_type=F32) + jnp.dot(rev, m_lo, preferred_element_type=F32))
        r = jnp.where(first, carry, r)
        o_ref[mirror(c), :] = x_ref[mirror(c), :] + r
        carry = m[0:1, :]


def _fnet_layer(x, g, w_out):
    wcs = _fnet_weights(w_out)
    cs, ss, rev = _fnet_tables()
    seq_block = pl.BlockSpec((None, SEQ, D_MODEL), lambda b: (b, 0, 0))
    half = pltpu.VMEM((FN_HALF, D_MODEL), BF16)
    vmem = 4 * SEQ * D_MODEL * 4 + 4 * FN_HALF * D_MODEL * 2 + 4 * FN_HALF * D_MODEL * 2 + (12 << 20)
    return pl.pallas_call(
        _fnet_body,
        grid=(BATCH,),
        in_specs=[seq_block, _resident((1, D_MODEL)), _resident((FN_HALF, FN_HALF)),
                  _resident((FN_HALF, FN_HALF)), _resident((D_MODEL, 2 * D_MODEL)),
                  _resident((FN_BLK, FN_BLK))],
        out_specs=seq_block,
        out_shape=jax.ShapeDtypeStruct((BATCH, SEQ, D_MODEL), F32),
        scratch_shapes=[half, half, half, half],
        compiler_params=_params(("parallel",), vmem),
        name="fnet_mix",
    )(x, g.reshape(1, D_MODEL), cs, ss, wcs, rev)


def _s5_matrices(lam_re, lam_im, log_dt, b_re, b_im, c_re, c_im):
    lam = lax.complex(lam_re.astype(F32), lam_im.astype(F32))
    dt = jnp.exp(log_dt.astype(F32))[..., None]
    lam_dt = lam * dt
    lam_bar = jnp.exp(lam_dt)
    b_bar = ((lam_bar - 1.0) / lam)[..., None] * lax.complex(b_re.astype(F32), b_im.astype(F32))
    c = lax.complex(c_re.astype(F32), c_im.astype(F32))
    L = S5_CHUNK
    taus = jnp.arange(L + 1, dtype=F32)
    pw = jnp.exp(lam_dt[..., None, :] * taus[:, None])
    def both(fwd, bwd):
        z = jnp.concatenate([fwd, bwd], axis=-1)
        return z.real[:, :, None, :], z.imag[:, :, None, :]
    def outer(pw_pair, mat_pair, sign):
        (ar, ai), (br, bi) = pw_pair, mat_pair
        re = ar * br - ai * bi
        im = ar * bi + ai * br
        return jnp.concatenate([re, sign * im], axis=-1).reshape(S5_GROUPS, -1, S5_CK)
    b_t = b_bar.transpose(0, 1, 3, 2)
    mat = lambda m: tuple(z[:, None, :, 0, :] for z in both(m[0], m[1]))
    s_in = outer(both(pw[0][:, ::-1][:, 1:], pw[1][:, :L]), mat(b_t), 1.0)
    wc_t = outer(both(pw[0][:, 1:], pw[1][:, ::-1][:, :L]), mat(c), -1.0)
    zeros = lambda n: jnp.zeros((S5_GROUPS, n, S5_STATE), pw.dtype)
    lag_f = jnp.concatenate([zeros(L - 1), pw[0][:, :L], zeros(1)], axis=1)
    lag_b = jnp.concatenate([pw[1][:, L - 1:0:-1], pw[1][:, :1], zeros(L)], axis=1)
    c_lag = outer(both(lag_f, lag_b), mat(c), -1.0)
    b_cat = jnp.concatenate(mat(b_t), axis=-1)[:, 0]
    strip = jnp.einsum('gqk,gjk->gqj', b_cat, c_lag, precision=lax.Precision.HIGHEST)
    al = pw[:, :, L]
    coef = jnp.stack([jnp.concatenate([al[0].real, al[1].real], -1),
                      jnp.concatenate([al[0].imag, al[1].imag], -1)], axis=1)
    return strip.astype(F32), s_in.astype(BF16), wc_t.astype(BF16), coef.astype(F32)


def _s5_norm_body(x_ref, g_ref, o_ref):
    h = _rms(x_ref[...], g_ref[...])
    o_ref[...] = h.reshape(o_ref.shape)


def _s5_norm(x, g, tm=512):
    cr = tm // S5_CHUNK
    return pl.pallas_call(
        _s5_norm_body,
        grid=(BATCH, SEQ // tm),
        in_specs=[pl.BlockSpec((None, tm, D_MODEL), lambda b, i: (b, i, 0)),
                  _resident((1, D_MODEL))],
        out_specs=pl.BlockSpec((cr, None, S5_CHUNK, D_MODEL), lambda b, i: (i, b, 0, 0)),
        out_shape=jax.ShapeDtypeStruct((S5_CHUNKS, BATCH, S5_CHUNK, D_MODEL), F32),
        compiler_params=_params(("parallel", "parallel"), 32 << 20),
        name="s5_norm",
    )(x, g.reshape(1, D_MODEL))


def _block_transpose8(a, lane_block):
    a = list(a)
    for d in (4, 2, 1):
        take_lo = (lane_block & d) == 0
        nxt = list(a)
        for i in range(8):
            if i & d:
                continue
            lo, hi = a[i], a[i + d]
            nxt[i] = jnp.where(take_lo, lo, pltpu.roll(hi, S5_GROUP * d, 1))
            nxt[i + d] = jnp.where(take_lo, pltpu.roll(lo, LANES - S5_GROUP * d, 1), hi)
        a = nxt
    return a


def _gelu_tanh(y):
    return 0.5 * y * (1.0 + jnp.tanh(math.sqrt(2.0 / math.pi) * (y + 0.044715 * (y * y * y))))


S5_RB = 64


def _s5_body(h_ref, strip_ref, ws_ref, wc_ref, coef_ref, d_ref, o_ref, x_sc, y_sc, s_sc, hs_sc, toep_sc):
    lane = lax.broadcasted_iota(jnp.int32, (1, LANES), 1)
    lane_block = lane // S5_GROUP
    fwd_lanes = lane < S5_STATE
    n_rb = S5_ROWS // S5_RB
    tok_rb = S5_RB * S5_CHUNK

    def relayout_in(i, carry):
        base = pl.multiple_of(i * tok_rb, tok_rb)
        rows = pl.multiple_of(i * S5_RB, S5_RB)
        for half in range(2):
            a = [h_ref[pl.ds(base + half * 8 + t, S5_RB, stride=S5_CHUNK), :] for t in range(8)]
            xt = _block_transpose8(a, lane_block)
            for gi in range(S5_TILE_GROUPS):
                x_sc[gi, pl.ds(rows, S5_RB), half * LANES:(half + 1) * LANES] = xt[gi].astype(BF16)
        return carry

    lax.fori_loop(0, n_rb, relayout_in, 0)

    nt = (((1,), (1,)), ((), ()))
    for gi in range(S5_TILE_GROUPS):
        strip = strip_ref[gi]
        for t in range(S5_CHUNK):
            off = (S5_CHUNK - 1 - t) * S5_GROUP
            win = strip if off == 0 else pltpu.roll(strip, strip.shape[1] - off, 1)
            toep_sc[t * S5_GROUP:(t + 1) * S5_GROUP, :] = win[:, :S5_CK].astype(BF16)
        xg = x_sc[gi]
        y_sc[gi] = jnp.dot(xg, toep_sc[...], preferred_element_type=F32)
        z = jnp.dot(xg, ws_ref[gi], preferred_element_type=F32)
        s_sc[0] = z[:, :LANES]
        s_sc[1] = z[:, LANES:]
        a_re = coef_ref[gi, 0:1, :]
        a_im = coef_ref[gi, 1:2, :]

        def scan_step(k, st):
            st_re, st_im = st
            rf = pl.multiple_of(k * BATCH, BATCH)
            rb = pl.multiple_of((S5_CHUNKS - 1 - k) * BATCH, BATCH)
            hs_sc[0, pl.ds(rf, BATCH), :S5_STATE] = st_re[:, :S5_STATE]
            hs_sc[1, pl.ds(rf, BATCH), :S5_STATE] = st_im[:, :S5_STATE]
            hs_sc[0, pl.ds(rb, BATCH), S5_STATE:] = st_re[:, S5_STATE:]
            hs_sc[1, pl.ds(rb, BATCH), S5_STATE:] = st_im[:, S5_STATE:]
            s_re = jnp.where(fwd_lanes, s_sc[0, pl.ds(rf, BATCH), :], s_sc[0, pl.ds(rb, BATCH), :])
            s_im = jnp.where(fwd_lanes, s_sc[1, pl.ds(rf, BATCH), :], s_sc[1, pl.ds(rb, BATCH), :])
            n_re = a_re * st_re - a_im * st_im + s_re
            n_im = a_re * st_im + a_im * st_re + s_im
            return n_re, n_im

        zero = jnp.zeros((BATCH, LANES), F32)
        lax.fori_loop(0, S5_CHUNKS, scan_step, (zero, zero))
        hcat = jnp.concatenate([hs_sc[0], hs_sc[1]], axis=1).astype(BF16)
        y_sc[gi] = y_sc[gi] + lax.dot_general(hcat, wc_ref[gi], nt, preferred_element_type=F32)

    d_skip = d_ref[...]

    def relayout_out(i, carry):
        base = pl.multiple_of(i * tok_rb, tok_rb)
        rows = pl.multiple_of(i * S5_RB, S5_RB)
        for half in range(2):
            yg = [y_sc[gi, pl.ds(rows, S5_RB), half * LANES:(half + 1) * LANES] for gi in range(S5_TILE_GROUPS)]
            yt = _block_transpose8(yg, lane_block)
            for t in range(8):
                tok = pl.ds(base + half * 8 + t, S5_RB, stride=S5_CHUNK)
                o_ref[tok, :] = _gelu_tanh(yt[t] + d_skip * h_ref[tok, :])
        return carry

    lax.fori_loop(0, n_rb, relayout_out, 0)


def _s5_core(h2, strip, s_in, wc_t, coef, d_skip):
    n_tiles = S5_GROUPS // S5_TILE_GROUPS
    tok_block = pl.BlockSpec((N_TOKENS, LANES), lambda j: (0, j))
    vmem = (4 * N_TOKENS * LANES * 4 + S5_TILE_GROUPS * S5_ROWS * S5_CK * (2 + 4)
            + 4 * S5_ROWS * LANES * 4 + 2 * S5_TILE_GROUPS * S5_CK * 3 * S5_CK * 2 + (12 << 20))
    return pl.pallas_call(
        _s5_body,
        grid=(n_tiles,),
        in_specs=[tok_block,
                  pl.BlockSpec((S5_TILE_GROUPS, S5_GROUP, 2 * S5_CK), lambda j: (j, 0, 0)),
                  pl.BlockSpec((S5_TILE_GROUPS, S5_CK, S5_CK), lambda j: (j, 0, 0)),
                  pl.BlockSpec((S5_TILE_GROUPS, S5_CK, S5_CK), lambda j: (j, 0, 0)),
                  pl.BlockSpec((S5_TILE_GROUPS, 2, LANES), lambda j: (j, 0, 0)),
                  pl.BlockSpec((1, LANES), lambda j: (0, j))],
        out_specs=tok_block,
        out_shape=jax.ShapeDtypeStruct((N_TOKENS, D_MODEL), F32),
        scratch_shapes=[pltpu.VMEM((S5_TILE_GROUPS, S5_ROWS, S5_CK), BF16),
                        pltpu.VMEM((S5_TILE_GROUPS, S5_ROWS, S5_CK), F32),
                        pltpu.VMEM((2, S5_ROWS, LANES), F32),
                        pltpu.VMEM((2, S5_ROWS, LANES), F32),
                        pltpu.VMEM((S5_CK, S5_CK), BF16)],
        compiler_params=_params(("arbitrary",), vmem),
        name="s5_core",
    )(h2, strip, s_in, wc_t, coef, d_skip.reshape(1, D_MODEL))


def _s5_glu_body(x_ref, a_ref, w_ref, o_ref):
    a = a_ref[...].reshape(x_ref.shape).astype(BF16)
    vg = jnp.dot(a, w_ref[...], preferred_element_type=F32)
    o_ref[...] = x_ref[...] + vg[:, :D_MODEL] * _sigmoid(vg[:, D_MODEL:])


def _s5_glu(x, act, w_glu_all, layer, tm=512):
    cr = tm // S5_CHUNK
    vmem = 2 * D_MODEL * D_MODEL * 2 + 6 * tm * D_MODEL * 4 + 2 * tm * 2 * D_MODEL * 4 + (4 << 20)
    return pl.pallas_call(
        _s5_glu_body,
        grid=(BATCH, SEQ // tm),
        in_specs=[pl.BlockSpec((None, tm, D_MODEL), lambda b, i: (b, i, 0)),
                  pl.BlockSpec((cr, None, S5_CHUNK, D_MODEL), lambda b, i: (i, b, 0, 0)),
                  _layer_block((D_MODEL, 2 * D_MODEL), layer)],
        out_specs=pl.BlockSpec((None, tm, D_MODEL), lambda b, i: (b, i, 0)),
        out_shape=jax.ShapeDtypeStruct((BATCH, SEQ, D_MODEL), F32),
        compiler_params=_params(("parallel", "parallel"), vmem),
        name="s5_glu",
    )(x, act, w_glu_all)


def _s5_layer(x, g, lam_re, lam_im, log_dt, b_re, b_im, c_re, c_im, d_skip, w_glu_all, layer):
    strip, s_in, wc_t, coef = _s5_matrices(lam_re, lam_im, log_dt, b_re, b_im, c_re, c_im)
    h = _s5_norm(x, g)
    act = _s5_core(h.reshape(N_TOKENS, D_MODEL), strip, s_in, wc_t, coef, d_skip)
    act = act.reshape(S5_CHUNKS, BATCH, S5_CHUNK, D_MODEL)
    return _s5_glu(x, act, w_glu_all, layer)


def _t5_bucket(rel):
    half = NUM_BUCKETS // 2
    max_exact = half // 2
    n = np.abs(rel)
    sign = (rel > 0).astype(np.int32) * half
    large = max_exact + (np.log(np.maximum(n, 1) / max_exact) / math.log(MAX_DISTANCE / max_exact)
                         * (half - max_exact)).astype(np.int32)
    large = np.minimum(large, half - 1)
    return (sign + np.where(n < max_exact, n, large)).astype(np.int32)


LOG2E = 1.4426950408889634
N_PAIRS = HEADS_PER_GROUP // 2
ATTN_PAD_ROWS = SEQ + 2 * ATTN_SIDE * max(d for _, d in DILATED_GROUPS)
QKV_GROUP_WIDTH = 3 * D_MODEL
QKV_SLAB = 2 * LANES
QKV_TILE_ROWS = 512


def _attn_bias_tables(rel_bias):
    n_off = ATTN_BK + ATTN_BQ - 1
    offs = np.arange(n_off) - (ATTN_BQ - 1) - ATTN_SIDE
    strips = []
    for gi, (_, dil) in enumerate(DILATED_GROUPS):
        onehot = jnp.asarray(_t5_bucket(offs * dil)[:, None] == np.arange(NUM_BUCKETS), F32)
        f = jnp.dot(onehot, rel_bias[:, gi * HEADS_PER_GROUP:(gi + 1) * HEADS_PER_GROUP].astype(F32),
                    precision=lax.Precision.HIGHEST).T
        strips.append(jnp.pad(f[:, ::-1], ((0, 0), (0, BIAS_STRIP - n_off))))
    strips = jnp.stack(strips).reshape(N_ATTN_GROUPS, HEADS_PER_GROUP, 1, BIAS_STRIP)
    return pl.pallas_call(
        _attn_bias_body,
        grid=(N_ATTN_GROUPS, N_PAIRS),
        in_specs=[pl.BlockSpec((None, 2, 1, BIAS_STRIP), lambda g, p: (g, p, 0, 0))],
        out_specs=pl.BlockSpec((None, None, ATTN_BK, 2 * ATTN_BQ), lambda g, p: (g, p, 0, 0)),
        out_shape=jax.ShapeDtypeStruct((N_ATTN_GROUPS, N_PAIRS, ATTN_BK, 2 * ATTN_BQ), F32),
        compiler_params=_params(("parallel", "parallel"), 16 << 20),
        name="attn_bias",
    )(strips)


BIAS_STRIP = 256


def _attn_bias_body(f_ref, o_ref):
    krow = lax.broadcasted_iota(jnp.int32, (ATTN_BK, 1), 0)
    lane = lax.broadcasted_iota(jnp.int32, (1, LANES), 1)
    head0 = lane < ATTN_BQ
    band = jnp.abs(krow - ATTN_SIDE - (lane & (ATTN_BQ - 1))) <= ATTN_SIDE
    halves = []
    for a in range(2):
        x = jnp.broadcast_to(f_ref[a], (ATTN_BK, BIAS_STRIP))
        shift = (a * ATTN_BQ - (ATTN_BK - 1)) % BIAS_STRIP
        halves.append(pltpu.roll(x, shift, 1, stride=1, stride_axis=0)[:, :LANES])
    o_ref[...] = jnp.where(band, jnp.where(head0, halves[0], halves[1]) * LOG2E, -1e30)


def _qkv_body(x_ref, g_ref, w_ref, qg_ref, kg_ref, perm_ref, q_out, k_out, v_out, *, n_res):
    rows = x_ref.shape[0] // n_res
    h = _rms(x_ref[...], g_ref[...]).astype(BF16)
    if n_res > 1:
        h = jnp.dot(perm_ref[...], h, preferred_element_type=F32).astype(BF16)
    lane = lax.broadcasted_iota(jnp.int32, (1, LANES), 1)
    head0 = lane < HEAD_DIM

    def head_norm(t, gain):
        sq = t * t
        s0 = jnp.sum(jnp.where(head0, sq, 0.0), axis=-1, keepdims=True)
        s1 = jnp.sum(jnp.where(head0, 0.0, sq), axis=-1, keepdims=True)
        ms = jnp.where(head0, s0, s1) * (1.0 / HEAD_DIM)
        return t * lax.rsqrt(ms + EPS) * gain

    for c in range(QKV_GROUP_WIDTH // QKV_SLAB):
        z = jnp.dot(h, w_ref[:, c * QKV_SLAB:(c + 1) * QKV_SLAB], preferred_element_type=F32)
        for half in range(QKV_SLAB // LANES):
            section, lo = divmod(c * QKV_SLAB + half * LANES, D_MODEL)
            zz = z[:, half * LANES:(half + 1) * LANES]
            if section == 0:
                zz = head_norm(zz, qg_ref[...]) * (HEAD_DIM ** -0.5 * LOG2E)
                qa = jnp.where(head0, zz, 0.0).astype(BF16)
                qb = jnp.where(head0, 0.0, zz).astype(BF16)
            elif section == 1:
                zz = head_norm(zz, kg_ref[...]).astype(BF16)
            else:
                zz = zz.astype(BF16)
            for r in range(n_res):
                piece = slice(r * rows, (r + 1) * rows)
                if section == 0:
                    q_out[r, 0, :, lo:lo + LANES] = qa[piece]
                    q_out[r, 1, :, lo:lo + LANES] = qb[piece]
                elif section == 1:
                    k_out[r, :, lo:lo + LANES] = zz[piece]
                else:
                    v_out[r, :, lo:lo + LANES] = zz[piece]


def _qkv_group(x, g, w_qkv_all, layer, gi, dil, q_gain, k_gain):
    seg = SEQ // dil
    tm = QKV_TILE_ROWS
    rows = tm // dil
    assert tm % dil == 0 and rows % 16 == 0
    gain2 = lambda gn: jnp.tile(gn.astype(F32), 2).reshape(1, LANES)
    src = np.arange(tm).reshape(rows, dil).T.reshape(tm)
    perm = jnp.asarray(src[:, None] == np.arange(tm)[None, :], BF16)
    kv_spec = pl.BlockSpec((None, dil, rows, D_MODEL), lambda b, i: (b, 0, i, 0))
    kv_shape = jax.ShapeDtypeStruct((BATCH, dil, seg, D_MODEL), BF16)
    vmem = (D_MODEL * QKV_GROUP_WIDTH * 2 + 2 * tm * D_MODEL * 4 + 2 * 4 * tm * D_MODEL * 2 + (12 << 20))
    return pl.pallas_call(
        functools.partial(_qkv_body, n_res=dil),
        grid=(BATCH, SEQ // tm),
        in_specs=[pl.BlockSpec((None, tm, D_MODEL), lambda b, i: (b, i, 0)),
                  _resident((1, D_MODEL)),
                  pl.BlockSpec((None, D_MODEL, QKV_GROUP_WIDTH), lambda b, i: (layer, 0, gi),
                               pipeline_mode=pl.Buffered(1)),
                  _resident((1, LANES)), _resident((1, LANES)), _resident((tm, tm))],
        out_specs=[pl.BlockSpec((None, dil, 2, rows, D_MODEL), lambda b, i: (b, 0, 0, i, 0)),
                   kv_spec, kv_spec],
        out_shape=[jax.ShapeDtypeStruct((BATCH, dil, 2, seg, D_MODEL), BF16), kv_shape, kv_shape],
        compiler_params=_params(("parallel", "parallel"), vmem),
        name=f"qkv_dil{dil}",
    )(x, g.reshape(1, D_MODEL), w_qkv_all, gain2(q_gain), gain2(k_gain), perm)


ATTN_AHEAD = 3


def _attn_body(q0, k0, v0, q1, k1, v1, q2, k2, v2, bias_ref, o_ref, kp_sc, vp_sc, lt_sc, og_sc, lg_sc):
    lane = lax.broadcasted_iota(jnp.int32, (1, LANES), 1)
    head0 = lane < HEAD_DIM
    krow = lax.broadcasted_iota(jnp.int32, (ATTN_BK, 1), 0)
    zpad = jnp.zeros((ATTN_SIDE, LANES), BF16)
    tn = (((0,), (0,)), ((), ()))

    qkv_refs = ((q0, k0, v0), (q1, k1, v1), (q2, k2, v2))
    for g, (_, dil) in enumerate(DILATED_GROUPS):
        q_ref, k_ref, v_ref = qkv_refs[g]
        seg = SEQ // dil
        nb = seg // ATTN_BQ
        pad_seg = seg + 2 * ATTN_SIDE
        lt_sc[...] = jnp.zeros(lt_sc.shape, F32)
        for r in range(dil):
            base = r * pad_seg
            for src, dst in ((k_ref, kp_sc), (v_ref, vp_sc)):
                dst[base:base + ATTN_SIDE] = zpad
                dst[base + ATTN_SIDE + seg:base + pad_seg] = zpad
                dst[base + ATTN_SIDE:base + ATTN_SIDE + seg] = src[r]

        def scores(r, i):
            base = r * pad_seg
            win = slice(base + i * ATTN_BQ, base + i * ATTN_BQ + ATTN_BK)
            qrows = slice(i * ATTN_BQ, (i + 1) * ATTN_BQ)
            qm = jnp.concatenate([q_ref[r, 0, qrows, :], q_ref[r, 1, qrows, :]], axis=0)
            s = lax.dot_general(kp_sc[win], qm, (((1,), (1,)), ((), ())), preferred_element_type=F32)
            s = s + bias_ref[g]
            if i == 0 or i == nb - 1:
                ok = None
                if i == 0:
                    ok = krow >= ATTN_SIDE
                if i == nb - 1:
                    ok_hi = krow < ATTN_BK - ATTN_SIDE
                    ok = ok_hi if ok is None else jnp.logical_and(ok, ok_hi)
                s = jnp.where(ok, s, -1e30)
            return s

        blocks = [(r, i) for r in range(dil) for i in range(nb)]
        pending = [scores(*blocks[j]) for j in range(min(ATTN_AHEAD, len(blocks)))]
        for blk, (r, i) in enumerate(blocks):
            if blk + ATTN_AHEAD < len(blocks):
                pending.append(scores(*blocks[blk + ATTN_AHEAD]))
            s = pending.pop(0)
            base = r * pad_seg
            win = slice(base + i * ATTN_BQ, base + i * ATTN_BQ + ATTN_BK)
            m = jnp.max(s, axis=0, keepdims=True)
            p = jnp.exp2(s - m)
            den = jnp.sum(p, axis=0, keepdims=True)
            pn = (p * (1.0 / den)).astype(BF16)
            u = lax.dot_general(pn, vp_sc[win], tn, preferred_element_type=F32)
            rows = pl.ds(r + i * ATTN_BQ * dil, ATTN_BQ, stride=dil)
            og_sc[g, rows, :] = jnp.where(head0, u[:ATTN_BQ], u[ATTN_BQ:])
            lt_sc[blk:blk + 1, :] = m + jnp.log2(den)

        ltt = lt_sc[...].T
        for r in range(dil):
            for i in range(nb):
                c = r * nb + i
                rows = pl.ds(r + i * ATTN_BQ * dil, ATTN_BQ, stride=dil)
                lg_sc[g, rows, :] = jnp.where(head0, ltt[:ATTN_BQ, c:c + 1], ltt[ATTN_BQ:, c:c + 1])

    l0, l1, l2 = lg_sc[0], lg_sc[1], lg_sc[2]
    m = jnp.maximum(jnp.maximum(l0, l1), l2)
    w0, w1, w2 = jnp.exp2(l0 - m), jnp.exp2(l1 - m), jnp.exp2(l2 - m)
    o = (w0 * og_sc[0] + w1 * og_sc[1] + w2 * og_sc[2]) / (w0 + w1 + w2)
    o_ref[...] = o.astype(o_ref.dtype)


def _attn_core(qkv, bias_tabs):
    in_specs = []
    for (_, dil) in DILATED_GROUPS:
        seg = SEQ // dil
        in_specs.append(pl.BlockSpec((None, dil, 2, seg, LANES), lambda b, p: (b, 0, 0, 0, p)))
        in_specs.append(pl.BlockSpec((None, dil, seg, LANES), lambda b, p: (b, 0, 0, p)))
        in_specs.append(pl.BlockSpec((None, dil, seg, LANES), lambda b, p: (b, 0, 0, p)))
    in_specs.append(pl.BlockSpec((N_ATTN_GROUPS, None, ATTN_BK, 2 * ATTN_BQ), lambda b, p: (0, p, 0, 0)))
    return pl.pallas_call(
        _attn_body,
        grid=(BATCH, N_PAIRS),
        in_specs=in_specs,
        out_specs=pl.BlockSpec((None, SEQ, LANES), lambda b, p: (b, 0, p)),
        out_shape=jax.ShapeDtypeStruct((BATCH, SEQ, D_MODEL), BF16),
        scratch_shapes=[pltpu.VMEM((ATTN_PAD_ROWS, LANES), BF16),
                        pltpu.VMEM((ATTN_PAD_ROWS, LANES), BF16),
                        pltpu.VMEM((LANES, LANES), F32),
                        pltpu.VMEM((N_ATTN_GROUPS, SEQ, LANES), F32),
                        pltpu.VMEM((N_ATTN_GROUPS, SEQ, LANES), F32)],
        compiler_params=_params(("parallel", "parallel"), 40 << 20),
        name="attn_core",
    )(*qkv, bias_tabs)


def _proj_residual_body(x_ref, a_ref, w_ref, o_ref):
    o_ref[...] = x_ref[...] + jnp.dot(a_ref[...], w_ref[...], preferred_element_type=F32)


def _proj_residual(x2, a2, w_all, layer, tm=512):
    tile = pl.BlockSpec((tm, D_MODEL), lambda i: (i, 0))
    return pl.pallas_call(
        _proj_residual_body,
        grid=(x2.shape[0] // tm,),
        in_specs=[tile, tile, _layer_block((D_MODEL, D_MODEL), layer)],
        out_specs=tile,
        out_shape=jax.ShapeDtypeStruct(x2.shape, F32),
        compiler_params=_params(("parallel",), 32 << 20),
        name="proj_residual",
    )(x2, a2, w_all)


def _attn_layer(x, g, w_qkv_all, q_gain, k_gain, w_o_all, rel_bias, layer):
    qkv = []
    for gi, (window, dil) in enumerate(DILATED_GROUPS):
        assert (window // 2) // dil == ATTN_SIDE and (SEQ // dil) % ATTN_BQ == 0
        qkv.extend(_qkv_group(x, g, w_qkv_all, layer, gi, dil, q_gain[gi], k_gain[gi]))
    o = _attn_core(qkv, _attn_bias_tables(rel_bias))
    x2 = x.reshape(N_TOKENS, D_MODEL)
    return _proj_residual(x2, o.reshape(N_TOKENS, D_MODEL), w_o_all, layer).reshape(BATCH, SEQ, D_MODEL)


def kernel(x, norm_mix_g, norm_ffn_g, fnet_w_out, s5_lambda_re, s5_lambda_im, s5_log_dt, s5_b_re, s5_b_im, s5_c_re, s5_c_im, s5_d, s5_w_glu, attn_w_qkv, attn_q_gain, attn_k_gain, attn_w_o, rel_bias, ffn_w_gate_up, ffn_w_down):
    w_gate_up, w_down = _cast_bf16(ffn_w_gate_up), _cast_bf16(ffn_w_down)
    w_glu, w_qkv, w_o = _cast_bf16(s5_w_glu), _cast_bf16(attn_w_qkv), _cast_bf16(attn_w_o)
    counts = [0, 0, 0]
    for i in range(DEPTH):
        kind = i % 3
        j = counts[kind]
        counts[kind] += 1
        if kind == 0:
            x = _fnet_layer(x, norm_mix_g[i], fnet_w_out[j])
        elif kind == 1:
            x = _s5_layer(x, norm_mix_g[i], s5_lambda_re[j], s5_lambda_im[j], s5_log_dt[j], s5_b_re[j],
                          s5_b_im[j], s5_c_re[j], s5_c_im[j], s5_d[j], w_glu, j)
        else:
            x = _attn_layer(x, norm_mix_g[i], w_qkv, attn_q_gain[j], attn_k_gain[j], w_o, rel_bias, j)
        x = _ffn(x.reshape(N_TOKENS, D_MODEL), norm_ffn_g[i], w_gate_up, w_down, i).reshape(BATCH, SEQ, D_MODEL)
    return x
```

```python
import functools
import math

import numpy as np
import jax
import jax.numpy as jnp
from jax import lax
from jax.experimental import pallas as pl
from jax.experimental.pallas import tpu as pltpu

F32 = jnp.float32
BF16 = jnp.bfloat16

D_MODEL = 1024
BATCH = 8
SEQ = 2048
DEPTH = 4
N_TOKENS = BATCH * SEQ
EPS = 1e-6
D_FF = 2816
FOURIER_GROUP = 128
S5_GROUP = 16
S5_GROUPS = 64
S5_STATE = 64
HEAD_DIM = 64
HEADS_PER_GROUP = 16
DILATED_GROUPS = ((128, 1), (512, 4), (2048, 16))
N_ATTN_GROUPS = 3
NUM_BUCKETS = 32
MAX_DISTANCE = 1024
ATTN_SIDE = 64

LANES = 128
VMEM_LIMIT_CAP = 60 * 1024 * 1024

S5_CHUNK = 16
S5_CHUNKS = SEQ // S5_CHUNK
S5_ROWS = BATCH * S5_CHUNKS
S5_TILE_GROUPS = LANES // S5_GROUP
S5_CK = S5_CHUNK * S5_GROUP

ATTN_BQ = 64
ATTN_BK = ATTN_BQ + 2 * ATTN_SIDE


def _params(sem, vmem_bytes):
    return pltpu.CompilerParams(dimension_semantics=sem,
                                vmem_limit_bytes=int(min(VMEM_LIMIT_CAP, vmem_bytes)))


def _rms(x, g):
    ms = jnp.mean(x * x, axis=-1, keepdims=True)
    return x * lax.rsqrt(ms + EPS) * g


def _sigmoid(x):
    return 1.0 / (1.0 + jnp.exp(-x))


def _resident(shape):
    nd = len(shape)
    return pl.BlockSpec(shape, lambda *_: (0,) * nd, pipeline_mode=pl.Buffered(1))


def _cast_body(w_ref, o_ref):
    o_ref[...] = w_ref[...].astype(BF16)


def _cast_bf16(w, rows=256):
    n_layers, n_rows, n_cols = w.shape
    block = pl.BlockSpec((None, rows, n_cols), lambda l, i: (l, i, 0))
    return pl.pallas_call(
        _cast_body,
        grid=(n_layers, n_rows // rows),
        in_specs=[block],
        out_specs=block,
        out_shape=jax.ShapeDtypeStruct(w.shape, BF16),
        compiler_params=_params(("parallel", "parallel"), 4 * rows * n_cols * 6 + (4 << 20)),
        name="cast_bf16",
    )(w)


FFN_CHUNK = 512


def _ffn_body(x_ref, g_ref, wgu_ref, wd_ref, o_ref):
    x = x_ref[...]
    h = _rms(x, g_ref[...]).astype(BF16)
    acc = x
    for c in range(0, D_FF, FFN_CHUNK):
        w = min(FFN_CHUNK, D_FF - c)
        gate = jnp.dot(h, wgu_ref[:, c:c + w].astype(BF16), preferred_element_type=F32)
        up = jnp.dot(h, wgu_ref[:, D_FF + c:D_FF + c + w].astype(BF16), preferred_element_type=F32)
        a = (gate * _sigmoid(gate) * up).astype(BF16)
        acc = acc + jnp.dot(a, wd_ref[c:c + w, :].astype(BF16), preferred_element_type=F32)
    o_ref[...] = acc


def _layer_block(shape, layer):
    return pl.BlockSpec((None,) + shape, lambda *_: (layer,) + (0,) * len(shape), pipeline_mode=pl.Buffered(1))


def _ffn(x2, g, wgu_all, wd_all, layer, tm=512):
    m = x2.shape[0]
    vmem = 3 * D_MODEL * D_FF * 4 + 6 * tm * D_MODEL * 4 + (10 << 20)
    return pl.pallas_call(
        _ffn_body,
        grid=(m // tm,),
        in_specs=[pl.BlockSpec((tm, D_MODEL), lambda i: (i, 0)),
                  _resident((1, D_MODEL)),
                  _layer_block((D_MODEL, 2 * D_FF), layer),
                  _layer_block((D_FF, D_MODEL), layer)],
        out_specs=pl.BlockSpec((tm, D_MODEL), lambda i: (i, 0)),
        out_shape=jax.ShapeDtypeStruct((m, D_MODEL), F32),
        compiler_params=_params(("parallel",), vmem),
        name="ffn",
    )(x2, g.reshape(1, D_MODEL), wgu_all, wd_all)


def _fnet_weight_body(cc_ref, sc_ref, w_ref, o_ref):
    w = w_ref[...]
    o_ref[:, :D_MODEL] = jnp.dot(cc_ref[...], w, preferred_element_type=F32,
                                 precision=lax.Precision.HIGHEST).astype(BF16)
    o_ref[:, D_MODEL:] = jnp.dot(sc_ref[...], w, preferred_element_type=F32,
                                 precision=lax.Precision.HIGHEST).astype(BF16)


def _fnet_weights(w_out):
    n = np.arange(FOURIER_GROUP)
    ang = 2.0 * np.pi * ((n[:, None] * n[None, :]) % FOURIER_GROUP) / FOURIER_GROUP
    cc = jnp.asarray(np.cos(ang) / math.sqrt(FOURIER_GROUP), F32)
    sc = jnp.asarray(np.sin(ang) / math.sqrt(FOURIER_GROUP), F32)
    ng = D_MODEL // FOURIER_GROUP
    return pl.pallas_call(
        _fnet_weight_body,
        grid=(ng,),
        in_specs=[_resident((FOURIER_GROUP, FOURIER_GROUP)),
                  _resident((FOURIER_GROUP, FOURIER_GROUP)),
                  pl.BlockSpec((FOURIER_GROUP, D_MODEL), lambda i: (i, 0))],
        out_specs=pl.BlockSpec((FOURIER_GROUP, 2 * D_MODEL), lambda i: (i, 0)),
        out_shape=jax.ShapeDtypeStruct((D_MODEL, 2 * D_MODEL), BF16),
        compiler_params=_params(("parallel",), 16 << 20),
        name="fnet_weights",
    )(cc, sc, w_out)


FN_HALF = SEQ // 2
FN_BLK = 256
FN_NB = FN_HALF // FN_BLK


def _fnet_tables():
    k = np.arange(FN_HALF)[:, None]
    n = np.arange(FN_HALF)[None, :]
    ang = 2.0 * np.pi * ((k * n) % SEQ) / SEQ
    scale = 1.0 / math.sqrt(SEQ)
    i = np.arange(FN_BLK)
    rev = (i[None, :] == FN_BLK - i[:, None]).astype(np.float32)
    return (jnp.asarray(np.cos(ang) * scale, BF16), jnp.asarray(np.sin(ang) * scale, BF16),
            jnp.asarray(rev, BF16))


def _fnet_body(x_ref, g_ref, cs_ref, ss_ref, w_ref, rev_ref, o_ref, e_sc, d_sc, zc_sc, zs_sc):
    g = g_ref[...]
    scale = 1.0 / math.sqrt(SEQ)
    row = lax.broadcasted_iota(jnp.int32, (FN_BLK, 1), 0)
    first = row == 0
    sign = jnp.where((row & 1) == 0, 1.0, -1.0)
    rev = rev_ref[...]
    blk = lambda a: pl.ds(a * FN_BLK, FN_BLK)
    mirror = lambda a: pl.ds(SEQ - (a + 1) * FN_BLK, FN_BLK)

    alt = jnp.zeros((1, D_MODEL), F32)
    carry = jnp.zeros((1, D_MODEL), F32)
    for a in range(FN_NB):
        lo = _rms(x_ref[blk(a), :], g)
        hi = _rms(x_ref[mirror(a), :], g)
        alt = alt + jnp.sum(sign * lo, axis=0, keepdims=True) + jnp.sum(sign * hi, axis=0, keepdims=True)
        r = jnp.dot(rev, hi.astype(BF16), preferred_element_type=F32)
        r = jnp.where(first, carry, r)
        e_sc[blk(a), :] = (lo + r).astype(BF16)
        d_sc[blk(a), :] = (lo - r).astype(BF16)
        carry = hi[0:1, :]
    h_nyq = carry * scale

    for c in range(FN_NB):
        zc = jnp.dot(cs_ref[blk(c), :], e_sc[...], preferred_element_type=F32) + sign * h_nyq
        zs = jnp.dot(ss_ref[blk(c), :], d_sc[...], preferred_element_type=F32)
        zc_sc[blk(c), :] = zc.astype(BF16)
        zs_sc[blk(c), :] = zs.astype(BF16)

    wc = w_ref[:, :D_MODEL]
    ws = w_ref[:, D_MODEL:]
    z_nyq = jnp.broadcast_to(alt * scale, (8, D_MODEL)).astype(BF16)
    carry = jnp.dot(z_nyq, wc, preferred_element_type=F32)[0:1, :]
    def products(c):
        return (jnp.dot(zc_sc[blk(c), :], wc, preferred_element_type=F32),
                jnp.dot(zs_sc[blk(c), :], ws, preferred_element_type=F32))

    ahead = products(FN_NB - 1)
    for c in reversed(range(FN_NB)):
        p, q = ahead
        if c > 0:
            ahead = products(c - 1)
        o_ref[blk(c), :] = x_ref[blk(c), :] + (p - q)
        m = p + q
        m_hi = m.astype(BF16)
        m_lo = (m - m_hi.astype(F32)).astype(BF16)
        r = (jnp.dot(rev, m_hi, preferred_element_type=F32) + jnp.dot(rev, m_lo, preferred_element_type=F32))
        r = jnp.where(first, carry, r)
        o_ref[mirror(c), :] = x_ref[mirror(c), :] + r
        carry = m[0:1, :]


def _fnet_layer(x, g, w_out):
    wcs = _fnet_weights(w_out)
    cs, ss, rev = _fnet_tables()
    seq_block = pl.BlockSpec((None, SEQ, D_MODEL), lambda b: (b, 0, 0))
    half = pltpu.VMEM((FN_HALF, D_MODEL), BF16)
    vmem = 4 * SEQ * D_MODEL * 4 + 4 * FN_HALF * D_MODEL * 2 + 4 * FN_HALF * D_MODEL * 2 + (12 << 20)
    return pl.pallas_call(
        _fnet_body,
        grid=(BATCH,),
        in_specs=[seq_block, _resident((1, D_MODEL)), _resident((FN_HALF, FN_HALF)),
                  _resident((FN_HALF, FN_HALF)), _resident((D_MODEL, 2 * D_MODEL)),
                  _resident((FN_BLK, FN_BLK))],
        out_specs=seq_block,
        out_shape=jax.ShapeDtypeStruct((BATCH, SEQ, D_MODEL), F32),
        scratch_shapes=[half, half, half, half],
        compiler_params=_params(("parallel",), vmem),
        name="fnet_mix",
    )(x, g.reshape(1, D_MODEL), cs, ss, wcs, rev)


def _s5_matrices(lam_re, lam_im, log_dt, b_re, b_im, c_re, c_im):
    lam = lax.complex(lam_re.astype(F32), lam_im.astype(F32))
    dt = jnp.exp(log_dt.astype(F32))[..., None]
    lam_dt = lam * dt
    lam_bar = jnp.exp(lam_dt)
    b_bar = ((lam_bar - 1.0) / lam)[..., None] * lax.complex(b_re.astype(F32), b_im.astype(F32))
    c = lax.complex(c_re.astype(F32), c_im.astype(F32))
    L = S5_CHUNK
    taus = jnp.arange(L + 1, dtype=F32)
    pw = jnp.exp(lam_dt[..., None, :] * taus[:, None])
    def both(fwd, bwd):
        z = jnp.concatenate([fwd, bwd], axis=-1)
        return z.real[:, :, None, :], z.imag[:, :, None, :]
    def outer(pw_pair, mat_pair, sign):
        (ar, ai), (br, bi) = pw_pair, mat_pair
        re = ar * br - ai * bi
        im = ar * bi + ai * br
        return jnp.concatenate([re, sign * im], axis=-1).reshape(S5_GROUPS, -1, S5_CK)
    b_t = b_bar.transpose(0, 1, 3, 2)
    mat = lambda m: tuple(z[:, None, :, 0, :] for z in both(m[0], m[1]))
    s_in = outer(both(pw[0][:, ::-1][:, 1:], pw[1][:, :L]), mat(b_t), 1.0)
    wc_t = outer(both(pw[0][:, 1:], pw[1][:, ::-1][:, :L]), mat(c), -1.0)
    zeros = lambda n: jnp.zeros((S5_GROUPS, n, S5_STATE), pw.dtype)
    lag_f = jnp.concatenate([zeros(L - 1), pw[0][:, :L], zeros(1)], axis=1)
    lag_b = jnp.concatenate([pw[1][:, L - 1:0:-1], pw[1][:, :1], zeros(L)], axis=1)
    c_lag = outer(both(lag_f, lag_b), mat(c), -1.0)
    b_cat = jnp.concatenate(mat(b_t), axis=-1)[:, 0]
    strip = jnp.einsum('gqk,gjk->gqj', b_cat, c_lag, precision=lax.Precision.HIGHEST)
    al = pw[:, :, L]
    coef = jnp.stack([jnp.concatenate([al[0].real, al[1].real], -1),
                      jnp.concatenate([al[0].imag, al[1].imag], -1)], axis=1)
    return strip.astype(F32), s_in.astype(BF16), wc_t.astype(BF16), coef.astype(F32)


def _s5_norm_body(x_ref, g_ref, o_ref):
    h = _rms(x_ref[...], g_ref[...])
    o_ref[...] = h.reshape(o_ref.shape)


def _s5_norm(x, g, tm=512):
    cr = tm // S5_CHUNK
    return pl.pallas_call(
        _s5_norm_body,
        grid=(BATCH, SEQ // tm),
        in_specs=[pl.BlockSpec((None, tm, D_MODEL), lambda b, i: (b, i, 0)),
                  _resident((1, D_MODEL))],
        out_specs=pl.BlockSpec((cr, None, S5_CHUNK, D_MODEL), lambda b, i: (i, b, 0, 0)),
        out_shape=jax.ShapeDtypeStruct((S5_CHUNKS, BATCH, S5_CHUNK, D_MODEL), F32),
        compiler_params=_params(("parallel", "parallel"), 32 << 20),
        name="s5_norm",
    )(x, g.reshape(1, D_MODEL))


def _block_transpose8(a, lane_block):
    a = list(a)
    for d in (4, 2, 1):
        take_lo = (lane_block & d) == 0
        nxt = list(a)
        for i in range(8):
            if i & d:
                continue
            lo, hi = a[i], a[i + d]
            nxt[i] = jnp.where(take_lo, lo, pltpu.roll(hi, S5_GROUP * d, 1))
            nxt[i + d] = jnp.where(take_lo, pltpu.roll(lo, LANES - S5_GROUP * d, 1), hi)
        a = nxt
    return a


def _gelu_tanh(y):
    return 0.5 * y * (1.0 + jnp.tanh(math.sqrt(2.0 / math.pi) * (y + 0.044715 * (y * y * y))))


S5_RB = 64


def _s5_body(h_ref, strip_ref, ws_ref, wc_ref, coef_ref, d_ref, o_ref, x_sc, y_sc, s_sc, hs_sc, toep_sc):
    lane = lax.broadcasted_iota(jnp.int32, (1, LANES), 1)
    lane_block = lane // S5_GROUP
    fwd_lanes = lane < S5_STATE
    n_rb = S5_ROWS // S5_RB
    tok_rb = S5_RB * S5_CHUNK

    def relayout_in(i, carry):
        base = pl.multiple_of(i * tok_rb, tok_rb)
        rows = pl.multiple_of(i * S5_RB, S5_RB)
        for half in range(2):
            a = [h_ref[pl.ds(base + half * 8 + t, S5_RB, stride=S5_CHUNK), :] for t in range(8)]
            xt = _block_transpose8(a, lane_block)
            for gi in range(S5_TILE_GROUPS):
                x_sc[gi, pl.ds(rows, S5_RB), half * LANES:(half + 1) * LANES] = xt[gi].astype(BF16)
        return carry

    lax.fori_loop(0, n_rb, relayout_in, 0)

    nt = (((1,), (1,)), ((), ()))
    for gi in range(S5_TILE_GROUPS):
        strip = strip_ref[gi]
        for t in range(S5_CHUNK):
            off = (S5_CHUNK - 1 - t) * S5_GROUP
            win = strip if off == 0 else pltpu.roll(strip, strip.shape[1] - off, 1)
            toep_sc[t * S5_GROUP:(t + 1) * S5_GROUP, :] = win[:, :S5_CK].astype(BF16)
        xg = x_sc[gi]
        y_sc[gi] = jnp.dot(xg, toep_sc[...], preferred_element_type=F32)
        z = jnp.dot(xg, ws_ref[gi], preferred_element_type=F32)
        s_sc[0] = z[:, :LANES]
        s_sc[1] = z[:, LANES:]
        a_re = coef_ref[gi, 0:1, :]
        a_im = coef_ref[gi, 1:2, :]

        def scan_step(k, st):
            st_re, st_im = st
            rf = pl.multiple_of(k * BATCH, BATCH)
            rb = pl.multiple_of((S5_CHUNKS - 1 - k) * BATCH, BATCH)
            hs_sc[0, pl.ds(rf, BATCH), :S5_STATE] = st_re[:, :S5_STATE]
            hs_sc[1, pl.ds(rf, BATCH), :S5_STATE] = st_im[:, :S5_STATE]
            hs_sc[0, pl.ds(rb, BATCH), S5_STATE:] = st_re[:, S5_STATE:]
            hs_sc[1, pl.ds(rb, BATCH), S5_STATE:] = st_im[:, S5_STATE:]
            s_re = jnp.where(fwd_lanes, s_sc[0, pl.ds(rf, BATCH), :], s_sc[0, pl.ds(rb, BATCH), :])
            s_im = jnp.where(fwd_lanes, s_sc[1, pl.ds(rf, BATCH), :], s_sc[1, pl.ds(rb, BATCH), :])
            n_re = a_re * st_re - a_im * st_im + s_re
            n_im = a_re * st_im + a_im * st_re + s_im
            return n_re, n_im

        zero = jnp.zeros((BATCH, LANES), F32)
        lax.fori_loop(0, S5_CHUNKS, scan_step, (zero, zero))
        hcat = jnp.concatenate([hs_sc[0], hs_sc[1]], axis=1).astype(BF16)
        y_sc[gi] = y_sc[gi] + lax.dot_general(hcat, wc_ref[gi], nt, preferred_element_type=F32)

    d_skip = d_ref[...]

    def relayout_out(i, carry):
        base = pl.multiple_of(i * tok_rb, tok_rb)
        rows = pl.multiple_of(i * S5_RB, S5_RB)
        for half in range(2):
            yg = [y_sc[gi, pl.ds(rows, S5_RB), half * LANES:(half + 1) * LANES] for gi in range(S5_TILE_GROUPS)]
            yt = _block_transpose8(yg, lane_block)
            for t in range(8):
                tok = pl.ds(base + half * 8 + t, S5_RB, stride=S5_CHUNK)
                o_ref[tok, :] = _gelu_tanh(yt[t] + d_skip * h_ref[tok, :])
        return carry

    lax.fori_loop(0, n_rb, relayout_out, 0)


def _s5_core(h2, strip, s_in, wc_t, coef, d_skip):
    n_tiles = S5_GROUPS // S5_TILE_GROUPS
    tok_block = pl.BlockSpec((N_TOKENS, LANES), lambda j: (0, j))
    vmem = (4 * N_TOKENS * LANES * 4 + S5_TILE_GROUPS * S5_ROWS * S5_CK * (2 + 4)
            + 4 * S5_ROWS * LANES * 4 + 2 * S5_TILE_GROUPS * S5_CK * 3 * S5_CK * 2 + (12 << 20))
    return pl.pallas_call(
        _s5_body,
        grid=(n_tiles,),
        in_specs=[tok_block,
                  pl.BlockSpec((S5_TILE_GROUPS, S5_GROUP, 2 * S5_CK), lambda j: (j, 0, 0)),
                  pl.BlockSpec((S5_TILE_GROUPS, S5_CK, S5_CK), lambda j: (j, 0, 0)),
                  pl.BlockSpec((S5_TILE_GROUPS, S5_CK, S5_CK), lambda j: (j, 0, 0)),
                  pl.BlockSpec((S5_TILE_GROUPS, 2, LANES), lambda j: (j, 0, 0)),
                  pl.BlockSpec((1, LANES), lambda j: (0, j))],
        out_specs=tok_block,
        out_shape=jax.ShapeDtypeStruct((N_TOKENS, D_MODEL), F32),
        scratch_shapes=[pltpu.VMEM((S5_TILE_GROUPS, S5_ROWS, S5_CK), BF16),
                        pltpu.VMEM((S5_TILE_GROUPS, S5_ROWS, S5_CK), F32),
                        pltpu.VMEM((2, S5_ROWS, LANES), F32),
                        pltpu.VMEM((2, S5_ROWS, LANES), F32),
                        pltpu.VMEM((S5_CK, S5_CK), BF16)],
        compiler_params=_params(("arbitrary",), vmem),
        name="s5_core",
    )(h2, strip, s_in, wc_t, coef, d_skip.reshape(1, D_MODEL))


def _s5_glu_body(x_ref, a_ref, w_ref, o_ref):
    a = a_ref[...].reshape(x_ref.shape).astype(BF16)
    vg = jnp.dot(a, w_ref[...], preferred_element_type=F32)
    o_ref[...] = x_ref[...] + vg[:, :D_MODEL] * _sigmoid(vg[:, D_MODEL:])


def _s5_glu(x, act, w_glu_all, layer, tm=512):
    cr = tm // S5_CHUNK
    vmem = 2 * D_MODEL * D_MODEL * 2 + 6 * tm * D_MODEL * 4 + 2 * tm * 2 * D_MODEL * 4 + (4 << 20)
    return pl.pallas_call(
        _s5_glu_body,
        grid=(BATCH, SEQ // tm),
        in_specs=[pl.BlockSpec((None, tm, D_MODEL), lambda b, i: (b, i, 0)),
                  pl.BlockSpec((cr, None, S5_CHUNK, D_MODEL), lambda b, i: (i, b, 0, 0)),
                  _layer_block((D_MODEL, 2 * D_MODEL), layer)],
        out_specs=pl.BlockSpec((None, tm, D_MODEL), lambda b, i: (b, i, 0)),
        out_shape=jax.ShapeDtypeStruct((BATCH, SEQ, D_MODEL), F32),
        compiler_params=_params(("parallel", "parallel"), vmem),
        name="s5_glu",
    )(x, act, w_glu_all)


def _s5_layer(x, g, lam_re, lam_im, log_dt, b_re, b_im, c_re, c_im, d_skip, w_glu_all, layer):
    strip, s_in, wc_t, coef = _s5_matrices(lam_re, lam_im, log_dt, b_re, b_im, c_re, c_im)
    h = _s5_norm(x, g)
    act = _s5_core(h.reshape(N_TOKENS, D_MODEL), strip, s_in, wc_t, coef, d_skip)
    act = act.reshape(S5_CHUNKS, BATCH, S5_CHUNK, D_MODEL)
    return _s5_glu(x, act, w_glu_all, layer)


def _t5_bucket(rel):
    half = NUM_BUCKETS // 2
    max_exact = half // 2
    n = np.abs(rel)
    sign = (rel > 0).astype(np.int32) * half
    large = max_exact + (np.log(np.maximum(n, 1) / max_exact) / math.log(MAX_DISTANCE / max_exact)
                         * (half - max_exact)).astype(np.int32)
    large = np.minimum(large, half - 1)
    return (sign + np.where(n < max_exact, n, large)).astype(np.int32)


LOG2E = 1.4426950408889634
N_PAIRS = HEADS_PER_GROUP // 2
ATTN_PAD_ROWS = SEQ + 2 * ATTN_SIDE * max(d for _, d in DILATED_GROUPS)
QKV_GROUP_WIDTH = 3 * D_MODEL
QKV_SLAB = 2 * LANES
QKV_TILE_ROWS = 512


def _attn_bias_tables(rel_bias):
    n_off = ATTN_BK + ATTN_BQ - 1
    offs = np.arange(n_off) - (ATTN_BQ - 1) - ATTN_SIDE
    strips = []
    for gi, (_, dil) in enumerate(DILATED_GROUPS):
        onehot = jnp.asarray(_t5_bucket(offs * dil)[:, None] == np.arange(NUM_BUCKETS), F32)
        f = jnp.dot(onehot, rel_bias[:, gi * HEADS_PER_GROUP:(gi + 1) * HEADS_PER_GROUP].astype(F32),
                    precision=lax.Precision.HIGHEST).T
        strips.append(jnp.pad(f[:, ::-1], ((0, 0), (0, BIAS_STRIP - n_off))))
    strips = jnp.stack(strips).reshape(N_ATTN_GROUPS, HEADS_PER_GROUP, 1, BIAS_STRIP)
    return pl.pallas_call(
        _attn_bias_body,
        grid=(N_ATTN_GROUPS, N_PAIRS),
        in_specs=[pl.BlockSpec((None, 2, 1, BIAS_STRIP), lambda g, p: (g, p, 0, 0))],
        out_specs=pl.BlockSpec((None, None, ATTN_BK, 2 * ATTN_BQ), lambda g, p: (g, p, 0, 0)),
        out_shape=jax.ShapeDtypeStruct((N_ATTN_GROUPS, N_PAIRS, ATTN_BK, 2 * ATTN_BQ), F32),
        compiler_params=_params(("parallel", "parallel"), 16 << 20),
        name="attn_bias",
    )(strips)


BIAS_STRIP = 256


def _attn_bias_body(f_ref, o_ref):
    krow = lax.broadcasted_iota(jnp.int32, (ATTN_BK, 1), 0)
    lane = lax.broadcasted_iota(jnp.int32, (1, LANES), 1)
    head0 = lane < ATTN_BQ
    band = jnp.abs(krow - ATTN_SIDE - (lane & (ATTN_BQ - 1))) <= ATTN_SIDE
    halves = []
    for a in range(2):
        x = jnp.broadcast_to(f_ref[a], (ATTN_BK, BIAS_STRIP))
        shift = (a * ATTN_BQ - (ATTN_BK - 1)) % BIAS_STRIP
        halves.append(pltpu.roll(x, shift, 1, stride=1, stride_axis=0)[:, :LANES])
    o_ref[...] = jnp.where(band, jnp.where(head0, halves[0], halves[1]) * LOG2E, -1e30)


def _qkv_body(x_ref, g_ref, w_ref, qg_ref, kg_ref, perm_ref, q_out, k_out, v_out, *, n_res):
    rows = x_ref.shape[0] // n_res
    h = _rms(x_ref[...], g_ref[...]).astype(BF16)
    if n_res > 1:
        h = jnp.dot(perm_ref[...], h, preferred_element_type=F32).astype(BF16)
    lane = lax.broadcasted_iota(jnp.int32, (1, LANES), 1)
    head0 = lane < HEAD_DIM

    def head_norm(t, gain):
        sq = t * t
        s0 = jnp.sum(jnp.where(head0, sq, 0.0), axis=-1, keepdims=True)
        s1 = jnp.sum(jnp.where(head0, 0.0, sq), axis=-1, keepdims=True)
        ms = jnp.where(head0, s0, s1) * (1.0 / HEAD_DIM)
        return t * lax.rsqrt(ms + EPS) * gain

    for c in range(QKV_GROUP_WIDTH // QKV_SLAB):
        z = jnp.dot(h, w_ref[:, c * QKV_SLAB:(c + 1) * QKV_SLAB].astype(BF16), preferred_element_type=F32)
        for half in range(QKV_SLAB // LANES):
            section, lo = divmod(c * QKV_SLAB + half * LANES, D_MODEL)
            zz = z[:, half * LANES:(half + 1) * LANES]
            if section == 0:
                zz = head_norm(zz, qg_ref[...]) * (HEAD_DIM ** -0.5 * LOG2E)
                qa = jnp.where(head0, zz, 0.0).astype(BF16)
                qb = jnp.where(head0, 0.0, zz).astype(BF16)
            elif section == 1:
                zz = head_norm(zz, kg_ref[...]).astype(BF16)
            else:
                zz = zz.astype(BF16)
            for r in range(n_res):
                piece = slice(r * rows, (r + 1) * rows)
                if section == 0:
                    q_out[r, 0, :, lo:lo + LANES] = qa[piece]
                    q_out[r, 1, :, lo:lo + LANES] = qb[piece]
                elif section == 1:
                    k_out[r, :, lo:lo + LANES] = zz[piece]
                else:
                    v_out[r, :, lo:lo + LANES] = zz[piece]


def _qkv_group(x, g, w_qkv_all, layer, gi, dil, q_gain, k_gain):
    seg = SEQ // dil
    tm = QKV_TILE_ROWS
    rows = tm // dil
    assert tm % dil == 0 and rows % 16 == 0
    gain2 = lambda gn: jnp.tile(gn.astype(F32), 2).reshape(1, LANES)
    src = np.arange(tm).reshape(rows, dil).T.reshape(tm)
    perm = jnp.asarray(src[:, None] == np.arange(tm)[None, :], BF16)
    kv_spec = pl.BlockSpec((None, dil, rows, D_MODEL), lambda b, i: (b, 0, i, 0))
    kv_shape = jax.ShapeDtypeStruct((BATCH, dil, seg, D_MODEL), BF16)
    vmem = (D_MODEL * QKV_GROUP_WIDTH * 4 + 2 * tm * D_MODEL * 4 + 2 * 4 * tm * D_MODEL * 2 + (12 << 20))
    return pl.pallas_call(
        functools.partial(_qkv_body, n_res=dil),
        grid=(BATCH, SEQ // tm),
        in_specs=[pl.BlockSpec((None, tm, D_MODEL), lambda b, i: (b, i, 0)),
                  _resident((1, D_MODEL)),
                  pl.BlockSpec((None, D_MODEL, QKV_GROUP_WIDTH), lambda b, i: (layer, 0, gi),
                               pipeline_mode=pl.Buffered(1)),
                  _resident((1, LANES)), _resident((1, LANES)), _resident((tm, tm))],
        out_specs=[pl.BlockSpec((None, dil, 2, rows, D_MODEL), lambda b, i: (b, 0, 0, i, 0)),
                   kv_spec, kv_spec],
        out_shape=[jax.ShapeDtypeStruct((BATCH, dil, 2, seg, D_MODEL), BF16), kv_shape, kv_shape],
        compiler_params=_params(("parallel", "parallel"), vmem),
        name=f"qkv_dil{dil}",
    )(x, g.reshape(1, D_MODEL), w_qkv_all, gain2(q_gain), gain2(k_gain), perm)


ATTN_AHEAD = 3


def _attn_body(q0, k0, v0, q1, k1, v1, q2, k2, v2, bias_ref, o_ref, kp_sc, vp_sc, lt_sc, og_sc, lg_sc):
    lane = lax.broadcasted_iota(jnp.int32, (1, LANES), 1)
    head0 = lane < HEAD_DIM
    krow = lax.broadcasted_iota(jnp.int32, (ATTN_BK, 1), 0)
    zpad = jnp.zeros((ATTN_SIDE, LANES), BF16)
    tn = (((0,), (0,)), ((), ()))

    qkv_refs = ((q0, k0, v0), (q1, k1, v1), (q2, k2, v2))
    for g, (_, dil) in enumerate(DILATED_GROUPS):
        q_ref, k_ref, v_ref = qkv_refs[g]
        seg = SEQ // dil
        nb = seg // ATTN_BQ
        pad_seg = seg + 2 * ATTN_SIDE
        lt_sc[...] = jnp.zeros(lt_sc.shape, F32)
        for r in range(dil):
            base = r * pad_seg
            for src, dst in ((k_ref, kp_sc), (v_ref, vp_sc)):
                dst[base:base + ATTN_SIDE] = zpad
                dst[base + ATTN_SIDE + seg:base + pad_seg] = zpad
                dst[base + ATTN_SIDE:base + ATTN_SIDE + seg] = src[r]

        def scores(r, i):
            base = r * pad_seg
            win = slice(base + i * ATTN_BQ, base + i * ATTN_BQ + ATTN_BK)
            qrows = slice(i * ATTN_BQ, (i + 1) * ATTN_BQ)
            qm = jnp.concatenate([q_ref[r, 0, qrows, :], q_ref[r, 1, qrows, :]], axis=0)
            s = lax.dot_general(kp_sc[win], qm, (((1,), (1,)), ((), ())), preferred_element_type=F32)
            s = s + bias_ref[g]
            if i == 0 or i == nb - 1:
                ok = None
                if i == 0:
                    ok = krow >= ATTN_SIDE
                if i == nb - 1:
                    ok_hi = krow < ATTN_BK - ATTN_SIDE
                    ok = ok_hi if ok is None else jnp.logical_and(ok, ok_hi)
                s = jnp.where(ok, s, -1e30)
            return s

        blocks = [(r, i) for r in range(dil) for i in range(nb)]
        pending = [scores(*blocks[j]) for j in range(min(ATTN_AHEAD, len(blocks)))]
        for blk, (r, i) in enumerate(blocks):
            if blk + ATTN_AHEAD < len(blocks):
                pending.append(scores(*blocks[blk + ATTN_AHEAD]))
            s = pending.pop(0)
            base = r * pad_seg
            win = slice(base + i * ATTN_BQ, base + i * ATTN_BQ + ATTN_BK)
            m = jnp.max(s, axis=0, keepdims=True)
            p = jnp.exp2(s - m)
            den = jnp.sum(p, axis=0, keepdims=True)
            pn = (p * (1.0 / den)).astype(BF16)
            u = lax.dot_general(pn, vp_sc[win], tn, preferred_element_type=F32)
            rows = pl.ds(r + i * ATTN_BQ * dil, ATTN_BQ, stride=dil)
            og_sc[g, rows, :] = jnp.where(head0, u[:ATTN_BQ], u[ATTN_BQ:])
            lt_sc[blk:blk + 1, :] = m + jnp.log2(den)

        ltt = lt_sc[...].T
        for r in range(dil):
            for i in range(nb):
                c = r * nb + i
                rows = pl.ds(r + i * ATTN_BQ * dil, ATTN_BQ, stride=dil)
                lg_sc[g, rows, :] = jnp.where(head0, ltt[:ATTN_BQ, c:c + 1], ltt[ATTN_BQ:, c:c + 1])

    l0, l1, l2 = lg_sc[0], lg_sc[1], lg_sc[2]
    m = jnp.maximum(jnp.maximum(l0, l1), l2)
    w0, w1, w2 = jnp.exp2(l0 - m), jnp.exp2(l1 - m), jnp.exp2(l2 - m)
    o = (w0 * og_sc[0] + w1 * og_sc[1] + w2 * og_sc[2]) / (w0 + w1 + w2)
    o_ref[...] = o.astype(o_ref.dtype)


def _attn_core(qkv, bias_tabs):
    in_specs = []
    for (_, dil) in DILATED_GROUPS:
        seg = SEQ // dil
        in_specs.append(pl.BlockSpec((None, dil, 2, seg, LANES), lambda b, p: (b, 0, 0, 0, p)))
        in_specs.append(pl.BlockSpec((None, dil, seg, LANES), lambda b, p: (b, 0, 0, p)))
        in_specs.append(pl.BlockSpec((None, dil, seg, LANES), lambda b, p: (b, 0, 0, p)))
    in_specs.append(pl.BlockSpec((N_ATTN_GROUPS, None, ATTN_BK, 2 * ATTN_BQ), lambda b, p: (0, p, 0, 0)))
    return pl.pallas_call(
        _attn_body,
        grid=(BATCH, N_PAIRS),
        in_specs=in_specs,
        out_specs=pl.BlockSpec((None, SEQ, LANES), lambda b, p: (b, 0, p)),
        out_shape=jax.ShapeDtypeStruct((BATCH, SEQ, D_MODEL), BF16),
        scratch_shapes=[pltpu.VMEM((ATTN_PAD_ROWS, LANES), BF16),
                        pltpu.VMEM((ATTN_PAD_ROWS, LANES), BF16),
                        pltpu.VMEM((LANES, LANES), F32),
                        pltpu.VMEM((N_ATTN_GROUPS, SEQ, LANES), F32),
                        pltpu.VMEM((N_ATTN_GROUPS, SEQ, LANES), F32)],
        compiler_params=_params(("parallel", "parallel"), 40 << 20),
        name="attn_core",
    )(*qkv, bias_tabs)


def _proj_residual_body(x_ref, a_ref, w_ref, o_ref):
    o_ref[...] = x_ref[...] + jnp.dot(a_ref[...], w_ref[...], preferred_element_type=F32)


def _proj_residual(x2, a2, w_all, layer, tm=512):
    tile = pl.BlockSpec((tm, D_MODEL), lambda i: (i, 0))
    return pl.pallas_call(
        _proj_residual_body,
        grid=(x2.shape[0] // tm,),
        in_specs=[tile, tile, _layer_block((D_MODEL, D_MODEL), layer)],
        out_specs=tile,
        out_shape=jax.ShapeDtypeStruct(x2.shape, F32),
        compiler_params=_params(("parallel",), 32 << 20),
        name="proj_residual",
    )(x2, a2, w_all)


def _attn_layer(x, g, w_qkv_all, q_gain, k_gain, w_o_all, rel_bias, layer):
    qkv = []
    for gi, (window, dil) in enumerate(DILATED_GROUPS):
        assert (window // 2) // dil == ATTN_SIDE and (SEQ // dil) % ATTN_BQ == 0
        qkv.extend(_qkv_group(x, g, w_qkv_all, layer, gi, dil, q_gain[gi], k_gain[gi]))
    o = _attn_core(qkv, _attn_bias_tables(rel_bias))
    x2 = x.reshape(N_TOKENS, D_MODEL)
    return _proj_residual(x2, o.reshape(N_TOKENS, D_MODEL), w_o_all, layer).reshape(BATCH, SEQ, D_MODEL)


def kernel(x, norm_mix_g, norm_ffn_g, fnet_w_out, s5_lambda_re, s5_lambda_im, s5_log_dt, s5_b_re, s5_b_im, s5_c_re, s5_c_im, s5_d, s5_w_glu, attn_w_qkv, attn_q_gain, attn_k_gain, attn_w_o, rel_bias, ffn_w_gate_up, ffn_w_down):
    w_gate_up, w_down, w_qkv = ffn_w_gate_up, ffn_w_down, attn_w_qkv
    w_glu, w_o = _cast_bf16(s5_w_glu), _cast_bf16(attn_w_o)
    counts = [0, 0, 0]
    for i in range(DEPTH):
        kind = i % 3
        j = counts[kind]
        counts[kind] += 1
        if kind == 0:
            x = _fnet_layer(x, norm_mix_g[i], fnet_w_out[j])
        elif kind == 1:
            x = _s5_layer(x, norm_mix_g[i], s5_lambda_re[j], s5_lambda_im[j], s5_log_dt[j], s5_b_re[j],
                          s5_b_im[j], s5_c_re[j], s5_c_im[j], s5_d[j], w_glu, j)
        else:
            x = _attn_layer(x, norm_mix_g[i], w_qkv, attn_q_gain[j], attn_k_gain[j], w_o, rel_bias, j)
        x = _ffn(x.reshape(N_TOKENS, D_MODEL), norm_ffn_g[i], w_gate_up, w_down, i).reshape(BATCH, SEQ, D_MODEL)
    return x
```

```python
import functools
import math

import numpy as np
import jax
import jax.numpy as jnp
from jax import lax
from jax.experimental import pallas as pl
from jax.experimental.pallas import tpu as pltpu

F32 = jnp.float32
BF16 = jnp.bfloat16

D_MODEL = 1024
BATCH = 8
SEQ = 2048
DEPTH = 4
N_TOKENS = BATCH * SEQ
EPS = 1e-6
D_FF = 2816
FOURIER_GROUP = 128
S5_GROUP = 16
S5_GROUPS = 64
S5_STATE = 64
HEAD_DIM = 64
HEADS_PER_GROUP = 16
DILATED_GROUPS = ((128, 1), (512, 4), (2048, 16))
N_ATTN_GROUPS = 3
NUM_BUCKETS = 32
MAX_DISTANCE = 1024
ATTN_SIDE = 64

LANES = 128
VMEM_LIMIT_CAP = 60 * 1024 * 1024

S5_CHUNK = 16
S5_CHUNKS = SEQ // S5_CHUNK
S5_ROWS = BATCH * S5_CHUNKS
S5_TILE_GROUPS = LANES // S5_GROUP
S5_CK = S5_CHUNK * S5_GROUP

ATTN_BQ = 64
ATTN_BK = ATTN_BQ + 2 * ATTN_SIDE


def _params(sem, vmem_bytes):
    return pltpu.CompilerParams(dimension_semantics=sem,
                                vmem_limit_bytes=int(min(VMEM_LIMIT_CAP, vmem_bytes)))


def _rms(x, g):
    ms = jnp.mean(x * x, axis=-1, keepdims=True)
    return x * lax.rsqrt(ms + EPS) * g


def _sigmoid(x):
    return 1.0 / (1.0 + jnp.exp(-x))


def _resident(shape):
    nd = len(shape)
    return pl.BlockSpec(shape, lambda *_: (0,) * nd, pipeline_mode=pl.Buffered(1))


def _cast_body(w_ref, o_ref):
    o_ref[...] = w_ref[...].astype(BF16)


def _cast_bf16(w, rows=256):
    n_layers, n_rows, n_cols = w.shape
    block = pl.BlockSpec((None, rows, n_cols), lambda l, i: (l, i, 0))
    return pl.pallas_call(
        _cast_body,
        grid=(n_layers, n_rows // rows),
        in_specs=[block],
        out_specs=block,
        out_shape=jax.ShapeDtypeStruct(w.shape, BF16),
        compiler_params=_params(("parallel", "parallel"), 4 * rows * n_cols * 6 + (4 << 20)),
        name="cast_bf16",
    )(w)


FFN_CHUNK = 512


def _ffn_body(*refs, has_proj, has_norm_out):
    refs = list(refs)
    x_ref, g_ref, wgu_ref, wd_ref = refs[:4]
    a_ref, wp_ref = refs[4:6] if has_proj else (None, None)
    gn_ref = refs[4 + 2 * has_proj] if has_norm_out else None
    o_ref = refs[4 + 2 * has_proj + has_norm_out]
    x = x_ref[...]
    if has_proj:
        x = x + jnp.dot(a_ref[...], wp_ref[...], preferred_element_type=F32)
    h = _rms(x, g_ref[...]).astype(BF16)
    acc = x
    for c in range(0, D_FF, FFN_CHUNK):
        w = min(FFN_CHUNK, D_FF - c)
        gate = jnp.dot(h, wgu_ref[:, c:c + w].astype(BF16), preferred_element_type=F32)
        up = jnp.dot(h, wgu_ref[:, D_FF + c:D_FF + c + w].astype(BF16), preferred_element_type=F32)
        a = (gate * _sigmoid(gate) * up).astype(BF16)
        acc = acc + jnp.dot(a, wd_ref[c:c + w, :].astype(BF16), preferred_element_type=F32)
    o_ref[...] = acc
    if has_norm_out:
        hn_ref = refs[-1]
        hn_ref[...] = _rms(acc, gn_ref[...]).reshape(hn_ref.shape)


def _layer_block(shape, layer):
    return pl.BlockSpec((None,) + shape, lambda *_: (layer,) + (0,) * len(shape), pipeline_mode=pl.Buffered(1))


def _ffn(x2, g, wgu_all, wd_all, layer, proj=None, norm_out_g=None, tm=512):
    m = x2.shape[0]
    tile = pl.BlockSpec((tm, D_MODEL), lambda i: (i, 0))
    in_specs = [tile, _resident((1, D_MODEL)), _layer_block((D_MODEL, 2 * D_FF), layer),
                _layer_block((D_FF, D_MODEL), layer)]
    args = [x2, g.reshape(1, D_MODEL), wgu_all, wd_all]
    out_specs, out_shape = [tile], [jax.ShapeDtypeStruct((m, D_MODEL), F32)]
    if proj is not None:
        in_specs += [tile, _layer_block((D_MODEL, D_MODEL), proj[2])]
        args += [proj[0], proj[1]]
    if norm_out_g is not None:
        tiles_per_seq = SEQ // tm
        in_specs.append(_resident((1, D_MODEL)))
        args.append(norm_out_g.reshape(1, D_MODEL))
        out_specs.append(pl.BlockSpec((tm // S5_CHUNK, None, S5_CHUNK, D_MODEL),
                                      lambda i: (i % tiles_per_seq, i // tiles_per_seq, 0, 0)))
        out_shape.append(jax.ShapeDtypeStruct((S5_CHUNKS, BATCH, S5_CHUNK, D_MODEL), F32))
    vmem = 3 * D_MODEL * D_FF * 4 + 10 * tm * D_MODEL * 4 + (10 << 20)
    outs = pl.pallas_call(
        functools.partial(_ffn_body, has_proj=proj is not None, has_norm_out=norm_out_g is not None),
        grid=(m // tm,),
        in_specs=in_specs,
        out_specs=out_specs,
        out_shape=out_shape,
        compiler_params=_params(("parallel",), vmem),
        name="ffn",
    )(*args)
    return outs if norm_out_g is not None else outs[0]


def _fnet_weight_body(cc_ref, sc_ref, w_ref, o_ref):
    w = w_ref[...]
    o_ref[:, :D_MODEL] = jnp.dot(cc_ref[...], w, preferred_element_type=F32,
                                 precision=lax.Precision.HIGHEST).astype(BF16)
    o_ref[:, D_MODEL:] = jnp.dot(sc_ref[...], w, preferred_element_type=F32,
                                 precision=lax.Precision.HIGHEST).astype(BF16)


def _fnet_weights(w_out):
    n = np.arange(FOURIER_GROUP)
    ang = 2.0 * np.pi * ((n[:, None] * n[None, :]) % FOURIER_GROUP) / FOURIER_GROUP
    cc = jnp.asarray(np.cos(ang) / math.sqrt(FOURIER_GROUP), F32)
    sc = jnp.asarray(np.sin(ang) / math.sqrt(FOURIER_GROUP), F32)
    ng = D_MODEL // FOURIER_GROUP
    return pl.pallas_call(
        _fnet_weight_body,
        grid=(ng,),
        in_specs=[_resident((FOURIER_GROUP, FOURIER_GROUP)),
                  _resident((FOURIER_GROUP, FOURIER_GROUP)),
                  pl.BlockSpec((FOURIER_GROUP, D_MODEL), lambda i: (i, 0))],
        out_specs=pl.BlockSpec((FOURIER_GROUP, 2 * D_MODEL), lambda i: (i, 0)),
        out_shape=jax.ShapeDtypeStruct((D_MODEL, 2 * D_MODEL), BF16),
        compiler_params=_params(("parallel",), 16 << 20),
        name="fnet_weights",
    )(cc, sc, w_out)


FN_HALF = SEQ // 2
FN_BLK = 256
FN_NB = FN_HALF // FN_BLK


def _fnet_tables():
    k = np.arange(FN_HALF)[:, None]
    n = np.arange(FN_HALF)[None, :]
    ang = 2.0 * np.pi * ((k * n) % SEQ) / SEQ
    scale = 1.0 / math.sqrt(SEQ)
    i = np.arange(FN_BLK)
    rev = (i[None, :] == FN_BLK - i[:, None]).astype(np.float32)
    return (jnp.asarray(np.cos(ang) * scale, BF16), jnp.asarray(np.sin(ang) * scale, BF16),
            jnp.asarray(rev, BF16))


def _fnet_body(x_ref, g_ref, cs_ref, ss_ref, w_ref, rev_ref, o_ref, e_sc, d_sc, zc_sc, zs_sc):
    g = g_ref[...]
    scale = 1.0 / math.sqrt(SEQ)
    row = lax.broadcasted_iota(jnp.int32, (FN_BLK, 1), 0)
    first = row == 0
    sign = jnp.where((row & 1) == 0, 1.0, -1.0)
    rev = rev_ref[...]
    blk = lambda a: pl.ds(a * FN_BLK, FN_BLK)
    mirror = lambda a: pl.ds(SEQ - (a + 1) * FN_BLK, FN_BLK)

    alt = jnp.zeros((1, D_MODEL), F32)
    carry = jnp.zeros((1, D_MODEL), F32)
    for a in range(FN_NB):
        lo = _rms(x_ref[blk(a), :], g)
        hi = _rms(x_ref[mirror(a), :], g)
        alt = alt + jnp.sum(sign * lo, axis=0, keepdims=True) + jnp.sum(sign * hi, axis=0, keepdims=True)
        r = jnp.dot(rev, hi.astype(BF16), preferred_element_type=F32)
        r = jnp.where(first, carry, r)
        e_sc[blk(a), :] = (lo + r).astype(BF16)
        d_sc[blk(a), :] = (lo - r).astype(BF16)
        carry = hi[0:1, :]
    h_nyq = carry * scale

    for c in range(FN_NB):
        zc = jnp.dot(cs_ref[blk(c), :], e_sc[...], preferred_element_type=F32) + sign * h_nyq
        zs = jnp.dot(ss_ref[blk(c), :], d_sc[...], preferred_element_type=F32)
        zc_sc[blk(c), :] = zc.astype(BF16)
        zs_sc[blk(c), :] = zs.astype(BF16)

    wc = w_ref[:, :D_MODEL]
    ws = w_ref[:, D_MODEL:]
    z_nyq = jnp.broadcast_to(alt * scale, (8, D_MODEL)).astype(BF16)
    carry = jnp.dot(z_nyq, wc, preferred_element_type=F32)[0:1, :]
    def products(c):
        return (jnp.dot(zc_sc[blk(c), :], wc, preferred_element_type=F32),
                jnp.dot(zs_sc[blk(c), :], ws, preferred_element_type=F32))

    ahead = products(FN_NB - 1)
    for c in reversed(range(FN_NB)):
        p, q = ahead
        if c > 0:
            ahead = products(c - 1)
        o_ref[blk(c), :] = x_ref[blk(c), :] + (p - q)
        m = p + q
        m_hi = m.astype(BF16)
        m_lo = (m - m_hi.astype(F32)).astype(BF16)
        r = (jnp.dot(rev, m_hi, preferred_element_type=F32) + jnp.dot(rev, m_lo, preferred_element_type=F32))
        r = jnp.where(first, carry, r)
        o_ref[mirror(c), :] = x_ref[mirror(c), :] + r
        carry = m[0:1, :]


def _fnet_layer(x, g, w_out):
    wcs = _fnet_weights(w_out)
    cs, ss, rev = _fnet_tables()
    seq_block = pl.BlockSpec((None, SEQ, D_MODEL), lambda b: (b, 0, 0))
    half = pltpu.VMEM((FN_HALF, D_MODEL), BF16)
    vmem = 4 * SEQ * D_MODEL * 4 + 4 * FN_HALF * D_MODEL * 2 + 4 * FN_HALF * D_MODEL * 2 + (12 << 20)
    return pl.pallas_call(
        _fnet_body,
        grid=(BATCH,),
        in_specs=[seq_block, _resident((1, D_MODEL)), _resident((FN_HALF, FN_HALF)),
                  _resident((FN_HALF, FN_HALF)), _resident((D_MODEL, 2 * D_MODEL)),
                  _resident((FN_BLK, FN_BLK))],
        out_specs=seq_block,
        out_shape=jax.ShapeDtypeStruct((BATCH, SEQ, D_MODEL), F32),
        scratch_shapes=[half, half, half, half],
        compiler_params=_params(("parallel",), vmem),
        name="fnet_mix",
    )(x, g.reshape(1, D_MODEL), cs, ss, wcs, rev)


def _s5_matrices(lam_re, lam_im, log_dt, b_re, b_im, c_re, c_im):
    lam = lax.complex(lam_re.astype(F32), lam_im.astype(F32))
    dt = jnp.exp(log_dt.astype(F32))[..., None]
    lam_dt = lam * dt
    lam_bar = jnp.exp(lam_dt)
    b_bar = ((lam_bar - 1.0) / lam)[..., None] * lax.complex(b_re.astype(F32), b_im.astype(F32))
    c = lax.complex(c_re.astype(F32), c_im.astype(F32))
    L = S5_CHUNK
    taus = jnp.arange(L + 1, dtype=F32)
    pw = jnp.exp(lam_dt[..., None, :] * taus[:, None])
    def both(fwd, bwd):
        z = jnp.concatenate([fwd, bwd], axis=-1)
        return z.real[:, :, None, :], z.imag[:, :, None, :]
    def outer(pw_pair, mat_pair, sign):
        (ar, ai), (br, bi) = pw_pair, mat_pair
        re = ar * br - ai * bi
        im = ar * bi + ai * br
        return jnp.concatenate([re, sign * im], axis=-1).reshape(S5_GROUPS, -1, S5_CK)
    b_t = b_bar.transpose(0, 1, 3, 2)
    mat = lambda m: tuple(z[:, None, :, 0, :] for z in both(m[0], m[1]))
    s_in = outer(both(pw[0][:, ::-1][:, 1:], pw[1][:, :L]), mat(b_t), 1.0)
    wc_t = outer(both(pw[0][:, 1:], pw[1][:, ::-1][:, :L]), mat(c), -1.0)
    zeros = lambda n: jnp.zeros((S5_GROUPS, n, S5_STATE), pw.dtype)
    lag_f = jnp.concatenate([zeros(L - 1), pw[0][:, :L], zeros(1)], axis=1)
    lag_b = jnp.concatenate([pw[1][:, L - 1:0:-1], pw[1][:, :1], zeros(L)], axis=1)
    c_lag = outer(both(lag_f, lag_b), mat(c), -1.0)
    b_cat = jnp.concatenate(mat(b_t), axis=-1)[:, 0]
    strip = jnp.einsum('gqk,gjk->gqj', b_cat, c_lag, precision=lax.Precision.HIGHEST)
    al = pw[:, :, L]
    coef = jnp.stack([jnp.concatenate([al[0].real, al[1].real], -1),
                      jnp.concatenate([al[0].imag, al[1].imag], -1)], axis=1)
    return strip.astype(F32), s_in.astype(BF16), wc_t.astype(BF16), coef.astype(F32)


def _s5_norm_body(x_ref, g_ref, o_ref):
    h = _rms(x_ref[...], g_ref[...])
    o_ref[...] = h.reshape(o_ref.shape)


def _s5_norm(x, g, tm=512):
    cr = tm // S5_CHUNK
    return pl.pallas_call(
        _s5_norm_body,
        grid=(BATCH, SEQ // tm),
        in_specs=[pl.BlockSpec((None, tm, D_MODEL), lambda b, i: (b, i, 0)),
                  _resident((1, D_MODEL))],
        out_specs=pl.BlockSpec((cr, None, S5_CHUNK, D_MODEL), lambda b, i: (i, b, 0, 0)),
        out_shape=jax.ShapeDtypeStruct((S5_CHUNKS, BATCH, S5_CHUNK, D_MODEL), F32),
        compiler_params=_params(("parallel", "parallel"), 32 << 20),
        name="s5_norm",
    )(x, g.reshape(1, D_MODEL))


def _block_transpose8(a, lane_block):
    a = list(a)
    for d in (4, 2, 1):
        take_lo = (lane_block & d) == 0
        nxt = list(a)
        for i in range(8):
            if i & d:
                continue
            lo, hi = a[i], a[i + d]
            nxt[i] = jnp.where(take_lo, lo, pltpu.roll(hi, S5_GROUP * d, 1))
            nxt[i + d] = jnp.where(take_lo, pltpu.roll(lo, LANES - S5_GROUP * d, 1), hi)
        a = nxt
    return a


def _gelu_tanh(y):
    return 0.5 * y * (1.0 + jnp.tanh(math.sqrt(2.0 / math.pi) * (y + 0.044715 * (y * y * y))))


S5_RB = 64


def _s5_body(h_ref, strip_ref, ws_ref, wc_ref, coef_ref, d_ref, o_ref, x_sc, y_sc, s_sc, hs_sc, toep_sc):
    lane = lax.broadcasted_iota(jnp.int32, (1, LANES), 1)
    lane_block = lane // S5_GROUP
    fwd_lanes = lane < S5_STATE
    n_rb = S5_ROWS // S5_RB
    tok_rb = S5_RB * S5_CHUNK

    def relayout_in(i, carry):
        base = pl.multiple_of(i * tok_rb, tok_rb)
        rows = pl.multiple_of(i * S5_RB, S5_RB)
        for half in range(2):
            a = [h_ref[pl.ds(base + half * 8 + t, S5_RB, stride=S5_CHUNK), :] for t in range(8)]
            xt = _block_transpose8(a, lane_block)
            for gi in range(S5_TILE_GROUPS):
                x_sc[gi, pl.ds(rows, S5_RB), half * LANES:(half + 1) * LANES] = xt[gi].astype(BF16)
        return carry

    lax.fori_loop(0, n_rb, relayout_in, 0)

    nt = (((1,), (1,)), ((), ()))
    for gi in range(S5_TILE_GROUPS):
        strip = strip_ref[gi]
        for t in range(S5_CHUNK):
            off = (S5_CHUNK - 1 - t) * S5_GROUP
            win = strip if off == 0 else pltpu.roll(strip, strip.shape[1] - off, 1)
            toep_sc[t * S5_GROUP:(t + 1) * S5_GROUP, :] = win[:, :S5_CK].astype(BF16)
        xg = x_sc[gi]
        y_sc[gi] = jnp.dot(xg, toep_sc[...], preferred_element_type=F32)
        z = jnp.dot(xg, ws_ref[gi], preferred_element_type=F32)
        s_sc[0] = z[:, :LANES]
        s_sc[1] = z[:, LANES:]
        a_re = coef_ref[gi, 0:1, :]
        a_im = coef_ref[gi, 1:2, :]

        def scan_step(k, st):
            st_re, st_im = st
            rf = pl.multiple_of(k * BATCH, BATCH)
            rb = pl.multiple_of((S5_CHUNKS - 1 - k) * BATCH, BATCH)
            hs_sc[0, pl.ds(rf, BATCH), :S5_STATE] = st_re[:, :S5_STATE]
            hs_sc[1, pl.ds(rf, BATCH), :S5_STATE] = st_im[:, :S5_STATE]
            hs_sc[0, pl.ds(rb, BATCH), S5_STATE:] = st_re[:, S5_STATE:]
            hs_sc[1, pl.ds(rb, BATCH), S5_STATE:] = st_im[:, S5_STATE:]
            s_re = jnp.where(fwd_lanes, s_sc[0, pl.ds(rf, BATCH), :], s_sc[0, pl.ds(rb, BATCH), :])
            s_im = jnp.where(fwd_lanes, s_sc[1, pl.ds(rf, BATCH), :], s_sc[1, pl.ds(rb, BATCH), :])
            n_re = a_re * st_re - a_im * st_im + s_re
            n_im = a_re * st_im + a_im * st_re + s_im
            return n_re, n_im

        zero = jnp.zeros((BATCH, LANES), F32)
        lax.fori_loop(0, S5_CHUNKS, scan_step, (zero, zero))
        hcat = jnp.concatenate([hs_sc[0], hs_sc[1]], axis=1).astype(BF16)
        y_sc[gi] = y_sc[gi] + lax.dot_general(hcat, wc_ref[gi], nt, preferred_element_type=F32)

    d_skip = d_ref[...]

    def relayout_out(i, carry):
        base = pl.multiple_of(i * tok_rb, tok_rb)
        rows = pl.multiple_of(i * S5_RB, S5_RB)
        for half in range(2):
            yg = [y_sc[gi, pl.ds(rows, S5_RB), half * LANES:(half + 1) * LANES] for gi in range(S5_TILE_GROUPS)]
            yt = _block_transpose8(yg, lane_block)
            for t in range(8):
                tok = pl.ds(base + half * 8 + t, S5_RB, stride=S5_CHUNK)
                o_ref[tok, :] = _gelu_tanh(yt[t] + d_skip * h_ref[tok, :])
        return carry

    lax.fori_loop(0, n_rb, relayout_out, 0)


def _s5_core(h2, strip, s_in, wc_t, coef, d_skip):
    n_tiles = S5_GROUPS // S5_TILE_GROUPS
    tok_block = pl.BlockSpec((N_TOKENS, LANES), lambda j: (0, j))
    vmem = (4 * N_TOKENS * LANES * 4 + S5_TILE_GROUPS * S5_ROWS * S5_CK * (2 + 4)
            + 4 * S5_ROWS * LANES * 4 + 2 * S5_TILE_GROUPS * S5_CK * 3 * S5_CK * 2 + (12 << 20))
    return pl.pallas_call(
        _s5_body,
        grid=(n_tiles,),
        in_specs=[tok_block,
                  pl.BlockSpec((S5_TILE_GROUPS, S5_GROUP, 2 * S5_CK), lambda j: (j, 0, 0)),
                  pl.BlockSpec((S5_TILE_GROUPS, S5_CK, S5_CK), lambda j: (j, 0, 0)),
                  pl.BlockSpec((S5_TILE_GROUPS, S5_CK, S5_CK), lambda j: (j, 0, 0)),
                  pl.BlockSpec((S5_TILE_GROUPS, 2, LANES), lambda j: (j, 0, 0)),
                  pl.BlockSpec((1, LANES), lambda j: (0, j))],
        out_specs=tok_block,
        out_shape=jax.ShapeDtypeStruct((N_TOKENS, D_MODEL), F32),
        scratch_shapes=[pltpu.VMEM((S5_TILE_GROUPS, S5_ROWS, S5_CK), BF16),
                        pltpu.VMEM((S5_TILE_GROUPS, S5_ROWS, S5_CK), F32),
                        pltpu.VMEM((2, S5_ROWS, LANES), F32),
                        pltpu.VMEM((2, S5_ROWS, LANES), F32),
                        pltpu.VMEM((S5_CK, S5_CK), BF16)],
        compiler_params=_params(("arbitrary",), vmem),
        name="s5_core",
    )(h2, strip, s_in, wc_t, coef, d_skip.reshape(1, D_MODEL))


def _s5_glu_body(x_ref, a_ref, w_ref, o_ref):
    a = a_ref[...].reshape(x_ref.shape).astype(BF16)
    vg = jnp.dot(a, w_ref[...], preferred_element_type=F32)
    o_ref[...] = x_ref[...] + vg[:, :D_MODEL] * _sigmoid(vg[:, D_MODEL:])


def _s5_glu(x, act, w_glu_all, layer, tm=512):
    cr = tm // S5_CHUNK
    vmem = 2 * D_MODEL * D_MODEL * 2 + 6 * tm * D_MODEL * 4 + 2 * tm * 2 * D_MODEL * 4 + (4 << 20)
    return pl.pallas_call(
        _s5_glu_body,
        grid=(BATCH, SEQ // tm),
        in_specs=[pl.BlockSpec((None, tm, D_MODEL), lambda b, i: (b, i, 0)),
                  pl.BlockSpec((cr, None, S5_CHUNK, D_MODEL), lambda b, i: (i, b, 0, 0)),
                  _layer_block((D_MODEL, 2 * D_MODEL), layer)],
        out_specs=pl.BlockSpec((None, tm, D_MODEL), lambda b, i: (b, i, 0)),
        out_shape=jax.ShapeDtypeStruct((BATCH, SEQ, D_MODEL), F32),
        compiler_params=_params(("parallel", "parallel"), vmem),
        name="s5_glu",
    )(x, act, w_glu_all)


def _s5_layer(x, h, lam_re, lam_im, log_dt, b_re, b_im, c_re, c_im, d_skip, w_glu_all, layer):
    strip, s_in, wc_t, coef = _s5_matrices(lam_re, lam_im, log_dt, b_re, b_im, c_re, c_im)
    act = _s5_core(h.reshape(N_TOKENS, D_MODEL), strip, s_in, wc_t, coef, d_skip)
    act = act.reshape(S5_CHUNKS, BATCH, S5_CHUNK, D_MODEL)
    return _s5_glu(x, act, w_glu_all, layer)


def _t5_bucket(rel):
    half = NUM_BUCKETS // 2
    max_exact = half // 2
    n = np.abs(rel)
    sign = (rel > 0).astype(np.int32) * half
    large = max_exact + (np.log(np.maximum(n, 1) / max_exact) / math.log(MAX_DISTANCE / max_exact)
                         * (half - max_exact)).astype(np.int32)
    large = np.minimum(large, half - 1)
    return (sign + np.where(n < max_exact, n, large)).astype(np.int32)


LOG2E = 1.4426950408889634
N_PAIRS = HEADS_PER_GROUP // 2
ATTN_PAD_ROWS = SEQ + 2 * ATTN_SIDE * max(d for _, d in DILATED_GROUPS)
QKV_GROUP_WIDTH = 3 * D_MODEL
QKV_SLAB = 2 * LANES
QKV_TILE_ROWS = 512


def _attn_bias_tables(rel_bias):
    n_off = ATTN_BK + ATTN_BQ - 1
    offs = np.arange(n_off) - (ATTN_BQ - 1) - ATTN_SIDE
    strips = []
    for gi, (_, dil) in enumerate(DILATED_GROUPS):
        onehot = jnp.asarray(_t5_bucket(offs * dil)[:, None] == np.arange(NUM_BUCKETS), F32)
        f = jnp.dot(onehot, rel_bias[:, gi * HEADS_PER_GROUP:(gi + 1) * HEADS_PER_GROUP].astype(F32),
                    precision=lax.Precision.HIGHEST).T
        strips.append(jnp.pad(f[:, ::-1], ((0, 0), (0, BIAS_STRIP - n_off))))
    strips = jnp.stack(strips).reshape(N_ATTN_GROUPS, HEADS_PER_GROUP, 1, BIAS_STRIP)
    return pl.pallas_call(
        _attn_bias_body,
        grid=(N_ATTN_GROUPS, N_PAIRS),
        in_specs=[pl.BlockSpec((None, 2, 1, BIAS_STRIP), lambda g, p: (g, p, 0, 0))],
        out_specs=pl.BlockSpec((None, None, ATTN_BK, 2 * ATTN_BQ), lambda g, p: (g, p, 0, 0)),
        out_shape=jax.ShapeDtypeStruct((N_ATTN_GROUPS, N_PAIRS, ATTN_BK, 2 * ATTN_BQ), F32),
        compiler_params=_params(("parallel", "parallel"), 16 << 20),
        name="attn_bias",
    )(strips)


BIAS_STRIP = 256


def _attn_bias_body(f_ref, o_ref):
    krow = lax.broadcasted_iota(jnp.int32, (ATTN_BK, 1), 0)
    lane = lax.broadcasted_iota(jnp.int32, (1, LANES), 1)
    head0 = lane < ATTN_BQ
    band = jnp.abs(krow - ATTN_SIDE - (lane & (ATTN_BQ - 1))) <= ATTN_SIDE
    halves = []
    for a in range(2):
        x = jnp.broadcast_to(f_ref[a], (ATTN_BK, BIAS_STRIP))
        shift = (a * ATTN_BQ - (ATTN_BK - 1)) % BIAS_STRIP
        halves.append(pltpu.roll(x, shift, 1, stride=1, stride_axis=0)[:, :LANES])
    o_ref[...] = jnp.where(band, jnp.where(head0, halves[0], halves[1]) * LOG2E, -1e30)


def _qkv_body(x_ref, g_ref, w_ref, qg_ref, kg_ref, perm_ref, q_out, k_out, v_out, *, n_res):
    rows = x_ref.shape[0] // n_res
    h = _rms(x_ref[...], g_ref[...]).astype(BF16)
    if n_res > 1:
        h = jnp.dot(perm_ref[...], h, preferred_element_type=F32).astype(BF16)
    lane = lax.broadcasted_iota(jnp.int32, (1, LANES), 1)
    head0 = lane < HEAD_DIM

    def head_norm(t, gain):
        sq = t * t
        s0 = jnp.sum(jnp.where(head0, sq, 0.0), axis=-1, keepdims=True)
        s1 = jnp.sum(jnp.where(head0, 0.0, sq), axis=-1, keepdims=True)
        ms = jnp.where(head0, s0, s1) * (1.0 / HEAD_DIM)
        return t * lax.rsqrt(ms + EPS) * gain

    for c in range(QKV_GROUP_WIDTH // QKV_SLAB):
        z = jnp.dot(h, w_ref[:, c * QKV_SLAB:(c + 1) * QKV_SLAB].astype(BF16), preferred_element_type=F32)
        for half in range(QKV_SLAB // LANES):
            section, lo = divmod(c * QKV_SLAB + half * LANES, D_MODEL)
            zz = z[:, half * LANES:(half + 1) * LANES]
            if section == 0:
                zz = head_norm(zz, qg_ref[...]) * (HEAD_DIM ** -0.5 * LOG2E)
                qa = jnp.where(head0, zz, 0.0).astype(BF16)
                qb = jnp.where(head0, 0.0, zz).astype(BF16)
            elif section == 1:
                zz = head_norm(zz, kg_ref[...]).astype(BF16)
            else:
                zz = zz.astype(BF16)
            for r in range(n_res):
                piece = slice(r * rows, (r + 1) * rows)
                if section == 0:
                    q_out[r, 0, :, lo:lo + LANES] = qa[piece]
                    q_out[r, 1, :, lo:lo + LANES] = qb[piece]
                elif section == 1:
                    k_out[r, :, lo:lo + LANES] = zz[piece]
                else:
                    v_out[r, :, lo:lo + LANES] = zz[piece]


def _qkv_group(x, g, w_qkv_all, layer, gi, dil, q_gain, k_gain):
    seg = SEQ // dil
    tm = QKV_TILE_ROWS
    rows = tm // dil
    assert tm % dil == 0 and rows % 16 == 0
    gain2 = lambda gn: jnp.tile(gn.astype(F32), 2).reshape(1, LANES)
    src = np.arange(tm).reshape(rows, dil).T.reshape(tm)
    perm = jnp.asarray(src[:, None] == np.arange(tm)[None, :], BF16)
    kv_spec = pl.BlockSpec((None, dil, rows, D_MODEL), lambda b, i: (b, 0, i, 0))
    kv_shape = jax.ShapeDtypeStruct((BATCH, dil, seg, D_MODEL), BF16)
    vmem = (D_MODEL * QKV_GROUP_WIDTH * 4 + 2 * tm * D_MODEL * 4 + 2 * 4 * tm * D_MODEL * 2 + (12 << 20))
    return pl.pallas_call(
        functools.partial(_qkv_body, n_res=dil),
        grid=(BATCH, SEQ // tm),
        in_specs=[pl.BlockSpec((None, tm, D_MODEL), lambda b, i: (b, i, 0)),
                  _resident((1, D_MODEL)),
                  pl.BlockSpec((None, D_MODEL, QKV_GROUP_WIDTH), lambda b, i: (layer, 0, gi),
                               pipeline_mode=pl.Buffered(1)),
                  _resident((1, LANES)), _resident((1, LANES)), _resident((tm, tm))],
        out_specs=[pl.BlockSpec((None, dil, 2, rows, D_MODEL), lambda b, i: (b, 0, 0, i, 0)),
                   kv_spec, kv_spec],
        out_shape=[jax.ShapeDtypeStruct((BATCH, dil, 2, seg, D_MODEL), BF16), kv_shape, kv_shape],
        compiler_params=_params(("parallel", "parallel"), vmem),
        name=f"qkv_dil{dil}",
    )(x, g.reshape(1, D_MODEL), w_qkv_all, gain2(q_gain), gain2(k_gain), perm)


ATTN_AHEAD = 3


def _attn_body(q0, k0, v0, q1, k1, v1, q2, k2, v2, bias_ref, o_ref, kp_sc, vp_sc, lt_sc, og_sc, lg_sc):
    lane = lax.broadcasted_iota(jnp.int32, (1, LANES), 1)
    head0 = lane < HEAD_DIM
    krow = lax.broadcasted_iota(jnp.int32, (ATTN_BK, 1), 0)
    zpad = jnp.zeros((ATTN_SIDE, LANES), BF16)
    tn = (((0,), (0,)), ((), ()))

    qkv_refs = ((q0, k0, v0), (q1, k1, v1), (q2, k2, v2))
    for g, (_, dil) in enumerate(DILATED_GROUPS):
        q_ref, k_ref, v_ref = qkv_refs[g]
        seg = SEQ // dil
        nb = seg // ATTN_BQ
        pad_seg = seg + 2 * ATTN_SIDE
        lt_sc[...] = jnp.zeros(lt_sc.shape, F32)
        for r in range(dil):
            base = r * pad_seg
            for src, dst in ((k_ref, kp_sc), (v_ref, vp_sc)):
                dst[base:base + ATTN_SIDE] = zpad
                dst[base + ATTN_SIDE + seg:base + pad_seg] = zpad
                dst[base + ATTN_SIDE:base + ATTN_SIDE + seg] = src[r]

        def scores(r, i):
            base = r * pad_seg
            win = slice(base + i * ATTN_BQ, base + i * ATTN_BQ + ATTN_BK)
            qrows = slice(i * ATTN_BQ, (i + 1) * ATTN_BQ)
            qm = jnp.concatenate([q_ref[r, 0, qrows, :], q_ref[r, 1, qrows, :]], axis=0)
            s = lax.dot_general(kp_sc[win], qm, (((1,), (1,)), ((), ())), preferred_element_type=F32)
            s = s + bias_ref[g]
            if i == 0 or i == nb - 1:
                ok = None
                if i == 0:
                    ok = krow >= ATTN_SIDE
                if i == nb - 1:
                    ok_hi = krow < ATTN_BK - ATTN_SIDE
                    ok = ok_hi if ok is None else jnp.logical_and(ok, ok_hi)
                s = jnp.where(ok, s, -1e30)
            return s

        blocks = [(r, i) for r in range(dil) for i in range(nb)]
        pending = [scores(*blocks[j]) for j in range(min(ATTN_AHEAD, len(blocks)))]
        for blk, (r, i) in enumerate(blocks):
            if blk + ATTN_AHEAD < len(blocks):
                pending.append(scores(*blocks[blk + ATTN_AHEAD]))
            s = pending.pop(0)
            base = r * pad_seg
            win = slice(base + i * ATTN_BQ, base + i * ATTN_BQ + ATTN_BK)
            m = jnp.max(s, axis=0, keepdims=True)
            p = jnp.exp2(s - m)
            den = jnp.sum(p, axis=0, keepdims=True)
            pn = (p * (1.0 / den)).astype(BF16)
            u = lax.dot_general(pn, vp_sc[win], tn, preferred_element_type=F32)
            rows = pl.ds(r + i * ATTN_BQ * dil, ATTN_BQ, stride=dil)
            og_sc[g, rows, :] = jnp.where(head0, u[:ATTN_BQ], u[ATTN_BQ:])
            lt_sc[blk:blk + 1, :] = m + jnp.log2(den)

        ltt = lt_sc[...].T
        for r in range(dil):
            for i in range(nb):
                c = r * nb + i
                rows = pl.ds(r + i * ATTN_BQ * dil, ATTN_BQ, stride=dil)
                lg_sc[g, rows, :] = jnp.where(head0, ltt[:ATTN_BQ, c:c + 1], ltt[ATTN_BQ:, c:c + 1])

    l0, l1, l2 = lg_sc[0], lg_sc[1], lg_sc[2]
    m = jnp.maximum(jnp.maximum(l0, l1), l2)
    w0, w1, w2 = jnp.exp2(l0 - m), jnp.exp2(l1 - m), jnp.exp2(l2 - m)
    o = (w0 * og_sc[0] + w1 * og_sc[1] + w2 * og_sc[2]) / (w0 + w1 + w2)
    o_ref[...] = o.astype(o_ref.dtype)


def _attn_core(qkv, bias_tabs):
    in_specs = []
    for (_, dil) in DILATED_GROUPS:
        seg = SEQ // dil
        in_specs.append(pl.BlockSpec((None, dil, 2, seg, LANES), lambda b, p: (b, 0, 0, 0, p)))
        in_specs.append(pl.BlockSpec((None, dil, seg, LANES), lambda b, p: (b, 0, 0, p)))
        in_specs.append(pl.BlockSpec((None, dil, seg, LANES), lambda b, p: (b, 0, 0, p)))
    in_specs.append(pl.BlockSpec((N_ATTN_GROUPS, None, ATTN_BK, 2 * ATTN_BQ), lambda b, p: (0, p, 0, 0)))
    return pl.pallas_call(
        _attn_body,
        grid=(BATCH, N_PAIRS),
        in_specs=in_specs,
        out_specs=pl.BlockSpec((None, SEQ, LANES), lambda b, p: (b, 0, p)),
        out_shape=jax.ShapeDtypeStruct((BATCH, SEQ, D_MODEL), BF16),
        scratch_shapes=[pltpu.VMEM((ATTN_PAD_ROWS, LANES), BF16),
                        pltpu.VMEM((ATTN_PAD_ROWS, LANES), BF16),
                        pltpu.VMEM((LANES, LANES), F32),
                        pltpu.VMEM((N_ATTN_GROUPS, SEQ, LANES), F32),
                        pltpu.VMEM((N_ATTN_GROUPS, SEQ, LANES), F32)],
        compiler_params=_params(("parallel", "parallel"), 40 << 20),
        name="attn_core",
    )(*qkv, bias_tabs)


def _attn_layer(x, g, w_qkv_all, q_gain, k_gain, rel_bias, layer):
    qkv = []
    for gi, (window, dil) in enumerate(DILATED_GROUPS):
        assert (window // 2) // dil == ATTN_SIDE and (SEQ // dil) % ATTN_BQ == 0
        qkv.extend(_qkv_group(x, g, w_qkv_all, layer, gi, dil, q_gain[gi], k_gain[gi]))
    return _attn_core(qkv, _attn_bias_tables(rel_bias)).reshape(N_TOKENS, D_MODEL)


def kernel(x, norm_mix_g, norm_ffn_g, fnet_w_out, s5_lambda_re, s5_lambda_im, s5_log_dt, s5_b_re, s5_b_im, s5_c_re, s5_c_im, s5_d, s5_w_glu, attn_w_qkv, attn_q_gain, attn_k_gain, attn_w_o, rel_bias, ffn_w_gate_up, ffn_w_down):
    w_gate_up, w_down, w_qkv = ffn_w_gate_up, ffn_w_down, attn_w_qkv
    w_glu, w_o = _cast_bf16(s5_w_glu), _cast_bf16(attn_w_o)
    counts = [0, 0, 0]
    h_next = None
    for i in range(DEPTH):
        kind = i % 3
        j = counts[kind]
        counts[kind] += 1
        proj = None
        if kind == 0:
            x = _fnet_layer(x, norm_mix_g[i], fnet_w_out[j])
        elif kind == 1:
            h = h_next if h_next is not None else _s5_norm(x, norm_mix_g[i])
            x = _s5_layer(x, h, s5_lambda_re[j], s5_lambda_im[j], s5_log_dt[j], s5_b_re[j],
                          s5_b_im[j], s5_c_re[j], s5_c_im[j], s5_d[j], w_glu, j)
        else:
            proj = (_attn_layer(x, norm_mix_g[i], w_qkv, attn_q_gain[j], attn_k_gain[j], rel_bias, j), w_o, j)
        next_is_s5 = i + 1 < DEPTH and (i + 1) % 3 == 1
        out = _ffn(x.reshape(N_TOKENS, D_MODEL), norm_ffn_g[i], w_gate_up, w_down, i, proj=proj,
                   norm_out_g=norm_mix_g[i + 1] if next_is_s5 else None)
        x, h_next = out if next_is_s5 else (out, None)
        x = x.reshape(BATCH, SEQ, D_MODEL)
    return x
```

```python
import functools
import math

import numpy as np
import jax
import jax.numpy as jnp
from jax import lax
from jax.experimental import pallas as pl
from jax.experimental.pallas import tpu as pltpu

F32 = jnp.float32
BF16 = jnp.bfloat16

D_MODEL = 1024
BATCH = 8
SEQ = 2048
DEPTH = 4
N_TOKENS = BATCH * SEQ
EPS = 1e-6
D_FF = 2816
FOURIER_GROUP = 128
S5_GROUP = 16
S5_GROUPS = 64
S5_STATE = 64
HEAD_DIM = 64
HEADS_PER_GROUP = 16
DILATED_GROUPS = ((128, 1), (512, 4), (2048, 16))
N_ATTN_GROUPS = 3
NUM_BUCKETS = 32
MAX_DISTANCE = 1024
ATTN_SIDE = 64

LANES = 128
VMEM_LIMIT_CAP = 60 * 1024 * 1024

S5_CHUNK = 16
S5_CHUNKS = SEQ // S5_CHUNK
S5_ROWS = BATCH * S5_CHUNKS
S5_TILE_GROUPS = LANES // S5_GROUP
S5_CK = S5_CHUNK * S5_GROUP

ATTN_BQ = 64
ATTN_BK = ATTN_BQ + 2 * ATTN_SIDE


def _params(sem, vmem_bytes):
    return pltpu.CompilerParams(dimension_semantics=sem,
                                vmem_limit_bytes=int(min(VMEM_LIMIT_CAP, vmem_bytes)))


def _rms(x, g):
    ms = jnp.mean(x * x, axis=-1, keepdims=True)
    return x * lax.rsqrt(ms + EPS) * g


def _sigmoid(x):
    return 1.0 / (1.0 + jnp.exp(-x))


def _resident(shape):
    nd = len(shape)
    return pl.BlockSpec(shape, lambda *_: (0,) * nd, pipeline_mode=pl.Buffered(1))


def _cast_body(w_ref, o_ref):
    o_ref[...] = w_ref[...].astype(BF16)


def _cast_bf16(w, rows=256):
    n_layers, n_rows, n_cols = w.shape
    block = pl.BlockSpec((None, rows, n_cols), lambda l, i: (l, i, 0))
    return pl.pallas_call(
        _cast_body,
        grid=(n_layers, n_rows // rows),
        in_specs=[block],
        out_specs=block,
        out_shape=jax.ShapeDtypeStruct(w.shape, BF16),
        compiler_params=_params(("parallel", "parallel"), 4 * rows * n_cols * 6 + (4 << 20)),
        name="cast_bf16",
    )(w)


FFN_CHUNK = 512


def _ffn_body(*refs, has_proj, has_norm_out):
    refs = list(refs)
    x_ref, g_ref, wgu_ref, wd_ref = refs[:4]
    a_ref, wp_ref = refs[4:6] if has_proj else (None, None)
    gn_ref = refs[4 + 2 * has_proj] if has_norm_out else None
    o_ref = refs[4 + 2 * has_proj + has_norm_out]
    x = x_ref[...]
    if has_proj:
        x = x + jnp.dot(a_ref[...], wp_ref[...], preferred_element_type=F32)
    h = _rms(x, g_ref[...]).astype(BF16)
    acc = x
    for c in range(0, D_FF, FFN_CHUNK):
        w = min(FFN_CHUNK, D_FF - c)
        gate = jnp.dot(h, wgu_ref[:, c:c + w].astype(BF16), preferred_element_type=F32)
        up = jnp.dot(h, wgu_ref[:, D_FF + c:D_FF + c + w].astype(BF16), preferred_element_type=F32)
        a = (gate * _sigmoid(gate) * up).astype(BF16)
        acc = acc + jnp.dot(a, wd_ref[c:c + w, :].astype(BF16), preferred_element_type=F32)
    o_ref[...] = acc
    if has_norm_out:
        hn_ref = refs[-1]
        hn_ref[...] = _rms(acc, gn_ref[...]).reshape(hn_ref.shape)


def _layer_block(shape, layer):
    return pl.BlockSpec((None,) + shape, lambda *_: (layer,) + (0,) * len(shape), pipeline_mode=pl.Buffered(1))


def _ffn(x2, g, wgu_all, wd_all, layer, proj=None, norm_out_g=None, tm=512):
    m = x2.shape[0]
    tile = pl.BlockSpec((tm, D_MODEL), lambda i: (i, 0))
    in_specs = [tile, _resident((1, D_MODEL)), _layer_block((D_MODEL, 2 * D_FF), layer),
                _layer_block((D_FF, D_MODEL), layer)]
    args = [x2, g.reshape(1, D_MODEL), wgu_all, wd_all]
    out_specs, out_shape = [tile], [jax.ShapeDtypeStruct((m, D_MODEL), F32)]
    if proj is not None:
        in_specs += [tile, _layer_block((D_MODEL, D_MODEL), proj[2])]
        args += [proj[0], proj[1]]
    if norm_out_g is not None:
        tiles_per_seq = SEQ // tm
        in_specs.append(_resident((1, D_MODEL)))
        args.append(norm_out_g.reshape(1, D_MODEL))
        out_specs.append(pl.BlockSpec((tm // S5_CHUNK, None, S5_CHUNK, D_MODEL),
                                      lambda i: (i % tiles_per_seq, i // tiles_per_seq, 0, 0)))
        out_shape.append(jax.ShapeDtypeStruct((S5_CHUNKS, BATCH, S5_CHUNK, D_MODEL), F32))
    vmem = 3 * D_MODEL * D_FF * 4 + 10 * tm * D_MODEL * 4 + (10 << 20)
    outs = pl.pallas_call(
        functools.partial(_ffn_body, has_proj=proj is not None, has_norm_out=norm_out_g is not None),
        grid=(m // tm,),
        in_specs=in_specs,
        out_specs=out_specs,
        out_shape=out_shape,
        compiler_params=_params(("parallel",), vmem),
        name="ffn",
    )(*args)
    return outs if norm_out_g is not None else outs[0]


def _fnet_weight_body(cc_ref, sc_ref, w_ref, o_ref):
    w = w_ref[...]
    o_ref[:, :D_MODEL] = jnp.dot(cc_ref[...], w, preferred_element_type=F32,
                                 precision=lax.Precision.HIGHEST).astype(BF16)
    o_ref[:, D_MODEL:] = jnp.dot(sc_ref[...], w, preferred_element_type=F32,
                                 precision=lax.Precision.HIGHEST).astype(BF16)


def _fnet_weights(w_out):
    n = np.arange(FOURIER_GROUP)
    ang = 2.0 * np.pi * ((n[:, None] * n[None, :]) % FOURIER_GROUP) / FOURIER_GROUP
    cc = jnp.asarray(np.cos(ang) / math.sqrt(FOURIER_GROUP), F32)
    sc = jnp.asarray(np.sin(ang) / math.sqrt(FOURIER_GROUP), F32)
    ng = D_MODEL // FOURIER_GROUP
    return pl.pallas_call(
        _fnet_weight_body,
        grid=(ng,),
        in_specs=[_resident((FOURIER_GROUP, FOURIER_GROUP)),
                  _resident((FOURIER_GROUP, FOURIER_GROUP)),
                  pl.BlockSpec((FOURIER_GROUP, D_MODEL), lambda i: (i, 0))],
        out_specs=pl.BlockSpec((FOURIER_GROUP, 2 * D_MODEL), lambda i: (i, 0)),
        out_shape=jax.ShapeDtypeStruct((D_MODEL, 2 * D_MODEL), BF16),
        compiler_params=_params(("parallel",), 16 << 20),
        name="fnet_weights",
    )(cc, sc, w_out)


FN_HALF = SEQ // 2
FN_BLK = 256
FN_NB = FN_HALF // FN_BLK


def _fnet_tables():
    k = np.arange(FN_HALF)[:, None]
    n = np.arange(FN_HALF)[None, :]
    ang = 2.0 * np.pi * ((k * n) % SEQ) / SEQ
    scale = 1.0 / math.sqrt(SEQ)
    i = np.arange(FN_BLK)
    rev = (i[None, :] == FN_BLK - i[:, None]).astype(np.float32)
    return (jnp.asarray(np.cos(ang) * scale, BF16), jnp.asarray(np.sin(ang) * scale, BF16),
            jnp.asarray(rev, BF16))


def _fnet_body(x_ref, g_ref, cs_ref, ss_ref, w_ref, rev_ref, o_ref, e_sc, d_sc, zc_sc, zs_sc):
    g = g_ref[...]
    scale = 1.0 / math.sqrt(SEQ)
    row = lax.broadcasted_iota(jnp.int32, (FN_BLK, 1), 0)
    first = row == 0
    sign = jnp.where((row & 1) == 0, 1.0, -1.0)
    rev = rev_ref[...]
    blk = lambda a: pl.ds(a * FN_BLK, FN_BLK)
    mirror = lambda a: pl.ds(SEQ - (a + 1) * FN_BLK, FN_BLK)

    alt = jnp.zeros((1, D_MODEL), F32)
    carry = jnp.zeros((1, D_MODEL), F32)
    for a in range(FN_NB):
        lo = _rms(x_ref[blk(a), :], g)
        hi = _rms(x_ref[mirror(a), :], g)
        alt = alt + jnp.sum(sign * lo, axis=0, keepdims=True) + jnp.sum(sign * hi, axis=0, keepdims=True)
        r = jnp.dot(rev, hi.astype(BF16), preferred_element_type=F32)
        r = jnp.where(first, carry, r)
        e_sc[blk(a), :] = (lo + r).astype(BF16)
        d_sc[blk(a), :] = (lo - r).astype(BF16)
        carry = hi[0:1, :]
    h_nyq = carry * scale

    for c in range(FN_NB):
        zc = jnp.dot(cs_ref[blk(c), :], e_sc[...], preferred_element_type=F32) + sign * h_nyq
        zs = jnp.dot(ss_ref[blk(c), :], d_sc[...], preferred_element_type=F32)
        zc_sc[blk(c), :] = zc.astype(BF16)
        zs_sc[blk(c), :] = zs.astype(BF16)

    wc = w_ref[:, :D_MODEL]
    ws = w_ref[:, D_MODEL:]
    z_nyq = jnp.broadcast_to(alt * scale, (8, D_MODEL)).astype(BF16)
    carry = jnp.dot(z_nyq, wc, preferred_element_type=F32)[0:1, :]
    def products(c):
        return (jnp.dot(zc_sc[blk(c), :], wc, preferred_element_type=F32),
                jnp.dot(zs_sc[blk(c), :], ws, preferred_element_type=F32))

    ahead = products(FN_NB - 1)
    for c in reversed(range(FN_NB)):
        p, q = ahead
        if c > 0:
            ahead = products(c - 1)
        o_ref[blk(c), :] = x_ref[blk(c), :] + (p - q)
        m = p + q
        m_hi = m.astype(BF16)
        m_lo = (m - m_hi.astype(F32)).astype(BF16)
        r = (jnp.dot(rev, m_hi, preferred_element_type=F32) + jnp.dot(rev, m_lo, preferred_element_type=F32))
        r = jnp.where(first, carry, r)
        o_ref[mirror(c), :] = x_ref[mirror(c), :] + r
        carry = m[0:1, :]


def _fnet_layer(x, g, w_out):
    wcs = _fnet_weights(w_out)
    cs, ss, rev = _fnet_tables()
    seq_block = pl.BlockSpec((None, SEQ, D_MODEL), lambda b: (b, 0, 0))
    half = pltpu.VMEM((FN_HALF, D_MODEL), BF16)
    vmem = 4 * SEQ * D_MODEL * 4 + 4 * FN_HALF * D_MODEL * 2 + 4 * FN_HALF * D_MODEL * 2 + (12 << 20)
    return pl.pallas_call(
        _fnet_body,
        grid=(BATCH,),
        in_specs=[seq_block, _resident((1, D_MODEL)), _resident((FN_HALF, FN_HALF)),
                  _resident((FN_HALF, FN_HALF)), _resident((D_MODEL, 2 * D_MODEL)),
                  _resident((FN_BLK, FN_BLK))],
        out_specs=seq_block,
        out_shape=jax.ShapeDtypeStruct((BATCH, SEQ, D_MODEL), F32),
        scratch_shapes=[half, half, half, half],
        compiler_params=_params(("parallel",), vmem),
        name="fnet_mix",
    )(x, g.reshape(1, D_MODEL), cs, ss, wcs, rev)


def _s5_matrices(lam_re, lam_im, log_dt, b_re, b_im, c_re, c_im):
    lam = lax.complex(lam_re.astype(F32), lam_im.astype(F32))
    dt = jnp.exp(log_dt.astype(F32))[..., None]
    lam_dt = lam * dt
    lam_bar = jnp.exp(lam_dt)
    b_bar = ((lam_bar - 1.0) / lam)[..., None] * lax.complex(b_re.astype(F32), b_im.astype(F32))
    c = lax.complex(c_re.astype(F32), c_im.astype(F32))
    L = S5_CHUNK
    taus = jnp.arange(L + 1, dtype=F32)
    pw = jnp.exp(lam_dt[..., None, :] * taus[:, None])
    def both(fwd, bwd):
        z = jnp.concatenate([fwd, bwd], axis=-1)
        return z.real[:, :, None, :], z.imag[:, :, None, :]
    def outer(pw_pair, mat_pair, sign):
        (ar, ai), (br, bi) = pw_pair, mat_pair
        re = ar * br - ai * bi
        im = ar * bi + ai * br
        return jnp.concatenate([re, sign * im], axis=-1).reshape(S5_GROUPS, -1, S5_CK)
    b_t = b_bar.transpose(0, 1, 3, 2)
    mat = lambda m: tuple(z[:, None, :, 0, :] for z in both(m[0], m[1]))
    s_in = outer(both(pw[0][:, ::-1][:, 1:], pw[1][:, :L]), mat(b_t), 1.0)
    wc_t = outer(both(pw[0][:, 1:], pw[1][:, ::-1][:, :L]), mat(c), -1.0)
    zeros = lambda n: jnp.zeros((S5_GROUPS, n, S5_STATE), pw.dtype)
    lag_f = jnp.concatenate([zeros(L - 1), pw[0][:, :L], zeros(1)], axis=1)
    lag_b = jnp.concatenate([pw[1][:, L - 1:0:-1], pw[1][:, :1], zeros(L)], axis=1)
    c_lag = outer(both(lag_f, lag_b), mat(c), -1.0)
    b_cat = jnp.concatenate(mat(b_t), axis=-1)[:, 0]
    strip = jnp.einsum('gqk,gjk->gqj', b_cat, c_lag, precision=lax.Precision.HIGHEST)
    al = pw[:, :, L]
    coef = jnp.stack([jnp.concatenate([al[0].real, al[1].real], -1),
                      jnp.concatenate([al[0].imag, al[1].imag], -1)], axis=1)
    return strip.astype(F32), s_in.astype(BF16), wc_t.astype(BF16), coef.astype(F32)


def _s5_norm_body(x_ref, g_ref, o_ref):
    h = _rms(x_ref[...], g_ref[...])
    o_ref[...] = h.reshape(o_ref.shape)


def _s5_norm(x, g, tm=512):
    cr = tm // S5_CHUNK
    return pl.pallas_call(
        _s5_norm_body,
        grid=(BATCH, SEQ // tm),
        in_specs=[pl.BlockSpec((None, tm, D_MODEL), lambda b, i: (b, i, 0)),
                  _resident((1, D_MODEL))],
        out_specs=pl.BlockSpec((cr, None, S5_CHUNK, D_MODEL), lambda b, i: (i, b, 0, 0)),
        out_shape=jax.ShapeDtypeStruct((S5_CHUNKS, BATCH, S5_CHUNK, D_MODEL), F32),
        compiler_params=_params(("parallel", "parallel"), 32 << 20),
        name="s5_norm",
    )(x, g.reshape(1, D_MODEL))


def _block_transpose8(a, lane_block):
    a = list(a)
    for d in (4, 2, 1):
        take_lo = (lane_block & d) == 0
        nxt = list(a)
        for i in range(8):
            if i & d:
                continue
            lo, hi = a[i], a[i + d]
            nxt[i] = jnp.where(take_lo, lo, pltpu.roll(hi, S5_GROUP * d, 1))
            nxt[i + d] = jnp.where(take_lo, pltpu.roll(lo, LANES - S5_GROUP * d, 1), hi)
        a = nxt
    return a


def _gelu_tanh(y):
    return 0.5 * y * (1.0 + jnp.tanh(math.sqrt(2.0 / math.pi) * (y + 0.044715 * (y * y * y))))


S5_RB = 128


def _s5_body(h_ref, strip_ref, ws_ref, wc_ref, coef_ref, d_ref, o_ref, x_sc, y_sc, toep_sc):
    lane = lax.broadcasted_iota(jnp.int32, (1, LANES), 1)
    lane_block = lane // S5_GROUP
    fwd_lanes = lane < S5_STATE
    n_rb = S5_ROWS // S5_RB
    tok_rb = S5_RB * S5_CHUNK

    def relayout_in(i, carry):
        base = pl.multiple_of(i * tok_rb, tok_rb)
        rows = pl.multiple_of(i * S5_RB, S5_RB)
        for half in range(2):
            a = [pltpu.bitcast(h_ref[pl.ds(base + half * 8 + t, S5_RB, stride=S5_CHUNK), :].astype(BF16), jnp.uint32)
                 for t in range(8)]
            xt = _block_transpose8(a, lane_block)
            for gi in range(S5_TILE_GROUPS):
                x_sc[gi, pl.ds(rows, S5_RB), half * LANES:(half + 1) * LANES] = pltpu.bitcast(xt[gi], BF16)
        return carry

    lax.fori_loop(0, n_rb, relayout_in, 0)

    nt = (((1,), (1,)), ((), ()))
    for gi in range(S5_TILE_GROUPS):
        y_sc[gi] = jnp.dot(x_sc[gi], ws_ref[gi], preferred_element_type=F32)
        strip = strip_ref[gi]
        for t in range(S5_CHUNK):
            off = (S5_CHUNK - 1 - t) * S5_GROUP
            win = strip if off == 0 else pltpu.roll(strip, strip.shape[1] - off, 1)
            toep_sc[gi, t * S5_GROUP:(t + 1) * S5_GROUP, :] = win[:, :S5_CK].astype(BF16)

    coef = [(coef_ref[gi, 0:1, :], coef_ref[gi, 1:2, :]) for gi in range(S5_TILE_GROUPS)]

    def scan_step(k, st):
        rf = pl.ds(pl.multiple_of(k * BATCH, BATCH), BATCH)
        rb = pl.ds(pl.multiple_of((S5_CHUNKS - 1 - k) * BATCH, BATCH), BATCH)
        s_in = [(jnp.where(fwd_lanes, y_sc[gi, rf, :LANES], y_sc[gi, rb, :LANES]),
                 jnp.where(fwd_lanes, y_sc[gi, rf, LANES:], y_sc[gi, rb, LANES:])) for gi in range(S5_TILE_GROUPS)]
        nxt = []
        for gi in range(S5_TILE_GROUPS):
            st_re, st_im = st[2 * gi], st[2 * gi + 1]
            a_re, a_im = coef[gi]
            s_re, s_im = s_in[gi]
            y_sc[gi, rf, :S5_STATE] = st_re[:, :S5_STATE]
            y_sc[gi, rb, S5_STATE:LANES] = st_re[:, S5_STATE:]
            y_sc[gi, rf, LANES:LANES + S5_STATE] = st_im[:, :S5_STATE]
            y_sc[gi, rb, LANES + S5_STATE:] = st_im[:, S5_STATE:]
            nxt.append(a_re * st_re - a_im * st_im + s_re)
            nxt.append(a_re * st_im + a_im * st_re + s_im)
        return tuple(nxt)

    zero = jnp.zeros((BATCH, LANES), F32)
    lax.fori_loop(0, S5_CHUNKS, scan_step, (zero,) * (2 * S5_TILE_GROUPS))

    for gi in range(S5_TILE_GROUPS):
        h_in = y_sc[gi].astype(BF16)
        y_sc[gi] = (jnp.dot(x_sc[gi], toep_sc[gi], preferred_element_type=F32)
                    + lax.dot_general(h_in, wc_ref[gi], nt, preferred_element_type=F32))

    d_skip = d_ref[...]

    def relayout_out(i, carry):
        base = pl.multiple_of(i * tok_rb, tok_rb)
        rows = pl.multiple_of(i * S5_RB, S5_RB)
        for half in range(2):
            yg = [y_sc[gi, pl.ds(rows, S5_RB), half * LANES:(half + 1) * LANES] for gi in range(S5_TILE_GROUPS)]
            yt = _block_transpose8(yg, lane_block)
            for t in range(8):
                tok = pl.ds(base + half * 8 + t, S5_RB, stride=S5_CHUNK)
                o_ref[tok, :] = _gelu_tanh(yt[t] + d_skip * h_ref[tok, :])
        return carry

    lax.fori_loop(0, n_rb, relayout_out, 0)


def _s5_core(h2, strip, s_in, wc_t, coef, d_skip):
    n_tiles = S5_GROUPS // S5_TILE_GROUPS
    tok_block = pl.BlockSpec((N_TOKENS, LANES), lambda j: (0, j))
    vmem = (4 * N_TOKENS * LANES * 4 + S5_TILE_GROUPS * S5_ROWS * S5_CK * (2 + 4)
            + 4 * S5_ROWS * LANES * 4 + 2 * S5_TILE_GROUPS * S5_CK * 3 * S5_CK * 2 + (12 << 20))
    return pl.pallas_call(
        _s5_body,
        grid=(n_tiles,),
        in_specs=[tok_block,
                  pl.BlockSpec((S5_TILE_GROUPS, S5_GROUP, 2 * S5_CK), lambda j: (j, 0, 0)),
                  pl.BlockSpec((S5_TILE_GROUPS, S5_CK, S5_CK), lambda j: (j, 0, 0)),
                  pl.BlockSpec((S5_TILE_GROUPS, S5_CK, S5_CK), lambda j: (j, 0, 0)),
                  pl.BlockSpec((S5_TILE_GROUPS, 2, LANES), lambda j: (j, 0, 0)),
                  pl.BlockSpec((1, LANES), lambda j: (0, j))],
        out_specs=tok_block,
        out_shape=jax.ShapeDtypeStruct((N_TOKENS, D_MODEL), F32),
        scratch_shapes=[pltpu.VMEM((S5_TILE_GROUPS, S5_ROWS, S5_CK), BF16),
                        pltpu.VMEM((S5_TILE_GROUPS, S5_ROWS, S5_CK), F32),
                        pltpu.VMEM((S5_TILE_GROUPS, S5_CK, S5_CK), BF16)],
        compiler_params=_params(("arbitrary",), vmem),
        name="s5_core",
    )(h2, strip, s_in, wc_t, coef, d_skip.reshape(1, D_MODEL))


def _s5_glu_body(x_ref, a_ref, w_ref, o_ref):
    a = a_ref[...].reshape(x_ref.shape).astype(BF16)
    vg = jnp.dot(a, w_ref[...], preferred_element_type=F32)
    o_ref[...] = x_ref[...] + vg[:, :D_MODEL] * _sigmoid(vg[:, D_MODEL:])


def _s5_glu(x, act, w_glu_all, layer, tm=512):
    cr = tm // S5_CHUNK
    vmem = 2 * D_MODEL * D_MODEL * 2 + 6 * tm * D_MODEL * 4 + 2 * tm * 2 * D_MODEL * 4 + (4 << 20)
    return pl.pallas_call(
        _s5_glu_body,
        grid=(BATCH, SEQ // tm),
        in_specs=[pl.BlockSpec((None, tm, D_MODEL), lambda b, i: (b, i, 0)),
                  pl.BlockSpec((cr, None, S5_CHUNK, D_MODEL), lambda b, i: (i, b, 0, 0)),
                  _layer_block((D_MODEL, 2 * D_MODEL), layer)],
        out_specs=pl.BlockSpec((None, tm, D_MODEL), lambda b, i: (b, i, 0)),
        out_shape=jax.ShapeDtypeStruct((BATCH, SEQ, D_MODEL), F32),
        compiler_params=_params(("parallel", "parallel"), vmem),
        name="s5_glu",
    )(x, act, w_glu_all)


def _s5_layer(x, h, lam_re, lam_im, log_dt, b_re, b_im, c_re, c_im, d_skip, w_glu_all, layer):
    strip, s_in, wc_t, coef = _s5_matrices(lam_re, lam_im, log_dt, b_re, b_im, c_re, c_im)
    act = _s5_core(h.reshape(N_TOKENS, D_MODEL), strip, s_in, wc_t, coef, d_skip)
    act = act.reshape(S5_CHUNKS, BATCH, S5_CHUNK, D_MODEL)
    return _s5_glu(x, act, w_glu_all, layer)


def _t5_bucket(rel):
    half = NUM_BUCKETS // 2
    max_exact = half // 2
    n = np.abs(rel)
    sign = (rel > 0).astype(np.int32) * half
    large = max_exact + (np.log(np.maximum(n, 1) / max_exact) / math.log(MAX_DISTANCE / max_exact)
                         * (half - max_exact)).astype(np.int32)
    large = np.minimum(large, half - 1)
    return (sign + np.where(n < max_exact, n, large)).astype(np.int32)


LOG2E = 1.4426950408889634
N_PAIRS = HEADS_PER_GROUP // 2
ATTN_PAD_ROWS = SEQ + 2 * ATTN_SIDE * max(d for _, d in DILATED_GROUPS)
QKV_GROUP_WIDTH = 3 * D_MODEL
QKV_SLAB = 2 * LANES
QKV_TILE_ROWS = 512


def _attn_bias_tables(rel_bias):
    n_off = ATTN_BK + ATTN_BQ - 1
    offs = np.arange(n_off) - (ATTN_BQ - 1) - ATTN_SIDE
    strips = []
    for gi, (_, dil) in enumerate(DILATED_GROUPS):
        onehot = jnp.asarray(_t5_bucket(offs * dil)[:, None] == np.arange(NUM_BUCKETS), F32)
        f = jnp.dot(onehot, rel_bias[:, gi * HEADS_PER_GROUP:(gi + 1) * HEADS_PER_GROUP].astype(F32),
                    precision=lax.Precision.HIGHEST).T
        strips.append(jnp.pad(f[:, ::-1], ((0, 0), (0, BIAS_STRIP - n_off))))
    strips = jnp.stack(strips).reshape(N_ATTN_GROUPS, HEADS_PER_GROUP, 1, BIAS_STRIP)
    return pl.pallas_call(
        _attn_bias_body,
        grid=(N_ATTN_GROUPS, N_PAIRS),
        in_specs=[pl.BlockSpec((None, 2, 1, BIAS_STRIP), lambda g, p: (g, p, 0, 0))],
        out_specs=pl.BlockSpec((None, None, ATTN_BK, 2 * ATTN_BQ), lambda g, p: (g, p, 0, 0)),
        out_shape=jax.ShapeDtypeStruct((N_ATTN_GROUPS, N_PAIRS, ATTN_BK, 2 * ATTN_BQ), F32),
        compiler_params=_params(("parallel", "parallel"), 16 << 20),
        name="attn_bias",
    )(strips)


BIAS_STRIP = 256


def _attn_bias_body(f_ref, o_ref):
    krow = lax.broadcasted_iota(jnp.int32, (ATTN_BK, 1), 0)
    lane = lax.broadcasted_iota(jnp.int32, (1, LANES), 1)
    head0 = lane < ATTN_BQ
    band = jnp.abs(krow - ATTN_SIDE - (lane & (ATTN_BQ - 1))) <= ATTN_SIDE
    halves = []
    for a in range(2):
        x = jnp.broadcast_to(f_ref[a], (ATTN_BK, BIAS_STRIP))
        shift = (a * ATTN_BQ - (ATTN_BK - 1)) % BIAS_STRIP
        halves.append(pltpu.roll(x, shift, 1, stride=1, stride_axis=0)[:, :LANES])
    o_ref[...] = jnp.where(band, jnp.where(head0, halves[0], halves[1]) * LOG2E, -1e30)


def _qkv_body(x_ref, g_ref, w_ref, qg_ref, kg_ref, perm_ref, q_out, k_out, v_out, *, n_res):
    rows = x_ref.shape[0] // n_res
    h = _rms(x_ref[...], g_ref[...]).astype(BF16)
    if n_res > 1:
        h = jnp.dot(perm_ref[...], h, preferred_element_type=F32).astype(BF16)
    lane = lax.broadcasted_iota(jnp.int32, (1, LANES), 1)
    head0 = lane < HEAD_DIM

    def head_norm(t, gain):
        sq = t * t
        s0 = jnp.sum(jnp.where(head0, sq, 0.0), axis=-1, keepdims=True)
        s1 = jnp.sum(jnp.where(head0, 0.0, sq), axis=-1, keepdims=True)
        ms = jnp.where(head0, s0, s1) * (1.0 / HEAD_DIM)
        return t * lax.rsqrt(ms + EPS) * gain

    for c in range(QKV_GROUP_WIDTH // QKV_SLAB):
        z = jnp.dot(h, w_ref[:, c * QKV_SLAB:(c + 1) * QKV_SLAB].astype(BF16), preferred_element_type=F32)
        for half in range(QKV_SLAB // LANES):
            section, lo = divmod(c * QKV_SLAB + half * LANES, D_MODEL)
            zz = z[:, half * LANES:(half + 1) * LANES]
            if section == 0:
                zz = head_norm(zz, qg_ref[...]) * (HEAD_DIM ** -0.5 * LOG2E)
                qa = jnp.where(head0, zz, 0.0).astype(BF16)
                qb = jnp.where(head0, 0.0, zz).astype(BF16)
            elif section == 1:
                zz = head_norm(zz, kg_ref[...]).astype(BF16)
            else:
                zz = zz.astype(BF16)
            for r in range(n_res):
                piece = slice(r * rows, (r + 1) * rows)
                if section == 0:
                    q_out[r, 0, :, lo:lo + LANES] = qa[piece]
                    q_out[r, 1, :, lo:lo + LANES] = qb[piece]
                elif section == 1:
                    k_out[r, :, lo:lo + LANES] = zz[piece]
                else:
                    v_out[r, :, lo:lo + LANES] = zz[piece]


def _qkv_group(x, g, w_qkv_all, layer, gi, dil, q_gain, k_gain):
    seg = SEQ // dil
    tm = QKV_TILE_ROWS
    rows = tm // dil
    assert tm % dil == 0 and rows % 16 == 0
    gain2 = lambda gn: jnp.tile(gn.astype(F32), 2).reshape(1, LANES)
    src = np.arange(tm).reshape(rows, dil).T.reshape(tm)
    perm = jnp.asarray(src[:, None] == np.arange(tm)[None, :], BF16)
    kv_spec = pl.BlockSpec((None, dil, rows, D_MODEL), lambda b, i: (b, 0, i, 0))
    kv_shape = jax.ShapeDtypeStruct((BATCH, dil, seg, D_MODEL), BF16)
    vmem = (D_MODEL * QKV_GROUP_WIDTH * 4 + 2 * tm * D_MODEL * 4 + 2 * 4 * tm * D_MODEL * 2 + (12 << 20))
    return pl.pallas_call(
        functools.partial(_qkv_body, n_res=dil),
        grid=(BATCH, SEQ // tm),
        in_specs=[pl.BlockSpec((None, tm, D_MODEL), lambda b, i: (b, i, 0)),
                  _resident((1, D_MODEL)),
                  pl.BlockSpec((None, D_MODEL, QKV_GROUP_WIDTH), lambda b, i: (layer, 0, gi),
                               pipeline_mode=pl.Buffered(1)),
                  _resident((1, LANES)), _resident((1, LANES)), _resident((tm, tm))],
        out_specs=[pl.BlockSpec((None, dil, 2, rows, D_MODEL), lambda b, i: (b, 0, 0, i, 0)),
                   kv_spec, kv_spec],
        out_shape=[jax.ShapeDtypeStruct((BATCH, dil, 2, seg, D_MODEL), BF16), kv_shape, kv_shape],
        compiler_params=_params(("parallel", "parallel"), vmem),
        name=f"qkv_dil{dil}",
    )(x, g.reshape(1, D_MODEL), w_qkv_all, gain2(q_gain), gain2(k_gain), perm)


ATTN_AHEAD = 3


def _attn_body(q0, k0, v0, q1, k1, v1, q2, k2, v2, bias_ref, o_ref, kp_sc, vp_sc, lt_sc, og_sc, lg_sc):
    lane = lax.broadcasted_iota(jnp.int32, (1, LANES), 1)
    head0 = lane < HEAD_DIM
    krow = lax.broadcasted_iota(jnp.int32, (ATTN_BK, 1), 0)
    zpad = jnp.zeros((ATTN_SIDE, LANES), BF16)
    tn = (((0,), (0,)), ((), ()))

    qkv_refs = ((q0, k0, v0), (q1, k1, v1), (q2, k2, v2))
    for g, (_, dil) in enumerate(DILATED_GROUPS):
        q_ref, k_ref, v_ref = qkv_refs[g]
        seg = SEQ // dil
        nb = seg // ATTN_BQ
        pad_seg = seg + 2 * ATTN_SIDE
        lt_sc[...] = jnp.zeros(lt_sc.shape, F32)
        for r in range(dil):
            base = r * pad_seg
            for src, dst in ((k_ref, kp_sc), (v_ref, vp_sc)):
                dst[base:base + ATTN_SIDE] = zpad
                dst[base + ATTN_SIDE + seg:base + pad_seg] = zpad
                dst[base + ATTN_SIDE:base + ATTN_SIDE + seg] = src[r]

        def scores(r, i):
            base = r * pad_seg
            win = slice(base + i * ATTN_BQ, base + i * ATTN_BQ + ATTN_BK)
            qrows = slice(i * ATTN_BQ, (i + 1) * ATTN_BQ)
            qm = jnp.concatenate([q_ref[r, 0, qrows, :], q_ref[r, 1, qrows, :]], axis=0)
            s = lax.dot_general(kp_sc[win], qm, (((1,), (1,)), ((), ())), preferred_element_type=F32)
            s = s + bias_ref[g]
            if i == 0 or i == nb - 1:
                ok = None
                if i == 0:
                    ok = krow >= ATTN_SIDE
                if i == nb - 1:
                    ok_hi = krow < ATTN_BK - ATTN_SIDE
                    ok = ok_hi if ok is None else jnp.logical_and(ok, ok_hi)
                s = jnp.where(ok, s, -1e30)
            return s

        blocks = [(r, i) for r in range(dil) for i in range(nb)]
        pending = [scores(*blocks[j]) for j in range(min(ATTN_AHEAD, len(blocks)))]
        for blk, (r, i) in enumerate(blocks):
            if blk + ATTN_AHEAD < len(blocks):
                pending.append(scores(*blocks[blk + ATTN_AHEAD]))
            s = pending.pop(0)
            base = r * pad_seg
            win = slice(base + i * ATTN_BQ, base + i * ATTN_BQ + ATTN_BK)
            m = jnp.max(s, axis=0, keepdims=True)
            p = jnp.exp2(s - m)
            den = jnp.sum(p, axis=0, keepdims=True)
            pn = (p * (1.0 / den)).astype(BF16)
            u = lax.dot_general(pn, vp_sc[win], tn, preferred_element_type=F32)
            rows = pl.ds(r + i * ATTN_BQ * dil, ATTN_BQ, stride=dil)
            og_sc[g, rows, :] = jnp.where(head0, u[:ATTN_BQ], u[ATTN_BQ:])
            lt_sc[blk:blk + 1, :] = m + jnp.log2(den)

        ltt = lt_sc[...].T
        for r in range(dil):
            for i in range(nb):
                c = r * nb + i
                rows = pl.ds(r + i * ATTN_BQ * dil, ATTN_BQ, stride=dil)
                lg_sc[g, rows, :] = jnp.where(head0, ltt[:ATTN_BQ, c:c + 1], ltt[ATTN_BQ:, c:c + 1])

    l0, l1, l2 = lg_sc[0], lg_sc[1], lg_sc[2]
    m = jnp.maximum(jnp.maximum(l0, l1), l2)
    w0, w1, w2 = jnp.exp2(l0 - m), jnp.exp2(l1 - m), jnp.exp2(l2 - m)
    o = (w0 * og_sc[0] + w1 * og_sc[1] + w2 * og_sc[2]) / (w0 + w1 + w2)
    o_ref[...] = o.astype(o_ref.dtype)


def _attn_core(qkv, bias_tabs):
    in_specs = []
    for (_, dil) in DILATED_GROUPS:
        seg = SEQ // dil
        in_specs.append(pl.BlockSpec((None, dil, 2, seg, LANES), lambda b, p: (b, 0, 0, 0, p)))
        in_specs.append(pl.BlockSpec((None, dil, seg, LANES), lambda b, p: (b, 0, 0, p)))
        in_specs.append(pl.BlockSpec((None, dil, seg, LANES), lambda b, p: (b, 0, 0, p)))
    in_specs.append(pl.BlockSpec((N_ATTN_GROUPS, None, ATTN_BK, 2 * ATTN_BQ), lambda b, p: (0, p, 0, 0)))
    return pl.pallas_call(
        _attn_body,
        grid=(BATCH, N_PAIRS),
        in_specs=in_specs,
        out_specs=pl.BlockSpec((None, SEQ, LANES), lambda b, p: (b, 0, p)),
        out_shape=jax.ShapeDtypeStruct((BATCH, SEQ, D_MODEL), BF16),
        scratch_shapes=[pltpu.VMEM((ATTN_PAD_ROWS, LANES), BF16),
                        pltpu.VMEM((ATTN_PAD_ROWS, LANES), BF16),
                        pltpu.VMEM((LANES, LANES), F32),
                        pltpu.VMEM((N_ATTN_GROUPS, SEQ, LANES), F32),
                        pltpu.VMEM((N_ATTN_GROUPS, SEQ, LANES), F32)],
        compiler_params=_params(("parallel", "parallel"), 40 << 20),
        name="attn_core",
    )(*qkv, bias_tabs)


def _attn_layer(x, g, w_qkv_all, q_gain, k_gain, rel_bias, layer):
    qkv = []
    for gi, (window, dil) in enumerate(DILATED_GROUPS):
        assert (window // 2) // dil == ATTN_SIDE and (SEQ // dil) % ATTN_BQ == 0
        qkv.extend(_qkv_group(x, g, w_qkv_all, layer, gi, dil, q_gain[gi], k_gain[gi]))
    return _attn_core(qkv, _attn_bias_tables(rel_bias)).reshape(N_TOKENS, D_MODEL)


def kernel(x, norm_mix_g, norm_ffn_g, fnet_w_out, s5_lambda_re, s5_lambda_im, s5_log_dt, s5_b_re, s5_b_im, s5_c_re, s5_c_im, s5_d, s5_w_glu, attn_w_qkv, attn_q_gain, attn_k_gain, attn_w_o, rel_bias, ffn_w_gate_up, ffn_w_down):
    w_gate_up, w_down, w_qkv = ffn_w_gate_up, ffn_w_down, attn_w_qkv
    w_glu, w_o = _cast_bf16(s5_w_glu), _cast_bf16(attn_w_o)
    counts = [0, 0, 0]
    h_next = None
    for i in range(DEPTH):
        kind = i % 3
        j = counts[kind]
        counts[kind] += 1
        proj = None
        if kind == 0:
            x = _fnet_layer(x, norm_mix_g[i], fnet_w_out[j])
        elif kind == 1:
            h = h_next if h_next is not None else _s5_norm(x, norm_mix_g[i])
            x = _s5_layer(x, h, s5_lambda_re[j], s5_lambda_im[j], s5_log_dt[j], s5_b_re[j],
                          s5_b_im[j], s5_c_re[j], s5_c_im[j], s5_d[j], w_glu, j)
        else:
            proj = (_attn_layer(x, norm_mix_g[i], w_qkv, attn_q_gain[j], attn_k_gain[j], rel_bias, j), w_o, j)
        next_is_s5 = i + 1 < DEPTH and (i + 1) % 3 == 1
        out = _ffn(x.reshape(N_TOKENS, D_MODEL), norm_ffn_g[i], w_gate_up, w_down, i, proj=proj,
                   norm_out_g=norm_mix_g[i + 1] if next_is_s5 else None)
        x, h_next = out if next_is_s5 else (out, None)
        x = x.reshape(BATCH, SEQ, D_MODEL)
    return x
```

```python
import functools
import math

import numpy as np
import jax
import jax.numpy as jnp
from jax import lax
from jax.experimental import pallas as pl
from jax.experimental.pallas import tpu as pltpu

F32 = jnp.float32
BF16 = jnp.bfloat16

D_MODEL = 1024
BATCH = 8
SEQ = 2048
DEPTH = 4
N_TOKENS = BATCH * SEQ
EPS = 1e-6
D_FF = 2816
FOURIER_GROUP = 128
S5_GROUP = 16
S5_GROUPS = 64
S5_STATE = 64
HEAD_DIM = 64
HEADS_PER_GROUP = 16
DILATED_GROUPS = ((128, 1), (512, 4), (2048, 16))
N_ATTN_GROUPS = 3
NUM_BUCKETS = 32
MAX_DISTANCE = 1024
ATTN_SIDE = 64

LANES = 128
VMEM_LIMIT_CAP = 60 * 1024 * 1024

S5_CHUNK = 16
S5_CHUNKS = SEQ // S5_CHUNK
S5_ROWS = BATCH * S5_CHUNKS
S5_TILE_GROUPS = LANES // S5_GROUP
S5_CK = S5_CHUNK * S5_GROUP

ATTN_BQ = 64
ATTN_BK = ATTN_BQ + 2 * ATTN_SIDE


def _params(sem, vmem_bytes):
    return pltpu.CompilerParams(dimension_semantics=sem,
                                vmem_limit_bytes=int(min(VMEM_LIMIT_CAP, vmem_bytes)))


def _rms(x, g):
    ms = jnp.mean(x * x, axis=-1, keepdims=True)
    return x * lax.rsqrt(ms + EPS) * g


def _sigmoid(x):
    return 1.0 / (1.0 + jnp.exp(-x))


def _resident(shape):
    nd = len(shape)
    return pl.BlockSpec(shape, lambda *_: (0,) * nd, pipeline_mode=pl.Buffered(1))


def _cast_body(w_ref, o_ref):
    o_ref[...] = w_ref[...].astype(BF16)


def _cast_bf16(w, rows=256):
    n_layers, n_rows, n_cols = w.shape
    block = pl.BlockSpec((None, rows, n_cols), lambda l, i: (l, i, 0))
    return pl.pallas_call(
        _cast_body,
        grid=(n_layers, n_rows // rows),
        in_specs=[block],
        out_specs=block,
        out_shape=jax.ShapeDtypeStruct(w.shape, BF16),
        compiler_params=_params(("parallel", "parallel"), 4 * rows * n_cols * 6 + (4 << 20)),
        name="cast_bf16",
    )(w)


FFN_CHUNK = 512


def _ffn_body(*refs, has_proj, has_norm_out):
    refs = list(refs)
    x_ref, g_ref, wgu_ref, wd_ref = refs[:4]
    a_ref, wp_ref = refs[4:6] if has_proj else (None, None)
    gn_ref = refs[4 + 2 * has_proj] if has_norm_out else None
    o_ref = refs[4 + 2 * has_proj + has_norm_out]
    x = x_ref[...]
    if has_proj:
        x = x + jnp.dot(a_ref[...], wp_ref[...], preferred_element_type=F32)
    h = _rms(x, g_ref[...]).astype(BF16)
    acc = x
    for c in range(0, D_FF, FFN_CHUNK):
        w = min(FFN_CHUNK, D_FF - c)
        gate = jnp.dot(h, wgu_ref[:, c:c + w].astype(BF16), preferred_element_type=F32)
        up = jnp.dot(h, wgu_ref[:, D_FF + c:D_FF + c + w].astype(BF16), preferred_element_type=F32)
        a = (gate * _sigmoid(gate) * up).astype(BF16)
        acc = acc + jnp.dot(a, wd_ref[c:c + w, :].astype(BF16), preferred_element_type=F32)
    o_ref[...] = acc
    if has_norm_out:
        hn_ref = refs[-1]
        hn_ref[...] = _rms(acc, gn_ref[...]).reshape(hn_ref.shape)


def _layer_block(shape, layer):
    return pl.BlockSpec((None,) + shape, lambda *_: (layer,) + (0,) * len(shape), pipeline_mode=pl.Buffered(1))


def _ffn(x2, g, wgu_all, wd_all, layer, proj=None, norm_out_g=None, tm=512):
    m = x2.shape[0]
    tile = pl.BlockSpec((tm, D_MODEL), lambda i: (i, 0))
    in_specs = [tile, _resident((1, D_MODEL)), _layer_block((D_MODEL, 2 * D_FF), layer),
                _layer_block((D_FF, D_MODEL), layer)]
    args = [x2, g.reshape(1, D_MODEL), wgu_all, wd_all]
    out_specs, out_shape = [tile], [jax.ShapeDtypeStruct((m, D_MODEL), F32)]
    if proj is not None:
        in_specs += [tile, _layer_block((D_MODEL, D_MODEL), proj[2])]
        args += [proj[0], proj[1]]
    if norm_out_g is not None:
        tiles_per_seq = SEQ // tm
        in_specs.append(_resident((1, D_MODEL)))
        args.append(norm_out_g.reshape(1, D_MODEL))
        out_specs.append(pl.BlockSpec((tm // S5_CHUNK, None, S5_CHUNK, D_MODEL),
                                      lambda i: (i % tiles_per_seq, i // tiles_per_seq, 0, 0)))
        out_shape.append(jax.ShapeDtypeStruct((S5_CHUNKS, BATCH, S5_CHUNK, D_MODEL), F32))
    vmem = 3 * D_MODEL * D_FF * 4 + 10 * tm * D_MODEL * 4 + (10 << 20)
    outs = pl.pallas_call(
        functools.partial(_ffn_body, has_proj=proj is not None, has_norm_out=norm_out_g is not None),
        grid=(m // tm,),
        in_specs=in_specs,
        out_specs=out_specs,
        out_shape=out_shape,
        compiler_params=_params(("parallel",), vmem),
        name="ffn",
    )(*args)
    return outs if norm_out_g is not None else outs[0]


def _fnet_weight_body(cc_ref, sc_ref, w_ref, o_ref):
    w = w_ref[...]
    o_ref[:, :D_MODEL] = jnp.dot(cc_ref[...], w, preferred_element_type=F32,
                                 precision=lax.Precision.HIGHEST).astype(BF16)
    o_ref[:, D_MODEL:] = jnp.dot(sc_ref[...], w, preferred_element_type=F32,
                                 precision=lax.Precision.HIGHEST).astype(BF16)


def _fnet_weights(w_out):
    n = np.arange(FOURIER_GROUP)
    ang = 2.0 * np.pi * ((n[:, None] * n[None, :]) % FOURIER_GROUP) / FOURIER_GROUP
    cc = jnp.asarray(np.cos(ang) / math.sqrt(FOURIER_GROUP), F32)
    sc = jnp.asarray(np.sin(ang) / math.sqrt(FOURIER_GROUP), F32)
    ng = D_MODEL // FOURIER_GROUP
    return pl.pallas_call(
        _fnet_weight_body,
        grid=(ng,),
        in_specs=[_resident((FOURIER_GROUP, FOURIER_GROUP)),
                  _resident((FOURIER_GROUP, FOURIER_GROUP)),
                  pl.BlockSpec((FOURIER_GROUP, D_MODEL), lambda i: (i, 0))],
        out_specs=pl.BlockSpec((FOURIER_GROUP, 2 * D_MODEL), lambda i: (i, 0)),
        out_shape=jax.ShapeDtypeStruct((D_MODEL, 2 * D_MODEL), BF16),
        compiler_params=_params(("parallel",), 16 << 20),
        name="fnet_weights",
    )(cc, sc, w_out)


FN_HALF = SEQ // 2
FN_BLK = 256
FN_NB = FN_HALF // FN_BLK


def _fnet_tables():
    k = np.arange(FN_HALF)[:, None]
    n = np.arange(FN_HALF)[None, :]
    ang = 2.0 * np.pi * ((k * n) % SEQ) / SEQ
    scale = 1.0 / math.sqrt(SEQ)
    i = np.arange(FN_BLK)
    rev = (i[None, :] == FN_BLK - i[:, None]).astype(np.float32)
    return (jnp.asarray(np.cos(ang) * scale, BF16), jnp.asarray(np.sin(ang) * scale, BF16),
            jnp.asarray(rev, BF16))


def _fnet_body(x_ref, g_ref, cs_ref, ss_ref, w_ref, rev_ref, o_ref, e_sc, d_sc, zc_sc, zs_sc):
    g = g_ref[...]
    scale = 1.0 / math.sqrt(SEQ)
    row = lax.broadcasted_iota(jnp.int32, (FN_BLK, 1), 0)
    first = row == 0
    sign = jnp.where((row & 1) == 0, 1.0, -1.0)
    rev = rev_ref[...]
    blk = lambda a: pl.ds(a * FN_BLK, FN_BLK)
    mirror = lambda a: pl.ds(SEQ - (a + 1) * FN_BLK, FN_BLK)

    alt = jnp.zeros((1, D_MODEL), F32)
    carry = jnp.zeros((1, D_MODEL), F32)
    for a in range(FN_NB):
        lo = _rms(x_ref[blk(a), :], g)
        hi = _rms(x_ref[mirror(a), :], g)
        alt = alt + jnp.sum(sign * lo, axis=0, keepdims=True) + jnp.sum(sign * hi, axis=0, keepdims=True)
        r = jnp.dot(rev, hi.astype(BF16), preferred_element_type=F32)
        r = jnp.where(first, carry, r)
        e_sc[blk(a), :] = (lo + r).astype(BF16)
        d_sc[blk(a), :] = (lo - r).astype(BF16)
        carry = hi[0:1, :]
    h_nyq = carry * scale

    for c in range(FN_NB):
        zc = jnp.dot(cs_ref[blk(c), :], e_sc[...], preferred_element_type=F32) + sign * h_nyq
        zs = jnp.dot(ss_ref[blk(c), :], d_sc[...], preferred_element_type=F32)
        zc_sc[blk(c), :] = zc.astype(BF16)
        zs_sc[blk(c), :] = zs.astype(BF16)

    wc = w_ref[:, :D_MODEL]
    ws = w_ref[:, D_MODEL:]
    z_nyq = jnp.broadcast_to(alt * scale, (8, D_MODEL)).astype(BF16)
    carry = jnp.dot(z_nyq, wc, preferred_element_type=F32)[0:1, :]
    def products(c):
        return (jnp.dot(zc_sc[blk(c), :], wc, preferred_element_type=F32),
                jnp.dot(zs_sc[blk(c), :], ws, preferred_element_type=F32))

    ahead = products(FN_NB - 1)
    for c in reversed(range(FN_NB)):
        p, q = ahead
        if c > 0:
            ahead = products(c - 1)
        o_ref[blk(c), :] = x_ref[blk(c), :] + (p - q)
        m = p + q
        m_hi = m.astype(BF16)
        m_lo = (m - m_hi.astype(F32)).astype(BF16)
        r = (jnp.dot(rev, m_hi, preferred_element_type=F32) + jnp.dot(rev, m_lo, preferred_element_type=F32))
        r = jnp.where(first, carry, r)
        o_ref[mirror(c), :] = x_ref[mirror(c), :] + r
        carry = m[0:1, :]


def _fnet_layer(x, g, w_out):
    wcs = _fnet_weights(w_out)
    cs, ss, rev = _fnet_tables()
    seq_block = pl.BlockSpec((None, SEQ, D_MODEL), lambda b: (b, 0, 0))
    half = pltpu.VMEM((FN_HALF, D_MODEL), BF16)
    vmem = 4 * SEQ * D_MODEL * 4 + 4 * FN_HALF * D_MODEL * 2 + 4 * FN_HALF * D_MODEL * 2 + (12 << 20)
    return pl.pallas_call(
        _fnet_body,
        grid=(BATCH,),
        in_specs=[seq_block, _resident((1, D_MODEL)), _resident((FN_HALF, FN_HALF)),
                  _resident((FN_HALF, FN_HALF)), _resident((D_MODEL, 2 * D_MODEL)),
                  _resident((FN_BLK, FN_BLK))],
        out_specs=seq_block,
        out_shape=jax.ShapeDtypeStruct((BATCH, SEQ, D_MODEL), F32),
        scratch_shapes=[half, half, half, half],
        compiler_params=_params(("parallel",), vmem),
        name="fnet_mix",
    )(x, g.reshape(1, D_MODEL), cs, ss, wcs, rev)


def _s5_prep_body(p_ref, bt_ref, ct_ref, strip_ref, s_ref, wc_ref, coef_ref, lag_sc):
    for gi in range(S5_TILE_GROUPS):
        _s5_prep_group(p_ref.at[gi], bt_ref.at[gi], ct_ref.at[gi], strip_ref.at[gi], s_ref.at[gi], wc_ref.at[gi],
                       coef_ref.at[gi], lag_sc.at[gi])


def _s5_prep_group(p_ref, bt_ref, ct_ref, strip_ref, s_ref, wc_ref, coef_ref, lag_sc):
    L = S5_CHUNK
    lane = lax.broadcasted_iota(jnp.int32, (1, LANES), 1)
    fwd = lane < S5_STATE
    l_re, l_im = p_ref[0:1, :], p_ref[1:2, :]
    dt = jnp.exp(p_ref[2:3, :])
    mag = jnp.exp(l_re * dt)
    lb_re, lb_im = mag * jnp.cos(l_im * dt), mag * jnp.sin(l_im * dt)
    n_re, n_im = lb_re - 1.0, lb_im
    den = l_re * l_re + l_im * l_im
    f_re, f_im = (n_re * l_re + n_im * l_im) / den, (n_im * l_re - n_re * l_im) / den
    b_re, b_im = bt_ref[0], bt_ref[1]
    bb_re, bb_im = f_re * b_re - f_im * b_im, f_re * b_im + f_im * b_re
    c_re, c_im = ct_ref[0], ct_ref[1]
    zero = jnp.zeros((1, LANES), F32)
    pw = [(jnp.ones((1, LANES), F32), zero)]
    for _ in range(L):
        pr, pi = pw[-1]
        pw.append((pr * lb_re - pi * lb_im, pr * lb_im + pi * lb_re))

    def mix(f, bk):
        return jnp.where(fwd, f[0], bk[0]), jnp.where(fwd, f[1], bk[1])

    def times(p, m_re, m_im):
        return p[0] * m_re - p[1] * m_im, p[0] * m_im + p[1] * m_re

    for t in range(L):
        rows = slice(t * S5_GROUP, (t + 1) * S5_GROUP)
        re, im = times(mix(pw[L - 1 - t], pw[t]), bb_re, bb_im)
        s_ref[rows, :LANES] = re.astype(BF16)
        s_ref[rows, LANES:] = im.astype(BF16)
        re, im = times(mix(pw[t + 1], pw[L - t]), c_re, c_im)
        wc_ref[rows, :LANES] = re.astype(BF16)
        wc_ref[rows, LANES:] = (-im).astype(BF16)
    none = (zero, zero)
    for j in range(2 * L):
        rows = slice(j * S5_GROUP, (j + 1) * S5_GROUP)
        f = pw[j - (L - 1)] if L - 1 <= j <= 2 * L - 2 else none
        bk = pw[L - 1 - j] if j <= L - 1 else none
        re, im = times(mix(f, bk), c_re, c_im)
        lag_sc[rows, :LANES] = re.astype(BF16)
        lag_sc[rows, LANES:] = (-im).astype(BF16)
    b_cat = jnp.concatenate([bb_re, bb_im], axis=1)
    b_hi = b_cat.astype(BF16)
    b_lo = (b_cat - b_hi.astype(F32)).astype(BF16)
    nt = (((1,), (1,)), ((), ()))
    lag = lag_sc[...]
    strip_ref[...] = (lax.dot_general(b_hi, lag, nt, preferred_element_type=F32)
                      + lax.dot_general(b_lo, lag, nt, preferred_element_type=F32))
    coef_ref[0:1, :] = pw[L][0]
    coef_ref[1:2, :] = pw[L][1]


def _s5_matrices(lam_re, lam_im, log_dt, b_re, b_im, c_re, c_im):
    def lanes(z):
        return jnp.concatenate([z[0], z[1]], axis=-1).astype(F32)
    params = jnp.stack([lanes(lam_re), lanes(lam_im), jnp.repeat(log_dt.T.astype(F32), S5_STATE, axis=1)], axis=1)
    params = jnp.pad(params, ((0, 0), (0, 5), (0, 0)))
    bt = jnp.stack([lanes(b_re.transpose(0, 1, 3, 2)), lanes(b_im.transpose(0, 1, 3, 2))], axis=1)
    ct = jnp.stack([lanes(c_re), lanes(c_im)], axis=1)
    tg = S5_TILE_GROUPS
    mat = pl.BlockSpec((tg, 2, S5_GROUP, LANES), lambda g: (g, 0, 0, 0))
    sq = pl.BlockSpec((tg, S5_CK, S5_CK), lambda g: (g, 0, 0))
    return pl.pallas_call(
        _s5_prep_body,
        grid=(S5_GROUPS // tg,),
        in_specs=[pl.BlockSpec((tg, 8, LANES), lambda g: (g, 0, 0)), mat, mat],
        out_specs=[pl.BlockSpec((tg, S5_GROUP, 2 * S5_CK), lambda g: (g, 0, 0)), sq, sq,
                   pl.BlockSpec((tg, 2, LANES), lambda g: (g, 0, 0))],
        out_shape=[jax.ShapeDtypeStruct((S5_GROUPS, S5_GROUP, 2 * S5_CK), F32),
                   jax.ShapeDtypeStruct((S5_GROUPS, S5_CK, S5_CK), BF16),
                   jax.ShapeDtypeStruct((S5_GROUPS, S5_CK, S5_CK), BF16),
                   jax.ShapeDtypeStruct((S5_GROUPS, 2, LANES), F32)],
        scratch_shapes=[pltpu.VMEM((S5_TILE_GROUPS, 2 * S5_CK, S5_CK), BF16)],
        compiler_params=_params(("parallel",), 24 << 20),
        name="s5_prep",
    )(params, bt, ct)


def _s5_norm_body(x_ref, g_ref, o_ref):
    h = _rms(x_ref[...], g_ref[...])
    o_ref[...] = h.reshape(o_ref.shape)


def _s5_norm(x, g, tm=512):
    cr = tm // S5_CHUNK
    return pl.pallas_call(
        _s5_norm_body,
        grid=(BATCH, SEQ // tm),
        in_specs=[pl.BlockSpec((None, tm, D_MODEL), lambda b, i: (b, i, 0)),
                  _resident((1, D_MODEL))],
        out_specs=pl.BlockSpec((cr, None, S5_CHUNK, D_MODEL), lambda b, i: (i, b, 0, 0)),
        out_shape=jax.ShapeDtypeStruct((S5_CHUNKS, BATCH, S5_CHUNK, D_MODEL), F32),
        compiler_params=_params(("parallel", "parallel"), 32 << 20),
        name="s5_norm",
    )(x, g.reshape(1, D_MODEL))


def _block_transpose8(a, lane_block):
    a = list(a)
    for d in (4, 2, 1):
        take_lo = (lane_block & d) == 0
        nxt = list(a)
        for i in range(8):
            if i & d:
                continue
            lo, hi = a[i], a[i + d]
            if 2 * S5_GROUP * d == LANES:
                both = pltpu.roll(jnp.where(take_lo, hi, lo), S5_GROUP * d, 1)
                nxt[i] = jnp.where(take_lo, lo, both)
                nxt[i + d] = jnp.where(take_lo, both, hi)
            else:
                nxt[i] = jnp.where(take_lo, lo, pltpu.roll(hi, S5_GROUP * d, 1))
                nxt[i + d] = jnp.where(take_lo, pltpu.roll(lo, LANES - S5_GROUP * d, 1), hi)
        a = nxt
    return a


def _gelu_tanh(y):
    return 0.5 * y * (1.0 + jnp.tanh(math.sqrt(2.0 / math.pi) * (y + 0.044715 * (y * y * y))))


S5_RB = 128


def _s5_body(h_ref, strip_ref, ws_ref, wc_ref, coef_ref, d_ref, o_ref, x_sc, y_sc, toep_sc):
    lane = lax.broadcasted_iota(jnp.int32, (1, LANES), 1)
    lane_block = lane // S5_GROUP
    fwd_lanes = lane < S5_STATE
    n_rb = S5_ROWS // S5_RB
    tok_rb = S5_RB * S5_CHUNK

    def relayout_in(i, carry):
        base = pl.multiple_of(i * tok_rb, tok_rb)
        rows = pl.multiple_of(i * S5_RB, S5_RB)
        for half in range(2):
            a = [pltpu.bitcast(h_ref[pl.ds(base + half * 8 + t, S5_RB, stride=S5_CHUNK), :].astype(BF16), jnp.uint32)
                 for t in range(8)]
            xt = _block_transpose8(a, lane_block)
            for gi in range(S5_TILE_GROUPS):
                x_sc[gi, pl.ds(rows, S5_RB), half * LANES:(half + 1) * LANES] = pltpu.bitcast(xt[gi], BF16)
        return carry

    lax.fori_loop(0, n_rb, relayout_in, 0)

    nt = (((1,), (1,)), ((), ()))
    for gi in range(S5_TILE_GROUPS):
        y_sc[gi] = jnp.dot(x_sc[gi], ws_ref[gi], preferred_element_type=F32)
        strip = strip_ref[gi]
        for t in range(S5_CHUNK):
            off = (S5_CHUNK - 1 - t) * S5_GROUP
            win = strip if off == 0 else pltpu.roll(strip, strip.shape[1] - off, 1)
            toep_sc[gi, t * S5_GROUP:(t + 1) * S5_GROUP, :] = win[:, :S5_CK].astype(BF16)

    coef = [(coef_ref[gi, 0:1, :], coef_ref[gi, 1:2, :]) for gi in range(S5_TILE_GROUPS)]

    def scan_step(k, st):
        rf = pl.ds(pl.multiple_of(k * BATCH, BATCH), BATCH)
        rb = pl.ds(pl.multiple_of((S5_CHUNKS - 1 - k) * BATCH, BATCH), BATCH)
        s_in = [(jnp.where(fwd_lanes, y_sc[gi, rf, :LANES], y_sc[gi, rb, :LANES]),
                 jnp.where(fwd_lanes, y_sc[gi, rf, LANES:], y_sc[gi, rb, LANES:])) for gi in range(S5_TILE_GROUPS)]
        nxt = []
        for gi in range(S5_TILE_GROUPS):
            st_re, st_im = st[2 * gi], st[2 * gi + 1]
            a_re, a_im = coef[gi]
            s_re, s_im = s_in[gi]
            y_sc[gi, rf, :S5_STATE] = st_re[:, :S5_STATE]
            y_sc[gi, rb, S5_STATE:LANES] = st_re[:, S5_STATE:]
            y_sc[gi, rf, LANES:LANES + S5_STATE] = st_im[:, :S5_STATE]
            y_sc[gi, rb, LANES + S5_STATE:] = st_im[:, S5_STATE:]
            nxt.append(a_re * st_re - a_im * st_im + s_re)
            nxt.append(a_re * st_im + a_im * st_re + s_im)
        return tuple(nxt)

    zero = jnp.zeros((BATCH, LANES), F32)
    lax.fori_loop(0, S5_CHUNKS, scan_step, (zero,) * (2 * S5_TILE_GROUPS))

    for gi in range(S5_TILE_GROUPS):
        h_in = y_sc[gi].astype(BF16)
        y_sc[gi] = (jnp.dot(x_sc[gi], toep_sc[gi], preferred_element_type=F32)
                    + lax.dot_general(h_in, wc_ref[gi], nt, preferred_element_type=F32))

    d_skip = d_ref[...]

    def relayout_out(i, carry):
        base = pl.multiple_of(i * tok_rb, tok_rb)
        rows = pl.multiple_of(i * S5_RB, S5_RB)
        for half in range(2):
            yg = [y_sc[gi, pl.ds(rows, S5_RB), half * LANES:(half + 1) * LANES] for gi in range(S5_TILE_GROUPS)]
            yt = _block_transpose8(yg, lane_block)
            for t in range(8):
                tok = pl.ds(base + half * 8 + t, S5_RB, stride=S5_CHUNK)
                o_ref[tok, :] = _gelu_tanh(yt[t] + d_skip * h_ref[tok, :])
        return carry

    lax.fori_loop(0, n_rb, relayout_out, 0)


def _s5_core(h2, strip, s_in, wc_t, coef, d_skip):
    n_tiles = S5_GROUPS // S5_TILE_GROUPS
    tok_block = pl.BlockSpec((N_TOKENS, LANES), lambda j: (0, j))
    vmem = (4 * N_TOKENS * LANES * 4 + S5_TILE_GROUPS * S5_ROWS * S5_CK * (2 + 4)
            + 4 * S5_ROWS * LANES * 4 + 2 * S5_TILE_GROUPS * S5_CK * 3 * S5_CK * 2 + (12 << 20))
    return pl.pallas_call(
        _s5_body,
        grid=(n_tiles,),
        in_specs=[tok_block,
                  pl.BlockSpec((S5_TILE_GROUPS, S5_GROUP, 2 * S5_CK), lambda j: (j, 0, 0)),
                  pl.BlockSpec((S5_TILE_GROUPS, S5_CK, S5_CK), lambda j: (j, 0, 0)),
                  pl.BlockSpec((S5_TILE_GROUPS, S5_CK, S5_CK), lambda j: (j, 0, 0)),
                  pl.BlockSpec((S5_TILE_GROUPS, 2, LANES), lambda j: (j, 0, 0)),
                  pl.BlockSpec((1, LANES), lambda j: (0, j))],
        out_specs=tok_block,
        out_shape=jax.ShapeDtypeStruct((N_TOKENS, D_MODEL), F32),
        scratch_shapes=[pltpu.VMEM((S5_TILE_GROUPS, S5_ROWS, S5_CK), BF16),
                        pltpu.VMEM((S5_TILE_GROUPS, S5_ROWS, S5_CK), F32),
                        pltpu.VMEM((S5_TILE_GROUPS, S5_CK, S5_CK), BF16)],
        compiler_params=_params(("arbitrary",), vmem),
        name="s5_core",
    )(h2, strip, s_in, wc_t, coef, d_skip.reshape(1, D_MODEL))


def _s5_glu_body(x_ref, a_ref, w_ref, o_ref):
    a = a_ref[...].reshape(x_ref.shape).astype(BF16)
    vg = jnp.dot(a, w_ref[...], preferred_element_type=F32)
    o_ref[...] = x_ref[...] + vg[:, :D_MODEL] * _sigmoid(vg[:, D_MODEL:])


def _s5_glu(x, act, w_glu_all, layer, tm=512):
    cr = tm // S5_CHUNK
    vmem = 2 * D_MODEL * D_MODEL * 2 + 6 * tm * D_MODEL * 4 + 2 * tm * 2 * D_MODEL * 4 + (4 << 20)
    return pl.pallas_call(
        _s5_glu_body,
        grid=(BATCH, SEQ // tm),
        in_specs=[pl.BlockSpec((None, tm, D_MODEL), lambda b, i: (b, i, 0)),
                  pl.BlockSpec((cr, None, S5_CHUNK, D_MODEL), lambda b, i: (i, b, 0, 0)),
                  _layer_block((D_MODEL, 2 * D_MODEL), layer)],
        out_specs=pl.BlockSpec((None, tm, D_MODEL), lambda b, i: (b, i, 0)),
        out_shape=jax.ShapeDtypeStruct((BATCH, SEQ, D_MODEL), F32),
        compiler_params=_params(("parallel", "parallel"), vmem),
        name="s5_glu",
    )(x, act, w_glu_all)


def _s5_layer(x, h, lam_re, lam_im, log_dt, b_re, b_im, c_re, c_im, d_skip, w_glu_all, layer):
    strip, s_in, wc_t, coef = _s5_matrices(lam_re, lam_im, log_dt, b_re, b_im, c_re, c_im)
    act = _s5_core(h.reshape(N_TOKENS, D_MODEL), strip, s_in, wc_t, coef, d_skip)
    act = act.reshape(S5_CHUNKS, BATCH, S5_CHUNK, D_MODEL)
    return _s5_glu(x, act, w_glu_all, layer)


def _t5_bucket(rel):
    half = NUM_BUCKETS // 2
    max_exact = half // 2
    n = np.abs(rel)
    sign = (rel > 0).astype(np.int32) * half
    large = max_exact + (np.log(np.maximum(n, 1) / max_exact) / math.log(MAX_DISTANCE / max_exact)
                         * (half - max_exact)).astype(np.int32)
    large = np.minimum(large, half - 1)
    return (sign + np.where(n < max_exact, n, large)).astype(np.int32)


LOG2E = 1.4426950408889634
N_PAIRS = HEADS_PER_GROUP // 2
ATTN_PAD_ROWS = SEQ + 2 * ATTN_SIDE * max(d for _, d in DILATED_GROUPS)
QKV_GROUP_WIDTH = 3 * D_MODEL
QKV_SLAB = 2 * LANES
QKV_TILE_ROWS = 512


def _attn_bias_tables(rel_bias):
    n_off = ATTN_BK + ATTN_BQ - 1
    offs = np.arange(n_off) - (ATTN_BQ - 1) - ATTN_SIDE
    strips = []
    for gi, (_, dil) in enumerate(DILATED_GROUPS):
        onehot = jnp.asarray(_t5_bucket(offs * dil)[:, None] == np.arange(NUM_BUCKETS), F32)
        f = jnp.dot(onehot, rel_bias[:, gi * HEADS_PER_GROUP:(gi + 1) * HEADS_PER_GROUP].astype(F32),
                    precision=lax.Precision.HIGHEST).T
        strips.append(jnp.pad(f[:, ::-1], ((0, 0), (0, BIAS_STRIP - n_off))))
    strips = jnp.stack(strips).reshape(N_ATTN_GROUPS, HEADS_PER_GROUP, 1, BIAS_STRIP)
    return pl.pallas_call(
        _attn_bias_body,
        grid=(N_ATTN_GROUPS, N_PAIRS),
        in_specs=[pl.BlockSpec((None, 2, 1, BIAS_STRIP), lambda g, p: (g, p, 0, 0))],
        out_specs=pl.BlockSpec((None, None, ATTN_BK, 2 * ATTN_BQ), lambda g, p: (g, p, 0, 0)),
        out_shape=jax.ShapeDtypeStruct((N_ATTN_GROUPS, N_PAIRS, ATTN_BK, 2 * ATTN_BQ), F32),
        compiler_params=_params(("parallel", "parallel"), 16 << 20),
        name="attn_bias",
    )(strips)


BIAS_STRIP = 256


def _attn_bias_body(f_ref, o_ref):
    krow = lax.broadcasted_iota(jnp.int32, (ATTN_BK, 1), 0)
    lane = lax.broadcasted_iota(jnp.int32, (1, LANES), 1)
    head0 = lane < ATTN_BQ
    band = jnp.abs(krow - ATTN_SIDE - (lane & (ATTN_BQ - 1))) <= ATTN_SIDE
    halves = []
    for a in range(2):
        x = jnp.broadcast_to(f_ref[a], (ATTN_BK, BIAS_STRIP))
        shift = (a * ATTN_BQ - (ATTN_BK - 1)) % BIAS_STRIP
        halves.append(pltpu.roll(x, shift, 1, stride=1, stride_axis=0)[:, :LANES])
    o_ref[...] = jnp.where(band, jnp.where(head0, halves[0], halves[1]) * LOG2E, -1e30)


def _qkv_body(x_ref, g_ref, w_ref, qg_ref, kg_ref, perm_ref, q_out, k_out, v_out, *, n_res):
    rows = x_ref.shape[0] // n_res
    h = _rms(x_ref[...], g_ref[...]).astype(BF16)
    if n_res > 1:
        h = jnp.dot(perm_ref[...], h, preferred_element_type=F32).astype(BF16)
    lane = lax.broadcasted_iota(jnp.int32, (1, LANES), 1)
    head0 = lane < HEAD_DIM

    def head_norm(t, gain):
        sq = t * t
        s0 = jnp.sum(jnp.where(head0, sq, 0.0), axis=-1, keepdims=True)
        s1 = jnp.sum(jnp.where(head0, 0.0, sq), axis=-1, keepdims=True)
        ms = jnp.where(head0, s0, s1) * (1.0 / HEAD_DIM)
        return t * lax.rsqrt(ms + EPS) * gain

    for c in range(QKV_GROUP_WIDTH // QKV_SLAB):
        z = jnp.dot(h, w_ref[:, c * QKV_SLAB:(c + 1) * QKV_SLAB].astype(BF16), preferred_element_type=F32)
        for half in range(QKV_SLAB // LANES):
            section, lo = divmod(c * QKV_SLAB + half * LANES, D_MODEL)
            zz = z[:, half * LANES:(half + 1) * LANES]
            if section == 0:
                zz = head_norm(zz, qg_ref[...]) * (HEAD_DIM ** -0.5 * LOG2E)
                qa = jnp.where(head0, zz, 0.0).astype(BF16)
                qb = jnp.where(head0, 0.0, zz).astype(BF16)
            elif section == 1:
                zz = head_norm(zz, kg_ref[...]).astype(BF16)
            else:
                zz = zz.astype(BF16)
            for r in range(n_res):
                piece = slice(r * rows, (r + 1) * rows)
                if section == 0:
                    q_out[r, 0, :, lo:lo + LANES] = qa[piece]
                    q_out[r, 1, :, lo:lo + LANES] = qb[piece]
                elif section == 1:
                    k_out[r, :, lo:lo + LANES] = zz[piece]
                else:
                    v_out[r, :, lo:lo + LANES] = zz[piece]


def _qkv_group(x, g, w_qkv_all, layer, gi, dil, q_gain, k_gain):
    seg = SEQ // dil
    tm = QKV_TILE_ROWS
    rows = tm // dil
    assert tm % dil == 0 and rows % 16 == 0
    gain2 = lambda gn: jnp.tile(gn.astype(F32), 2).reshape(1, LANES)
    src = np.arange(tm).reshape(rows, dil).T.reshape(tm)
    perm = jnp.asarray(src[:, None] == np.arange(tm)[None, :], BF16)
    kv_spec = pl.BlockSpec((None, dil, rows, D_MODEL), lambda b, i: (b, 0, i, 0))
    kv_shape = jax.ShapeDtypeStruct((BATCH, dil, seg, D_MODEL), BF16)
    vmem = (D_MODEL * QKV_GROUP_WIDTH * 4 + 2 * tm * D_MODEL * 4 + 2 * 4 * tm * D_MODEL * 2 + (12 << 20))
    return pl.pallas_call(
        functools.partial(_qkv_body, n_res=dil),
        grid=(BATCH, SEQ // tm),
        in_specs=[pl.BlockSpec((None, tm, D_MODEL), lambda b, i: (b, i, 0)),
                  _resident((1, D_MODEL)),
                  pl.BlockSpec((None, D_MODEL, QKV_GROUP_WIDTH), lambda b, i: (layer, 0, gi),
                               pipeline_mode=pl.Buffered(1)),
                  _resident((1, LANES)), _resident((1, LANES)), _resident((tm, tm))],
        out_specs=[pl.BlockSpec((None, dil, 2, rows, D_MODEL), lambda b, i: (b, 0, 0, i, 0)),
                   kv_spec, kv_spec],
        out_shape=[jax.ShapeDtypeStruct((BATCH, dil, 2, seg, D_MODEL), BF16), kv_shape, kv_shape],
        compiler_params=_params(("parallel", "parallel"), vmem),
        name=f"qkv_dil{dil}",
    )(x, g.reshape(1, D_MODEL), w_qkv_all, gain2(q_gain), gain2(k_gain), perm)


ATTN_AHEAD = 3


def _attn_body(q0, k0, v0, q1, k1, v1, q2, k2, v2, bias_ref, o_ref, kp_sc, vp_sc, lt_sc, og_sc, lg_sc):
    lane = lax.broadcasted_iota(jnp.int32, (1, LANES), 1)
    head0 = lane < HEAD_DIM
    krow = lax.broadcasted_iota(jnp.int32, (ATTN_BK, 1), 0)
    zpad = jnp.zeros((ATTN_SIDE, LANES), BF16)
    tn = (((0,), (0,)), ((), ()))

    qkv_refs = ((q0, k0, v0), (q1, k1, v1), (q2, k2, v2))
    for g, (_, dil) in enumerate(DILATED_GROUPS):
        q_ref, k_ref, v_ref = qkv_refs[g]
        seg = SEQ // dil
        nb = seg // ATTN_BQ
        pad_seg = seg + 2 * ATTN_SIDE
        lt_sc[...] = jnp.zeros(lt_sc.shape, F32)
        for r in range(dil):
            base = r * pad_seg
            for src, dst in ((k_ref, kp_sc), (v_ref, vp_sc)):
                dst[base:base + ATTN_SIDE] = zpad
                dst[base + ATTN_SIDE + seg:base + pad_seg] = zpad
                dst[base + ATTN_SIDE:base + ATTN_SIDE + seg] = src[r]

        def scores(r, i):
            base = r * pad_seg
            win = slice(base + i * ATTN_BQ, base + i * ATTN_BQ + ATTN_BK)
            qrows = slice(i * ATTN_BQ, (i + 1) * ATTN_BQ)
            qm = jnp.concatenate([q_ref[r, 0, qrows, :], q_ref[r, 1, qrows, :]], axis=0)
            s = lax.dot_general(kp_sc[win], qm, (((1,), (1,)), ((), ())), preferred_element_type=F32)
            s = s + bias_ref[g]
            if i == 0 or i == nb - 1:
                ok = None
                if i == 0:
                    ok = krow >= ATTN_SIDE
                if i == nb - 1:
                    ok_hi = krow < ATTN_BK - ATTN_SIDE
                    ok = ok_hi if ok is None else jnp.logical_and(ok, ok_hi)
                s = jnp.where(ok, s, -1e30)
            return s

        blocks = [(r, i) for r in range(dil) for i in range(nb)]
        pending = [scores(*blocks[j]) for j in range(min(ATTN_AHEAD, len(blocks)))]
        for blk, (r, i) in enumerate(blocks):
            if blk + ATTN_AHEAD < len(blocks):
                pending.append(scores(*blocks[blk + ATTN_AHEAD]))
            s = pending.pop(0)
            base = r * pad_seg
            win = slice(base + i * ATTN_BQ, base + i * ATTN_BQ + ATTN_BK)
            m = jnp.max(s, axis=0, keepdims=True)
            p = jnp.exp2(s - m)
            den = jnp.sum(p, axis=0, keepdims=True)
            pn = (p * (1.0 / den)).astype(BF16)
            u = lax.dot_general(pn, vp_sc[win], tn, preferred_element_type=F32)
            rows = pl.ds(r + i * ATTN_BQ * dil, ATTN_BQ, stride=dil)
            og_sc[g, rows, :] = jnp.where(head0, u[:ATTN_BQ], u[ATTN_BQ:])
            lt_sc[blk:blk + 1, :] = m + jnp.log2(den)

        ltt = lt_sc[...].T
        for r in range(dil):
            for i in range(nb):
                c = r * nb + i
                rows = pl.ds(r + i * ATTN_BQ * dil, ATTN_BQ, stride=dil)
                lg_sc[g, rows, :] = jnp.where(head0, ltt[:ATTN_BQ, c:c + 1], ltt[ATTN_BQ:, c:c + 1])

    l0, l1, l2 = lg_sc[0], lg_sc[1], lg_sc[2]
    m = jnp.maximum(jnp.maximum(l0, l1), l2)
    w0, w1, w2 = jnp.exp2(l0 - m), jnp.exp2(l1 - m), jnp.exp2(l2 - m)
    o = (w0 * og_sc[0] + w1 * og_sc[1] + w2 * og_sc[2]) / (w0 + w1 + w2)
    o_ref[...] = o.astype(o_ref.dtype)


def _attn_core(qkv, bias_tabs):
    in_specs = []
    for (_, dil) in DILATED_GROUPS:
        seg = SEQ // dil
        in_specs.append(pl.BlockSpec((None, dil, 2, seg, LANES), lambda b, p: (b, 0, 0, 0, p)))
        in_specs.append(pl.BlockSpec((None, dil, seg, LANES), lambda b, p: (b, 0, 0, p)))
        in_specs.append(pl.BlockSpec((None, dil, seg, LANES), lambda b, p: (b, 0, 0, p)))
    in_specs.append(pl.BlockSpec((N_ATTN_GROUPS, None, ATTN_BK, 2 * ATTN_BQ), lambda b, p: (0, p, 0, 0)))
    return pl.pallas_call(
        _attn_body,
        grid=(BATCH, N_PAIRS),
        in_specs=in_specs,
        out_specs=pl.BlockSpec((None, SEQ, LANES), lambda b, p: (b, 0, p)),
        out_shape=jax.ShapeDtypeStruct((BATCH, SEQ, D_MODEL), BF16),
        scratch_shapes=[pltpu.VMEM((ATTN_PAD_ROWS, LANES), BF16),
                        pltpu.VMEM((ATTN_PAD_ROWS, LANES), BF16),
                        pltpu.VMEM((LANES, LANES), F32),
                        pltpu.VMEM((N_ATTN_GROUPS, SEQ, LANES), F32),
                        pltpu.VMEM((N_ATTN_GROUPS, SEQ, LANES), F32)],
        compiler_params=_params(("parallel", "parallel"), 40 << 20),
        name="attn_core",
    )(*qkv, bias_tabs)


def _attn_layer(x, g, w_qkv_all, q_gain, k_gain, rel_bias, layer):
    qkv = []
    for gi, (window, dil) in enumerate(DILATED_GROUPS):
        assert (window // 2) // dil == ATTN_SIDE and (SEQ // dil) % ATTN_BQ == 0
        qkv.extend(_qkv_group(x, g, w_qkv_all, layer, gi, dil, q_gain[gi], k_gain[gi]))
    return _attn_core(qkv, _attn_bias_tables(rel_bias)).reshape(N_TOKENS, D_MODEL)


def kernel(x, norm_mix_g, norm_ffn_g, fnet_w_out, s5_lambda_re, s5_lambda_im, s5_log_dt, s5_b_re, s5_b_im, s5_c_re, s5_c_im, s5_d, s5_w_glu, attn_w_qkv, attn_q_gain, attn_k_gain, attn_w_o, rel_bias, ffn_w_gate_up, ffn_w_down):
    w_gate_up, w_down, w_qkv = ffn_w_gate_up, ffn_w_down, attn_w_qkv
    w_glu, w_o = _cast_bf16(s5_w_glu), _cast_bf16(attn_w_o)
    counts = [0, 0, 0]
    h_next = None
    for i in range(DEPTH):
        kind = i % 3
        j = counts[kind]
        counts[kind] += 1
        proj = None
        if kind == 0:
            x = _fnet_layer(x, norm_mix_g[i], fnet_w_out[j])
        elif kind == 1:
            h = h_next if h_next is not None else _s5_norm(x, norm_mix_g[i])
            x = _s5_layer(x, h, s5_lambda_re[j], s5_lambda_im[j], s5_log_dt[j], s5_b_re[j],
                          s5_b_im[j], s5_c_re[j], s5_c_im[j], s5_d[j], w_glu, j)
        else:
            proj = (_attn_layer(x, norm_mix_g[i], w_qkv, attn_q_gain[j], attn_k_gain[j], rel_bias, j), w_o, j)
        next_is_s5 = i + 1 < DEPTH and (i + 1) % 3 == 1
        out = _ffn(x.reshape(N_TOKENS, D_MODEL), norm_ffn_g[i], w_gate_up, w_down, i, proj=proj,
                   norm_out_g=norm_mix_g[i + 1] if next_is_s5 else None)
        x, h_next = out if next_is_s5 else (out, None)
        x = x.reshape(BATCH, SEQ, D_MODEL)
    return x
```

```python
import functools
import math

import numpy as np
import jax
import jax.numpy as jnp
from jax import lax
from jax.experimental import pallas as pl
from jax.experimental.pallas import tpu as pltpu

F32 = jnp.float32
BF16 = jnp.bfloat16

D_MODEL = 1024
BATCH = 8
SEQ = 2048
DEPTH = 4
N_TOKENS = BATCH * SEQ
EPS = 1e-6
D_FF = 2816
FOURIER_GROUP = 128
S5_GROUP = 16
S5_GROUPS = 64
S5_STATE = 64
HEAD_DIM = 64
HEADS_PER_GROUP = 16
DILATED_GROUPS = ((128, 1), (512, 4), (2048, 16))
N_ATTN_GROUPS = 3
NUM_BUCKETS = 32
MAX_DISTANCE = 1024
ATTN_SIDE = 64

LANES = 128
VMEM_LIMIT_CAP = 60 * 1024 * 1024

S5_CHUNK = 16
S5_CHUNKS = SEQ // S5_CHUNK
S5_ROWS = BATCH * S5_CHUNKS
S5_TILE_GROUPS = LANES // S5_GROUP
S5_CK = S5_CHUNK * S5_GROUP

ATTN_BQ = 64
ATTN_BK = ATTN_BQ + 2 * ATTN_SIDE


def _params(sem, vmem_bytes):
    return pltpu.CompilerParams(dimension_semantics=sem,
                                vmem_limit_bytes=int(min(VMEM_LIMIT_CAP, vmem_bytes)))


def _rms(x, g):
    ms = jnp.mean(x * x, axis=-1, keepdims=True)
    return x * lax.rsqrt(ms + EPS) * g


def _sigmoid(x):
    return 1.0 / (1.0 + jnp.exp(-x))


def _resident(shape):
    nd = len(shape)
    return pl.BlockSpec(shape, lambda *_: (0,) * nd, pipeline_mode=pl.Buffered(1))


def _cast_body(w_ref, o_ref):
    o_ref[...] = w_ref[...].astype(BF16)


def _cast_bf16(w, rows=256):
    n_layers, n_rows, n_cols = w.shape
    block = pl.BlockSpec((None, rows, n_cols), lambda l, i: (l, i, 0))
    return pl.pallas_call(
        _cast_body,
        grid=(n_layers, n_rows // rows),
        in_specs=[block],
        out_specs=block,
        out_shape=jax.ShapeDtypeStruct(w.shape, BF16),
        compiler_params=_params(("parallel", "parallel"), 4 * rows * n_cols * 6 + (4 << 20)),
        name="cast_bf16",
    )(w)


FFN_CHUNK = 256


def _ffn_body(*refs, has_proj, has_norm_out):
    refs = list(refs)
    x_ref, g_ref, wgu_ref, wd_ref = refs[:4]
    a_ref, wp_ref = refs[4:6] if has_proj else (None, None)
    gn_ref = refs[4 + 2 * has_proj] if has_norm_out else None
    o_ref = refs[4 + 2 * has_proj + has_norm_out]
    x = x_ref[...]
    if has_proj:
        x = x + jnp.dot(a_ref[...], wp_ref[...], preferred_element_type=F32)
    h = _rms(x, g_ref[...]).astype(BF16)
    acc = x
    for c in range(0, D_FF, FFN_CHUNK):
        w = min(FFN_CHUNK, D_FF - c)
        gate = jnp.dot(h, wgu_ref[:, c:c + w].astype(BF16), preferred_element_type=F32)
        up = jnp.dot(h, wgu_ref[:, D_FF + c:D_FF + c + w].astype(BF16), preferred_element_type=F32)
        a = (gate * _sigmoid(gate) * up).astype(BF16)
        acc = acc + jnp.dot(a, wd_ref[c:c + w, :].astype(BF16), preferred_element_type=F32)
    o_ref[...] = acc
    if has_norm_out:
        hn_ref = refs[-1]
        hn_ref[...] = _rms(acc, gn_ref[...]).reshape(hn_ref.shape)


def _layer_block(shape, layer):
    return pl.BlockSpec((None,) + shape, lambda *_: (layer,) + (0,) * len(shape), pipeline_mode=pl.Buffered(1))


def _ffn(x2, g, wgu_all, wd_all, layer, proj=None, norm_out_g=None, tm=512):
    m = x2.shape[0]
    tile = pl.BlockSpec((tm, D_MODEL), lambda i: (i, 0))
    in_specs = [tile, _resident((1, D_MODEL)), _layer_block((D_MODEL, 2 * D_FF), layer),
                _layer_block((D_FF, D_MODEL), layer)]
    args = [x2, g.reshape(1, D_MODEL), wgu_all, wd_all]
    out_specs, out_shape = [tile], [jax.ShapeDtypeStruct((m, D_MODEL), F32)]
    if proj is not None:
        in_specs += [tile, _layer_block((D_MODEL, D_MODEL), proj[2])]
        args += [proj[0], proj[1]]
    if norm_out_g is not None:
        tiles_per_seq = SEQ // tm
        in_specs.append(_resident((1, D_MODEL)))
        args.append(norm_out_g.reshape(1, D_MODEL))
        out_specs.append(pl.BlockSpec((tm // S5_CHUNK, None, S5_CHUNK, D_MODEL),
                                      lambda i: (i % tiles_per_seq, i // tiles_per_seq, 0, 0)))
        out_shape.append(jax.ShapeDtypeStruct((S5_CHUNKS, BATCH, S5_CHUNK, D_MODEL), F32))
    vmem = 3 * D_MODEL * D_FF * 4 + 10 * tm * D_MODEL * 4 + (10 << 20)
    outs = pl.pallas_call(
        functools.partial(_ffn_body, has_proj=proj is not None, has_norm_out=norm_out_g is not None),
        grid=(m // tm,),
        in_specs=in_specs,
        out_specs=out_specs,
        out_shape=out_shape,
        compiler_params=_params(("parallel",), vmem),
        name="ffn",
    )(*args)
    return outs if norm_out_g is not None else outs[0]


def _fnet_weight_body(cc_ref, sc_ref, w_ref, o_ref):
    w = w_ref[...]
    o_ref[:, :D_MODEL] = jnp.dot(cc_ref[...], w, preferred_element_type=F32,
                                 precision=lax.Precision.HIGHEST).astype(BF16)
    o_ref[:, D_MODEL:] = jnp.dot(sc_ref[...], w, preferred_element_type=F32,
                                 precision=lax.Precision.HIGHEST).astype(BF16)


def _fnet_weights(w_out):
    n = np.arange(FOURIER_GROUP)
    ang = 2.0 * np.pi * ((n[:, None] * n[None, :]) % FOURIER_GROUP) / FOURIER_GROUP
    cc = jnp.asarray(np.cos(ang) / math.sqrt(FOURIER_GROUP), F32)
    sc = jnp.asarray(np.sin(ang) / math.sqrt(FOURIER_GROUP), F32)
    ng = D_MODEL // FOURIER_GROUP
    return pl.pallas_call(
        _fnet_weight_body,
        grid=(ng,),
        in_specs=[_resident((FOURIER_GROUP, FOURIER_GROUP)),
                  _resident((FOURIER_GROUP, FOURIER_GROUP)),
                  pl.BlockSpec((FOURIER_GROUP, D_MODEL), lambda i: (i, 0))],
        out_specs=pl.BlockSpec((FOURIER_GROUP, 2 * D_MODEL), lambda i: (i, 0)),
        out_shape=jax.ShapeDtypeStruct((D_MODEL, 2 * D_MODEL), BF16),
        compiler_params=_params(("parallel",), 16 << 20),
        name="fnet_weights",
    )(cc, sc, w_out)


FN_HALF = SEQ // 2
FN_BLK = 256
FN_NB = FN_HALF // FN_BLK


def _fnet_tables():
    k = np.arange(FN_HALF)[:, None]
    n = np.arange(FN_HALF)[None, :]
    ang = 2.0 * np.pi * ((k * n) % SEQ) / SEQ
    scale = 1.0 / math.sqrt(SEQ)
    i = np.arange(FN_BLK)
    rev = (i[None, :] == FN_BLK - i[:, None]).astype(np.float32)
    return (jnp.asarray(np.cos(ang) * scale, BF16), jnp.asarray(np.sin(ang) * scale, BF16),
            jnp.asarray(rev, BF16))


def _fnet_body(x_ref, g_ref, cs_ref, ss_ref, w_ref, rev_ref, o_ref, e_sc, d_sc, zc_sc, zs_sc):
    g = g_ref[...]
    scale = 1.0 / math.sqrt(SEQ)
    row = lax.broadcasted_iota(jnp.int32, (FN_BLK, 1), 0)
    first = row == 0
    sign = jnp.where((row & 1) == 0, 1.0, -1.0)
    rev = rev_ref[...]
    blk = lambda a: pl.ds(a * FN_BLK, FN_BLK)
    mirror = lambda a: pl.ds(SEQ - (a + 1) * FN_BLK, FN_BLK)

    alt = jnp.zeros((1, D_MODEL), F32)
    carry = jnp.zeros((1, D_MODEL), F32)
    for a in range(FN_NB):
        lo = _rms(x_ref[blk(a), :], g)
        hi = _rms(x_ref[mirror(a), :], g)
        alt = alt + jnp.sum(sign * lo, axis=0, keepdims=True) + jnp.sum(sign * hi, axis=0, keepdims=True)
        r = jnp.dot(rev, hi.astype(BF16), preferred_element_type=F32)
        r = jnp.where(first, carry, r)
        e_sc[blk(a), :] = (lo + r).astype(BF16)
        d_sc[blk(a), :] = (lo - r).astype(BF16)
        carry = hi[0:1, :]
    h_nyq = carry * scale

    for c in range(FN_NB):
        zc = jnp.dot(cs_ref[blk(c), :], e_sc[...], preferred_element_type=F32) + sign * h_nyq
        zs = jnp.dot(ss_ref[blk(c), :], d_sc[...], preferred_element_type=F32)
        zc_sc[blk(c), :] = zc.astype(BF16)
        zs_sc[blk(c), :] = zs.astype(BF16)

    wc = w_ref[:, :D_MODEL]
    ws = w_ref[:, D_MODEL:]
    z_nyq = jnp.broadcast_to(alt * scale, (8, D_MODEL)).astype(BF16)
    carry = jnp.dot(z_nyq, wc, preferred_element_type=F32)[0:1, :]
    def products(c):
        return (jnp.dot(zc_sc[blk(c), :], wc, preferred_element_type=F32),
                jnp.dot(zs_sc[blk(c), :], ws, preferred_element_type=F32))

    ahead = products(FN_NB - 1)
    for c in reversed(range(FN_NB)):
        p, q = ahead
        if c > 0:
            ahead = products(c - 1)
        o_ref[blk(c), :] = x_ref[blk(c), :] + (p - q)
        m = p + q
        m_hi = m.astype(BF16)
        m_lo = (m - m_hi.astype(F32)).astype(BF16)
        r = (jnp.dot(rev, m_hi, preferred_element_type=F32) + jnp.dot(rev, m_lo, preferred_element_type=F32))
        r = jnp.where(first, carry, r)
        o_ref[mirror(c), :] = x_ref[mirror(c), :] + r
        carry = m[0:1, :]


def _fnet_layer(x, g, w_out):
    wcs = _fnet_weights(w_out)
    cs, ss, rev = _fnet_tables()
    seq_block = pl.BlockSpec((None, SEQ, D_MODEL), lambda b: (b, 0, 0))
    half = pltpu.VMEM((FN_HALF, D_MODEL), BF16)
    vmem = 4 * SEQ * D_MODEL * 4 + 4 * FN_HALF * D_MODEL * 2 + 4 * FN_HALF * D_MODEL * 2 + (12 << 20)
    return pl.pallas_call(
        _fnet_body,
        grid=(BATCH,),
        in_specs=[seq_block, _resident((1, D_MODEL)), _resident((FN_HALF, FN_HALF)),
                  _resident((FN_HALF, FN_HALF)), _resident((D_MODEL, 2 * D_MODEL)),
                  _resident((FN_BLK, FN_BLK))],
        out_specs=seq_block,
        out_shape=jax.ShapeDtypeStruct((BATCH, SEQ, D_MODEL), F32),
        scratch_shapes=[half, half, half, half],
        compiler_params=_params(("parallel",), vmem),
        name="fnet_mix",
    )(x, g.reshape(1, D_MODEL), cs, ss, wcs, rev)


def _s5_prep_body(p_ref, bt_ref, ct_ref, strip_ref, s_ref, wc_ref, coef_ref, lag_sc):
    for gi in range(S5_TILE_GROUPS):
        _s5_prep_group(p_ref.at[gi], bt_ref.at[gi], ct_ref.at[gi], strip_ref.at[gi], s_ref.at[gi], wc_ref.at[gi],
                       coef_ref.at[gi], lag_sc.at[gi])


def _s5_prep_group(p_ref, bt_ref, ct_ref, strip_ref, s_ref, wc_ref, coef_ref, lag_sc):
    L = S5_CHUNK
    lane = lax.broadcasted_iota(jnp.int32, (1, LANES), 1)
    fwd = lane < S5_STATE
    l_re, l_im = p_ref[0:1, :], p_ref[1:2, :]
    dt = jnp.exp(p_ref[2:3, :])
    mag = jnp.exp(l_re * dt)
    lb_re, lb_im = mag * jnp.cos(l_im * dt), mag * jnp.sin(l_im * dt)
    n_re, n_im = lb_re - 1.0, lb_im
    den = l_re * l_re + l_im * l_im
    f_re, f_im = (n_re * l_re + n_im * l_im) / den, (n_im * l_re - n_re * l_im) / den
    b_re, b_im = bt_ref[0], bt_ref[1]
    bb_re, bb_im = f_re * b_re - f_im * b_im, f_re * b_im + f_im * b_re
    c_re, c_im = ct_ref[0], ct_ref[1]
    zero = jnp.zeros((1, LANES), F32)
    pw = [(jnp.ones((1, LANES), F32), zero)]
    for _ in range(L):
        pr, pi = pw[-1]
        pw.append((pr * lb_re - pi * lb_im, pr * lb_im + pi * lb_re))

    def mix(f, bk):
        return jnp.where(fwd, f[0], bk[0]), jnp.where(fwd, f[1], bk[1])

    def times(p, m_re, m_im):
        return p[0] * m_re - p[1] * m_im, p[0] * m_im + p[1] * m_re

    for t in range(L):
        rows = slice(t * S5_GROUP, (t + 1) * S5_GROUP)
        re, im = times(mix(pw[L - 1 - t], pw[t]), bb_re, bb_im)
        s_ref[rows, :LANES] = re.astype(BF16)
        s_ref[rows, LANES:] = im.astype(BF16)
        re, im = times(mix(pw[t + 1], pw[L - t]), c_re, c_im)
        wc_ref[rows, :LANES] = re.astype(BF16)
        wc_ref[rows, LANES:] = (-im).astype(BF16)
    none = (zero, zero)
    for j in range(2 * L):
        rows = slice(j * S5_GROUP, (j + 1) * S5_GROUP)
        f = pw[j - (L - 1)] if L - 1 <= j <= 2 * L - 2 else none
        bk = pw[L - 1 - j] if j <= L - 1 else none
        re, im = times(mix(f, bk), c_re, c_im)
        lag_sc[rows, :LANES] = re.astype(BF16)
        lag_sc[rows, LANES:] = (-im).astype(BF16)
    b_cat = jnp.concatenate([bb_re, bb_im], axis=1)
    b_hi = b_cat.astype(BF16)
    b_lo = (b_cat - b_hi.astype(F32)).astype(BF16)
    nt = (((1,), (1,)), ((), ()))
    lag = lag_sc[...]
    strip_ref[...] = (lax.dot_general(b_hi, lag, nt, preferred_element_type=F32)
                      + lax.dot_general(b_lo, lag, nt, preferred_element_type=F32))
    coef_ref[0:1, :] = pw[L][0]
    coef_ref[1:2, :] = pw[L][1]


def _s5_matrices(lam_re, lam_im, log_dt, b_re, b_im, c_re, c_im):
    def lanes(z):
        return jnp.concatenate([z[0], z[1]], axis=-1).astype(F32)
    params = jnp.stack([lanes(lam_re), lanes(lam_im), jnp.repeat(log_dt.T.astype(F32), S5_STATE, axis=1)], axis=1)
    params = jnp.pad(params, ((0, 0), (0, 5), (0, 0)))
    bt = jnp.stack([lanes(b_re.transpose(0, 1, 3, 2)), lanes(b_im.transpose(0, 1, 3, 2))], axis=1)
    ct = jnp.stack([lanes(c_re), lanes(c_im)], axis=1)
    tg = S5_TILE_GROUPS
    mat = pl.BlockSpec((tg, 2, S5_GROUP, LANES), lambda g: (g, 0, 0, 0))
    sq = pl.BlockSpec((tg, S5_CK, S5_CK), lambda g: (g, 0, 0))
    return pl.pallas_call(
        _s5_prep_body,
        grid=(S5_GROUPS // tg,),
        in_specs=[pl.BlockSpec((tg, 8, LANES), lambda g: (g, 0, 0)), mat, mat],
        out_specs=[pl.BlockSpec((tg, S5_GROUP, 2 * S5_CK), lambda g: (g, 0, 0)), sq, sq,
                   pl.BlockSpec((tg, 2, LANES), lambda g: (g, 0, 0))],
        out_shape=[jax.ShapeDtypeStruct((S5_GROUPS, S5_GROUP, 2 * S5_CK), F32),
                   jax.ShapeDtypeStruct((S5_GROUPS, S5_CK, S5_CK), BF16),
                   jax.ShapeDtypeStruct((S5_GROUPS, S5_CK, S5_CK), BF16),
                   jax.ShapeDtypeStruct((S5_GROUPS, 2, LANES), F32)],
        scratch_shapes=[pltpu.VMEM((S5_TILE_GROUPS, 2 * S5_CK, S5_CK), BF16)],
        compiler_params=_params(("parallel",), 24 << 20),
        name="s5_prep",
    )(params, bt, ct)


def _s5_norm_body(x_ref, g_ref, o_ref):
    h = _rms(x_ref[...], g_ref[...])
    o_ref[...] = h.reshape(o_ref.shape)


def _s5_norm(x, g, tm=512):
    cr = tm // S5_CHUNK
    return pl.pallas_call(
        _s5_norm_body,
        grid=(BATCH, SEQ // tm),
        in_specs=[pl.BlockSpec((None, tm, D_MODEL), lambda b, i: (b, i, 0)),
                  _resident((1, D_MODEL))],
        out_specs=pl.BlockSpec((cr, None, S5_CHUNK, D_MODEL), lambda b, i: (i, b, 0, 0)),
        out_shape=jax.ShapeDtypeStruct((S5_CHUNKS, BATCH, S5_CHUNK, D_MODEL), F32),
        compiler_params=_params(("parallel", "parallel"), 32 << 20),
        name="s5_norm",
    )(x, g.reshape(1, D_MODEL))


def _block_transpose8(a, lane_block):
    a = list(a)
    for d in (4, 2, 1):
        take_lo = (lane_block & d) == 0
        nxt = list(a)
        for i in range(8):
            if i & d:
                continue
            lo, hi = a[i], a[i + d]
            if 2 * S5_GROUP * d == LANES:
                both = pltpu.roll(jnp.where(take_lo, hi, lo), S5_GROUP * d, 1)
                nxt[i] = jnp.where(take_lo, lo, both)
                nxt[i + d] = jnp.where(take_lo, both, hi)
            else:
                nxt[i] = jnp.where(take_lo, lo, pltpu.roll(hi, S5_GROUP * d, 1))
                nxt[i + d] = jnp.where(take_lo, pltpu.roll(lo, LANES - S5_GROUP * d, 1), hi)
        a = nxt
    return a


def _gelu_tanh(y):
    return 0.5 * y * (1.0 + jnp.tanh(math.sqrt(2.0 / math.pi) * (y + 0.044715 * (y * y * y))))


S5_RB = 128


def _s5_body(h_ref, strip_ref, ws_ref, wc_ref, coef_ref, d_ref, o_ref, x_sc, y_sc, toep_sc):
    lane = lax.broadcasted_iota(jnp.int32, (1, LANES), 1)
    lane_block = lane // S5_GROUP
    fwd_lanes = lane < S5_STATE
    n_rb = S5_ROWS // S5_RB
    tok_rb = S5_RB * S5_CHUNK

    def relayout_in(i, carry):
        base = pl.multiple_of(i * tok_rb, tok_rb)
        rows = pl.multiple_of(i * S5_RB, S5_RB)
        for half in range(2):
            a = [pltpu.bitcast(h_ref[pl.ds(base + half * 8 + t, S5_RB, stride=S5_CHUNK), :].astype(BF16), jnp.uint32)
                 for t in range(8)]
            xt = _block_transpose8(a, lane_block)
            for gi in range(S5_TILE_GROUPS):
                x_sc[gi, pl.ds(rows, S5_RB), half * LANES:(half + 1) * LANES] = pltpu.bitcast(xt[gi], BF16)
        return carry

    lax.fori_loop(0, n_rb, relayout_in, 0)

    nt = (((1,), (1,)), ((), ()))
    for gi in range(S5_TILE_GROUPS):
        y_sc[gi] = jnp.dot(x_sc[gi], ws_ref[gi], preferred_element_type=F32)
        strip = strip_ref[gi]
        for t in range(S5_CHUNK):
            off = (S5_CHUNK - 1 - t) * S5_GROUP
            win = strip if off == 0 else pltpu.roll(strip, strip.shape[1] - off, 1)
            toep_sc[gi, t * S5_GROUP:(t + 1) * S5_GROUP, :] = win[:, :S5_CK].astype(BF16)

    coef = [(coef_ref[gi, 0:1, :], coef_ref[gi, 1:2, :]) for gi in range(S5_TILE_GROUPS)]

    def scan_step(k, st):
        rf = pl.ds(pl.multiple_of(k * BATCH, BATCH), BATCH)
        rb = pl.ds(pl.multiple_of((S5_CHUNKS - 1 - k) * BATCH, BATCH), BATCH)
        s_in = [(jnp.where(fwd_lanes, y_sc[gi, rf, :LANES], y_sc[gi, rb, :LANES]),
                 jnp.where(fwd_lanes, y_sc[gi, rf, LANES:], y_sc[gi, rb, LANES:])) for gi in range(S5_TILE_GROUPS)]
        nxt = []
        for gi in range(S5_TILE_GROUPS):
            st_re, st_im = st[2 * gi], st[2 * gi + 1]
            a_re, a_im = coef[gi]
            s_re, s_im = s_in[gi]
            y_sc[gi, rf, :S5_STATE] = st_re[:, :S5_STATE]
            y_sc[gi, rb, S5_STATE:LANES] = st_re[:, S5_STATE:]
            y_sc[gi, rf, LANES:LANES + S5_STATE] = st_im[:, :S5_STATE]
            y_sc[gi, rb, LANES + S5_STATE:] = st_im[:, S5_STATE:]
            nxt.append(a_re * st_re - a_im * st_im + s_re)
            nxt.append(a_re * st_im + a_im * st_re + s_im)
        return tuple(nxt)

    zero = jnp.zeros((BATCH, LANES), F32)
    lax.fori_loop(0, S5_CHUNKS, scan_step, (zero,) * (2 * S5_TILE_GROUPS))

    for gi in range(S5_TILE_GROUPS):
        h_in = y_sc[gi].astype(BF16)
        y_sc[gi] = (jnp.dot(x_sc[gi], toep_sc[gi], preferred_element_type=F32)
                    + lax.dot_general(h_in, wc_ref[gi], nt, preferred_element_type=F32))

    d_skip = d_ref[...]

    def relayout_out(i, carry):
        base = pl.multiple_of(i * tok_rb, tok_rb)
        rows = pl.multiple_of(i * S5_RB, S5_RB)
        for half in range(2):
            yg = [y_sc[gi, pl.ds(rows, S5_RB), half * LANES:(half + 1) * LANES] for gi in range(S5_TILE_GROUPS)]
            yt = _block_transpose8(yg, lane_block)
            for t in range(8):
                tok = pl.ds(base + half * 8 + t, S5_RB, stride=S5_CHUNK)
                o_ref[tok, :] = _gelu_tanh(yt[t] + d_skip * h_ref[tok, :])
        return carry

    lax.fori_loop(0, n_rb, relayout_out, 0)


def _s5_core(h2, strip, s_in, wc_t, coef, d_skip):
    n_tiles = S5_GROUPS // S5_TILE_GROUPS
    tok_block = pl.BlockSpec((N_TOKENS, LANES), lambda j: (0, j))
    vmem = (4 * N_TOKENS * LANES * 4 + S5_TILE_GROUPS * S5_ROWS * S5_CK * (2 + 4)
            + 4 * S5_ROWS * LANES * 4 + 2 * S5_TILE_GROUPS * S5_CK * 3 * S5_CK * 2 + (12 << 20))
    return pl.pallas_call(
        _s5_body,
        grid=(n_tiles,),
        in_specs=[tok_block,
                  pl.BlockSpec((S5_TILE_GROUPS, S5_GROUP, 2 * S5_CK), lambda j: (j, 0, 0)),
                  pl.BlockSpec((S5_TILE_GROUPS, S5_CK, S5_CK), lambda j: (j, 0, 0)),
                  pl.BlockSpec((S5_TILE_GROUPS, S5_CK, S5_CK), lambda j: (j, 0, 0)),
                  pl.BlockSpec((S5_TILE_GROUPS, 2, LANES), lambda j: (j, 0, 0)),
                  pl.BlockSpec((1, LANES), lambda j: (0, j))],
        out_specs=tok_block,
        out_shape=jax.ShapeDtypeStruct((N_TOKENS, D_MODEL), F32),
        scratch_shapes=[pltpu.VMEM((S5_TILE_GROUPS, S5_ROWS, S5_CK), BF16),
                        pltpu.VMEM((S5_TILE_GROUPS, S5_ROWS, S5_CK), F32),
                        pltpu.VMEM((S5_TILE_GROUPS, S5_CK, S5_CK), BF16)],
        compiler_params=_params(("arbitrary",), vmem),
        name="s5_core",
    )(h2, strip, s_in, wc_t, coef, d_skip.reshape(1, D_MODEL))


def _s5_glu_body(x_ref, a_ref, w_ref, o_ref):
    a = a_ref[...].reshape(x_ref.shape).astype(BF16)
    vg = jnp.dot(a, w_ref[...], preferred_element_type=F32)
    o_ref[...] = x_ref[...] + vg[:, :D_MODEL] * _sigmoid(vg[:, D_MODEL:])


def _s5_glu(x, act, w_glu_all, layer, tm=512):
    cr = tm // S5_CHUNK
    vmem = 2 * D_MODEL * D_MODEL * 2 + 6 * tm * D_MODEL * 4 + 2 * tm * 2 * D_MODEL * 4 + (4 << 20)
    return pl.pallas_call(
        _s5_glu_body,
        grid=(BATCH, SEQ // tm),
        in_specs=[pl.BlockSpec((None, tm, D_MODEL), lambda b, i: (b, i, 0)),
                  pl.BlockSpec((cr, None, S5_CHUNK, D_MODEL), lambda b, i: (i, b, 0, 0)),
                  _layer_block((D_MODEL, 2 * D_MODEL), layer)],
        out_specs=pl.BlockSpec((None, tm, D_MODEL), lambda b, i: (b, i, 0)),
        out_shape=jax.ShapeDtypeStruct((BATCH, SEQ, D_MODEL), F32),
        compiler_params=_params(("parallel", "parallel"), vmem),
        name="s5_glu",
    )(x, act, w_glu_all)


def _s5_layer(x, h, lam_re, lam_im, log_dt, b_re, b_im, c_re, c_im, d_skip, w_glu_all, layer):
    strip, s_in, wc_t, coef = _s5_matrices(lam_re, lam_im, log_dt, b_re, b_im, c_re, c_im)
    act = _s5_core(h.reshape(N_TOKENS, D_MODEL), strip, s_in, wc_t, coef, d_skip)
    act = act.reshape(S5_CHUNKS, BATCH, S5_CHUNK, D_MODEL)
    return _s5_glu(x, act, w_glu_all, layer)


def _t5_bucket(rel):
    half = NUM_BUCKETS // 2
    max_exact = half // 2
    n = np.abs(rel)
    sign = (rel > 0).astype(np.int32) * half
    large = max_exact + (np.log(np.maximum(n, 1) / max_exact) / math.log(MAX_DISTANCE / max_exact)
                         * (half - max_exact)).astype(np.int32)
    large = np.minimum(large, half - 1)
    return (sign + np.where(n < max_exact, n, large)).astype(np.int32)


LOG2E = 1.4426950408889634
N_PAIRS = HEADS_PER_GROUP // 2
ATTN_PAD_ROWS = SEQ + 2 * ATTN_SIDE * max(d for _, d in DILATED_GROUPS)
QKV_GROUP_WIDTH = 3 * D_MODEL
QKV_SLAB = 4 * LANES
QKV_TILE_ROWS = 512


def _attn_bias_tables(rel_bias):
    n_off = ATTN_BK + ATTN_BQ - 1
    offs = np.arange(n_off) - (ATTN_BQ - 1) - ATTN_SIDE
    strips = []
    for gi, (_, dil) in enumerate(DILATED_GROUPS):
        onehot = jnp.asarray(_t5_bucket(offs * dil)[:, None] == np.arange(NUM_BUCKETS), F32)
        f = jnp.dot(onehot, rel_bias[:, gi * HEADS_PER_GROUP:(gi + 1) * HEADS_PER_GROUP].astype(F32),
                    precision=lax.Precision.HIGHEST).T
        strips.append(jnp.pad(f[:, ::-1], ((0, 0), (0, BIAS_STRIP - n_off))))
    strips = jnp.stack(strips).reshape(N_ATTN_GROUPS, HEADS_PER_GROUP, 1, BIAS_STRIP)
    return pl.pallas_call(
        _attn_bias_body,
        grid=(N_ATTN_GROUPS, N_PAIRS),
        in_specs=[pl.BlockSpec((None, 2, 1, BIAS_STRIP), lambda g, p: (g, p, 0, 0))],
        out_specs=pl.BlockSpec((None, None, ATTN_BK, 2 * ATTN_BQ), lambda g, p: (g, p, 0, 0)),
        out_shape=jax.ShapeDtypeStruct((N_ATTN_GROUPS, N_PAIRS, ATTN_BK, 2 * ATTN_BQ), F32),
        compiler_params=_params(("parallel", "parallel"), 16 << 20),
        name="attn_bias",
    )(strips)


BIAS_STRIP = 256


def _attn_bias_body(f_ref, o_ref):
    krow = lax.broadcasted_iota(jnp.int32, (ATTN_BK, 1), 0)
    lane = lax.broadcasted_iota(jnp.int32, (1, LANES), 1)
    head0 = lane < ATTN_BQ
    band = jnp.abs(krow - ATTN_SIDE - (lane & (ATTN_BQ - 1))) <= ATTN_SIDE
    halves = []
    for a in range(2):
        x = jnp.broadcast_to(f_ref[a], (ATTN_BK, BIAS_STRIP))
        shift = (a * ATTN_BQ - (ATTN_BK - 1)) % BIAS_STRIP
        halves.append(pltpu.roll(x, shift, 1, stride=1, stride_axis=0)[:, :LANES])
    o_ref[...] = jnp.where(band, jnp.where(head0, halves[0], halves[1]) * LOG2E, -1e30)


def _qkv_body(x_ref, g_ref, w_ref, qg_ref, kg_ref, perm_ref, q_out, k_out, v_out, *, n_res):
    rows = x_ref.shape[0] // n_res
    h = _rms(x_ref[...], g_ref[...]).astype(BF16)
    if n_res > 1:
        h = jnp.dot(perm_ref[...], h, preferred_element_type=F32).astype(BF16)
    lane = lax.broadcasted_iota(jnp.int32, (1, LANES), 1)
    head0 = lane < HEAD_DIM

    def head_norm(t, gain):
        sq = t * t
        s0 = jnp.sum(jnp.where(head0, sq, 0.0), axis=-1, keepdims=True)
        s1 = jnp.sum(jnp.where(head0, 0.0, sq), axis=-1, keepdims=True)
        ms = jnp.where(head0, s0, s1) * (1.0 / HEAD_DIM)
        return t * lax.rsqrt(ms + EPS) * gain

    for c in range(QKV_GROUP_WIDTH // QKV_SLAB):
        z = jnp.dot(h, w_ref[:, c * QKV_SLAB:(c + 1) * QKV_SLAB].astype(BF16), preferred_element_type=F32)
        for half in range(QKV_SLAB // LANES):
            section, lo = divmod(c * QKV_SLAB + half * LANES, D_MODEL)
            zz = z[:, half * LANES:(half + 1) * LANES]
            if section == 0:
                zz = head_norm(zz, qg_ref[...]) * (HEAD_DIM ** -0.5 * LOG2E)
                qa = jnp.where(head0, zz, 0.0).astype(BF16)
                qb = jnp.where(head0, 0.0, zz).astype(BF16)
            elif section == 1:
                zz = head_norm(zz, kg_ref[...]).astype(BF16)
            else:
                zz = zz.astype(BF16)
            for r in range(n_res):
                piece = slice(r * rows, (r + 1) * rows)
                if section == 0:
                    q_out[r, 0, :, lo:lo + LANES] = qa[piece]
                    q_out[r, 1, :, lo:lo + LANES] = qb[piece]
                elif section == 1:
                    k_out[r, :, lo:lo + LANES] = zz[piece]
                else:
                    v_out[r, :, lo:lo + LANES] = zz[piece]


def _qkv_group(x, g, w_qkv_all, layer, gi, dil, q_gain, k_gain):
    seg = SEQ // dil
    tm = QKV_TILE_ROWS
    rows = tm // dil
    assert tm % dil == 0 and rows % 16 == 0
    gain2 = lambda gn: jnp.tile(gn.astype(F32), 2).reshape(1, LANES)
    src = np.arange(tm).reshape(rows, dil).T.reshape(tm)
    perm = jnp.asarray(src[:, None] == np.arange(tm)[None, :], BF16)
    kv_spec = pl.BlockSpec((None, dil, rows, D_MODEL), lambda b, i: (b, 0, i, 0))
    kv_shape = jax.ShapeDtypeStruct((BATCH, dil, seg, D_MODEL), BF16)
    vmem = (D_MODEL * QKV_GROUP_WIDTH * 4 + 2 * tm * D_MODEL * 4 + 2 * 4 * tm * D_MODEL * 2 + (12 << 20))
    return pl.pallas_call(
        functools.partial(_qkv_body, n_res=dil),
        grid=(BATCH, SEQ // tm),
        in_specs=[pl.BlockSpec((None, tm, D_MODEL), lambda b, i: (b, i, 0)),
                  _resident((1, D_MODEL)),
                  pl.BlockSpec((None, D_MODEL, QKV_GROUP_WIDTH), lambda b, i: (layer, 0, gi),
                               pipeline_mode=pl.Buffered(1)),
                  _resident((1, LANES)), _resident((1, LANES)), _resident((tm, tm))],
        out_specs=[pl.BlockSpec((None, dil, 2, rows, D_MODEL), lambda b, i: (b, 0, 0, i, 0)),
                   kv_spec, kv_spec],
        out_shape=[jax.ShapeDtypeStruct((BATCH, dil, 2, seg, D_MODEL), BF16), kv_shape, kv_shape],
        compiler_params=_params(("parallel", "parallel"), vmem),
        name=f"qkv_dil{dil}",
    )(x, g.reshape(1, D_MODEL), w_qkv_all, gain2(q_gain), gain2(k_gain), perm)


ATTN_AHEAD = 5


def _attn_body(q0, k0, v0, q1, k1, v1, q2, k2, v2, bias_ref, o_ref, kp_sc, vp_sc, lt_sc, og_sc, lg_sc):
    lane = lax.broadcasted_iota(jnp.int32, (1, LANES), 1)
    head0 = lane < HEAD_DIM
    krow = lax.broadcasted_iota(jnp.int32, (ATTN_BK, 1), 0)
    zpad = jnp.zeros((ATTN_SIDE, LANES), BF16)
    tn = (((0,), (0,)), ((), ()))

    qkv_refs = ((q0, k0, v0), (q1, k1, v1), (q2, k2, v2))
    for g, (_, dil) in enumerate(DILATED_GROUPS):
        q_ref, k_ref, v_ref = qkv_refs[g]
        seg = SEQ // dil
        nb = seg // ATTN_BQ
        pad_seg = seg + 2 * ATTN_SIDE
        lt_sc[...] = jnp.zeros(lt_sc.shape, F32)
        for r in range(dil):
            base = r * pad_seg
            for src, dst in ((k_ref, kp_sc), (v_ref, vp_sc)):
                dst[base:base + ATTN_SIDE] = zpad
                dst[base + ATTN_SIDE + seg:base + pad_seg] = zpad
                dst[base + ATTN_SIDE:base + ATTN_SIDE + seg] = src[r]

        def scores(r, i):
            base = r * pad_seg
            win = slice(base + i * ATTN_BQ, base + i * ATTN_BQ + ATTN_BK)
            qrows = slice(i * ATTN_BQ, (i + 1) * ATTN_BQ)
            qm = jnp.concatenate([q_ref[r, 0, qrows, :], q_ref[r, 1, qrows, :]], axis=0)
            s = lax.dot_general(kp_sc[win], qm, (((1,), (1,)), ((), ())), preferred_element_type=F32)
            s = s + bias_ref[g]
            if i == 0 or i == nb - 1:
                ok = None
                if i == 0:
                    ok = krow >= ATTN_SIDE
                if i == nb - 1:
                    ok_hi = krow < ATTN_BK - ATTN_SIDE
                    ok = ok_hi if ok is None else jnp.logical_and(ok, ok_hi)
                s = jnp.where(ok, s, -1e30)
            return s

        blocks = [(r, i) for r in range(dil) for i in range(nb)]
        pending = [scores(*blocks[j]) for j in range(min(ATTN_AHEAD, len(blocks)))]
        for blk, (r, i) in enumerate(blocks):
            if blk + ATTN_AHEAD < len(blocks):
                pending.append(scores(*blocks[blk + ATTN_AHEAD]))
            s = pending.pop(0)
            base = r * pad_seg
            win = slice(base + i * ATTN_BQ, base + i * ATTN_BQ + ATTN_BK)
            m = jnp.max(s, axis=0, keepdims=True)
            p = jnp.exp2(s - m)
            den = jnp.sum(p, axis=0, keepdims=True)
            pn = (p * (1.0 / den)).astype(BF16)
            u = lax.dot_general(pn, vp_sc[win], tn, preferred_element_type=F32)
            rows = pl.ds(r + i * ATTN_BQ * dil, ATTN_BQ, stride=dil)
            og_sc[g, rows, :] = jnp.where(head0, u[:ATTN_BQ], u[ATTN_BQ:])
            lt_sc[blk:blk + 1, :] = m + jnp.log2(den)

        ltt = lt_sc[...].T
        for r in range(dil):
            for i in range(nb):
                c = r * nb + i
                rows = pl.ds(r + i * ATTN_BQ * dil, ATTN_BQ, stride=dil)
                lg_sc[g, rows, :] = jnp.where(head0, ltt[:ATTN_BQ, c:c + 1], ltt[ATTN_BQ:, c:c + 1])

    l0, l1, l2 = lg_sc[0], lg_sc[1], lg_sc[2]
    m = jnp.maximum(jnp.maximum(l0, l1), l2)
    w0, w1, w2 = jnp.exp2(l0 - m), jnp.exp2(l1 - m), jnp.exp2(l2 - m)
    o = (w0 * og_sc[0] + w1 * og_sc[1] + w2 * og_sc[2]) / (w0 + w1 + w2)
    o_ref[...] = o.astype(o_ref.dtype)


def _attn_core(qkv, bias_tabs):
    in_specs = []
    for (_, dil) in DILATED_GROUPS:
        seg = SEQ // dil
        in_specs.append(pl.BlockSpec((None, dil, 2, seg, LANES), lambda b, p: (b, 0, 0, 0, p)))
        in_specs.append(pl.BlockSpec((None, dil, seg, LANES), lambda b, p: (b, 0, 0, p)))
        in_specs.append(pl.BlockSpec((None, dil, seg, LANES), lambda b, p: (b, 0, 0, p)))
    in_specs.append(pl.BlockSpec((N_ATTN_GROUPS, None, ATTN_BK, 2 * ATTN_BQ), lambda b, p: (0, p, 0, 0)))
    return pl.pallas_call(
        _attn_body,
        grid=(BATCH, N_PAIRS),
        in_specs=in_specs,
        out_specs=pl.BlockSpec((None, SEQ, LANES), lambda b, p: (b, 0, p)),
        out_shape=jax.ShapeDtypeStruct((BATCH, SEQ, D_MODEL), BF16),
        scratch_shapes=[pltpu.VMEM((ATTN_PAD_ROWS, LANES), BF16),
                        pltpu.VMEM((ATTN_PAD_ROWS, LANES), BF16),
                        pltpu.VMEM((LANES, LANES), F32),
                        pltpu.VMEM((N_ATTN_GROUPS, SEQ, LANES), F32),
                        pltpu.VMEM((N_ATTN_GROUPS, SEQ, LANES), F32)],
        compiler_params=_params(("parallel", "parallel"), 40 << 20),
        name="attn_core",
    )(*qkv, bias_tabs)


def _attn_layer(x, g, w_qkv_all, q_gain, k_gain, rel_bias, layer):
    qkv = []
    for gi, (window, dil) in enumerate(DILATED_GROUPS):
        assert (window // 2) // dil == ATTN_SIDE and (SEQ // dil) % ATTN_BQ == 0
        qkv.extend(_qkv_group(x, g, w_qkv_all, layer, gi, dil, q_gain[gi], k_gain[gi]))
    return _attn_core(qkv, _attn_bias_tables(rel_bias)).reshape(N_TOKENS, D_MODEL)


def kernel(x, norm_mix_g, norm_ffn_g, fnet_w_out, s5_lambda_re, s5_lambda_im, s5_log_dt, s5_b_re, s5_b_im, s5_c_re, s5_c_im, s5_d, s5_w_glu, attn_w_qkv, attn_q_gain, attn_k_gain, attn_w_o, rel_bias, ffn_w_gate_up, ffn_w_down):
    w_gate_up, w_down, w_qkv = ffn_w_gate_up, ffn_w_down, attn_w_qkv
    w_glu, w_o = _cast_bf16(s5_w_glu), _cast_bf16(attn_w_o)
    counts = [0, 0, 0]
    h_next = None
    for i in range(DEPTH):
        kind = i % 3
        j = counts[kind]
        counts[kind] += 1
        proj = None
        if kind == 0:
            x = _fnet_layer(x, norm_mix_g[i], fnet_w_out[j])
        elif kind == 1:
            h = h_next if h_next is not None else _s5_norm(x, norm_mix_g[i])
            x = _s5_layer(x, h, s5_lambda_re[j], s5_lambda_im[j], s5_log_dt[j], s5_b_re[j],
                          s5_b_im[j], s5_c_re[j], s5_c_im[j], s5_d[j], w_glu, j)
        else:
            proj = (_attn_layer(x, norm_mix_g[i], w_qkv, attn_q_gain[j], attn_k_gain[j], rel_bias, j), w_o, j)
        next_is_s5 = i + 1 < DEPTH and (i + 1) % 3 == 1
        out = _ffn(x.reshape(N_TOKENS, D_MODEL), norm_ffn_g[i], w_gate_up, w_down, i, proj=proj,
                   norm_out_g=norm_mix_g[i + 1] if next_is_s5 else None)
        x, h_next = out if next_is_s5 else (out, None)
        x = x.reshape(BATCH, SEQ, D_MODEL)
    return x
```

```python
import functools
import math

import numpy as np
import jax
import jax.numpy as jnp
from jax import lax
from jax.experimental import pallas as pl
from jax.experimental.pallas import tpu as pltpu

F32 = jnp.float32
BF16 = jnp.bfloat16

D_MODEL = 1024
BATCH = 8
SEQ = 2048
DEPTH = 4
N_TOKENS = BATCH * SEQ
EPS = 1e-6
D_FF = 2816
FOURIER_GROUP = 128
S5_GROUP = 16
S5_GROUPS = 64
S5_STATE = 64
HEAD_DIM = 64
HEADS_PER_GROUP = 16
DILATED_GROUPS = ((128, 1), (512, 4), (2048, 16))
N_ATTN_GROUPS = 3
NUM_BUCKETS = 32
MAX_DISTANCE = 1024
ATTN_SIDE = 64

LANES = 128
VMEM_LIMIT_CAP = 60 * 1024 * 1024

S5_CHUNK = 16
S5_CHUNKS = SEQ // S5_CHUNK
S5_ROWS = BATCH * S5_CHUNKS
S5_TILE_GROUPS = LANES // S5_GROUP
S5_CK = S5_CHUNK * S5_GROUP

ATTN_BQ = 64
ATTN_BK = ATTN_BQ + 2 * ATTN_SIDE


def _params(sem, vmem_bytes):
    return pltpu.CompilerParams(dimension_semantics=sem,
                                vmem_limit_bytes=int(min(VMEM_LIMIT_CAP, vmem_bytes)))


def _rms(x, g):
    ms = jnp.mean(x * x, axis=-1, keepdims=True)
    return x * lax.rsqrt(ms + EPS) * g


def _sigmoid(x):
    return 1.0 / (1.0 + jnp.exp(-x))


def _resident(shape):
    nd = len(shape)
    return pl.BlockSpec(shape, lambda *_: (0,) * nd, pipeline_mode=pl.Buffered(1))


def _cast_body(w_ref, o_ref):
    o_ref[...] = w_ref[...].astype(BF16)


def _cast_bf16(w, rows=256):
    n_layers, n_rows, n_cols = w.shape
    block = pl.BlockSpec((None, rows, n_cols), lambda l, i: (l, i, 0))
    return pl.pallas_call(
        _cast_body,
        grid=(n_layers, n_rows // rows),
        in_specs=[block],
        out_specs=block,
        out_shape=jax.ShapeDtypeStruct(w.shape, BF16),
        compiler_params=_params(("parallel", "parallel"), 4 * rows * n_cols * 6 + (4 << 20)),
        name="cast_bf16",
    )(w)


FFN_CHUNK = 256


def _ffn_body(*refs, has_proj, has_norm_out):
    refs = list(refs)
    x_ref, g_ref, wgu_ref, wd_ref = refs[:4]
    a_ref, wp_ref = refs[4:6] if has_proj else (None, None)
    gn_ref = refs[4 + 2 * has_proj] if has_norm_out else None
    o_ref = refs[4 + 2 * has_proj + has_norm_out]
    x = x_ref[...]
    if has_proj:
        x = x + jnp.dot(a_ref[...], wp_ref[...], preferred_element_type=F32)
    h = _rms(x, g_ref[...]).astype(BF16)
    acc = x
    for c in range(0, D_FF, FFN_CHUNK):
        w = min(FFN_CHUNK, D_FF - c)
        gate = jnp.dot(h, wgu_ref[:, c:c + w].astype(BF16), preferred_element_type=F32)
        up = jnp.dot(h, wgu_ref[:, D_FF + c:D_FF + c + w].astype(BF16), preferred_element_type=F32)
        a = (gate * _sigmoid(gate) * up).astype(BF16)
        acc = acc + jnp.dot(a, wd_ref[c:c + w, :].astype(BF16), preferred_element_type=F32)
    o_ref[...] = acc
    if has_norm_out:
        hn_ref = refs[-1]
        hn_ref[...] = _rms(acc, gn_ref[...]).reshape(hn_ref.shape)


def _layer_block(shape, layer):
    return pl.BlockSpec((None,) + shape, lambda *_: (layer,) + (0,) * len(shape), pipeline_mode=pl.Buffered(1))


def _ffn(x2, g, wgu_all, wd_all, layer, proj=None, norm_out_g=None, tm=512):
    m = x2.shape[0]
    tile = pl.BlockSpec((tm, D_MODEL), lambda i: (i, 0))
    in_specs = [tile, _resident((1, D_MODEL)), _layer_block((D_MODEL, 2 * D_FF), layer),
                _layer_block((D_FF, D_MODEL), layer)]
    args = [x2, g.reshape(1, D_MODEL), wgu_all, wd_all]
    out_specs, out_shape = [tile], [jax.ShapeDtypeStruct((m, D_MODEL), F32)]
    if proj is not None:
        in_specs += [tile, _layer_block((D_MODEL, D_MODEL), proj[2])]
        args += [proj[0], proj[1]]
    if norm_out_g is not None:
        tiles_per_seq = SEQ // tm
        in_specs.append(_resident((1, D_MODEL)))
        args.append(norm_out_g.reshape(1, D_MODEL))
        out_specs.append(pl.BlockSpec((tm // S5_CHUNK, None, S5_CHUNK, D_MODEL),
                                      lambda i: (i % tiles_per_seq, i // tiles_per_seq, 0, 0)))
        out_shape.append(jax.ShapeDtypeStruct((S5_CHUNKS, BATCH, S5_CHUNK, D_MODEL), F32))
    vmem = 3 * D_MODEL * D_FF * 4 + 10 * tm * D_MODEL * 4 + (10 << 20)
    outs = pl.pallas_call(
        functools.partial(_ffn_body, has_proj=proj is not None, has_norm_out=norm_out_g is not None),
        grid=(m // tm,),
        in_specs=in_specs,
        out_specs=out_specs,
        out_shape=out_shape,
        compiler_params=_params(("parallel",), vmem),
        name="ffn",
    )(*args)
    return outs if norm_out_g is not None else outs[0]


def _fnet_weight_body(cc_ref, sc_ref, w_ref, o_ref):
    w = w_ref[...]
    o_ref[:, :D_MODEL] = jnp.dot(cc_ref[...], w, preferred_element_type=F32,
                                 precision=lax.Precision.HIGHEST).astype(BF16)
    o_ref[:, D_MODEL:] = jnp.dot(sc_ref[...], w, preferred_element_type=F32,
                                 precision=lax.Precision.HIGHEST).astype(BF16)


def _fnet_weights(w_out):
    n = np.arange(FOURIER_GROUP)
    ang = 2.0 * np.pi * ((n[:, None] * n[None, :]) % FOURIER_GROUP) / FOURIER_GROUP
    cc = jnp.asarray(np.cos(ang) / math.sqrt(FOURIER_GROUP), F32)
    sc = jnp.asarray(np.sin(ang) / math.sqrt(FOURIER_GROUP), F32)
    ng = D_MODEL // FOURIER_GROUP
    return pl.pallas_call(
        _fnet_weight_body,
        grid=(ng,),
        in_specs=[_resident((FOURIER_GROUP, FOURIER_GROUP)),
                  _resident((FOURIER_GROUP, FOURIER_GROUP)),
                  pl.BlockSpec((FOURIER_GROUP, D_MODEL), lambda i: (i, 0))],
        out_specs=pl.BlockSpec((FOURIER_GROUP, 2 * D_MODEL), lambda i: (i, 0)),
        out_shape=jax.ShapeDtypeStruct((D_MODEL, 2 * D_MODEL), BF16),
        compiler_params=_params(("parallel",), 16 << 20),
        name="fnet_weights",
    )(cc, sc, w_out)


FN_HALF = SEQ // 2
FN_BLK = 256
FN_NB = FN_HALF // FN_BLK


def _fnet_tables():
    k = np.arange(FN_HALF)[:, None]
    n = np.arange(FN_HALF)[None, :]
    ang = 2.0 * np.pi * ((k * n) % SEQ) / SEQ
    scale = 1.0 / math.sqrt(SEQ)
    i = np.arange(FN_BLK)
    rev = (i[None, :] == FN_BLK - i[:, None]).astype(np.float32)
    return (jnp.asarray(np.cos(ang) * scale, BF16), jnp.asarray(np.sin(ang) * scale, BF16),
            jnp.asarray(rev, BF16))


def _fnet_body(x_ref, g_ref, cs_ref, ss_ref, w_ref, rev_ref, o_ref, e_sc, d_sc, zc_sc, zs_sc):
    g = g_ref[...]
    scale = 1.0 / math.sqrt(SEQ)
    row = lax.broadcasted_iota(jnp.int32, (FN_BLK, 1), 0)
    first = row == 0
    sign = jnp.where((row & 1) == 0, 1.0, -1.0)
    rev = rev_ref[...]
    blk = lambda a: pl.ds(a * FN_BLK, FN_BLK)
    mirror = lambda a: pl.ds(SEQ - (a + 1) * FN_BLK, FN_BLK)

    alt = jnp.zeros((1, D_MODEL), F32)
    carry = jnp.zeros((1, D_MODEL), F32)
    for a in range(FN_NB):
        lo = _rms(x_ref[blk(a), :], g)
        hi = _rms(x_ref[mirror(a), :], g)
        alt = alt + jnp.sum(sign * lo, axis=0, keepdims=True) + jnp.sum(sign * hi, axis=0, keepdims=True)
        r = jnp.dot(rev, hi.astype(BF16), preferred_element_type=F32)
        r = jnp.where(first, carry, r)
        e_sc[blk(a), :] = (lo + r).astype(BF16)
        d_sc[blk(a), :] = (lo - r).astype(BF16)
        carry = hi[0:1, :]
    h_nyq = carry * scale

    for c in range(FN_NB):
        zc = jnp.dot(cs_ref[blk(c), :], e_sc[...], preferred_element_type=F32) + sign * h_nyq
        zs = jnp.dot(ss_ref[blk(c), :], d_sc[...], preferred_element_type=F32)
        zc_sc[blk(c), :] = zc.astype(BF16)
        zs_sc[blk(c), :] = zs.astype(BF16)

    wc = w_ref[:, :D_MODEL]
    ws = w_ref[:, D_MODEL:]
    z_nyq = jnp.broadcast_to(alt * scale, (8, D_MODEL)).astype(BF16)
    carry = jnp.dot(z_nyq, wc, preferred_element_type=F32)[0:1, :]
    def products(c):
        return (jnp.dot(zc_sc[blk(c), :], wc, preferred_element_type=F32),
                jnp.dot(zs_sc[blk(c), :], ws, preferred_element_type=F32))

    ahead = products(FN_NB - 1)
    for c in reversed(range(FN_NB)):
        p, q = ahead
        if c > 0:
            ahead = products(c - 1)
        o_ref[blk(c), :] = x_ref[blk(c), :] + (p - q)
        m = p + q
        m_hi = m.astype(BF16)
        m_lo = (m - m_hi.astype(F32)).astype(BF16)
        r = (jnp.dot(rev, m_hi, preferred_element_type=F32) + jnp.dot(rev, m_lo, preferred_element_type=F32))
        r = jnp.where(first, carry, r)
        o_ref[mirror(c), :] = x_ref[mirror(c), :] + r
        carry = m[0:1, :]


def _fnet_layer(x, g, w_out):
    wcs = _fnet_weights(w_out)
    cs, ss, rev = _fnet_tables()
    seq_block = pl.BlockSpec((None, SEQ, D_MODEL), lambda b: (b, 0, 0))
    half = pltpu.VMEM((FN_HALF, D_MODEL), BF16)
    vmem = 4 * SEQ * D_MODEL * 4 + 4 * FN_HALF * D_MODEL * 2 + 4 * FN_HALF * D_MODEL * 2 + (12 << 20)
    return pl.pallas_call(
        _fnet_body,
        grid=(BATCH,),
        in_specs=[seq_block, _resident((1, D_MODEL)), _resident((FN_HALF, FN_HALF)),
                  _resident((FN_HALF, FN_HALF)), _resident((D_MODEL, 2 * D_MODEL)),
                  _resident((FN_BLK, FN_BLK))],
        out_specs=seq_block,
        out_shape=jax.ShapeDtypeStruct((BATCH, SEQ, D_MODEL), F32),
        scratch_shapes=[half, half, half, half],
        compiler_params=_params(("parallel",), vmem),
        name="fnet_mix",
    )(x, g.reshape(1, D_MODEL), cs, ss, wcs, rev)


def _s5_prep_body(p_ref, bt_ref, ct_ref, strip_ref, s_ref, wc_ref, coef_ref, lag_sc):
    for gi in range(S5_TILE_GROUPS):
        _s5_prep_group(p_ref.at[gi], bt_ref.at[gi], ct_ref.at[gi], strip_ref.at[gi], s_ref.at[gi], wc_ref.at[gi],
                       coef_ref.at[gi], lag_sc.at[gi])


def _s5_prep_group(p_ref, bt_ref, ct_ref, strip_ref, s_ref, wc_ref, coef_ref, lag_sc):
    L = S5_CHUNK
    lane = lax.broadcasted_iota(jnp.int32, (1, LANES), 1)
    fwd = lane < S5_STATE
    l_re, l_im = p_ref[0:1, :], p_ref[1:2, :]
    dt = jnp.exp(p_ref[2:3, :])
    mag = jnp.exp(l_re * dt)
    lb_re, lb_im = mag * jnp.cos(l_im * dt), mag * jnp.sin(l_im * dt)
    n_re, n_im = lb_re - 1.0, lb_im
    den = l_re * l_re + l_im * l_im
    f_re, f_im = (n_re * l_re + n_im * l_im) / den, (n_im * l_re - n_re * l_im) / den
    b_re, b_im = bt_ref[0], bt_ref[1]
    bb_re, bb_im = f_re * b_re - f_im * b_im, f_re * b_im + f_im * b_re
    c_re, c_im = ct_ref[0], ct_ref[1]
    zero = jnp.zeros((1, LANES), F32)
    pw = [(jnp.ones((1, LANES), F32), zero)]
    for _ in range(L):
        pr, pi = pw[-1]
        pw.append((pr * lb_re - pi * lb_im, pr * lb_im + pi * lb_re))

    def mix(f, bk):
        return jnp.where(fwd, f[0], bk[0]), jnp.where(fwd, f[1], bk[1])

    def times(p, m_re, m_im):
        return p[0] * m_re - p[1] * m_im, p[0] * m_im + p[1] * m_re

    for t in range(L):
        rows = slice(t * S5_GROUP, (t + 1) * S5_GROUP)
        re, im = times(mix(pw[L - 1 - t], pw[t]), bb_re, bb_im)
        s_ref[rows, :LANES] = re.astype(BF16)
        s_ref[rows, LANES:] = im.astype(BF16)
        re, im = times(mix(pw[t + 1], pw[L - t]), c_re, c_im)
        wc_ref[rows, :LANES] = re.astype(BF16)
        wc_ref[rows, LANES:] = (-im).astype(BF16)
    none = (zero, zero)
    for j in range(2 * L):
        rows = slice(j * S5_GROUP, (j + 1) * S5_GROUP)
        f = pw[j - (L - 1)] if L - 1 <= j <= 2 * L - 2 else none
        bk = pw[L - 1 - j] if j <= L - 1 else none
        re, im = times(mix(f, bk), c_re, c_im)
        lag_sc[rows, :LANES] = re.astype(BF16)
        lag_sc[rows, LANES:] = (-im).astype(BF16)
    b_cat = jnp.concatenate([bb_re, bb_im], axis=1)
    b_hi = b_cat.astype(BF16)
    b_lo = (b_cat - b_hi.astype(F32)).astype(BF16)
    nt = (((1,), (1,)), ((), ()))
    lag = lag_sc[...]
    strip_ref[...] = (lax.dot_general(b_hi, lag, nt, preferred_element_type=F32)
                      + lax.dot_general(b_lo, lag, nt, preferred_element_type=F32))
    coef_ref[0:1, :] = pw[L][0]
    coef_ref[1:2, :] = pw[L][1]


def _s5_matrices(lam_re, lam_im, log_dt, b_re, b_im, c_re, c_im):
    def lanes(z):
        return jnp.concatenate([z[0], z[1]], axis=-1).astype(F32)
    params = jnp.stack([lanes(lam_re), lanes(lam_im), jnp.repeat(log_dt.T.astype(F32), S5_STATE, axis=1)], axis=1)
    params = jnp.pad(params, ((0, 0), (0, 5), (0, 0)))
    bt = jnp.stack([lanes(b_re.transpose(0, 1, 3, 2)), lanes(b_im.transpose(0, 1, 3, 2))], axis=1)
    ct = jnp.stack([lanes(c_re), lanes(c_im)], axis=1)
    tg = S5_TILE_GROUPS
    mat = pl.BlockSpec((tg, 2, S5_GROUP, LANES), lambda g: (g, 0, 0, 0))
    sq = pl.BlockSpec((tg, S5_CK, S5_CK), lambda g: (g, 0, 0))
    return pl.pallas_call(
        _s5_prep_body,
        grid=(S5_GROUPS // tg,),
        in_specs=[pl.BlockSpec((tg, 8, LANES), lambda g: (g, 0, 0)), mat, mat],
        out_specs=[pl.BlockSpec((tg, S5_GROUP, 2 * S5_CK), lambda g: (g, 0, 0)), sq, sq,
                   pl.BlockSpec((tg, 2, LANES), lambda g: (g, 0, 0))],
        out_shape=[jax.ShapeDtypeStruct((S5_GROUPS, S5_GROUP, 2 * S5_CK), F32),
                   jax.ShapeDtypeStruct((S5_GROUPS, S5_CK, S5_CK), BF16),
                   jax.ShapeDtypeStruct((S5_GROUPS, S5_CK, S5_CK), BF16),
                   jax.ShapeDtypeStruct((S5_GROUPS, 2, LANES), F32)],
        scratch_shapes=[pltpu.VMEM((S5_TILE_GROUPS, 2 * S5_CK, S5_CK), BF16)],
        compiler_params=_params(("parallel",), 24 << 20),
        name="s5_prep",
    )(params, bt, ct)


def _s5_norm_body(x_ref, g_ref, o_ref):
    h = _rms(x_ref[...], g_ref[...])
    o_ref[...] = h.reshape(o_ref.shape)


def _s5_norm(x, g, tm=512):
    cr = tm // S5_CHUNK
    return pl.pallas_call(
        _s5_norm_body,
        grid=(BATCH, SEQ // tm),
        in_specs=[pl.BlockSpec((None, tm, D_MODEL), lambda b, i: (b, i, 0)),
                  _resident((1, D_MODEL))],
        out_specs=pl.BlockSpec((cr, None, S5_CHUNK, D_MODEL), lambda b, i: (i, b, 0, 0)),
        out_shape=jax.ShapeDtypeStruct((S5_CHUNKS, BATCH, S5_CHUNK, D_MODEL), F32),
        compiler_params=_params(("parallel", "parallel"), 32 << 20),
        name="s5_norm",
    )(x, g.reshape(1, D_MODEL))


def _block_transpose8(a, lane_block):
    a = list(a)
    for d in (4, 2, 1):
        take_lo = (lane_block & d) == 0
        nxt = list(a)
        for i in range(8):
            if i & d:
                continue
            lo, hi = a[i], a[i + d]
            if 2 * S5_GROUP * d == LANES:
                both = pltpu.roll(jnp.where(take_lo, hi, lo), S5_GROUP * d, 1)
                nxt[i] = jnp.where(take_lo, lo, both)
                nxt[i + d] = jnp.where(take_lo, both, hi)
            else:
                nxt[i] = jnp.where(take_lo, lo, pltpu.roll(hi, S5_GROUP * d, 1))
                nxt[i + d] = jnp.where(take_lo, pltpu.roll(lo, LANES - S5_GROUP * d, 1), hi)
        a = nxt
    return a


def _gelu_tanh(y):
    return 0.5 * y * (1.0 + jnp.tanh(math.sqrt(2.0 / math.pi) * (y + 0.044715 * (y * y * y))))


S5_RB = 128


def _s5_body(h_ref, strip_ref, ws_ref, wc_ref, coef_ref, d_ref, o_ref, x_sc, y_sc, toep_sc):
    lane = lax.broadcasted_iota(jnp.int32, (1, LANES), 1)
    lane_block = lane // S5_GROUP
    fwd_lanes = lane < S5_STATE
    n_rb = S5_ROWS // S5_RB
    tok_rb = S5_RB * S5_CHUNK

    def relayout_in(i, carry):
        base = pl.multiple_of(i * tok_rb, tok_rb)
        rows = pl.multiple_of(i * S5_RB, S5_RB)
        for half in range(2):
            a = [pltpu.bitcast(h_ref[pl.ds(base + half * 8 + t, S5_RB, stride=S5_CHUNK), :].astype(BF16), jnp.uint32)
                 for t in range(8)]
            xt = _block_transpose8(a, lane_block)
            for gi in range(S5_TILE_GROUPS):
                x_sc[gi, pl.ds(rows, S5_RB), half * LANES:(half + 1) * LANES] = pltpu.bitcast(xt[gi], BF16)
        return carry

    lax.fori_loop(0, n_rb, relayout_in, 0)

    nt = (((1,), (1,)), ((), ()))
    for gi in range(S5_TILE_GROUPS):
        y_sc[gi] = jnp.dot(x_sc[gi], ws_ref[gi], preferred_element_type=F32)
        strip = strip_ref[gi]
        for t in range(S5_CHUNK):
            off = (S5_CHUNK - 1 - t) * S5_GROUP
            win = strip if off == 0 else pltpu.roll(strip, strip.shape[1] - off, 1)
            toep_sc[gi, t * S5_GROUP:(t + 1) * S5_GROUP, :] = win[:, :S5_CK].astype(BF16)

    coef = [(coef_ref[gi, 0:1, :], coef_ref[gi, 1:2, :]) for gi in range(S5_TILE_GROUPS)]

    def scan_step(k, st):
        rf = pl.ds(pl.multiple_of(k * BATCH, BATCH), BATCH)
        rb = pl.ds(pl.multiple_of((S5_CHUNKS - 1 - k) * BATCH, BATCH), BATCH)
        s_in = [(jnp.where(fwd_lanes, y_sc[gi, rf, :LANES], y_sc[gi, rb, :LANES]),
                 jnp.where(fwd_lanes, y_sc[gi, rf, LANES:], y_sc[gi, rb, LANES:])) for gi in range(S5_TILE_GROUPS)]
        nxt = []
        for gi in range(S5_TILE_GROUPS):
            st_re, st_im = st[2 * gi], st[2 * gi + 1]
            a_re, a_im = coef[gi]
            s_re, s_im = s_in[gi]
            y_sc[gi, rf, :S5_STATE] = st_re[:, :S5_STATE]
            y_sc[gi, rb, S5_STATE:LANES] = st_re[:, S5_STATE:]
            y_sc[gi, rf, LANES:LANES + S5_STATE] = st_im[:, :S5_STATE]
            y_sc[gi, rb, LANES + S5_STATE:] = st_im[:, S5_STATE:]
            nxt.append(a_re * st_re - a_im * st_im + s_re)
            nxt.append(a_re * st_im + a_im * st_re + s_im)
        return tuple(nxt)

    zero = jnp.zeros((BATCH, LANES), F32)
    lax.fori_loop(0, S5_CHUNKS, scan_step, (zero,) * (2 * S5_TILE_GROUPS))

    for gi in range(S5_TILE_GROUPS):
        h_in = y_sc[gi].astype(BF16)
        y_sc[gi] = (jnp.dot(x_sc[gi], toep_sc[gi], preferred_element_type=F32)
                    + lax.dot_general(h_in, wc_ref[gi], nt, preferred_element_type=F32))

    d_skip = d_ref[...]

    def relayout_out(i, carry):
        base = pl.multiple_of(i * tok_rb, tok_rb)
        rows = pl.multiple_of(i * S5_RB, S5_RB)
        for half in range(2):
            yg = [y_sc[gi, pl.ds(rows, S5_RB), half * LANES:(half + 1) * LANES] for gi in range(S5_TILE_GROUPS)]
            yt = _block_transpose8(yg, lane_block)
            for t in range(8):
                tok = pl.ds(base + half * 8 + t, S5_RB, stride=S5_CHUNK)
                o_ref[tok, :] = _gelu_tanh(yt[t] + d_skip * h_ref[tok, :])
        return carry

    lax.fori_loop(0, n_rb, relayout_out, 0)


def _s5_core(h2, strip, s_in, wc_t, coef, d_skip):
    n_tiles = S5_GROUPS // S5_TILE_GROUPS
    tok_block = pl.BlockSpec((N_TOKENS, LANES), lambda j: (0, j))
    vmem = (4 * N_TOKENS * LANES * 4 + S5_TILE_GROUPS * S5_ROWS * S5_CK * (2 + 4)
            + 4 * S5_ROWS * LANES * 4 + 2 * S5_TILE_GROUPS * S5_CK * 3 * S5_CK * 2 + (12 << 20))
    return pl.pallas_call(
        _s5_body,
        grid=(n_tiles,),
        in_specs=[tok_block,
                  pl.BlockSpec((S5_TILE_GROUPS, S5_GROUP, 2 * S5_CK), lambda j: (j, 0, 0)),
                  pl.BlockSpec((S5_TILE_GROUPS, S5_CK, S5_CK), lambda j: (j, 0, 0)),
                  pl.BlockSpec((S5_TILE_GROUPS, S5_CK, S5_CK), lambda j: (j, 0, 0)),
                  pl.BlockSpec((S5_TILE_GROUPS, 2, LANES), lambda j: (j, 0, 0)),
                  pl.BlockSpec((1, LANES), lambda j: (0, j))],
        out_specs=tok_block,
        out_shape=jax.ShapeDtypeStruct((N_TOKENS, D_MODEL), F32),
        scratch_shapes=[pltpu.VMEM((S5_TILE_GROUPS, S5_ROWS, S5_CK), BF16),
                        pltpu.VMEM((S5_TILE_GROUPS, S5_ROWS, S5_CK), F32),
                        pltpu.VMEM((S5_TILE_GROUPS, S5_CK, S5_CK), BF16)],
        compiler_params=_params(("arbitrary",), vmem),
        name="s5_core",
    )(h2, strip, s_in, wc_t, coef, d_skip.reshape(1, D_MODEL))


def _s5_glu_body(x_ref, a_ref, w_ref, o_ref):
    a = a_ref[...].reshape(x_ref.shape).astype(BF16)
    vg = jnp.dot(a, w_ref[...], preferred_element_type=F32)
    o_ref[...] = x_ref[...] + vg[:, :D_MODEL] * _sigmoid(vg[:, D_MODEL:])


def _s5_glu(x, act, w_glu_all, layer, tm=512):
    cr = tm // S5_CHUNK
    vmem = 2 * D_MODEL * D_MODEL * 2 + 6 * tm * D_MODEL * 4 + 2 * tm * 2 * D_MODEL * 4 + (4 << 20)
    return pl.pallas_call(
        _s5_glu_body,
        grid=(BATCH, SEQ // tm),
        in_specs=[pl.BlockSpec((None, tm, D_MODEL), lambda b, i: (b, i, 0)),
                  pl.BlockSpec((cr, None, S5_CHUNK, D_MODEL), lambda b, i: (i, b, 0, 0)),
                  _layer_block((D_MODEL, 2 * D_MODEL), layer)],
        out_specs=pl.BlockSpec((None, tm, D_MODEL), lambda b, i: (b, i, 0)),
        out_shape=jax.ShapeDtypeStruct((BATCH, SEQ, D_MODEL), F32),
        compiler_params=_params(("parallel", "parallel"), vmem),
        name="s5_glu",
    )(x, act, w_glu_all)


def _s5_layer(x, h, lam_re, lam_im, log_dt, b_re, b_im, c_re, c_im, d_skip, w_glu_all, layer):
    strip, s_in, wc_t, coef = _s5_matrices(lam_re, lam_im, log_dt, b_re, b_im, c_re, c_im)
    act = _s5_core(h.reshape(N_TOKENS, D_MODEL), strip, s_in, wc_t, coef, d_skip)
    act = act.reshape(S5_CHUNKS, BATCH, S5_CHUNK, D_MODEL)
    return _s5_glu(x, act, w_glu_all, layer)


def _t5_bucket(rel):
    half = NUM_BUCKETS // 2
    max_exact = half // 2
    n = np.abs(rel)
    sign = (rel > 0).astype(np.int32) * half
    large = max_exact + (np.log(np.maximum(n, 1) / max_exact) / math.log(MAX_DISTANCE / max_exact)
                         * (half - max_exact)).astype(np.int32)
    large = np.minimum(large, half - 1)
    return (sign + np.where(n < max_exact, n, large)).astype(np.int32)


LOG2E = 1.4426950408889634
N_PAIRS = HEADS_PER_GROUP // 2
ATTN_PAD_ROWS = SEQ + 2 * ATTN_SIDE * max(d for _, d in DILATED_GROUPS)
QKV_GROUP_WIDTH = 3 * D_MODEL
QKV_SLAB = 4 * LANES
QKV_TILE_ROWS = 512


def _attn_bias_tables(rel_bias):
    n_off = ATTN_BK + ATTN_BQ - 1
    offs = np.arange(n_off) - (ATTN_BQ - 1) - ATTN_SIDE
    strips = []
    for gi, (_, dil) in enumerate(DILATED_GROUPS):
        onehot = jnp.asarray(_t5_bucket(offs * dil)[:, None] == np.arange(NUM_BUCKETS), F32)
        f = jnp.dot(onehot, rel_bias[:, gi * HEADS_PER_GROUP:(gi + 1) * HEADS_PER_GROUP].astype(F32),
                    precision=lax.Precision.HIGHEST).T
        strips.append(jnp.pad(f[:, ::-1], ((0, 0), (0, BIAS_STRIP - n_off))))
    strips = jnp.stack(strips).reshape(N_ATTN_GROUPS, HEADS_PER_GROUP, 1, BIAS_STRIP)
    return pl.pallas_call(
        _attn_bias_body,
        grid=(N_ATTN_GROUPS, N_PAIRS),
        in_specs=[pl.BlockSpec((None, 2, 1, BIAS_STRIP), lambda g, p: (g, p, 0, 0))],
        out_specs=pl.BlockSpec((None, None, ATTN_BK, 2 * ATTN_BQ), lambda g, p: (g, p, 0, 0)),
        out_shape=jax.ShapeDtypeStruct((N_ATTN_GROUPS, N_PAIRS, ATTN_BK, 2 * ATTN_BQ), F32),
        compiler_params=_params(("parallel", "parallel"), 16 << 20),
        name="attn_bias",
    )(strips)


BIAS_STRIP = 256


def _attn_bias_body(f_ref, o_ref):
    krow = lax.broadcasted_iota(jnp.int32, (ATTN_BK, 1), 0)
    lane = lax.broadcasted_iota(jnp.int32, (1, LANES), 1)
    head0 = lane < ATTN_BQ
    band = jnp.abs(krow - ATTN_SIDE - (lane & (ATTN_BQ - 1))) <= ATTN_SIDE
    halves = []
    for a in range(2):
        x = jnp.broadcast_to(f_ref[a], (ATTN_BK, BIAS_STRIP))
        shift = (a * ATTN_BQ - (ATTN_BK - 1)) % BIAS_STRIP
        halves.append(pltpu.roll(x, shift, 1, stride=1, stride_axis=0)[:, :LANES])
    o_ref[...] = jnp.where(band, jnp.where(head0, halves[0], halves[1]) * LOG2E, -1e30)


def _qkv_body(x_ref, g_ref, w_ref, qg_ref, kg_ref, q_out, k_out, v_out, h_sc, *, n_res):
    rows = x_ref.shape[0] // n_res
    h = _rms(x_ref[...], g_ref[...])
    if n_res > 1:
        for c in range(D_MODEL // LANES):
            h_sc[c] = h[:, c * LANES:(c + 1) * LANES]
        h = jnp.concatenate(
            [jnp.concatenate([h_sc[c, pl.ds(r, rows, stride=n_res), :] for r in range(n_res)], axis=0)
             for c in range(D_MODEL // LANES)], axis=1)
    h = h.astype(BF16)
    lane = lax.broadcasted_iota(jnp.int32, (1, LANES), 1)
    head0 = lane < HEAD_DIM

    def head_norm(t, gain):
        sq = t * t
        s0 = jnp.sum(jnp.where(head0, sq, 0.0), axis=-1, keepdims=True)
        s1 = jnp.sum(jnp.where(head0, 0.0, sq), axis=-1, keepdims=True)
        ms = jnp.where(head0, s0, s1) * (1.0 / HEAD_DIM)
        return t * lax.rsqrt(ms + EPS) * gain

    for c in range(QKV_GROUP_WIDTH // QKV_SLAB):
        z = jnp.dot(h, w_ref[:, c * QKV_SLAB:(c + 1) * QKV_SLAB].astype(BF16), preferred_element_type=F32)
        for half in range(QKV_SLAB // LANES):
            section, lo = divmod(c * QKV_SLAB + half * LANES, D_MODEL)
            zz = z[:, half * LANES:(half + 1) * LANES]
            if section == 0:
                zz = head_norm(zz, qg_ref[...]) * (HEAD_DIM ** -0.5 * LOG2E)
                qa = jnp.where(head0, zz, 0.0).astype(BF16)
                qb = jnp.where(head0, 0.0, zz).astype(BF16)
            elif section == 1:
                zz = head_norm(zz, kg_ref[...]).astype(BF16)
            else:
                zz = zz.astype(BF16)
            for r in range(n_res):
                piece = slice(r * rows, (r + 1) * rows)
                if section == 0:
                    q_out[r, 0, :, lo:lo + LANES] = qa[piece]
                    q_out[r, 1, :, lo:lo + LANES] = qb[piece]
                elif section == 1:
                    k_out[r, :, lo:lo + LANES] = zz[piece]
                else:
                    v_out[r, :, lo:lo + LANES] = zz[piece]


def _qkv_group(x, g, w_qkv_all, layer, gi, dil, q_gain, k_gain):
    seg = SEQ // dil
    tm = QKV_TILE_ROWS
    rows = tm // dil
    assert tm % dil == 0 and rows % 16 == 0
    gain2 = lambda gn: jnp.tile(gn.astype(F32), 2).reshape(1, LANES)
    kv_spec = pl.BlockSpec((None, dil, rows, D_MODEL), lambda b, i: (b, 0, i, 0))
    kv_shape = jax.ShapeDtypeStruct((BATCH, dil, seg, D_MODEL), BF16)
    vmem = (D_MODEL * QKV_GROUP_WIDTH * 4 + 2 * tm * D_MODEL * 4 + 2 * 4 * tm * D_MODEL * 2 + (12 << 20))
    return pl.pallas_call(
        functools.partial(_qkv_body, n_res=dil),
        grid=(BATCH, SEQ // tm),
        in_specs=[pl.BlockSpec((None, tm, D_MODEL), lambda b, i: (b, i, 0)),
                  _resident((1, D_MODEL)),
                  pl.BlockSpec((None, D_MODEL, QKV_GROUP_WIDTH), lambda b, i: (layer, 0, gi),
                               pipeline_mode=pl.Buffered(1)),
                  _resident((1, LANES)), _resident((1, LANES))],
        out_specs=[pl.BlockSpec((None, dil, 2, rows, D_MODEL), lambda b, i: (b, 0, 0, i, 0)),
                   kv_spec, kv_spec],
        out_shape=[jax.ShapeDtypeStruct((BATCH, dil, 2, seg, D_MODEL), BF16), kv_shape, kv_shape],
        scratch_shapes=[pltpu.VMEM((D_MODEL // LANES, tm, LANES), F32)],
        compiler_params=_params(("parallel", "parallel"), vmem),
        name=f"qkv_dil{dil}",
    )(x, g.reshape(1, D_MODEL), w_qkv_all, gain2(q_gain), gain2(k_gain))


ATTN_AHEAD = 5


def _attn_body(q0, k0, v0, q1, k1, v1, q2, k2, v2, bias_ref, o_ref, kp_sc, vp_sc, lt_sc, og_sc, lg_sc):
    lane = lax.broadcasted_iota(jnp.int32, (1, LANES), 1)
    head0 = lane < HEAD_DIM
    krow = lax.broadcasted_iota(jnp.int32, (ATTN_BK, 1), 0)
    zpad = jnp.zeros((ATTN_SIDE, LANES), BF16)
    tn = (((0,), (0,)), ((), ()))

    qkv_refs = ((q0, k0, v0), (q1, k1, v1), (q2, k2, v2))
    for g, (_, dil) in enumerate(DILATED_GROUPS):
        q_ref, k_ref, v_ref = qkv_refs[g]
        seg = SEQ // dil
        nb = seg // ATTN_BQ
        pad_seg = seg + 2 * ATTN_SIDE
        lt_sc[...] = jnp.zeros(lt_sc.shape, F32)
        for r in range(dil):
            base = r * pad_seg
            for src, dst in ((k_ref, kp_sc), (v_ref, vp_sc)):
                dst[base:base + ATTN_SIDE] = zpad
                dst[base + ATTN_SIDE + seg:base + pad_seg] = zpad
                dst[base + ATTN_SIDE:base + ATTN_SIDE + seg] = src[r]

        def scores(r, i):
            base = r * pad_seg
            win = slice(base + i * ATTN_BQ, base + i * ATTN_BQ + ATTN_BK)
            qrows = slice(i * ATTN_BQ, (i + 1) * ATTN_BQ)
            qm = jnp.concatenate([q_ref[r, 0, qrows, :], q_ref[r, 1, qrows, :]], axis=0)
            s = lax.dot_general(kp_sc[win], qm, (((1,), (1,)), ((), ())), preferred_element_type=F32)
            s = s + bias_ref[g]
            if i == 0 or i == nb - 1:
                ok = None
                if i == 0:
                    ok = krow >= ATTN_SIDE
                if i == nb - 1:
                    ok_hi = krow < ATTN_BK - ATTN_SIDE
                    ok = ok_hi if ok is None else jnp.logical_and(ok, ok_hi)
                s = jnp.where(ok, s, -1e30)
            return s

        blocks = [(r, i) for r in range(dil) for i in range(nb)]
        pending = [scores(*blocks[j]) for j in range(min(ATTN_AHEAD, len(blocks)))]
        for blk, (r, i) in enumerate(blocks):
            if blk + ATTN_AHEAD < len(blocks):
                pending.append(scores(*blocks[blk + ATTN_AHEAD]))
            s = pending.pop(0)
            base = r * pad_seg
            win = slice(base + i * ATTN_BQ, base + i * ATTN_BQ + ATTN_BK)
            m = jnp.max(s, axis=0, keepdims=True)
            p = jnp.exp2(s - m)
            den = jnp.sum(p, axis=0, keepdims=True)
            pn = (p * (1.0 / den)).astype(BF16)
            u = lax.dot_general(pn, vp_sc[win], tn, preferred_element_type=F32)
            rows = pl.ds(r + i * ATTN_BQ * dil, ATTN_BQ, stride=dil)
            og_sc[g, rows, :] = jnp.where(head0, u[:ATTN_BQ], u[ATTN_BQ:])
            lt_sc[blk:blk + 1, :] = m + jnp.log2(den)

        ltt = lt_sc[...].T
        for r in range(dil):
            for i in range(nb):
                c = r * nb + i
                rows = pl.ds(r + i * ATTN_BQ * dil, ATTN_BQ, stride=dil)
                lg_sc[g, rows, :] = jnp.where(head0, ltt[:ATTN_BQ, c:c + 1], ltt[ATTN_BQ:, c:c + 1])

    l0, l1, l2 = lg_sc[0], lg_sc[1], lg_sc[2]
    m = jnp.maximum(jnp.maximum(l0, l1), l2)
    w0, w1, w2 = jnp.exp2(l0 - m), jnp.exp2(l1 - m), jnp.exp2(l2 - m)
    o = (w0 * og_sc[0] + w1 * og_sc[1] + w2 * og_sc[2]) / (w0 + w1 + w2)
    o_ref[...] = o.astype(o_ref.dtype)


def _attn_core(qkv, bias_tabs):
    in_specs = []
    for (_, dil) in DILATED_GROUPS:
        seg = SEQ // dil
        in_specs.append(pl.BlockSpec((None, dil, 2, seg, LANES), lambda b, p: (b, 0, 0, 0, p)))
        in_specs.append(pl.BlockSpec((None, dil, seg, LANES), lambda b, p: (b, 0, 0, p)))
        in_specs.append(pl.BlockSpec((None, dil, seg, LANES), lambda b, p: (b, 0, 0, p)))
    in_specs.append(pl.BlockSpec((N_ATTN_GROUPS, None, ATTN_BK, 2 * ATTN_BQ), lambda b, p: (0, p, 0, 0)))
    return pl.pallas_call(
        _attn_body,
        grid=(BATCH, N_PAIRS),
        in_specs=in_specs,
        out_specs=pl.BlockSpec((None, SEQ, LANES), lambda b, p: (b, 0, p)),
        out_shape=jax.ShapeDtypeStruct((BATCH, SEQ, D_MODEL), BF16),
        scratch_shapes=[pltpu.VMEM((ATTN_PAD_ROWS, LANES), BF16),
                        pltpu.VMEM((ATTN_PAD_ROWS, LANES), BF16),
                        pltpu.VMEM((LANES, LANES), F32),
                        pltpu.VMEM((N_ATTN_GROUPS, SEQ, LANES), F32),
                        pltpu.VMEM((N_ATTN_GROUPS, SEQ, LANES), F32)],
        compiler_params=_params(("parallel", "parallel"), 40 << 20),
        name="attn_core",
    )(*qkv, bias_tabs)


def _attn_layer(x, g, w_qkv_all, q_gain, k_gain, rel_bias, layer):
    qkv = []
    for gi, (window, dil) in enumerate(DILATED_GROUPS):
        assert (window // 2) // dil == ATTN_SIDE and (SEQ // dil) % ATTN_BQ == 0
        qkv.extend(_qkv_group(x, g, w_qkv_all, layer, gi, dil, q_gain[gi], k_gain[gi]))
    return _attn_core(qkv, _attn_bias_tables(rel_bias)).reshape(N_TOKENS, D_MODEL)


def kernel(x, norm_mix_g, norm_ffn_g, fnet_w_out, s5_lambda_re, s5_lambda_im, s5_log_dt, s5_b_re, s5_b_im, s5_c_re, s5_c_im, s5_d, s5_w_glu, attn_w_qkv, attn_q_gain, attn_k_gain, attn_w_o, rel_bias, ffn_w_gate_up, ffn_w_down):
    w_gate_up, w_down, w_qkv = ffn_w_gate_up, ffn_w_down, attn_w_qkv
    w_glu, w_o = _cast_bf16(s5_w_glu), _cast_bf16(attn_w_o)
    counts = [0, 0, 0]
    h_next = None
    for i in range(DEPTH):
        kind = i % 3
        j = counts[kind]
        counts[kind] += 1
        proj = None
        if kind == 0:
            x = _fnet_layer(x, norm_mix_g[i], fnet_w_out[j])
        elif kind == 1:
            h = h_next if h_next is not None else _s5_norm(x, norm_mix_g[i])
            x = _s5_layer(x, h, s5_lambda_re[j], s5_lambda_im[j], s5_log_dt[j], s5_b_re[j],
                          s5_b_im[j], s5_c_re[j], s5_c_im[j], s5_d[j], w_glu, j)
        else:
            proj = (_attn_layer(x, norm_mix_g[i], w_qkv, attn_q_gain[j], attn_k_gain[j], rel_bias, j), w_o, j)
        next_is_s5 = i + 1 < DEPTH and (i + 1) % 3 == 1
        out = _ffn(x.reshape(N_TOKENS, D_MODEL), norm_ffn_g[i], w_gate_up, w_down, i, proj=proj,
                   norm_out_g=norm_mix_g[i + 1] if next_is_s5 else None)
        x, h_next = out if next_is_s5 else (out, None)
        x = x.reshape(BATCH, SEQ, D_MODEL)
    return x
```

```python
import functools
import math

import numpy as np
import jax
import jax.numpy as jnp
from jax import lax
from jax.experimental import pallas as pl
from jax.experimental.pallas import tpu as pltpu

F32 = jnp.float32
BF16 = jnp.bfloat16

D_MODEL = 1024
BATCH = 8
SEQ = 2048
DEPTH = 4
N_TOKENS = BATCH * SEQ
EPS = 1e-6
D_FF = 2816
FOURIER_GROUP = 128
S5_GROUP = 16
S5_GROUPS = 64
S5_STATE = 64
HEAD_DIM = 64
HEADS_PER_GROUP = 16
DILATED_GROUPS = ((128, 1), (512, 4), (2048, 16))
N_ATTN_GROUPS = 3
NUM_BUCKETS = 32
MAX_DISTANCE = 1024
ATTN_SIDE = 64

LANES = 128
VMEM_LIMIT_CAP = 60 * 1024 * 1024

S5_CHUNK = 16
S5_CHUNKS = SEQ // S5_CHUNK
S5_ROWS = BATCH * S5_CHUNKS
S5_TILE_GROUPS = LANES // S5_GROUP
S5_CK = S5_CHUNK * S5_GROUP

ATTN_BQ = 64
ATTN_BK = ATTN_BQ + 2 * ATTN_SIDE


def _params(sem, vmem_bytes):
    return pltpu.CompilerParams(dimension_semantics=sem,
                                vmem_limit_bytes=int(min(VMEM_LIMIT_CAP, vmem_bytes)))


def _rms(x, g):
    ms = jnp.mean(x * x, axis=-1, keepdims=True)
    return x * lax.rsqrt(ms + EPS) * g


def _sigmoid(x):
    return 1.0 / (1.0 + jnp.exp(-x))


def _resident(shape):
    nd = len(shape)
    return pl.BlockSpec(shape, lambda *_: (0,) * nd, pipeline_mode=pl.Buffered(1))


def _cast_body(w_ref, o_ref):
    o_ref[...] = w_ref[...].astype(BF16)


def _cast_bf16(w, rows=256):
    n_layers, n_rows, n_cols = w.shape
    block = pl.BlockSpec((None, rows, n_cols), lambda l, i: (l, i, 0))
    return pl.pallas_call(
        _cast_body,
        grid=(n_layers, n_rows // rows),
        in_specs=[block],
        out_specs=block,
        out_shape=jax.ShapeDtypeStruct(w.shape, BF16),
        compiler_params=_params(("parallel", "parallel"), 4 * rows * n_cols * 6 + (4 << 20)),
        name="cast_bf16",
    )(w)


FFN_CHUNK = 256


def _ffn_body(*refs, has_proj, has_norm_out):
    refs = list(refs)
    x_ref, g_ref, wgu_ref, wd_ref = refs[:4]
    a_ref, wp_ref = refs[4:6] if has_proj else (None, None)
    gn_ref = refs[4 + 2 * has_proj] if has_norm_out else None
    o_ref = refs[4 + 2 * has_proj + has_norm_out]
    x = x_ref[...]
    if has_proj:
        x = x + jnp.dot(a_ref[...], wp_ref[...], preferred_element_type=F32)
    h = _rms(x, g_ref[...]).astype(BF16)
    acc = x
    for c in range(0, D_FF, FFN_CHUNK):
        w = min(FFN_CHUNK, D_FF - c)
        gate = jnp.dot(h, wgu_ref[:, c:c + w].astype(BF16), preferred_element_type=F32)
        up = jnp.dot(h, wgu_ref[:, D_FF + c:D_FF + c + w].astype(BF16), preferred_element_type=F32)
        a = (gate * _sigmoid(gate) * up).astype(BF16)
        acc = acc + jnp.dot(a, wd_ref[c:c + w, :].astype(BF16), preferred_element_type=F32)
    o_ref[...] = acc
    if has_norm_out:
        hn_ref = refs[-1]
        hn_ref[...] = _rms(acc, gn_ref[...]).reshape(hn_ref.shape)


def _layer_block(shape, layer):
    return pl.BlockSpec((None,) + shape, lambda *_: (layer,) + (0,) * len(shape), pipeline_mode=pl.Buffered(1))


def _ffn(x2, g, wgu_all, wd_all, layer, proj=None, norm_out_g=None, tm=512):
    m = x2.shape[0]
    tile = pl.BlockSpec((tm, D_MODEL), lambda i: (i, 0))
    in_specs = [tile, _resident((1, D_MODEL)), _layer_block((D_MODEL, 2 * D_FF), layer),
                _layer_block((D_FF, D_MODEL), layer)]
    args = [x2, g.reshape(1, D_MODEL), wgu_all, wd_all]
    out_specs, out_shape = [tile], [jax.ShapeDtypeStruct((m, D_MODEL), F32)]
    if proj is not None:
        in_specs += [tile, _layer_block((D_MODEL, D_MODEL), proj[2])]
        args += [proj[0], proj[1]]
    if norm_out_g is not None:
        tiles_per_seq = SEQ // tm
        in_specs.append(_resident((1, D_MODEL)))
        args.append(norm_out_g.reshape(1, D_MODEL))
        out_specs.append(pl.BlockSpec((tm // S5_CHUNK, None, S5_CHUNK, D_MODEL),
                                      lambda i: (i % tiles_per_seq, i // tiles_per_seq, 0, 0)))
        out_shape.append(jax.ShapeDtypeStruct((S5_CHUNKS, BATCH, S5_CHUNK, D_MODEL), F32))
    vmem = 3 * D_MODEL * D_FF * 4 + 10 * tm * D_MODEL * 4 + (10 << 20)
    outs = pl.pallas_call(
        functools.partial(_ffn_body, has_proj=proj is not None, has_norm_out=norm_out_g is not None),
        grid=(m // tm,),
        in_specs=in_specs,
        out_specs=out_specs,
        out_shape=out_shape,
        compiler_params=_params(("parallel",), vmem),
        name="ffn",
    )(*args)
    return outs if norm_out_g is not None else outs[0]


def _fnet_weight_body(cc_ref, sc_ref, w_ref, o_ref):
    w = w_ref[...]
    o_ref[:, :D_MODEL] = jnp.dot(cc_ref[...], w, preferred_element_type=F32,
                                 precision=lax.Precision.HIGHEST).astype(BF16)
    o_ref[:, D_MODEL:] = jnp.dot(sc_ref[...], w, preferred_element_type=F32,
                                 precision=lax.Precision.HIGHEST).astype(BF16)


def _fnet_weights(w_out):
    n = np.arange(FOURIER_GROUP)
    ang = 2.0 * np.pi * ((n[:, None] * n[None, :]) % FOURIER_GROUP) / FOURIER_GROUP
    cc = jnp.asarray(np.cos(ang) / math.sqrt(FOURIER_GROUP), F32)
    sc = jnp.asarray(np.sin(ang) / math.sqrt(FOURIER_GROUP), F32)
    ng = D_MODEL // FOURIER_GROUP
    return pl.pallas_call(
        _fnet_weight_body,
        grid=(ng,),
        in_specs=[_resident((FOURIER_GROUP, FOURIER_GROUP)),
                  _resident((FOURIER_GROUP, FOURIER_GROUP)),
                  pl.BlockSpec((FOURIER_GROUP, D_MODEL), lambda i: (i, 0))],
        out_specs=pl.BlockSpec((FOURIER_GROUP, 2 * D_MODEL), lambda i: (i, 0)),
        out_shape=jax.ShapeDtypeStruct((D_MODEL, 2 * D_MODEL), BF16),
        compiler_params=_params(("parallel",), 16 << 20),
        name="fnet_weights",
    )(cc, sc, w_out)


FN_HALF = SEQ // 2
FN_BLK = 256
FN_NB = FN_HALF // FN_BLK


def _fnet_tables():
    k = np.arange(FN_HALF)[:, None]
    n = np.arange(FN_HALF)[None, :]
    ang = 2.0 * np.pi * ((k * n) % SEQ) / SEQ
    scale = 1.0 / math.sqrt(SEQ)
    i = np.arange(FN_BLK)
    rev = (i[None, :] == FN_BLK - i[:, None]).astype(np.float32)
    return (jnp.asarray(np.cos(ang) * scale, BF16), jnp.asarray(np.sin(ang) * scale, BF16),
            jnp.asarray(rev, BF16))


def _fnet_body(x_ref, g_ref, cs_ref, ss_ref, w_ref, rev_ref, o_ref, e_sc, d_sc, zc_sc, zs_sc):
    g = g_ref[...]
    scale = 1.0 / math.sqrt(SEQ)
    row = lax.broadcasted_iota(jnp.int32, (FN_BLK, 1), 0)
    first = row == 0
    sign = jnp.where((row & 1) == 0, 1.0, -1.0)
    rev = rev_ref[...]
    blk = lambda a: pl.ds(a * FN_BLK, FN_BLK)
    mirror = lambda a: pl.ds(SEQ - (a + 1) * FN_BLK, FN_BLK)

    alt = jnp.zeros((1, D_MODEL), F32)
    carry = jnp.zeros((1, D_MODEL), F32)
    for a in range(FN_NB):
        lo = _rms(x_ref[blk(a), :], g)
        hi = _rms(x_ref[mirror(a), :], g)
        alt = alt + jnp.sum(sign * lo, axis=0, keepdims=True) + jnp.sum(sign * hi, axis=0, keepdims=True)
        r = jnp.dot(rev, hi.astype(BF16), preferred_element_type=F32)
        r = jnp.where(first, carry, r)
        e_sc[blk(a), :] = (lo + r).astype(BF16)
        d_sc[blk(a), :] = (lo - r).astype(BF16)
        carry = hi[0:1, :]
    h_nyq = carry * scale

    for c in range(FN_NB):
        zc = jnp.dot(cs_ref[blk(c), :], e_sc[...], preferred_element_type=F32) + sign * h_nyq
        zs = jnp.dot(ss_ref[blk(c), :], d_sc[...], preferred_element_type=F32)
        zc_sc[blk(c), :] = zc.astype(BF16)
        zs_sc[blk(c), :] = zs.astype(BF16)

    wc = w_ref[:, :D_MODEL]
    ws = w_ref[:, D_MODEL:]
    z_nyq = jnp.broadcast_to(alt * scale, (8, D_MODEL)).astype(BF16)
    carry = jnp.dot(z_nyq, wc, preferred_element_type=F32)[0:1, :]
    def products(c):
        return (jnp.dot(zc_sc[blk(c), :], wc, preferred_element_type=F32),
                jnp.dot(zs_sc[blk(c), :], ws, preferred_element_type=F32))

    ahead = products(FN_NB - 1)
    for c in reversed(range(FN_NB)):
        p, q = ahead
        if c > 0:
            ahead = products(c - 1)
        o_ref[blk(c), :] = x_ref[blk(c), :] + (p - q)
        m = p + q
        m_hi = m.astype(BF16)
        m_lo = (m - m_hi.astype(F32)).astype(BF16)
        r = (jnp.dot(rev, m_hi, preferred_element_type=F32) + jnp.dot(rev, m_lo, preferred_element_type=F32))
        r = jnp.where(first, carry, r)
        o_ref[mirror(c), :] = x_ref[mirror(c), :] + r
        carry = m[0:1, :]


def _fnet_layer(x, g, w_out):
    wcs = _fnet_weights(w_out)
    cs, ss, rev = _fnet_tables()
    seq_block = pl.BlockSpec((None, SEQ, D_MODEL), lambda b: (b, 0, 0))
    half = pltpu.VMEM((FN_HALF, D_MODEL), BF16)
    vmem = 4 * SEQ * D_MODEL * 4 + 4 * FN_HALF * D_MODEL * 2 + 4 * FN_HALF * D_MODEL * 2 + (12 << 20)
    return pl.pallas_call(
        _fnet_body,
        grid=(BATCH,),
        in_specs=[seq_block, _resident((1, D_MODEL)), _resident((FN_HALF, FN_HALF)),
                  _resident((FN_HALF, FN_HALF)), _resident((D_MODEL, 2 * D_MODEL)),
                  _resident((FN_BLK, FN_BLK))],
        out_specs=seq_block,
        out_shape=jax.ShapeDtypeStruct((BATCH, SEQ, D_MODEL), F32),
        scratch_shapes=[half, half, half, half],
        compiler_params=_params(("parallel",), vmem),
        name="fnet_mix",
    )(x, g.reshape(1, D_MODEL), cs, ss, wcs, rev)


def _s5_prep_body(p_ref, bt_ref, ct_ref, strip_ref, s_ref, wc_ref, coef_ref, lag_sc):
    for gi in range(S5_TILE_GROUPS):
        _s5_prep_group(p_ref.at[gi], bt_ref.at[gi], ct_ref.at[gi], strip_ref.at[gi], s_ref.at[gi], wc_ref.at[gi],
                       coef_ref.at[gi], lag_sc.at[gi])


def _s5_prep_group(p_ref, bt_ref, ct_ref, strip_ref, s_ref, wc_ref, coef_ref, lag_sc):
    L = S5_CHUNK
    lane = lax.broadcasted_iota(jnp.int32, (1, LANES), 1)
    fwd = lane < S5_STATE
    l_re, l_im = p_ref[0:1, :], p_ref[1:2, :]
    dt = jnp.exp(p_ref[2:3, :])
    mag = jnp.exp(l_re * dt)
    lb_re, lb_im = mag * jnp.cos(l_im * dt), mag * jnp.sin(l_im * dt)
    n_re, n_im = lb_re - 1.0, lb_im
    den = l_re * l_re + l_im * l_im
    f_re, f_im = (n_re * l_re + n_im * l_im) / den, (n_im * l_re - n_re * l_im) / den
    b_re, b_im = bt_ref[0], bt_ref[1]
    bb_re, bb_im = f_re * b_re - f_im * b_im, f_re * b_im + f_im * b_re
    c_re, c_im = ct_ref[0], ct_ref[1]
    zero = jnp.zeros((1, LANES), F32)
    pw = [(jnp.ones((1, LANES), F32), zero)]
    for _ in range(L):
        pr, pi = pw[-1]
        pw.append((pr * lb_re - pi * lb_im, pr * lb_im + pi * lb_re))

    def mix(f, bk):
        return jnp.where(fwd, f[0], bk[0]), jnp.where(fwd, f[1], bk[1])

    def times(p, m_re, m_im):
        return p[0] * m_re - p[1] * m_im, p[0] * m_im + p[1] * m_re

    for t in range(L):
        rows = slice(t * S5_GROUP, (t + 1) * S5_GROUP)
        re, im = times(mix(pw[L - 1 - t], pw[t]), bb_re, bb_im)
        s_ref[rows, :LANES] = re.astype(BF16)
        s_ref[rows, LANES:] = im.astype(BF16)
        re, im = times(mix(pw[t + 1], pw[L - t]), c_re, c_im)
        wc_ref[rows, :LANES] = re.astype(BF16)
        wc_ref[rows, LANES:] = (-im).astype(BF16)
    none = (zero, zero)
    for j in range(2 * L):
        rows = slice(j * S5_GROUP, (j + 1) * S5_GROUP)
        f = pw[j - (L - 1)] if L - 1 <= j <= 2 * L - 2 else none
        bk = pw[L - 1 - j] if j <= L - 1 else none
        re, im = times(mix(f, bk), c_re, c_im)
        lag_sc[rows, :LANES] = re.astype(BF16)
        lag_sc[rows, LANES:] = (-im).astype(BF16)
    b_cat = jnp.concatenate([bb_re, bb_im], axis=1)
    b_hi = b_cat.astype(BF16)
    b_lo = (b_cat - b_hi.astype(F32)).astype(BF16)
    nt = (((1,), (1,)), ((), ()))
    lag = lag_sc[...]
    strip_ref[...] = (lax.dot_general(b_hi, lag, nt, preferred_element_type=F32)
                      + lax.dot_general(b_lo, lag, nt, preferred_element_type=F32))
    coef_ref[0:1, :] = pw[L][0]
    coef_ref[1:2, :] = pw[L][1]


def _s5_matrices(lam_re, lam_im, log_dt, b_re, b_im, c_re, c_im):
    def lanes(z):
        return jnp.concatenate([z[0], z[1]], axis=-1).astype(F32)
    params = jnp.stack([lanes(lam_re), lanes(lam_im), jnp.repeat(log_dt.T.astype(F32), S5_STATE, axis=1)], axis=1)
    params = jnp.pad(params, ((0, 0), (0, 5), (0, 0)))
    bt = jnp.stack([lanes(b_re.transpose(0, 1, 3, 2)), lanes(b_im.transpose(0, 1, 3, 2))], axis=1)
    ct = jnp.stack([lanes(c_re), lanes(c_im)], axis=1)
    tg = S5_TILE_GROUPS
    mat = pl.BlockSpec((tg, 2, S5_GROUP, LANES), lambda g: (g, 0, 0, 0))
    sq = pl.BlockSpec((tg, S5_CK, S5_CK), lambda g: (g, 0, 0))
    return pl.pallas_call(
        _s5_prep_body,
        grid=(S5_GROUPS // tg,),
        in_specs=[pl.BlockSpec((tg, 8, LANES), lambda g: (g, 0, 0)), mat, mat],
        out_specs=[pl.BlockSpec((tg, S5_GROUP, 2 * S5_CK), lambda g: (g, 0, 0)), sq, sq,
                   pl.BlockSpec((tg, 2, LANES), lambda g: (g, 0, 0))],
        out_shape=[jax.ShapeDtypeStruct((S5_GROUPS, S5_GROUP, 2 * S5_CK), F32),
                   jax.ShapeDtypeStruct((S5_GROUPS, S5_CK, S5_CK), BF16),
                   jax.ShapeDtypeStruct((S5_GROUPS, S5_CK, S5_CK), BF16),
                   jax.ShapeDtypeStruct((S5_GROUPS, 2, LANES), F32)],
        scratch_shapes=[pltpu.VMEM((S5_TILE_GROUPS, 2 * S5_CK, S5_CK), BF16)],
        compiler_params=_params(("parallel",), 24 << 20),
        name="s5_prep",
    )(params, bt, ct)


def _s5_norm_body(x_ref, g_ref, o_ref):
    h = _rms(x_ref[...], g_ref[...])
    o_ref[...] = h.reshape(o_ref.shape)


def _s5_norm(x, g, tm=512):
    cr = tm // S5_CHUNK
    return pl.pallas_call(
        _s5_norm_body,
        grid=(BATCH, SEQ // tm),
        in_specs=[pl.BlockSpec((None, tm, D_MODEL), lambda b, i: (b, i, 0)),
                  _resident((1, D_MODEL))],
        out_specs=pl.BlockSpec((cr, None, S5_CHUNK, D_MODEL), lambda b, i: (i, b, 0, 0)),
        out_shape=jax.ShapeDtypeStruct((S5_CHUNKS, BATCH, S5_CHUNK, D_MODEL), F32),
        compiler_params=_params(("parallel", "parallel"), 32 << 20),
        name="s5_norm",
    )(x, g.reshape(1, D_MODEL))


def _block_transpose8(a, lane_block):
    a = list(a)
    for d in (4, 2, 1):
        take_lo = (lane_block & d) == 0
        nxt = list(a)
        for i in range(8):
            if i & d:
                continue
            lo, hi = a[i], a[i + d]
            if 2 * S5_GROUP * d == LANES:
                both = pltpu.roll(jnp.where(take_lo, hi, lo), S5_GROUP * d, 1)
                nxt[i] = jnp.where(take_lo, lo, both)
                nxt[i + d] = jnp.where(take_lo, both, hi)
            else:
                nxt[i] = jnp.where(take_lo, lo, pltpu.roll(hi, S5_GROUP * d, 1))
                nxt[i + d] = jnp.where(take_lo, pltpu.roll(lo, LANES - S5_GROUP * d, 1), hi)
        a = nxt
    return a


def _gelu_tanh(y):
    return 0.5 * y * (1.0 + jnp.tanh(math.sqrt(2.0 / math.pi) * (y + 0.044715 * (y * y * y))))


S5_RB = 128


def _s5_body(h_ref, strip_ref, ws_ref, wc_ref, coef_ref, d_ref, o_ref, x_sc, y_sc, toep_sc):
    lane = lax.broadcasted_iota(jnp.int32, (1, LANES), 1)
    lane_block = lane // S5_GROUP
    fwd_lanes = lane < S5_STATE
    n_rb = S5_ROWS // S5_RB
    tok_rb = S5_RB * S5_CHUNK

    def relayout_in(i, carry):
        base = pl.multiple_of(i * tok_rb, tok_rb)
        rows = pl.multiple_of(i * S5_RB, S5_RB)
        for half in range(2):
            a = [pltpu.bitcast(h_ref[pl.ds(base + half * 8 + t, S5_RB, stride=S5_CHUNK), :].astype(BF16), jnp.uint32)
                 for t in range(8)]
            xt = _block_transpose8(a, lane_block)
            for gi in range(S5_TILE_GROUPS):
                x_sc[gi, pl.ds(rows, S5_RB), half * LANES:(half + 1) * LANES] = pltpu.bitcast(xt[gi], BF16)
        return carry

    lax.fori_loop(0, n_rb, relayout_in, 0)

    nt = (((1,), (1,)), ((), ()))
    for gi in range(S5_TILE_GROUPS):
        y_sc[gi] = jnp.dot(x_sc[gi], ws_ref[gi], preferred_element_type=F32)
        strip = strip_ref[gi]
        for t in range(S5_CHUNK):
            off = (S5_CHUNK - 1 - t) * S5_GROUP
            win = strip if off == 0 else pltpu.roll(strip, strip.shape[1] - off, 1)
            toep_sc[gi, t * S5_GROUP:(t + 1) * S5_GROUP, :] = win[:, :S5_CK].astype(BF16)

    coef = [(coef_ref[gi, 0:1, :], coef_ref[gi, 1:2, :]) for gi in range(S5_TILE_GROUPS)]

    def scan_step(k, st):
        rf = pl.ds(pl.multiple_of(k * BATCH, BATCH), BATCH)
        rb = pl.ds(pl.multiple_of((S5_CHUNKS - 1 - k) * BATCH, BATCH), BATCH)
        s_in = [(jnp.where(fwd_lanes, y_sc[gi, rf, :LANES], y_sc[gi, rb, :LANES]),
                 jnp.where(fwd_lanes, y_sc[gi, rf, LANES:], y_sc[gi, rb, LANES:])) for gi in range(S5_TILE_GROUPS)]
        nxt = []
        for gi in range(S5_TILE_GROUPS):
            st_re, st_im = st[2 * gi], st[2 * gi + 1]
            a_re, a_im = coef[gi]
            s_re, s_im = s_in[gi]
            y_sc[gi, rf, :S5_STATE] = st_re[:, :S5_STATE]
            y_sc[gi, rb, S5_STATE:LANES] = st_re[:, S5_STATE:]
            y_sc[gi, rf, LANES:LANES + S5_STATE] = st_im[:, :S5_STATE]
            y_sc[gi, rb, LANES + S5_STATE:] = st_im[:, S5_STATE:]
            nxt.append(a_re * st_re - a_im * st_im + s_re)
            nxt.append(a_re * st_im + a_im * st_re + s_im)
        return tuple(nxt)

    zero = jnp.zeros((BATCH, LANES), F32)
    lax.fori_loop(0, S5_CHUNKS, scan_step, (zero,) * (2 * S5_TILE_GROUPS))

    for gi in range(S5_TILE_GROUPS):
        h_in = y_sc[gi].astype(BF16)
        y_sc[gi] = (jnp.dot(x_sc[gi], toep_sc[gi], preferred_element_type=F32)
                    + lax.dot_general(h_in, wc_ref[gi], nt, preferred_element_type=F32))

    d_skip = d_ref[...]

    def relayout_out(i, carry):
        base = pl.multiple_of(i * tok_rb, tok_rb)
        rows = pl.multiple_of(i * S5_RB, S5_RB)
        for half in range(2):
            yg = [y_sc[gi, pl.ds(rows, S5_RB), half * LANES:(half + 1) * LANES] for gi in range(S5_TILE_GROUPS)]
            yt = _block_transpose8(yg, lane_block)
            for t in range(8):
                tok = pl.ds(base + half * 8 + t, S5_RB, stride=S5_CHUNK)
                o_ref[tok, :] = _gelu_tanh(yt[t] + d_skip * h_ref[tok, :])
        return carry

    lax.fori_loop(0, n_rb, relayout_out, 0)


def _s5_core(h2, strip, s_in, wc_t, coef, d_skip):
    n_tiles = S5_GROUPS // S5_TILE_GROUPS
    tok_block = pl.BlockSpec((N_TOKENS, LANES), lambda j: (0, j))
    vmem = (4 * N_TOKENS * LANES * 4 + S5_TILE_GROUPS * S5_ROWS * S5_CK * (2 + 4)
            + 4 * S5_ROWS * LANES * 4 + 2 * S5_TILE_GROUPS * S5_CK * 3 * S5_CK * 2 + (12 << 20))
    return pl.pallas_call(
        _s5_body,
        grid=(n_tiles,),
        in_specs=[tok_block,
                  pl.BlockSpec((S5_TILE_GROUPS, S5_GROUP, 2 * S5_CK), lambda j: (j, 0, 0)),
                  pl.BlockSpec((S5_TILE_GROUPS, S5_CK, S5_CK), lambda j: (j, 0, 0)),
                  pl.BlockSpec((S5_TILE_GROUPS, S5_CK, S5_CK), lambda j: (j, 0, 0)),
                  pl.BlockSpec((S5_TILE_GROUPS, 2, LANES), lambda j: (j, 0, 0)),
                  pl.BlockSpec((1, LANES), lambda j: (0, j))],
        out_specs=tok_block,
        out_shape=jax.ShapeDtypeStruct((N_TOKENS, D_MODEL), F32),
        scratch_shapes=[pltpu.VMEM((S5_TILE_GROUPS, S5_ROWS, S5_CK), BF16),
                        pltpu.VMEM((S5_TILE_GROUPS, S5_ROWS, S5_CK), F32),
                        pltpu.VMEM((S5_TILE_GROUPS, S5_CK, S5_CK), BF16)],
        compiler_params=_params(("arbitrary",), vmem),
        name="s5_core",
    )(h2, strip, s_in, wc_t, coef, d_skip.reshape(1, D_MODEL))


GLU_CHUNK = 256


def _s5_glu_body(x_ref, a_ref, w_ref, o_ref):
    a = a_ref[...].reshape(x_ref.shape).astype(BF16)
    for c in range(0, D_MODEL, GLU_CHUNK):
        val = jnp.dot(a, w_ref[:, c:c + GLU_CHUNK], preferred_element_type=F32)
        gate = jnp.dot(a, w_ref[:, D_MODEL + c:D_MODEL + c + GLU_CHUNK], preferred_element_type=F32)
        o_ref[:, c:c + GLU_CHUNK] = x_ref[:, c:c + GLU_CHUNK] + val * _sigmoid(gate)


def _s5_glu(x, act, w_glu_all, layer, tm=1024):
    cr = tm // S5_CHUNK
    vmem = 2 * D_MODEL * D_MODEL * 2 + 6 * tm * D_MODEL * 4 + 2 * tm * 2 * D_MODEL * 4 + (4 << 20)
    return pl.pallas_call(
        _s5_glu_body,
        grid=(BATCH, SEQ // tm),
        in_specs=[pl.BlockSpec((None, tm, D_MODEL), lambda b, i: (b, i, 0)),
                  pl.BlockSpec((cr, None, S5_CHUNK, D_MODEL), lambda b, i: (i, b, 0, 0)),
                  _layer_block((D_MODEL, 2 * D_MODEL), layer)],
        out_specs=pl.BlockSpec((None, tm, D_MODEL), lambda b, i: (b, i, 0)),
        out_shape=jax.ShapeDtypeStruct((BATCH, SEQ, D_MODEL), F32),
        compiler_params=_params(("parallel", "parallel"), vmem),
        name="s5_glu",
    )(x, act, w_glu_all)


def _s5_layer(x, h, lam_re, lam_im, log_dt, b_re, b_im, c_re, c_im, d_skip, w_glu_all, layer):
    strip, s_in, wc_t, coef = _s5_matrices(lam_re, lam_im, log_dt, b_re, b_im, c_re, c_im)
    act = _s5_core(h.reshape(N_TOKENS, D_MODEL), strip, s_in, wc_t, coef, d_skip)
    act = act.reshape(S5_CHUNKS, BATCH, S5_CHUNK, D_MODEL)
    return _s5_glu(x, act, w_glu_all, layer)


def _t5_bucket(rel):
    half = NUM_BUCKETS // 2
    max_exact = half // 2
    n = np.abs(rel)
    sign = (rel > 0).astype(np.int32) * half
    large = max_exact + (np.log(np.maximum(n, 1) / max_exact) / math.log(MAX_DISTANCE / max_exact)
                         * (half - max_exact)).astype(np.int32)
    large = np.minimum(large, half - 1)
    return (sign + np.where(n < max_exact, n, large)).astype(np.int32)


LOG2E = 1.4426950408889634
N_PAIRS = HEADS_PER_GROUP // 2
ATTN_PAD_ROWS = SEQ + 2 * ATTN_SIDE * max(d for _, d in DILATED_GROUPS)
QKV_GROUP_WIDTH = 3 * D_MODEL
QKV_SLAB = 4 * LANES
QKV_TILE_ROWS = 512


def _attn_bias_tables(rel_bias):
    n_off = ATTN_BK + ATTN_BQ - 1
    offs = np.arange(n_off) - (ATTN_BQ - 1) - ATTN_SIDE
    strips = []
    for gi, (_, dil) in enumerate(DILATED_GROUPS):
        onehot = jnp.asarray(_t5_bucket(offs * dil)[:, None] == np.arange(NUM_BUCKETS), F32)
        f = jnp.dot(onehot, rel_bias[:, gi * HEADS_PER_GROUP:(gi + 1) * HEADS_PER_GROUP].astype(F32),
                    precision=lax.Precision.HIGHEST).T
        strips.append(jnp.pad(f[:, ::-1], ((0, 0), (0, BIAS_STRIP - n_off))))
    strips = jnp.stack(strips).reshape(N_ATTN_GROUPS, HEADS_PER_GROUP, 1, BIAS_STRIP)
    return pl.pallas_call(
        _attn_bias_body,
        grid=(N_ATTN_GROUPS, N_PAIRS),
        in_specs=[pl.BlockSpec((None, 2, 1, BIAS_STRIP), lambda g, p: (g, p, 0, 0))],
        out_specs=pl.BlockSpec((None, None, ATTN_BK, 2 * ATTN_BQ), lambda g, p: (g, p, 0, 0)),
        out_shape=jax.ShapeDtypeStruct((N_ATTN_GROUPS, N_PAIRS, ATTN_BK, 2 * ATTN_BQ), F32),
        compiler_params=_params(("parallel", "parallel"), 16 << 20),
        name="attn_bias",
    )(strips)


BIAS_STRIP = 256


def _attn_bias_body(f_ref, o_ref):
    krow = lax.broadcasted_iota(jnp.int32, (ATTN_BK, 1), 0)
    lane = lax.broadcasted_iota(jnp.int32, (1, LANES), 1)
    head0 = lane < ATTN_BQ
    band = jnp.abs(krow - ATTN_SIDE - (lane & (ATTN_BQ - 1))) <= ATTN_SIDE
    halves = []
    for a in range(2):
        x = jnp.broadcast_to(f_ref[a], (ATTN_BK, BIAS_STRIP))
        shift = (a * ATTN_BQ - (ATTN_BK - 1)) % BIAS_STRIP
        halves.append(pltpu.roll(x, shift, 1, stride=1, stride_axis=0)[:, :LANES])
    o_ref[...] = jnp.where(band, jnp.where(head0, halves[0], halves[1]) * LOG2E, -1e30)


def _qkv_body(x_ref, g_ref, w_ref, qg_ref, kg_ref, q_out, k_out, v_out, h_sc, *, n_res):
    rows = x_ref.shape[0] // n_res
    h = _rms(x_ref[...], g_ref[...])
    if n_res > 1:
        for c in range(D_MODEL // LANES):
            h_sc[c] = h[:, c * LANES:(c + 1) * LANES]
        h = jnp.concatenate(
            [jnp.concatenate([h_sc[c, pl.ds(r, rows, stride=n_res), :] for r in range(n_res)], axis=0)
             for c in range(D_MODEL // LANES)], axis=1)
    h = h.astype(BF16)
    lane = lax.broadcasted_iota(jnp.int32, (1, LANES), 1)
    head0 = lane < HEAD_DIM

    def head_norm(t, gain):
        sq = t * t
        s0 = jnp.sum(jnp.where(head0, sq, 0.0), axis=-1, keepdims=True)
        s1 = jnp.sum(jnp.where(head0, 0.0, sq), axis=-1, keepdims=True)
        ms = jnp.where(head0, s0, s1) * (1.0 / HEAD_DIM)
        return t * lax.rsqrt(ms + EPS) * gain

    for c in range(QKV_GROUP_WIDTH // QKV_SLAB):
        z = jnp.dot(h, w_ref[:, c * QKV_SLAB:(c + 1) * QKV_SLAB].astype(BF16), preferred_element_type=F32)
        for half in range(QKV_SLAB // LANES):
            section, lo = divmod(c * QKV_SLAB + half * LANES, D_MODEL)
            zz = z[:, half * LANES:(half + 1) * LANES]
            if section == 0:
                zz = head_norm(zz, qg_ref[...]) * (HEAD_DIM ** -0.5 * LOG2E)
                qa = jnp.where(head0, zz, 0.0).astype(BF16)
                qb = jnp.where(head0, 0.0, zz).astype(BF16)
            elif section == 1:
                zz = head_norm(zz, kg_ref[...]).astype(BF16)
            else:
                zz = zz.astype(BF16)
            for r in range(n_res):
                piece = slice(r * rows, (r + 1) * rows)
                if section == 0:
                    q_out[r, 0, :, lo:lo + LANES] = qa[piece]
                    q_out[r, 1, :, lo:lo + LANES] = qb[piece]
                elif section == 1:
                    k_out[r, :, lo:lo + LANES] = zz[piece]
                else:
                    v_out[r, :, lo:lo + LANES] = zz[piece]


def _qkv_group(x, g, w_qkv_all, layer, gi, dil, q_gain, k_gain):
    seg = SEQ // dil
    tm = QKV_TILE_ROWS
    rows = tm // dil
    assert tm % dil == 0 and rows % 16 == 0
    gain2 = lambda gn: jnp.tile(gn.astype(F32), 2).reshape(1, LANES)
    kv_spec = pl.BlockSpec((None, dil, rows, D_MODEL), lambda b, i: (b, 0, i, 0))
    kv_shape = jax.ShapeDtypeStruct((BATCH, dil, seg, D_MODEL), BF16)
    vmem = (D_MODEL * QKV_GROUP_WIDTH * 4 + 2 * tm * D_MODEL * 4 + 2 * 4 * tm * D_MODEL * 2 + (12 << 20))
    return pl.pallas_call(
        functools.partial(_qkv_body, n_res=dil),
        grid=(BATCH, SEQ // tm),
        in_specs=[pl.BlockSpec((None, tm, D_MODEL), lambda b, i: (b, i, 0)),
                  _resident((1, D_MODEL)),
                  pl.BlockSpec((None, D_MODEL, QKV_GROUP_WIDTH), lambda b, i: (layer, 0, gi),
                               pipeline_mode=pl.Buffered(1)),
                  _resident((1, LANES)), _resident((1, LANES))],
        out_specs=[pl.BlockSpec((None, dil, 2, rows, D_MODEL), lambda b, i: (b, 0, 0, i, 0)),
                   kv_spec, kv_spec],
        out_shape=[jax.ShapeDtypeStruct((BATCH, dil, 2, seg, D_MODEL), BF16), kv_shape, kv_shape],
        scratch_shapes=[pltpu.VMEM((D_MODEL // LANES, tm, LANES), F32)],
        compiler_params=_params(("parallel", "parallel"), vmem),
        name=f"qkv_dil{dil}",
    )(x, g.reshape(1, D_MODEL), w_qkv_all, gain2(q_gain), gain2(k_gain))


ATTN_AHEAD = 5


def _attn_body(q0, k0, v0, q1, k1, v1, q2, k2, v2, bias_ref, o_ref, kp_sc, vp_sc, lt_sc, og_sc, lg_sc):
    lane = lax.broadcasted_iota(jnp.int32, (1, LANES), 1)
    head0 = lane < HEAD_DIM
    krow = lax.broadcasted_iota(jnp.int32, (ATTN_BK, 1), 0)
    zpad = jnp.zeros((ATTN_SIDE, LANES), BF16)
    tn = (((0,), (0,)), ((), ()))

    qkv_refs = ((q0, k0, v0), (q1, k1, v1), (q2, k2, v2))
    for g, (_, dil) in enumerate(DILATED_GROUPS):
        q_ref, k_ref, v_ref = qkv_refs[g]
        seg = SEQ // dil
        nb = seg // ATTN_BQ
        pad_seg = seg + 2 * ATTN_SIDE
        lt_sc[...] = jnp.zeros(lt_sc.shape, F32)
        for r in range(dil):
            base = r * pad_seg
            for src, dst in ((k_ref, kp_sc), (v_ref, vp_sc)):
                dst[base:base + ATTN_SIDE] = zpad
                dst[base + ATTN_SIDE + seg:base + pad_seg] = zpad
                dst[base + ATTN_SIDE:base + ATTN_SIDE + seg] = src[r]

        def scores(r, i):
            base = r * pad_seg
            win = slice(base + i * ATTN_BQ, base + i * ATTN_BQ + ATTN_BK)
            qrows = slice(i * ATTN_BQ, (i + 1) * ATTN_BQ)
            qm = jnp.concatenate([q_ref[r, 0, qrows, :], q_ref[r, 1, qrows, :]], axis=0)
            s = lax.dot_general(kp_sc[win], qm, (((1,), (1,)), ((), ())), preferred_element_type=F32)
            s = s + bias_ref[g]
            if i == 0 or i == nb - 1:
                ok = None
                if i == 0:
                    ok = krow >= ATTN_SIDE
                if i == nb - 1:
                    ok_hi = krow < ATTN_BK - ATTN_SIDE
                    ok = ok_hi if ok is None else jnp.logical_and(ok, ok_hi)
                s = jnp.where(ok, s, -1e30)
            return s

        blocks = [(r, i) for r in range(dil) for i in range(nb)]
        pending = [scores(*blocks[j]) for j in range(min(ATTN_AHEAD, len(blocks)))]
        for blk, (r, i) in enumerate(blocks):
            if blk + ATTN_AHEAD < len(blocks):
                pending.append(scores(*blocks[blk + ATTN_AHEAD]))
            s = pending.pop(0)
            base = r * pad_seg
            win = slice(base + i * ATTN_BQ, base + i * ATTN_BQ + ATTN_BK)
            m = jnp.max(s, axis=0, keepdims=True)
            p = jnp.exp2(s - m)
            den = jnp.sum(p, axis=0, keepdims=True)
            pn = (p * (1.0 / den)).astype(BF16)
            u = lax.dot_general(pn, vp_sc[win], tn, preferred_element_type=F32)
            rows = pl.ds(r + i * ATTN_BQ * dil, ATTN_BQ, stride=dil)
            og_sc[g, rows, :] = jnp.where(head0, u[:ATTN_BQ], u[ATTN_BQ:])
            lt_sc[blk:blk + 1, :] = m + jnp.log2(den)

        ltt = lt_sc[...].T
        for r in range(dil):
            for i in range(nb):
                c = r * nb + i
                rows = pl.ds(r + i * ATTN_BQ * dil, ATTN_BQ, stride=dil)
                lg_sc[g, rows, :] = jnp.where(head0, ltt[:ATTN_BQ, c:c + 1], ltt[ATTN_BQ:, c:c + 1])

    l0, l1, l2 = lg_sc[0], lg_sc[1], lg_sc[2]
    m = jnp.maximum(jnp.maximum(l0, l1), l2)
    w0, w1, w2 = jnp.exp2(l0 - m), jnp.exp2(l1 - m), jnp.exp2(l2 - m)
    o = (w0 * og_sc[0] + w1 * og_sc[1] + w2 * og_sc[2]) / (w0 + w1 + w2)
    o_ref[...] = o.astype(o_ref.dtype)


def _attn_core(qkv, bias_tabs):
    in_specs = []
    for (_, dil) in DILATED_GROUPS:
        seg = SEQ // dil
        in_specs.append(pl.BlockSpec((None, dil, 2, seg, LANES), lambda b, p: (b, 0, 0, 0, p)))
        in_specs.append(pl.BlockSpec((None, dil, seg, LANES), lambda b, p: (b, 0, 0, p)))
        in_specs.append(pl.BlockSpec((None, dil, seg, LANES), lambda b, p: (b, 0, 0, p)))
    in_specs.append(pl.BlockSpec((N_ATTN_GROUPS, None, ATTN_BK, 2 * ATTN_BQ), lambda b, p: (0, p, 0, 0)))
    return pl.pallas_call(
        _attn_body,
        grid=(BATCH, N_PAIRS),
        in_specs=in_specs,
        out_specs=pl.BlockSpec((None, SEQ, LANES), lambda b, p: (b, 0, p)),
        out_shape=jax.ShapeDtypeStruct((BATCH, SEQ, D_MODEL), BF16),
        scratch_shapes=[pltpu.VMEM((ATTN_PAD_ROWS, LANES), BF16),
                        pltpu.VMEM((ATTN_PAD_ROWS, LANES), BF16),
                        pltpu.VMEM((LANES, LANES), F32),
                        pltpu.VMEM((N_ATTN_GROUPS, SEQ, LANES), F32),
                        pltpu.VMEM((N_ATTN_GROUPS, SEQ, LANES), F32)],
        compiler_params=_params(("parallel", "parallel"), 40 << 20),
        name="attn_core",
    )(*qkv, bias_tabs)


def _attn_layer(x, g, w_qkv_all, q_gain, k_gain, rel_bias, layer):
    qkv = []
    for gi, (window, dil) in enumerate(DILATED_GROUPS):
        assert (window // 2) // dil == ATTN_SIDE and (SEQ // dil) % ATTN_BQ == 0
        qkv.extend(_qkv_group(x, g, w_qkv_all, layer, gi, dil, q_gain[gi], k_gain[gi]))
    return _attn_core(qkv, _attn_bias_tables(rel_bias)).reshape(N_TOKENS, D_MODEL)


def kernel(x, norm_mix_g, norm_ffn_g, fnet_w_out, s5_lambda_re, s5_lambda_im, s5_log_dt, s5_b_re, s5_b_im, s5_c_re, s5_c_im, s5_d, s5_w_glu, attn_w_qkv, attn_q_gain, attn_k_gain, attn_w_o, rel_bias, ffn_w_gate_up, ffn_w_down):
    w_gate_up, w_down, w_qkv = ffn_w_gate_up, ffn_w_down, attn_w_qkv
    w_glu, w_o = _cast_bf16(s5_w_glu), _cast_bf16(attn_w_o)
    counts = [0, 0, 0]
    h_next = None
    for i in range(DEPTH):
        kind = i % 3
        j = counts[kind]
        counts[kind] += 1
        proj = None
        if kind == 0:
            x = _fnet_layer(x, norm_mix_g[i], fnet_w_out[j])
        elif kind == 1:
            h = h_next if h_next is not None else _s5_norm(x, norm_mix_g[i])
            x = _s5_layer(x, h, s5_lambda_re[j], s5_lambda_im[j], s5_log_dt[j], s5_b_re[j],
                          s5_b_im[j], s5_c_re[j], s5_c_im[j], s5_d[j], w_glu, j)
        else:
            proj = (_attn_layer(x, norm_mix_g[i], w_qkv, attn_q_gain[j], attn_k_gain[j], rel_bias, j), w_o, j)
        next_is_s5 = i + 1 < DEPTH and (i + 1) % 3 == 1
        out = _ffn(x.reshape(N_TOKENS, D_MODEL), norm_ffn_g[i], w_gate_up, w_down, i, proj=proj,
                   norm_out_g=norm_mix_g[i + 1] if next_is_s5 else None)
        x, h_next = out if next_is_s5 else (out, None)
        x = x.reshape(BATCH, SEQ, D_MODEL)
    return x
```

```python
import functools
import math

import numpy as np
import jax
import jax.numpy as jnp
from jax import lax
from jax.experimental import pallas as pl
from jax.experimental.pallas import tpu as pltpu

F32 = jnp.float32
BF16 = jnp.bfloat16

D_MODEL = 1024
BATCH = 8
SEQ = 2048
DEPTH = 4
N_TOKENS = BATCH * SEQ
EPS = 1e-6
D_FF = 2816
FOURIER_GROUP = 128
S5_GROUP = 16
S5_GROUPS = 64
S5_STATE = 64
HEAD_DIM = 64
HEADS_PER_GROUP = 16
DILATED_GROUPS = ((128, 1), (512, 4), (2048, 16))
N_ATTN_GROUPS = 3
NUM_BUCKETS = 32
MAX_DISTANCE = 1024
ATTN_SIDE = 64

LANES = 128
VMEM_LIMIT_CAP = 60 * 1024 * 1024

S5_CHUNK = 16
S5_CHUNKS = SEQ // S5_CHUNK
S5_ROWS = BATCH * S5_CHUNKS
S5_TILE_GROUPS = LANES // S5_GROUP
S5_CK = S5_CHUNK * S5_GROUP

ATTN_BQ = 64
ATTN_BK = ATTN_BQ + 2 * ATTN_SIDE


def _params(sem, vmem_bytes):
    return pltpu.CompilerParams(dimension_semantics=sem,
                                vmem_limit_bytes=int(min(VMEM_LIMIT_CAP, vmem_bytes)))


def _rms(x, g):
    ms = jnp.mean(x * x, axis=-1, keepdims=True)
    return x * lax.rsqrt(ms + EPS) * g


def _sigmoid(x):
    return 1.0 / (1.0 + jnp.exp(-x))


def _resident(shape):
    nd = len(shape)
    return pl.BlockSpec(shape, lambda *_: (0,) * nd, pipeline_mode=pl.Buffered(1))


def _cast_body(w_ref, o_ref):
    o_ref[...] = w_ref[...].astype(BF16)


def _cast_bf16(w, rows=256):
    n_layers, n_rows, n_cols = w.shape
    block = pl.BlockSpec((None, rows, n_cols), lambda l, i: (l, i, 0))
    return pl.pallas_call(
        _cast_body,
        grid=(n_layers, n_rows // rows),
        in_specs=[block],
        out_specs=block,
        out_shape=jax.ShapeDtypeStruct(w.shape, BF16),
        compiler_params=_params(("parallel", "parallel"), 4 * rows * n_cols * 6 + (4 << 20)),
        name="cast_bf16",
    )(w)


FFN_CHUNK = 256


def _ffn_body(*refs, has_proj, has_norm_out):
    refs = list(refs)
    x_ref, g_ref, wgu_ref, wd_ref = refs[:4]
    a_ref, wp_ref = refs[4:6] if has_proj else (None, None)
    gn_ref = refs[4 + 2 * has_proj] if has_norm_out else None
    o_ref = refs[4 + 2 * has_proj + has_norm_out]
    x = x_ref[...]
    if has_proj:
        x = x + jnp.dot(a_ref[...], wp_ref[...], preferred_element_type=F32)
    h = _rms(x, g_ref[...]).astype(BF16)
    acc = x
    for c in range(0, D_FF, FFN_CHUNK):
        w = min(FFN_CHUNK, D_FF - c)
        gate = jnp.dot(h, wgu_ref[:, c:c + w].astype(BF16), preferred_element_type=F32)
        up = jnp.dot(h, wgu_ref[:, D_FF + c:D_FF + c + w].astype(BF16), preferred_element_type=F32)
        a = (gate * _sigmoid(gate) * up).astype(BF16)
        acc = acc + jnp.dot(a, wd_ref[c:c + w, :].astype(BF16), preferred_element_type=F32)
    o_ref[...] = acc
    if has_norm_out:
        hn_ref = refs[-1]
        hn_ref[...] = _rms(acc, gn_ref[...]).reshape(hn_ref.shape).astype(hn_ref.dtype)


def _layer_block(shape, layer):
    return pl.BlockSpec((None,) + shape, lambda *_: (layer,) + (0,) * len(shape), pipeline_mode=pl.Buffered(1))


def _ffn(x2, g, wgu_all, wd_all, layer, proj=None, norm_out_g=None, norm_out_chunks=False, tm=512):
    m = x2.shape[0]
    tile = pl.BlockSpec((tm, D_MODEL), lambda i: (i, 0))
    in_specs = [tile, _resident((1, D_MODEL)), _layer_block((D_MODEL, 2 * D_FF), layer),
                _layer_block((D_FF, D_MODEL), layer)]
    args = [x2, g.reshape(1, D_MODEL), wgu_all, wd_all]
    out_specs, out_shape = [tile], [jax.ShapeDtypeStruct((m, D_MODEL), F32)]
    if proj is not None:
        in_specs += [tile, _layer_block((D_MODEL, D_MODEL), proj[2])]
        args += [proj[0], proj[1]]
    if norm_out_g is not None:
        tiles_per_seq = SEQ // tm
        in_specs.append(_resident((1, D_MODEL)))
        args.append(norm_out_g.reshape(1, D_MODEL))
        if norm_out_chunks:
            out_specs.append(pl.BlockSpec((tm // S5_CHUNK, None, S5_CHUNK, D_MODEL),
                                          lambda i: (i % tiles_per_seq, i // tiles_per_seq, 0, 0)))
            out_shape.append(jax.ShapeDtypeStruct((S5_CHUNKS, BATCH, S5_CHUNK, D_MODEL), F32))
        else:
            out_specs.append(tile)
            out_shape.append(jax.ShapeDtypeStruct((m, D_MODEL), BF16))
    vmem = 3 * D_MODEL * D_FF * 4 + 10 * tm * D_MODEL * 4 + (10 << 20)
    outs = pl.pallas_call(
        functools.partial(_ffn_body, has_proj=proj is not None, has_norm_out=norm_out_g is not None),
        grid=(m // tm,),
        in_specs=in_specs,
        out_specs=out_specs,
        out_shape=out_shape,
        compiler_params=_params(("parallel",), vmem),
        name="ffn",
    )(*args)
    return outs if norm_out_g is not None else outs[0]


def _fnet_weight_body(cc_ref, sc_ref, w_ref, o_ref):
    w = w_ref[...]
    o_ref[:, :D_MODEL] = jnp.dot(cc_ref[...], w, preferred_element_type=F32,
                                 precision=lax.Precision.HIGHEST).astype(BF16)
    o_ref[:, D_MODEL:] = jnp.dot(sc_ref[...], w, preferred_element_type=F32,
                                 precision=lax.Precision.HIGHEST).astype(BF16)


def _fnet_weights(w_out):
    n = np.arange(FOURIER_GROUP)
    ang = 2.0 * np.pi * ((n[:, None] * n[None, :]) % FOURIER_GROUP) / FOURIER_GROUP
    cc = jnp.asarray(np.cos(ang) / math.sqrt(FOURIER_GROUP), F32)
    sc = jnp.asarray(np.sin(ang) / math.sqrt(FOURIER_GROUP), F32)
    ng = D_MODEL // FOURIER_GROUP
    return pl.pallas_call(
        _fnet_weight_body,
        grid=(ng,),
        in_specs=[_resident((FOURIER_GROUP, FOURIER_GROUP)),
                  _resident((FOURIER_GROUP, FOURIER_GROUP)),
                  pl.BlockSpec((FOURIER_GROUP, D_MODEL), lambda i: (i, 0))],
        out_specs=pl.BlockSpec((FOURIER_GROUP, 2 * D_MODEL), lambda i: (i, 0)),
        out_shape=jax.ShapeDtypeStruct((D_MODEL, 2 * D_MODEL), BF16),
        compiler_params=_params(("parallel",), 16 << 20),
        name="fnet_weights",
    )(cc, sc, w_out)


FN_HALF = SEQ // 2
FN_BLK = 256
FN_NB = FN_HALF // FN_BLK


def _fnet_tables():
    k = np.arange(FN_HALF)[:, None]
    n = np.arange(FN_HALF)[None, :]
    ang = 2.0 * np.pi * ((k * n) % SEQ) / SEQ
    scale = 1.0 / math.sqrt(SEQ)
    i = np.arange(FN_BLK)
    rev = (i[None, :] == FN_BLK - i[:, None]).astype(np.float32)
    return (jnp.asarray(np.cos(ang) * scale, BF16), jnp.asarray(np.sin(ang) * scale, BF16),
            jnp.asarray(rev, BF16))


def _fnet_body(x_ref, g_ref, cs_ref, ss_ref, w_ref, rev_ref, o_ref, e_sc, d_sc, zc_sc, zs_sc):
    g = g_ref[...]
    scale = 1.0 / math.sqrt(SEQ)
    row = lax.broadcasted_iota(jnp.int32, (FN_BLK, 1), 0)
    first = row == 0
    sign = jnp.where((row & 1) == 0, 1.0, -1.0)
    rev = rev_ref[...]
    blk = lambda a: pl.ds(a * FN_BLK, FN_BLK)
    mirror = lambda a: pl.ds(SEQ - (a + 1) * FN_BLK, FN_BLK)

    alt = jnp.zeros((1, D_MODEL), F32)
    carry = jnp.zeros((1, D_MODEL), F32)
    for a in range(FN_NB):
        lo = _rms(x_ref[blk(a), :], g)
        hi = _rms(x_ref[mirror(a), :], g)
        alt = alt + jnp.sum(sign * lo, axis=0, keepdims=True) + jnp.sum(sign * hi, axis=0, keepdims=True)
        r = jnp.dot(rev, hi.astype(BF16), preferred_element_type=F32)
        r = jnp.where(first, carry, r)
        e_sc[blk(a), :] = (lo + r).astype(BF16)
        d_sc[blk(a), :] = (lo - r).astype(BF16)
        carry = hi[0:1, :]
    h_nyq = carry * scale

    for c in range(FN_NB):
        zc = jnp.dot(cs_ref[blk(c), :], e_sc[...], preferred_element_type=F32) + sign * h_nyq
        zs = jnp.dot(ss_ref[blk(c), :], d_sc[...], preferred_element_type=F32)
        zc_sc[blk(c), :] = zc.astype(BF16)
        zs_sc[blk(c), :] = zs.astype(BF16)

    wc = w_ref[:, :D_MODEL]
    ws = w_ref[:, D_MODEL:]
    z_nyq = jnp.broadcast_to(alt * scale, (8, D_MODEL)).astype(BF16)
    carry = jnp.dot(z_nyq, wc, preferred_element_type=F32)[0:1, :]
    def products(c):
        return (jnp.dot(zc_sc[blk(c), :], wc, preferred_element_type=F32),
                jnp.dot(zs_sc[blk(c), :], ws, preferred_element_type=F32))

    ahead = products(FN_NB - 1)
    for c in reversed(range(FN_NB)):
        p, q = ahead
        if c > 0:
            ahead = products(c - 1)
        o_ref[blk(c), :] = x_ref[blk(c), :] + (p - q)
        m = p + q
        m_hi = m.astype(BF16)
        m_lo = (m - m_hi.astype(F32)).astype(BF16)
        r = (jnp.dot(rev, m_hi, preferred_element_type=F32) + jnp.dot(rev, m_lo, preferred_element_type=F32))
        r = jnp.where(first, carry, r)
        o_ref[mirror(c), :] = x_ref[mirror(c), :] + r
        carry = m[0:1, :]


def _fnet_layer(x, g, w_out):
    wcs = _fnet_weights(w_out)
    cs, ss, rev = _fnet_tables()
    seq_block = pl.BlockSpec((None, SEQ, D_MODEL), lambda b: (b, 0, 0))
    half = pltpu.VMEM((FN_HALF, D_MODEL), BF16)
    vmem = 4 * SEQ * D_MODEL * 4 + 4 * FN_HALF * D_MODEL * 2 + 4 * FN_HALF * D_MODEL * 2 + (12 << 20)
    return pl.pallas_call(
        _fnet_body,
        grid=(BATCH,),
        in_specs=[seq_block, _resident((1, D_MODEL)), _resident((FN_HALF, FN_HALF)),
                  _resident((FN_HALF, FN_HALF)), _resident((D_MODEL, 2 * D_MODEL)),
                  _resident((FN_BLK, FN_BLK))],
        out_specs=seq_block,
        out_shape=jax.ShapeDtypeStruct((BATCH, SEQ, D_MODEL), F32),
        scratch_shapes=[half, half, half, half],
        compiler_params=_params(("parallel",), vmem),
        name="fnet_mix",
    )(x, g.reshape(1, D_MODEL), cs, ss, wcs, rev)


def _s5_prep_body(p_ref, bt_ref, ct_ref, strip_ref, s_ref, wc_ref, coef_ref, lag_sc):
    for gi in range(S5_TILE_GROUPS):
        _s5_prep_group(p_ref.at[gi], bt_ref.at[gi], ct_ref.at[gi], strip_ref.at[gi], s_ref.at[gi], wc_ref.at[gi],
                       coef_ref.at[gi], lag_sc.at[gi])


def _s5_prep_group(p_ref, bt_ref, ct_ref, strip_ref, s_ref, wc_ref, coef_ref, lag_sc):
    L = S5_CHUNK
    lane = lax.broadcasted_iota(jnp.int32, (1, LANES), 1)
    fwd = lane < S5_STATE
    l_re, l_im = p_ref[0:1, :], p_ref[1:2, :]
    dt = jnp.exp(p_ref[2:3, :])
    mag = jnp.exp(l_re * dt)
    lb_re, lb_im = mag * jnp.cos(l_im * dt), mag * jnp.sin(l_im * dt)
    n_re, n_im = lb_re - 1.0, lb_im
    den = l_re * l_re + l_im * l_im
    f_re, f_im = (n_re * l_re + n_im * l_im) / den, (n_im * l_re - n_re * l_im) / den
    b_re, b_im = bt_ref[0], bt_ref[1]
    bb_re, bb_im = f_re * b_re - f_im * b_im, f_re * b_im + f_im * b_re
    c_re, c_im = ct_ref[0], ct_ref[1]
    zero = jnp.zeros((1, LANES), F32)
    pw = [(jnp.ones((1, LANES), F32), zero)]
    for _ in range(L):
        pr, pi = pw[-1]
        pw.append((pr * lb_re - pi * lb_im, pr * lb_im + pi * lb_re))

    def mix(f, bk):
        return jnp.where(fwd, f[0], bk[0]), jnp.where(fwd, f[1], bk[1])

    def times(p, m_re, m_im):
        return p[0] * m_re - p[1] * m_im, p[0] * m_im + p[1] * m_re

    for t in range(L):
        rows = slice(t * S5_GROUP, (t + 1) * S5_GROUP)
        re, im = times(mix(pw[L - 1 - t], pw[t]), bb_re, bb_im)
        s_ref[rows, :LANES] = re.astype(BF16)
        s_ref[rows, LANES:] = im.astype(BF16)
        re, im = times(mix(pw[t + 1], pw[L - t]), c_re, c_im)
        wc_ref[rows, :LANES] = re.astype(BF16)
        wc_ref[rows, LANES:] = (-im).astype(BF16)
    none = (zero, zero)
    for j in range(2 * L):
        rows = slice(j * S5_GROUP, (j + 1) * S5_GROUP)
        f = pw[j - (L - 1)] if L - 1 <= j <= 2 * L - 2 else none
        bk = pw[L - 1 - j] if j <= L - 1 else none
        re, im = times(mix(f, bk), c_re, c_im)
        lag_sc[rows, :LANES] = re.astype(BF16)
        lag_sc[rows, LANES:] = (-im).astype(BF16)
    b_cat = jnp.concatenate([bb_re, bb_im], axis=1)
    b_hi = b_cat.astype(BF16)
    b_lo = (b_cat - b_hi.astype(F32)).astype(BF16)
    nt = (((1,), (1,)), ((), ()))
    lag = lag_sc[...]
    strip_ref[...] = (lax.dot_general(b_hi, lag, nt, preferred_element_type=F32)
                      + lax.dot_general(b_lo, lag, nt, preferred_element_type=F32))
    coef_ref[0:1, :] = pw[L][0]
    coef_ref[1:2, :] = pw[L][1]


def _s5_matrices(lam_re, lam_im, log_dt, b_re, b_im, c_re, c_im):
    def lanes(z):
        return jnp.concatenate([z[0], z[1]], axis=-1).astype(F32)
    params = jnp.stack([lanes(lam_re), lanes(lam_im), jnp.repeat(log_dt.T.astype(F32), S5_STATE, axis=1)], axis=1)
    params = jnp.pad(params, ((0, 0), (0, 5), (0, 0)))
    bt = jnp.stack([lanes(b_re.transpose(0, 1, 3, 2)), lanes(b_im.transpose(0, 1, 3, 2))], axis=1)
    ct = jnp.stack([lanes(c_re), lanes(c_im)], axis=1)
    tg = S5_TILE_GROUPS
    mat = pl.BlockSpec((tg, 2, S5_GROUP, LANES), lambda g: (g, 0, 0, 0))
    sq = pl.BlockSpec((tg, S5_CK, S5_CK), lambda g: (g, 0, 0))
    return pl.pallas_call(
        _s5_prep_body,
        grid=(S5_GROUPS // tg,),
        in_specs=[pl.BlockSpec((tg, 8, LANES), lambda g: (g, 0, 0)), mat, mat],
        out_specs=[pl.BlockSpec((tg, S5_GROUP, 2 * S5_CK), lambda g: (g, 0, 0)), sq, sq,
                   pl.BlockSpec((tg, 2, LANES), lambda g: (g, 0, 0))],
        out_shape=[jax.ShapeDtypeStruct((S5_GROUPS, S5_GROUP, 2 * S5_CK), F32),
                   jax.ShapeDtypeStruct((S5_GROUPS, S5_CK, S5_CK), BF16),
                   jax.ShapeDtypeStruct((S5_GROUPS, S5_CK, S5_CK), BF16),
                   jax.ShapeDtypeStruct((S5_GROUPS, 2, LANES), F32)],
        scratch_shapes=[pltpu.VMEM((S5_TILE_GROUPS, 2 * S5_CK, S5_CK), BF16)],
        compiler_params=_params(("parallel",), 24 << 20),
        name="s5_prep",
    )(params, bt, ct)


def _s5_norm_body(x_ref, g_ref, o_ref):
    h = _rms(x_ref[...], g_ref[...])
    o_ref[...] = h.reshape(o_ref.shape)


def _s5_norm(x, g, tm=512):
    cr = tm // S5_CHUNK
    return pl.pallas_call(
        _s5_norm_body,
        grid=(BATCH, SEQ // tm),
        in_specs=[pl.BlockSpec((None, tm, D_MODEL), lambda b, i: (b, i, 0)),
                  _resident((1, D_MODEL))],
        out_specs=pl.BlockSpec((cr, None, S5_CHUNK, D_MODEL), lambda b, i: (i, b, 0, 0)),
        out_shape=jax.ShapeDtypeStruct((S5_CHUNKS, BATCH, S5_CHUNK, D_MODEL), F32),
        compiler_params=_params(("parallel", "parallel"), 32 << 20),
        name="s5_norm",
    )(x, g.reshape(1, D_MODEL))


def _block_transpose8(a, lane_block):
    a = list(a)
    for d in (4, 2, 1):
        take_lo = (lane_block & d) == 0
        nxt = list(a)
        for i in range(8):
            if i & d:
                continue
            lo, hi = a[i], a[i + d]
            if 2 * S5_GROUP * d == LANES:
                both = pltpu.roll(jnp.where(take_lo, hi, lo), S5_GROUP * d, 1)
                nxt[i] = jnp.where(take_lo, lo, both)
                nxt[i + d] = jnp.where(take_lo, both, hi)
            else:
                nxt[i] = jnp.where(take_lo, lo, pltpu.roll(hi, S5_GROUP * d, 1))
                nxt[i + d] = jnp.where(take_lo, pltpu.roll(lo, LANES - S5_GROUP * d, 1), hi)
        a = nxt
    return a


def _gelu_tanh(y):
    return 0.5 * y * (1.0 + jnp.tanh(math.sqrt(2.0 / math.pi) * (y + 0.044715 * (y * y * y))))


S5_RB = 128


def _s5_body(h_ref, strip_ref, ws_ref, wc_ref, coef_ref, d_ref, o_ref, x_sc, y_sc, toep_sc):
    lane = lax.broadcasted_iota(jnp.int32, (1, LANES), 1)
    lane_block = lane // S5_GROUP
    fwd_lanes = lane < S5_STATE
    n_rb = S5_ROWS // S5_RB
    tok_rb = S5_RB * S5_CHUNK

    def relayout_in(i, carry):
        base = pl.multiple_of(i * tok_rb, tok_rb)
        rows = pl.multiple_of(i * S5_RB, S5_RB)
        for half in range(2):
            a = [pltpu.bitcast(h_ref[pl.ds(base + half * 8 + t, S5_RB, stride=S5_CHUNK), :].astype(BF16), jnp.uint32)
                 for t in range(8)]
            xt = _block_transpose8(a, lane_block)
            for gi in range(S5_TILE_GROUPS):
                x_sc[gi, pl.ds(rows, S5_RB), half * LANES:(half + 1) * LANES] = pltpu.bitcast(xt[gi], BF16)
        return carry

    lax.fori_loop(0, n_rb, relayout_in, 0)

    nt = (((1,), (1,)), ((), ()))
    for gi in range(S5_TILE_GROUPS):
        y_sc[gi] = jnp.dot(x_sc[gi], ws_ref[gi], preferred_element_type=F32)
        strip = strip_ref[gi]
        for t in range(S5_CHUNK):
            off = (S5_CHUNK - 1 - t) * S5_GROUP
            win = strip if off == 0 else pltpu.roll(strip, strip.shape[1] - off, 1)
            toep_sc[gi, t * S5_GROUP:(t + 1) * S5_GROUP, :] = win[:, :S5_CK].astype(BF16)

    coef = [(coef_ref[gi, 0:1, :], coef_ref[gi, 1:2, :]) for gi in range(S5_TILE_GROUPS)]

    def scan_step(k, st):
        rf = pl.ds(pl.multiple_of(k * BATCH, BATCH), BATCH)
        rb = pl.ds(pl.multiple_of((S5_CHUNKS - 1 - k) * BATCH, BATCH), BATCH)
        s_in = [(jnp.where(fwd_lanes, y_sc[gi, rf, :LANES], y_sc[gi, rb, :LANES]),
                 jnp.where(fwd_lanes, y_sc[gi, rf, LANES:], y_sc[gi, rb, LANES:])) for gi in range(S5_TILE_GROUPS)]
        nxt = []
        for gi in range(S5_TILE_GROUPS):
            st_re, st_im = st[2 * gi], st[2 * gi + 1]
            a_re, a_im = coef[gi]
            s_re, s_im = s_in[gi]
            y_sc[gi, rf, :S5_STATE] = st_re[:, :S5_STATE]
            y_sc[gi, rb, S5_STATE:LANES] = st_re[:, S5_STATE:]
            y_sc[gi, rf, LANES:LANES + S5_STATE] = st_im[:, :S5_STATE]
            y_sc[gi, rb, LANES + S5_STATE:] = st_im[:, S5_STATE:]
            nxt.append(a_re * st_re - a_im * st_im + s_re)
            nxt.append(a_re * st_im + a_im * st_re + s_im)
        return tuple(nxt)

    zero = jnp.zeros((BATCH, LANES), F32)
    lax.fori_loop(0, S5_CHUNKS, scan_step, (zero,) * (2 * S5_TILE_GROUPS))

    for gi in range(S5_TILE_GROUPS):
        h_in = y_sc[gi].astype(BF16)
        y_sc[gi] = (jnp.dot(x_sc[gi], toep_sc[gi], preferred_element_type=F32)
                    + lax.dot_general(h_in, wc_ref[gi], nt, preferred_element_type=F32))

    d_skip = d_ref[...]

    def relayout_out(i, carry):
        base = pl.multiple_of(i * tok_rb, tok_rb)
        rows = pl.multiple_of(i * S5_RB, S5_RB)
        for half in range(2):
            yg = [y_sc[gi, pl.ds(rows, S5_RB), half * LANES:(half + 1) * LANES] for gi in range(S5_TILE_GROUPS)]
            yt = _block_transpose8(yg, lane_block)
            for t in range(8):
                tok = pl.ds(base + half * 8 + t, S5_RB, stride=S5_CHUNK)
                o_ref[tok, :] = _gelu_tanh(yt[t] + d_skip * h_ref[tok, :])
        return carry

    lax.fori_loop(0, n_rb, relayout_out, 0)


def _s5_core(h2, strip, s_in, wc_t, coef, d_skip):
    n_tiles = S5_GROUPS // S5_TILE_GROUPS
    tok_block = pl.BlockSpec((N_TOKENS, LANES), lambda j: (0, j))
    vmem = (4 * N_TOKENS * LANES * 4 + S5_TILE_GROUPS * S5_ROWS * S5_CK * (2 + 4)
            + 4 * S5_ROWS * LANES * 4 + 2 * S5_TILE_GROUPS * S5_CK * 3 * S5_CK * 2 + (12 << 20))
    return pl.pallas_call(
        _s5_body,
        grid=(n_tiles,),
        in_specs=[tok_block,
                  pl.BlockSpec((S5_TILE_GROUPS, S5_GROUP, 2 * S5_CK), lambda j: (j, 0, 0)),
                  pl.BlockSpec((S5_TILE_GROUPS, S5_CK, S5_CK), lambda j: (j, 0, 0)),
                  pl.BlockSpec((S5_TILE_GROUPS, S5_CK, S5_CK), lambda j: (j, 0, 0)),
                  pl.BlockSpec((S5_TILE_GROUPS, 2, LANES), lambda j: (j, 0, 0)),
                  pl.BlockSpec((1, LANES), lambda j: (0, j))],
        out_specs=tok_block,
        out_shape=jax.ShapeDtypeStruct((N_TOKENS, D_MODEL), F32),
        scratch_shapes=[pltpu.VMEM((S5_TILE_GROUPS, S5_ROWS, S5_CK), BF16),
                        pltpu.VMEM((S5_TILE_GROUPS, S5_ROWS, S5_CK), F32),
                        pltpu.VMEM((S5_TILE_GROUPS, S5_CK, S5_CK), BF16)],
        compiler_params=_params(("arbitrary",), vmem),
        name="s5_core",
    )(h2, strip, s_in, wc_t, coef, d_skip.reshape(1, D_MODEL))


GLU_CHUNK = 256


def _s5_glu_body(x_ref, a_ref, w_ref, o_ref):
    a = a_ref[...].reshape(x_ref.shape).astype(BF16)
    for c in range(0, D_MODEL, GLU_CHUNK):
        val = jnp.dot(a, w_ref[:, c:c + GLU_CHUNK], preferred_element_type=F32)
        gate = jnp.dot(a, w_ref[:, D_MODEL + c:D_MODEL + c + GLU_CHUNK], preferred_element_type=F32)
        o_ref[:, c:c + GLU_CHUNK] = x_ref[:, c:c + GLU_CHUNK] + val * _sigmoid(gate)


def _s5_glu(x, act, w_glu_all, layer, tm=1024):
    cr = tm // S5_CHUNK
    vmem = 2 * D_MODEL * D_MODEL * 2 + 6 * tm * D_MODEL * 4 + 2 * tm * 2 * D_MODEL * 4 + (4 << 20)
    return pl.pallas_call(
        _s5_glu_body,
        grid=(BATCH, SEQ // tm),
        in_specs=[pl.BlockSpec((None, tm, D_MODEL), lambda b, i: (b, i, 0)),
                  pl.BlockSpec((cr, None, S5_CHUNK, D_MODEL), lambda b, i: (i, b, 0, 0)),
                  _layer_block((D_MODEL, 2 * D_MODEL), layer)],
        out_specs=pl.BlockSpec((None, tm, D_MODEL), lambda b, i: (b, i, 0)),
        out_shape=jax.ShapeDtypeStruct((BATCH, SEQ, D_MODEL), F32),
        compiler_params=_params(("parallel", "parallel"), vmem),
        name="s5_glu",
    )(x, act, w_glu_all)


def _s5_layer(x, h, lam_re, lam_im, log_dt, b_re, b_im, c_re, c_im, d_skip, w_glu_all, layer):
    strip, s_in, wc_t, coef = _s5_matrices(lam_re, lam_im, log_dt, b_re, b_im, c_re, c_im)
    act = _s5_core(h.reshape(N_TOKENS, D_MODEL), strip, s_in, wc_t, coef, d_skip)
    act = act.reshape(S5_CHUNKS, BATCH, S5_CHUNK, D_MODEL)
    return _s5_glu(x, act, w_glu_all, layer)


def _t5_bucket(rel):
    half = NUM_BUCKETS // 2
    max_exact = half // 2
    n = np.abs(rel)
    sign = (rel > 0).astype(np.int32) * half
    large = max_exact + (np.log(np.maximum(n, 1) / max_exact) / math.log(MAX_DISTANCE / max_exact)
                         * (half - max_exact)).astype(np.int32)
    large = np.minimum(large, half - 1)
    return (sign + np.where(n < max_exact, n, large)).astype(np.int32)


LOG2E = 1.4426950408889634
N_PAIRS = HEADS_PER_GROUP // 2
ATTN_PAD_ROWS = SEQ + 2 * ATTN_SIDE * max(d for _, d in DILATED_GROUPS)
QKV_GROUP_WIDTH = 3 * D_MODEL
QKV_SLAB = 4 * LANES
QKV_TILE_ROWS = 512


def _attn_bias_tables(rel_bias):
    n_off = ATTN_BK + ATTN_BQ - 1
    offs = np.arange(n_off) - (ATTN_BQ - 1) - ATTN_SIDE
    strips = []
    for gi, (_, dil) in enumerate(DILATED_GROUPS):
        onehot = jnp.asarray(_t5_bucket(offs * dil)[:, None] == np.arange(NUM_BUCKETS), F32)
        f = jnp.dot(onehot, rel_bias[:, gi * HEADS_PER_GROUP:(gi + 1) * HEADS_PER_GROUP].astype(F32),
                    precision=lax.Precision.HIGHEST).T
        strips.append(jnp.pad(f[:, ::-1], ((0, 0), (0, BIAS_STRIP - n_off))))
    strips = jnp.stack(strips).reshape(N_ATTN_GROUPS, HEADS_PER_GROUP, 1, BIAS_STRIP)
    return pl.pallas_call(
        _attn_bias_body,
        grid=(N_ATTN_GROUPS, N_PAIRS),
        in_specs=[pl.BlockSpec((None, 2, 1, BIAS_STRIP), lambda g, p: (g, p, 0, 0))],
        out_specs=pl.BlockSpec((None, None, ATTN_BK, 2 * ATTN_BQ), lambda g, p: (g, p, 0, 0)),
        out_shape=jax.ShapeDtypeStruct((N_ATTN_GROUPS, N_PAIRS, ATTN_BK, 2 * ATTN_BQ), F32),
        compiler_params=_params(("parallel", "parallel"), 16 << 20),
        name="attn_bias",
    )(strips)


BIAS_STRIP = 256


def _attn_bias_body(f_ref, o_ref):
    krow = lax.broadcasted_iota(jnp.int32, (ATTN_BK, 1), 0)
    lane = lax.broadcasted_iota(jnp.int32, (1, LANES), 1)
    head0 = lane < ATTN_BQ
    band = jnp.abs(krow - ATTN_SIDE - (lane & (ATTN_BQ - 1))) <= ATTN_SIDE
    halves = []
    for a in range(2):
        x = jnp.broadcast_to(f_ref[a], (ATTN_BK, BIAS_STRIP))
        shift = (a * ATTN_BQ - (ATTN_BK - 1)) % BIAS_STRIP
        halves.append(pltpu.roll(x, shift, 1, stride=1, stride_axis=0)[:, :LANES])
    o_ref[...] = jnp.where(band, jnp.where(head0, halves[0], halves[1]) * LOG2E, -1e30)


def _qkv_body(h_ref, w_ref, qg_ref, kg_ref, q_out, k_out, v_out, h_sc, *, n_res):
    rows = h_ref.shape[0] // n_res
    h = h_ref[...]
    if n_res > 1:
        for c in range(D_MODEL // LANES):
            h_sc[c] = h[:, c * LANES:(c + 1) * LANES].astype(F32)
        h = jnp.concatenate(
            [jnp.concatenate([h_sc[c, pl.ds(r, rows, stride=n_res), :] for r in range(n_res)], axis=0)
             for c in range(D_MODEL // LANES)], axis=1).astype(BF16)
    lane = lax.broadcasted_iota(jnp.int32, (1, LANES), 1)
    head0 = lane < HEAD_DIM

    def head_norm(t, gain):
        sq = t * t
        s0 = jnp.sum(jnp.where(head0, sq, 0.0), axis=-1, keepdims=True)
        s1 = jnp.sum(jnp.where(head0, 0.0, sq), axis=-1, keepdims=True)
        ms = jnp.where(head0, s0, s1) * (1.0 / HEAD_DIM)
        return t * lax.rsqrt(ms + EPS) * gain

    for c in range(QKV_GROUP_WIDTH // QKV_SLAB):
        z = jnp.dot(h, w_ref[:, c * QKV_SLAB:(c + 1) * QKV_SLAB].astype(BF16), preferred_element_type=F32)
        for half in range(QKV_SLAB // LANES):
            section, lo = divmod(c * QKV_SLAB + half * LANES, D_MODEL)
            zz = z[:, half * LANES:(half + 1) * LANES]
            if section == 0:
                zz = head_norm(zz, qg_ref[...]) * (HEAD_DIM ** -0.5 * LOG2E)
                qa = jnp.where(head0, zz, 0.0).astype(BF16)
                qb = jnp.where(head0, 0.0, zz).astype(BF16)
            elif section == 1:
                zz = head_norm(zz, kg_ref[...]).astype(BF16)
            else:
                zz = zz.astype(BF16)
            for r in range(n_res):
                piece = slice(r * rows, (r + 1) * rows)
                if section == 0:
                    q_out[r, 0, :, lo:lo + LANES] = qa[piece]
                    q_out[r, 1, :, lo:lo + LANES] = qb[piece]
                elif section == 1:
                    k_out[r, :, lo:lo + LANES] = zz[piece]
                else:
                    v_out[r, :, lo:lo + LANES] = zz[piece]


def _qkv_group(h, w_qkv_all, layer, gi, dil, q_gain, k_gain):
    seg = SEQ // dil
    tm = QKV_TILE_ROWS
    rows = tm // dil
    assert tm % dil == 0 and rows % 16 == 0
    gain2 = lambda gn: jnp.tile(gn.astype(F32), 2).reshape(1, LANES)
    kv_spec = pl.BlockSpec((None, dil, rows, D_MODEL), lambda b, i: (b, 0, i, 0))
    kv_shape = jax.ShapeDtypeStruct((BATCH, dil, seg, D_MODEL), BF16)
    vmem = (D_MODEL * QKV_GROUP_WIDTH * 4 + 3 * tm * D_MODEL * 4 + 2 * 4 * tm * D_MODEL * 2 + (12 << 20))
    return pl.pallas_call(
        functools.partial(_qkv_body, n_res=dil),
        grid=(BATCH, SEQ // tm),
        in_specs=[pl.BlockSpec((None, tm, D_MODEL), lambda b, i: (b, i, 0)),
                  pl.BlockSpec((None, D_MODEL, QKV_GROUP_WIDTH), lambda b, i: (layer, 0, gi),
                               pipeline_mode=pl.Buffered(1)),
                  _resident((1, LANES)), _resident((1, LANES))],
        out_specs=[pl.BlockSpec((None, dil, 2, rows, D_MODEL), lambda b, i: (b, 0, 0, i, 0)),
                   kv_spec, kv_spec],
        out_shape=[jax.ShapeDtypeStruct((BATCH, dil, 2, seg, D_MODEL), BF16), kv_shape, kv_shape],
        scratch_shapes=[pltpu.VMEM((D_MODEL // LANES, tm, LANES), F32)],
        compiler_params=_params(("parallel", "parallel"), vmem),
        name=f"qkv_dil{dil}",
    )(h, w_qkv_all, gain2(q_gain), gain2(k_gain))


ATTN_AHEAD = 5


def _attn_body(q0, k0, v0, q1, k1, v1, q2, k2, v2, bias_ref, o_ref, kp_sc, vp_sc, lt_sc, og_sc, lg_sc):
    lane = lax.broadcasted_iota(jnp.int32, (1, LANES), 1)
    head0 = lane < HEAD_DIM
    krow = lax.broadcasted_iota(jnp.int32, (ATTN_BK, 1), 0)
    zpad = jnp.zeros((ATTN_SIDE, LANES), BF16)
    tn = (((0,), (0,)), ((), ()))

    qkv_refs = ((q0, k0, v0), (q1, k1, v1), (q2, k2, v2))
    for g, (_, dil) in enumerate(DILATED_GROUPS):
        q_ref, k_ref, v_ref = qkv_refs[g]
        seg = SEQ // dil
        nb = seg // ATTN_BQ
        pad_seg = seg + 2 * ATTN_SIDE
        lt_sc[...] = jnp.zeros(lt_sc.shape, F32)
        for r in range(dil):
            base = r * pad_seg
            for src, dst in ((k_ref, kp_sc), (v_ref, vp_sc)):
                dst[base:base + ATTN_SIDE] = zpad
                dst[base + ATTN_SIDE + seg:base + pad_seg] = zpad
                dst[base + ATTN_SIDE:base + ATTN_SIDE + seg] = src[r]

        def scores(r, i):
            base = r * pad_seg
            win = slice(base + i * ATTN_BQ, base + i * ATTN_BQ + ATTN_BK)
            qrows = slice(i * ATTN_BQ, (i + 1) * ATTN_BQ)
            qm = jnp.concatenate([q_ref[r, 0, qrows, :], q_ref[r, 1, qrows, :]], axis=0)
            s = lax.dot_general(kp_sc[win], qm, (((1,), (1,)), ((), ())), preferred_element_type=F32)
            s = s + bias_ref[g]
            if i == 0 or i == nb - 1:
                ok = None
                if i == 0:
                    ok = krow >= ATTN_SIDE
                if i == nb - 1:
                    ok_hi = krow < ATTN_BK - ATTN_SIDE
                    ok = ok_hi if ok is None else jnp.logical_and(ok, ok_hi)
                s = jnp.where(ok, s, -1e30)
            return s

        blocks = [(r, i) for r in range(dil) for i in range(nb)]
        pending = [scores(*blocks[j]) for j in range(min(ATTN_AHEAD, len(blocks)))]
        for blk, (r, i) in enumerate(blocks):
            if blk + ATTN_AHEAD < len(blocks):
                pending.append(scores(*blocks[blk + ATTN_AHEAD]))
            s = pending.pop(0)
            base = r * pad_seg
            win = slice(base + i * ATTN_BQ, base + i * ATTN_BQ + ATTN_BK)
            m = jnp.max(s, axis=0, keepdims=True)
            p = jnp.exp2(s - m)
            den = jnp.sum(p, axis=0, keepdims=True)
            pn = (p * (1.0 / den)).astype(BF16)
            u = lax.dot_general(pn, vp_sc[win], tn, preferred_element_type=F32)
            rows = pl.ds(r + i * ATTN_BQ * dil, ATTN_BQ, stride=dil)
            og_sc[g, rows, :] = jnp.where(head0, u[:ATTN_BQ], u[ATTN_BQ:])
            lt_sc[blk:blk + 1, :] = m + jnp.log2(den)

        ltt = lt_sc[...].T
        for r in range(dil):
            for i in range(nb):
                c = r * nb + i
                rows = pl.ds(r + i * ATTN_BQ * dil, ATTN_BQ, stride=dil)
                lg_sc[g, rows, :] = jnp.where(head0, ltt[:ATTN_BQ, c:c + 1], ltt[ATTN_BQ:, c:c + 1])

    l0, l1, l2 = lg_sc[0], lg_sc[1], lg_sc[2]
    m = jnp.maximum(jnp.maximum(l0, l1), l2)
    w0, w1, w2 = jnp.exp2(l0 - m), jnp.exp2(l1 - m), jnp.exp2(l2 - m)
    o = (w0 * og_sc[0] + w1 * og_sc[1] + w2 * og_sc[2]) / (w0 + w1 + w2)
    o_ref[...] = o.astype(o_ref.dtype)


def _attn_core(qkv, bias_tabs):
    in_specs = []
    for (_, dil) in DILATED_GROUPS:
        seg = SEQ // dil
        in_specs.append(pl.BlockSpec((None, dil, 2, seg, LANES), lambda b, p: (b, 0, 0, 0, p)))
        in_specs.append(pl.BlockSpec((None, dil, seg, LANES), lambda b, p: (b, 0, 0, p)))
        in_specs.append(pl.BlockSpec((None, dil, seg, LANES), lambda b, p: (b, 0, 0, p)))
    in_specs.append(pl.BlockSpec((N_ATTN_GROUPS, None, ATTN_BK, 2 * ATTN_BQ), lambda b, p: (0, p, 0, 0)))
    return pl.pallas_call(
        _attn_body,
        grid=(BATCH, N_PAIRS),
        in_specs=in_specs,
        out_specs=pl.BlockSpec((None, SEQ, LANES), lambda b, p: (b, 0, p)),
        out_shape=jax.ShapeDtypeStruct((BATCH, SEQ, D_MODEL), BF16),
        scratch_shapes=[pltpu.VMEM((ATTN_PAD_ROWS, LANES), BF16),
                        pltpu.VMEM((ATTN_PAD_ROWS, LANES), BF16),
                        pltpu.VMEM((LANES, LANES), F32),
                        pltpu.VMEM((N_ATTN_GROUPS, SEQ, LANES), F32),
                        pltpu.VMEM((N_ATTN_GROUPS, SEQ, LANES), F32)],
        compiler_params=_params(("parallel", "parallel"), 40 << 20),
        name="attn_core",
    )(*qkv, bias_tabs)


def _attn_layer(h, w_qkv_all, q_gain, k_gain, rel_bias, layer):
    qkv = []
    for gi, (window, dil) in enumerate(DILATED_GROUPS):
        assert (window // 2) // dil == ATTN_SIDE and (SEQ // dil) % ATTN_BQ == 0
        qkv.extend(_qkv_group(h, w_qkv_all, layer, gi, dil, q_gain[gi], k_gain[gi]))
    return _attn_core(qkv, _attn_bias_tables(rel_bias)).reshape(N_TOKENS, D_MODEL)


def kernel(x, norm_mix_g, norm_ffn_g, fnet_w_out, s5_lambda_re, s5_lambda_im, s5_log_dt, s5_b_re, s5_b_im, s5_c_re, s5_c_im, s5_d, s5_w_glu, attn_w_qkv, attn_q_gain, attn_k_gain, attn_w_o, rel_bias, ffn_w_gate_up, ffn_w_down):
    w_gate_up, w_down, w_qkv = ffn_w_gate_up, ffn_w_down, attn_w_qkv
    w_glu, w_o = _cast_bf16(s5_w_glu), _cast_bf16(attn_w_o)
    counts = [0, 0, 0]
    h_next = None
    for i in range(DEPTH):
        kind = i % 3
        j = counts[kind]
        counts[kind] += 1
        proj = None
        if kind == 0:
            x = _fnet_layer(x, norm_mix_g[i], fnet_w_out[j])
        elif kind == 1:
            h = h_next if h_next is not None else _s5_norm(x, norm_mix_g[i])
            x = _s5_layer(x, h, s5_lambda_re[j], s5_lambda_im[j], s5_log_dt[j], s5_b_re[j],
                          s5_b_im[j], s5_c_re[j], s5_c_im[j], s5_d[j], w_glu, j)
        else:
            assert h_next is not None, "an attention layer takes its normalised input from the previous FFN call"
            h = h_next.reshape(BATCH, SEQ, D_MODEL)
            proj = (_attn_layer(h, w_qkv, attn_q_gain[j], attn_k_gain[j], rel_bias, j), w_o, j)
        next_kind = (i + 1) % 3 if i + 1 < DEPTH else 0
        out = _ffn(x.reshape(N_TOKENS, D_MODEL), norm_ffn_g[i], w_gate_up, w_down, i, proj=proj,
                   norm_out_g=norm_mix_g[i + 1] if next_kind else None, norm_out_chunks=next_kind == 1)
        x, h_next = out if next_kind else (out, None)
        x = x.reshape(BATCH, SEQ, D_MODEL)
    return x
```

```python
import functools
import math

import numpy as np
import jax
import jax.numpy as jnp
from jax import lax
from jax.experimental import pallas as pl
from jax.experimental.pallas import tpu as pltpu

F32 = jnp.float32
BF16 = jnp.bfloat16

D_MODEL = 1024
BATCH = 8
SEQ = 2048
DEPTH = 4
N_TOKENS = BATCH * SEQ
EPS = 1e-6
D_FF = 2816
FOURIER_GROUP = 128
S5_GROUP = 16
S5_GROUPS = 64
S5_STATE = 64
HEAD_DIM = 64
HEADS_PER_GROUP = 16
DILATED_GROUPS = ((128, 1), (512, 4), (2048, 16))
N_ATTN_GROUPS = 3
NUM_BUCKETS = 32
MAX_DISTANCE = 1024
ATTN_SIDE = 64

LANES = 128
VMEM_LIMIT_CAP = 60 * 1024 * 1024

S5_CHUNK = 16
S5_CHUNKS = SEQ // S5_CHUNK
S5_ROWS = BATCH * S5_CHUNKS
S5_TILE_GROUPS = LANES // S5_GROUP
S5_CK = S5_CHUNK * S5_GROUP

ATTN_BQ = 64
ATTN_BK = ATTN_BQ + 2 * ATTN_SIDE


def _params(sem, vmem_bytes):
    return pltpu.CompilerParams(dimension_semantics=sem,
                                vmem_limit_bytes=int(min(VMEM_LIMIT_CAP, vmem_bytes)))


def _rms(x, g):
    ms = jnp.mean(x * x, axis=-1, keepdims=True)
    return x * lax.rsqrt(ms + EPS) * g


def _sigmoid(x):
    return 1.0 / (1.0 + jnp.exp(-x))


def _resident(shape):
    nd = len(shape)
    return pl.BlockSpec(shape, lambda *_: (0,) * nd, pipeline_mode=pl.Buffered(1))


def _cast_body(w_ref, o_ref):
    o_ref[...] = w_ref[...].astype(BF16)


def _cast_bf16(w, rows=256):
    n_layers, n_rows, n_cols = w.shape
    block = pl.BlockSpec((None, rows, n_cols), lambda l, i: (l, i, 0))
    return pl.pallas_call(
        _cast_body,
        grid=(n_layers, n_rows // rows),
        in_specs=[block],
        out_specs=block,
        out_shape=jax.ShapeDtypeStruct(w.shape, BF16),
        compiler_params=_params(("parallel", "parallel"), 4 * rows * n_cols * 6 + (4 << 20)),
        name="cast_bf16",
    )(w)


FFN_CHUNK = 256


def _ffn_body(*refs, has_proj, has_norm_out):
    refs = list(refs)
    x_ref, g_ref, wgu_ref, wd_ref = refs[:4]
    a_ref, wp_ref = refs[4:6] if has_proj else (None, None)
    gn_ref = refs[4 + 2 * has_proj] if has_norm_out else None
    o_ref = refs[4 + 2 * has_proj + has_norm_out]
    x = x_ref[...]
    if has_proj:
        x = x + jnp.dot(a_ref[...], wp_ref[...], preferred_element_type=F32)
    h = _rms(x, g_ref[...]).astype(BF16)
    acc = x
    for c in range(0, D_FF, FFN_CHUNK):
        w = min(FFN_CHUNK, D_FF - c)
        gate = jnp.dot(h, wgu_ref[:, c:c + w].astype(BF16), preferred_element_type=F32)
        up = jnp.dot(h, wgu_ref[:, D_FF + c:D_FF + c + w].astype(BF16), preferred_element_type=F32)
        a = (gate * _sigmoid(gate) * up).astype(BF16)
        acc = acc + jnp.dot(a, wd_ref[c:c + w, :].astype(BF16), preferred_element_type=F32)
    o_ref[...] = acc
    if has_norm_out:
        hn_ref = refs[-1]
        hn_ref[...] = _rms(acc, gn_ref[...]).reshape(hn_ref.shape).astype(hn_ref.dtype)


def _layer_block(shape, layer):
    return pl.BlockSpec((None,) + shape, lambda *_: (layer,) + (0,) * len(shape), pipeline_mode=pl.Buffered(1))


def _ffn(x2, g, wgu_all, wd_all, layer, proj=None, norm_out_g=None, norm_out_chunks=False, tm=512):
    m = x2.shape[0]
    tile = pl.BlockSpec((tm, D_MODEL), lambda i: (i, 0))
    in_specs = [tile, _resident((1, D_MODEL)), _layer_block((D_MODEL, 2 * D_FF), layer),
                _layer_block((D_FF, D_MODEL), layer)]
    args = [x2, g.reshape(1, D_MODEL), wgu_all, wd_all]
    out_specs, out_shape = [tile], [jax.ShapeDtypeStruct((m, D_MODEL), F32)]
    if proj is not None:
        in_specs += [tile, _layer_block((D_MODEL, D_MODEL), proj[2])]
        args += [proj[0], proj[1]]
    if norm_out_g is not None:
        tiles_per_seq = SEQ // tm
        in_specs.append(_resident((1, D_MODEL)))
        args.append(norm_out_g.reshape(1, D_MODEL))
        if norm_out_chunks:
            out_specs.append(pl.BlockSpec((tm // S5_CHUNK, None, S5_CHUNK, D_MODEL),
                                          lambda i: (i % tiles_per_seq, i // tiles_per_seq, 0, 0)))
            out_shape.append(jax.ShapeDtypeStruct((S5_CHUNKS, BATCH, S5_CHUNK, D_MODEL), F32))
        else:
            out_specs.append(tile)
            out_shape.append(jax.ShapeDtypeStruct((m, D_MODEL), BF16))
    vmem = 3 * D_MODEL * D_FF * 4 + 10 * tm * D_MODEL * 4 + (10 << 20)
    outs = pl.pallas_call(
        functools.partial(_ffn_body, has_proj=proj is not None, has_norm_out=norm_out_g is not None),
        grid=(m // tm,),
        in_specs=in_specs,
        out_specs=out_specs,
        out_shape=out_shape,
        compiler_params=_params(("parallel",), vmem),
        name="ffn",
    )(*args)
    return outs if norm_out_g is not None else outs[0]


def _fnet_weight_body(cc_ref, sc_ref, w_ref, o_ref):
    w = w_ref[...]
    o_ref[:, :D_MODEL] = jnp.dot(cc_ref[...], w, preferred_element_type=F32,
                                 precision=lax.Precision.HIGHEST).astype(BF16)
    o_ref[:, D_MODEL:] = jnp.dot(sc_ref[...], w, preferred_element_type=F32,
                                 precision=lax.Precision.HIGHEST).astype(BF16)


def _fnet_weights(w_out):
    n = np.arange(FOURIER_GROUP)
    ang = 2.0 * np.pi * ((n[:, None] * n[None, :]) % FOURIER_GROUP) / FOURIER_GROUP
    cc = jnp.asarray(np.cos(ang) / math.sqrt(FOURIER_GROUP), F32)
    sc = jnp.asarray(np.sin(ang) / math.sqrt(FOURIER_GROUP), F32)
    ng = D_MODEL // FOURIER_GROUP
    return pl.pallas_call(
        _fnet_weight_body,
        grid=(ng,),
        in_specs=[_resident((FOURIER_GROUP, FOURIER_GROUP)),
                  _resident((FOURIER_GROUP, FOURIER_GROUP)),
                  pl.BlockSpec((FOURIER_GROUP, D_MODEL), lambda i: (i, 0))],
        out_specs=pl.BlockSpec((FOURIER_GROUP, 2 * D_MODEL), lambda i: (i, 0)),
        out_shape=jax.ShapeDtypeStruct((D_MODEL, 2 * D_MODEL), BF16),
        compiler_params=_params(("parallel",), 16 << 20),
        name="fnet_weights",
    )(cc, sc, w_out)


FN_HALF = SEQ // 2
FN_BLK = 256
FN_NB = FN_HALF // FN_BLK


def _fnet_tables():
    k = np.arange(FN_HALF)[:, None]
    n = np.arange(FN_HALF)[None, :]
    ang = 2.0 * np.pi * ((k * n) % SEQ) / SEQ
    scale = 1.0 / math.sqrt(SEQ)
    i = np.arange(FN_BLK)
    rev = (i[None, :] == FN_BLK - i[:, None]).astype(np.float32)
    return (jnp.asarray(np.cos(ang) * scale, BF16), jnp.asarray(np.sin(ang) * scale, BF16),
            jnp.asarray(rev, BF16))


def _fnet_body(x_ref, g_ref, cs_ref, ss_ref, w_ref, rev_ref, o_ref, e_sc, d_sc, zc_sc, zs_sc):
    g = g_ref[...]
    scale = 1.0 / math.sqrt(SEQ)
    row = lax.broadcasted_iota(jnp.int32, (FN_BLK, 1), 0)
    first = row == 0
    sign = jnp.where((row & 1) == 0, 1.0, -1.0)
    rev = rev_ref[...]
    blk = lambda a: pl.ds(a * FN_BLK, FN_BLK)
    mirror = lambda a: pl.ds(SEQ - (a + 1) * FN_BLK, FN_BLK)

    alt = jnp.zeros((1, D_MODEL), F32)
    carry = jnp.zeros((1, D_MODEL), F32)
    for a in range(FN_NB):
        lo = _rms(x_ref[blk(a), :], g)
        hi = _rms(x_ref[mirror(a), :], g)
        alt = alt + jnp.sum(sign * lo, axis=0, keepdims=True) + jnp.sum(sign * hi, axis=0, keepdims=True)
        r = jnp.dot(rev, hi.astype(BF16), preferred_element_type=F32)
        r = jnp.where(first, carry, r)
        e_sc[blk(a), :] = (lo + r).astype(BF16)
        d_sc[blk(a), :] = (lo - r).astype(BF16)
        carry = hi[0:1, :]
    h_nyq = carry * scale

    for c in range(FN_NB):
        zc = jnp.dot(cs_ref[blk(c), :], e_sc[...], preferred_element_type=F32) + sign * h_nyq
        zs = jnp.dot(ss_ref[blk(c), :], d_sc[...], preferred_element_type=F32)
        zc_sc[blk(c), :] = zc.astype(BF16)
        zs_sc[blk(c), :] = zs.astype(BF16)

    wc = w_ref[:, :D_MODEL]
    ws = w_ref[:, D_MODEL:]
    z_nyq = jnp.broadcast_to(alt * scale, (8, D_MODEL)).astype(BF16)
    carry = jnp.dot(z_nyq, wc, preferred_element_type=F32)[0:1, :]
    def products(c):
        return (jnp.dot(zc_sc[blk(c), :], wc, preferred_element_type=F32),
                jnp.dot(zs_sc[blk(c), :], ws, preferred_element_type=F32))

    ahead = products(FN_NB - 1)
    for c in reversed(range(FN_NB)):
        p, q = ahead
        if c > 0:
            ahead = products(c - 1)
        o_ref[blk(c), :] = x_ref[blk(c), :] + (p - q)
        m = p + q
        m_hi = m.astype(BF16)
        m_lo = (m - m_hi.astype(F32)).astype(BF16)
        r = (jnp.dot(rev, m_hi, preferred_element_type=F32) + jnp.dot(rev, m_lo, preferred_element_type=F32))
        r = jnp.where(first, carry, r)
        o_ref[mirror(c), :] = x_ref[mirror(c), :] + r
        carry = m[0:1, :]


def _fnet_layer(x, g, w_out):
    wcs = _fnet_weights(w_out)
    cs, ss, rev = _fnet_tables()
    seq_block = pl.BlockSpec((None, SEQ, D_MODEL), lambda b: (b, 0, 0))
    half = pltpu.VMEM((FN_HALF, D_MODEL), BF16)
    vmem = 4 * SEQ * D_MODEL * 4 + 4 * FN_HALF * D_MODEL * 2 + 4 * FN_HALF * D_MODEL * 2 + (12 << 20)
    return pl.pallas_call(
        _fnet_body,
        grid=(BATCH,),
        in_specs=[seq_block, _resident((1, D_MODEL)), _resident((FN_HALF, FN_HALF)),
                  _resident((FN_HALF, FN_HALF)), _resident((D_MODEL, 2 * D_MODEL)),
                  _resident((FN_BLK, FN_BLK))],
        out_specs=seq_block,
        out_shape=jax.ShapeDtypeStruct((BATCH, SEQ, D_MODEL), F32),
        scratch_shapes=[half, half, half, half],
        compiler_params=_params(("parallel",), vmem),
        name="fnet_mix",
    )(x, g.reshape(1, D_MODEL), cs, ss, wcs, rev)


def _s5_prep_body(p_ref, bt_ref, ct_ref, strip_ref, s_ref, wc_ref, coef_ref, lag_sc):
    for gi in range(S5_TILE_GROUPS):
        _s5_prep_group(p_ref.at[gi], bt_ref.at[gi], ct_ref.at[gi], strip_ref.at[gi], s_ref.at[gi], wc_ref.at[gi],
                       coef_ref.at[gi], lag_sc.at[gi])


def _s5_prep_group(p_ref, bt_ref, ct_ref, strip_ref, s_ref, wc_ref, coef_ref, lag_sc):
    L = S5_CHUNK
    lane = lax.broadcasted_iota(jnp.int32, (1, LANES), 1)
    fwd = lane < S5_STATE
    l_re, l_im = p_ref[0:1, :], p_ref[1:2, :]
    dt = jnp.exp(p_ref[2:3, :])
    mag = jnp.exp(l_re * dt)
    lb_re, lb_im = mag * jnp.cos(l_im * dt), mag * jnp.sin(l_im * dt)
    n_re, n_im = lb_re - 1.0, lb_im
    den = l_re * l_re + l_im * l_im
    f_re, f_im = (n_re * l_re + n_im * l_im) / den, (n_im * l_re - n_re * l_im) / den
    b_re, b_im = bt_ref[0], bt_ref[1]
    bb_re, bb_im = f_re * b_re - f_im * b_im, f_re * b_im + f_im * b_re
    c_re, c_im = ct_ref[0], ct_ref[1]
    zero = jnp.zeros((1, LANES), F32)
    pw = [(jnp.ones((1, LANES), F32), zero)]
    for _ in range(L):
        pr, pi = pw[-1]
        pw.append((pr * lb_re - pi * lb_im, pr * lb_im + pi * lb_re))

    def mix(f, bk):
        return jnp.where(fwd, f[0], bk[0]), jnp.where(fwd, f[1], bk[1])

    def times(p, m_re, m_im):
        return p[0] * m_re - p[1] * m_im, p[0] * m_im + p[1] * m_re

    for t in range(L):
        rows = slice(t * S5_GROUP, (t + 1) * S5_GROUP)
        re, im = times(mix(pw[L - 1 - t], pw[t]), bb_re, bb_im)
        s_ref[rows, :LANES] = re.astype(BF16)
        s_ref[rows, LANES:] = im.astype(BF16)
        re, im = times(mix(pw[t + 1], pw[L - t]), c_re, c_im)
        wc_ref[rows, :LANES] = re.astype(BF16)
        wc_ref[rows, LANES:] = (-im).astype(BF16)
    none = (zero, zero)
    for j in range(2 * L):
        rows = slice(j * S5_GROUP, (j + 1) * S5_GROUP)
        f = pw[j - (L - 1)] if L - 1 <= j <= 2 * L - 2 else none
        bk = pw[L - 1 - j] if j <= L - 1 else none
        re, im = times(mix(f, bk), c_re, c_im)
        lag_sc[rows, :LANES] = re.astype(BF16)
        lag_sc[rows, LANES:] = (-im).astype(BF16)
    b_cat = jnp.concatenate([bb_re, bb_im], axis=1)
    b_hi = b_cat.astype(BF16)
    b_lo = (b_cat - b_hi.astype(F32)).astype(BF16)
    nt = (((1,), (1,)), ((), ()))
    lag = lag_sc[...]
    strip_ref[...] = (lax.dot_general(b_hi, lag, nt, preferred_element_type=F32)
                      + lax.dot_general(b_lo, lag, nt, preferred_element_type=F32))
    coef_ref[0:1, :] = pw[L][0]
    coef_ref[1:2, :] = pw[L][1]


def _s5_matrices(lam_re, lam_im, log_dt, b_re, b_im, c_re, c_im):
    def lanes(z):
        return jnp.concatenate([z[0], z[1]], axis=-1).astype(F32)
    params = jnp.stack([lanes(lam_re), lanes(lam_im), jnp.repeat(log_dt.T.astype(F32), S5_STATE, axis=1)], axis=1)
    params = jnp.pad(params, ((0, 0), (0, 5), (0, 0)))
    bt = jnp.stack([lanes(b_re.transpose(0, 1, 3, 2)), lanes(b_im.transpose(0, 1, 3, 2))], axis=1)
    ct = jnp.stack([lanes(c_re), lanes(c_im)], axis=1)
    tg = S5_TILE_GROUPS
    mat = pl.BlockSpec((tg, 2, S5_GROUP, LANES), lambda g: (g, 0, 0, 0))
    sq = pl.BlockSpec((tg, S5_CK, S5_CK), lambda g: (g, 0, 0))
    return pl.pallas_call(
        _s5_prep_body,
        grid=(S5_GROUPS // tg,),
        in_specs=[pl.BlockSpec((tg, 8, LANES), lambda g: (g, 0, 0)), mat, mat],
        out_specs=[pl.BlockSpec((tg, S5_GROUP, 2 * S5_CK), lambda g: (g, 0, 0)), sq, sq,
                   pl.BlockSpec((tg, 2, LANES), lambda g: (g, 0, 0))],
        out_shape=[jax.ShapeDtypeStruct((S5_GROUPS, S5_GROUP, 2 * S5_CK), F32),
                   jax.ShapeDtypeStruct((S5_GROUPS, S5_CK, S5_CK), BF16),
                   jax.ShapeDtypeStruct((S5_GROUPS, S5_CK, S5_CK), BF16),
                   jax.ShapeDtypeStruct((S5_GROUPS, 2, LANES), F32)],
        scratch_shapes=[pltpu.VMEM((S5_TILE_GROUPS, 2 * S5_CK, S5_CK), BF16)],
        compiler_params=_params(("parallel",), 24 << 20),
        name="s5_prep",
    )(params, bt, ct)


def _s5_norm_body(x_ref, g_ref, o_ref):
    h = _rms(x_ref[...], g_ref[...])
    o_ref[...] = h.reshape(o_ref.shape)


def _s5_norm(x, g, tm=512):
    cr = tm // S5_CHUNK
    return pl.pallas_call(
        _s5_norm_body,
        grid=(BATCH, SEQ // tm),
        in_specs=[pl.BlockSpec((None, tm, D_MODEL), lambda b, i: (b, i, 0)),
                  _resident((1, D_MODEL))],
        out_specs=pl.BlockSpec((cr, None, S5_CHUNK, D_MODEL), lambda b, i: (i, b, 0, 0)),
        out_shape=jax.ShapeDtypeStruct((S5_CHUNKS, BATCH, S5_CHUNK, D_MODEL), F32),
        compiler_params=_params(("parallel", "parallel"), 32 << 20),
        name="s5_norm",
    )(x, g.reshape(1, D_MODEL))


def _block_transpose8(a, lane_block):
    a = list(a)
    for d in (4, 2, 1):
        take_lo = (lane_block & d) == 0
        nxt = list(a)
        for i in range(8):
            if i & d:
                continue
            lo, hi = a[i], a[i + d]
            if 2 * S5_GROUP * d == LANES:
                both = pltpu.roll(jnp.where(take_lo, hi, lo), S5_GROUP * d, 1)
                nxt[i] = jnp.where(take_lo, lo, both)
                nxt[i + d] = jnp.where(take_lo, both, hi)
            else:
                nxt[i] = jnp.where(take_lo, lo, pltpu.roll(hi, S5_GROUP * d, 1))
                nxt[i + d] = jnp.where(take_lo, pltpu.roll(lo, LANES - S5_GROUP * d, 1), hi)
        a = nxt
    return a


def _gelu_tanh(y):
    return 0.5 * y * (1.0 + jnp.tanh(math.sqrt(2.0 / math.pi) * (y + 0.044715 * (y * y * y))))


S5_RB = 128


def _s5_body(h_ref, strip_ref, ws_ref, wc_ref, coef_ref, d_ref, o_ref, x_sc, y_sc, toep_sc):
    lane = lax.broadcasted_iota(jnp.int32, (1, LANES), 1)
    lane_block = lane // S5_GROUP
    fwd_lanes = lane < S5_STATE
    n_rb = S5_ROWS // S5_RB
    tok_rb = S5_RB * S5_CHUNK

    def relayout_in(i, carry):
        base = pl.multiple_of(i * tok_rb, tok_rb)
        rows = pl.multiple_of(i * S5_RB, S5_RB)
        for half in range(2):
            a = [pltpu.bitcast(h_ref[pl.ds(base + half * 8 + t, S5_RB, stride=S5_CHUNK), :].astype(BF16), jnp.uint32)
                 for t in range(8)]
            xt = _block_transpose8(a, lane_block)
            for gi in range(S5_TILE_GROUPS):
                x_sc[gi, pl.ds(rows, S5_RB), half * LANES:(half + 1) * LANES] = pltpu.bitcast(xt[gi], BF16)
        return carry

    lax.fori_loop(0, n_rb, relayout_in, 0)

    nt = (((1,), (1,)), ((), ()))
    for gi in range(S5_TILE_GROUPS):
        y_sc[gi] = jnp.dot(x_sc[gi], ws_ref[gi], preferred_element_type=F32)
        strip = strip_ref[gi]
        for t in range(S5_CHUNK):
            off = (S5_CHUNK - 1 - t) * S5_GROUP
            win = strip if off == 0 else pltpu.roll(strip, strip.shape[1] - off, 1)
            toep_sc[gi, t * S5_GROUP:(t + 1) * S5_GROUP, :] = win[:, :S5_CK].astype(BF16)

    coef = [(coef_ref[gi, 0:1, :], coef_ref[gi, 1:2, :]) for gi in range(S5_TILE_GROUPS)]

    def scan_step(k, st):
        rf = pl.ds(pl.multiple_of(k * BATCH, BATCH), BATCH)
        rb = pl.ds(pl.multiple_of((S5_CHUNKS - 1 - k) * BATCH, BATCH), BATCH)
        s_in = [(jnp.where(fwd_lanes, y_sc[gi, rf, :LANES], y_sc[gi, rb, :LANES]),
                 jnp.where(fwd_lanes, y_sc[gi, rf, LANES:], y_sc[gi, rb, LANES:])) for gi in range(S5_TILE_GROUPS)]
        nxt = []
        for gi in range(S5_TILE_GROUPS):
            st_re, st_im = st[2 * gi], st[2 * gi + 1]
            a_re, a_im = coef[gi]
            s_re, s_im = s_in[gi]
            y_sc[gi, rf, :S5_STATE] = st_re[:, :S5_STATE]
            y_sc[gi, rb, S5_STATE:LANES] = st_re[:, S5_STATE:]
            y_sc[gi, rf, LANES:LANES + S5_STATE] = st_im[:, :S5_STATE]
            y_sc[gi, rb, LANES + S5_STATE:] = st_im[:, S5_STATE:]
            nxt.append(a_re * st_re - a_im * st_im + s_re)
            nxt.append(a_re * st_im + a_im * st_re + s_im)
        return tuple(nxt)

    zero = jnp.zeros((BATCH, LANES), F32)
    lax.fori_loop(0, S5_CHUNKS, scan_step, (zero,) * (2 * S5_TILE_GROUPS))

    for gi in range(S5_TILE_GROUPS):
        h_in = y_sc[gi].astype(BF16)
        y_sc[gi] = (jnp.dot(x_sc[gi], toep_sc[gi], preferred_element_type=F32)
                    + lax.dot_general(h_in, wc_ref[gi], nt, preferred_element_type=F32))

    d_skip = d_ref[...]

    def relayout_out(i, carry):
        base = pl.multiple_of(i * tok_rb, tok_rb)
        rows = pl.multiple_of(i * S5_RB, S5_RB)
        for half in range(2):
            yg = [y_sc[gi, pl.ds(rows, S5_RB), half * LANES:(half + 1) * LANES] for gi in range(S5_TILE_GROUPS)]
            yt = _block_transpose8(yg, lane_block)
            for t in range(8):
                tok = pl.ds(base + half * 8 + t, S5_RB, stride=S5_CHUNK)
                o_ref[tok, :] = _gelu_tanh(yt[t] + d_skip * h_ref[tok, :])
        return carry

    lax.fori_loop(0, n_rb, relayout_out, 0)


def _s5_core(h2, strip, s_in, wc_t, coef, d_skip):
    n_tiles = S5_GROUPS // S5_TILE_GROUPS
    tok_block = pl.BlockSpec((N_TOKENS, LANES), lambda j: (0, j))
    vmem = (4 * N_TOKENS * LANES * 4 + S5_TILE_GROUPS * S5_ROWS * S5_CK * (2 + 4)
            + 4 * S5_ROWS * LANES * 4 + 2 * S5_TILE_GROUPS * S5_CK * 3 * S5_CK * 2 + (12 << 20))
    return pl.pallas_call(
        _s5_body,
        grid=(n_tiles,),
        in_specs=[tok_block,
                  pl.BlockSpec((S5_TILE_GROUPS, S5_GROUP, 2 * S5_CK), lambda j: (j, 0, 0)),
                  pl.BlockSpec((S5_TILE_GROUPS, S5_CK, S5_CK), lambda j: (j, 0, 0)),
                  pl.BlockSpec((S5_TILE_GROUPS, S5_CK, S5_CK), lambda j: (j, 0, 0)),
                  pl.BlockSpec((S5_TILE_GROUPS, 2, LANES), lambda j: (j, 0, 0)),
                  pl.BlockSpec((1, LANES), lambda j: (0, j))],
        out_specs=tok_block,
        out_shape=jax.ShapeDtypeStruct((N_TOKENS, D_MODEL), F32),
        scratch_shapes=[pltpu.VMEM((S5_TILE_GROUPS, S5_ROWS, S5_CK), BF16),
                        pltpu.VMEM((S5_TILE_GROUPS, S5_ROWS, S5_CK), F32),
                        pltpu.VMEM((S5_TILE_GROUPS, S5_CK, S5_CK), BF16)],
        compiler_params=_params(("arbitrary",), vmem),
        name="s5_core",
    )(h2, strip, s_in, wc_t, coef, d_skip.reshape(1, D_MODEL))


GLU_CHUNK = 256


def _s5_glu_body(x_ref, a_ref, w_ref, o_ref):
    a = a_ref[...].reshape(x_ref.shape).astype(BF16)
    for c in range(0, D_MODEL, GLU_CHUNK):
        val = jnp.dot(a, w_ref[:, c:c + GLU_CHUNK], preferred_element_type=F32)
        gate = jnp.dot(a, w_ref[:, D_MODEL + c:D_MODEL + c + GLU_CHUNK], preferred_element_type=F32)
        o_ref[:, c:c + GLU_CHUNK] = x_ref[:, c:c + GLU_CHUNK] + val * _sigmoid(gate)


def _s5_glu(x, act, w_glu_all, layer, tm=1024):
    cr = tm // S5_CHUNK
    vmem = 2 * D_MODEL * D_MODEL * 2 + 6 * tm * D_MODEL * 4 + 2 * tm * 2 * D_MODEL * 4 + (4 << 20)
    return pl.pallas_call(
        _s5_glu_body,
        grid=(BATCH, SEQ // tm),
        in_specs=[pl.BlockSpec((None, tm, D_MODEL), lambda b, i: (b, i, 0)),
                  pl.BlockSpec((cr, None, S5_CHUNK, D_MODEL), lambda b, i: (i, b, 0, 0)),
                  _layer_block((D_MODEL, 2 * D_MODEL), layer)],
        out_specs=pl.BlockSpec((None, tm, D_MODEL), lambda b, i: (b, i, 0)),
        out_shape=jax.ShapeDtypeStruct((BATCH, SEQ, D_MODEL), F32),
        compiler_params=_params(("parallel", "parallel"), vmem),
        name="s5_glu",
    )(x, act, w_glu_all)


def _s5_layer(x, h, lam_re, lam_im, log_dt, b_re, b_im, c_re, c_im, d_skip, w_glu_all, layer):
    strip, s_in, wc_t, coef = _s5_matrices(lam_re, lam_im, log_dt, b_re, b_im, c_re, c_im)
    act = _s5_core(h.reshape(N_TOKENS, D_MODEL), strip, s_in, wc_t, coef, d_skip)
    act = act.reshape(S5_CHUNKS, BATCH, S5_CHUNK, D_MODEL)
    return _s5_glu(x, act, w_glu_all, layer)


def _t5_bucket(rel):
    half = NUM_BUCKETS // 2
    max_exact = half // 2
    n = np.abs(rel)
    sign = (rel > 0).astype(np.int32) * half
    large = max_exact + (np.log(np.maximum(n, 1) / max_exact) / math.log(MAX_DISTANCE / max_exact)
                         * (half - max_exact)).astype(np.int32)
    large = np.minimum(large, half - 1)
    return (sign + np.where(n < max_exact, n, large)).astype(np.int32)


LOG2E = 1.4426950408889634
N_PAIRS = HEADS_PER_GROUP // 2
ATTN_PAD_ROWS = SEQ + 2 * ATTN_SIDE * max(d for _, d in DILATED_GROUPS)
QKV_GROUP_WIDTH = 3 * D_MODEL
QKV_SLAB = 4 * LANES
QKV_TILE_ROWS = 512
QKV_REGROUP_STRIDE = 4


def _attn_bias_tables(rel_bias):
    n_off = ATTN_BK + ATTN_BQ - 1
    offs = np.arange(n_off) - (ATTN_BQ - 1) - ATTN_SIDE
    strips = []
    for gi, (_, dil) in enumerate(DILATED_GROUPS):
        onehot = jnp.asarray(_t5_bucket(offs * dil)[:, None] == np.arange(NUM_BUCKETS), F32)
        f = jnp.dot(onehot, rel_bias[:, gi * HEADS_PER_GROUP:(gi + 1) * HEADS_PER_GROUP].astype(F32),
                    precision=lax.Precision.HIGHEST).T
        strips.append(jnp.pad(f[:, ::-1], ((0, 0), (0, BIAS_STRIP - n_off))))
    strips = jnp.stack(strips).reshape(N_ATTN_GROUPS, HEADS_PER_GROUP, 1, BIAS_STRIP)
    return pl.pallas_call(
        _attn_bias_body,
        grid=(N_ATTN_GROUPS, N_PAIRS),
        in_specs=[pl.BlockSpec((None, 2, 1, BIAS_STRIP), lambda g, p: (g, p, 0, 0))],
        out_specs=pl.BlockSpec((None, None, ATTN_BK, 2 * ATTN_BQ), lambda g, p: (g, p, 0, 0)),
        out_shape=jax.ShapeDtypeStruct((N_ATTN_GROUPS, N_PAIRS, ATTN_BK, 2 * ATTN_BQ), F32),
        compiler_params=_params(("parallel", "parallel"), 16 << 20),
        name="attn_bias",
    )(strips)


BIAS_STRIP = 256


def _attn_bias_body(f_ref, o_ref):
    krow = lax.broadcasted_iota(jnp.int32, (ATTN_BK, 1), 0)
    lane = lax.broadcasted_iota(jnp.int32, (1, LANES), 1)
    head0 = lane < ATTN_BQ
    band = jnp.abs(krow - ATTN_SIDE - (lane & (ATTN_BQ - 1))) <= ATTN_SIDE
    halves = []
    for a in range(2):
        x = jnp.broadcast_to(f_ref[a], (ATTN_BK, BIAS_STRIP))
        shift = (a * ATTN_BQ - (ATTN_BK - 1)) % BIAS_STRIP
        halves.append(pltpu.roll(x, shift, 1, stride=1, stride_axis=0)[:, :LANES])
    o_ref[...] = jnp.where(band, jnp.where(head0, halves[0], halves[1]) * LOG2E, -1e30)


def _qkv_body(h_ref, w_ref, qg_ref, kg_ref, q_out, k_out, v_out, h_sc, *, n_res):
    rows = h_ref.shape[0] // n_res
    h = h_ref[...]
    if n_res > 1:
        tm = h_ref.shape[0]
        cols = []
        for c in range(D_MODEL // LANES):
            part = h[:, c * LANES:(c + 1) * LANES].astype(F32)
            done, left = 1, n_res
            residue = [0]
            while left > 1:
                step = min(left, QKV_REGROUP_STRIDE)
                h_sc[c] = part
                seg = tm // done
                part = jnp.concatenate(
                    [h_sc[c, pl.ds(g * seg + r, seg // step, stride=step), :] for g in range(done) for r in range(step)],
                    axis=0)
                residue = [residue[g] + done * r for g in range(done) for r in range(step)]
                done, left = done * step, left // step
            cols.append(part)
        h = jnp.concatenate(cols, axis=1).astype(BF16)
    else:
        residue = [0]
    lane = lax.broadcasted_iota(jnp.int32, (1, LANES), 1)
    head0 = lane < HEAD_DIM

    def head_norm(t, gain):
        sq = t * t
        s0 = jnp.sum(jnp.where(head0, sq, 0.0), axis=-1, keepdims=True)
        s1 = jnp.sum(jnp.where(head0, 0.0, sq), axis=-1, keepdims=True)
        ms = jnp.where(head0, s0, s1) * (1.0 / HEAD_DIM)
        return t * lax.rsqrt(ms + EPS) * gain

    for c in range(QKV_GROUP_WIDTH // QKV_SLAB):
        z = jnp.dot(h, w_ref[:, c * QKV_SLAB:(c + 1) * QKV_SLAB].astype(BF16), preferred_element_type=F32)
        for half in range(QKV_SLAB // LANES):
            section, lo = divmod(c * QKV_SLAB + half * LANES, D_MODEL)
            zz = z[:, half * LANES:(half + 1) * LANES]
            if section == 0:
                zz = head_norm(zz, qg_ref[...]) * (HEAD_DIM ** -0.5 * LOG2E)
                qa = jnp.where(head0, zz, 0.0).astype(BF16)
                qb = jnp.where(head0, 0.0, zz).astype(BF16)
            elif section == 1:
                zz = head_norm(zz, kg_ref[...]).astype(BF16)
            else:
                zz = zz.astype(BF16)
            for g, r in enumerate(residue):
                piece = slice(g * rows, (g + 1) * rows)
                if section == 0:
                    q_out[r, 0, :, lo:lo + LANES] = qa[piece]
                    q_out[r, 1, :, lo:lo + LANES] = qb[piece]
                elif section == 1:
                    k_out[r, :, lo:lo + LANES] = zz[piece]
                else:
                    v_out[r, :, lo:lo + LANES] = zz[piece]


def _qkv_group(h, w_qkv_all, layer, gi, dil, q_gain, k_gain):
    seg = SEQ // dil
    tm = QKV_TILE_ROWS
    rows = tm // dil
    assert tm % dil == 0 and rows % 16 == 0
    gain2 = lambda gn: jnp.tile(gn.astype(F32), 2).reshape(1, LANES)
    kv_spec = pl.BlockSpec((None, dil, rows, D_MODEL), lambda b, i: (b, 0, i, 0))
    kv_shape = jax.ShapeDtypeStruct((BATCH, dil, seg, D_MODEL), BF16)
    vmem = (D_MODEL * QKV_GROUP_WIDTH * 4 + 3 * tm * D_MODEL * 4 + 2 * 4 * tm * D_MODEL * 2 + (12 << 20))
    return pl.pallas_call(
        functools.partial(_qkv_body, n_res=dil),
        grid=(BATCH, SEQ // tm),
        in_specs=[pl.BlockSpec((None, tm, D_MODEL), lambda b, i: (b, i, 0)),
                  pl.BlockSpec((None, D_MODEL, QKV_GROUP_WIDTH), lambda b, i: (layer, 0, gi),
                               pipeline_mode=pl.Buffered(1)),
                  _resident((1, LANES)), _resident((1, LANES))],
        out_specs=[pl.BlockSpec((None, dil, 2, rows, D_MODEL), lambda b, i: (b, 0, 0, i, 0)),
                   kv_spec, kv_spec],
        out_shape=[jax.ShapeDtypeStruct((BATCH, dil, 2, seg, D_MODEL), BF16), kv_shape, kv_shape],
        scratch_shapes=[pltpu.VMEM((D_MODEL // LANES, tm, LANES), F32)],
        compiler_params=_params(("parallel", "parallel"), vmem),
        name=f"qkv_dil{dil}",
    )(h, w_qkv_all, gain2(q_gain), gain2(k_gain))


ATTN_AHEAD = 5


def _attn_body(q0, k0, v0, q1, k1, v1, q2, k2, v2, bias_ref, o_ref, kp_sc, vp_sc, lt_sc, og_sc, lg_sc):
    lane = lax.broadcasted_iota(jnp.int32, (1, LANES), 1)
    head0 = lane < HEAD_DIM
    krow = lax.broadcasted_iota(jnp.int32, (ATTN_BK, 1), 0)
    zpad = jnp.zeros((ATTN_SIDE, LANES), BF16)
    tn = (((0,), (0,)), ((), ()))

    qkv_refs = ((q0, k0, v0), (q1, k1, v1), (q2, k2, v2))
    for g, (_, dil) in enumerate(DILATED_GROUPS):
        q_ref, k_ref, v_ref = qkv_refs[g]
        seg = SEQ // dil
        nb = seg // ATTN_BQ
        pad_seg = seg + 2 * ATTN_SIDE
        lt_sc[...] = jnp.zeros(lt_sc.shape, F32)
        for r in range(dil):
            base = r * pad_seg
            for src, dst in ((k_ref, kp_sc), (v_ref, vp_sc)):
                dst[base:base + ATTN_SIDE] = zpad
                dst[base + ATTN_SIDE + seg:base + pad_seg] = zpad
                dst[base + ATTN_SIDE:base + ATTN_SIDE + seg] = src[r]

        def scores(r, i):
            base = r * pad_seg
            win = slice(base + i * ATTN_BQ, base + i * ATTN_BQ + ATTN_BK)
            qrows = slice(i * ATTN_BQ, (i + 1) * ATTN_BQ)
            qm = jnp.concatenate([q_ref[r, 0, qrows, :], q_ref[r, 1, qrows, :]], axis=0)
            s = lax.dot_general(kp_sc[win], qm, (((1,), (1,)), ((), ())), preferred_element_type=F32)
            s = s + bias_ref[g]
            if i == 0 or i == nb - 1:
                ok = None
                if i == 0:
                    ok = krow >= ATTN_SIDE
                if i == nb - 1:
                    ok_hi = krow < ATTN_BK - ATTN_SIDE
                    ok = ok_hi if ok is None else jnp.logical_and(ok, ok_hi)
                s = jnp.where(ok, s, -1e30)
            return s

        blocks = [(r, i) for r in range(dil) for i in range(nb)]
        pending = [scores(*blocks[j]) for j in range(min(ATTN_AHEAD, len(blocks)))]
        for blk, (r, i) in enumerate(blocks):
            if blk + ATTN_AHEAD < len(blocks):
                pending.append(scores(*blocks[blk + ATTN_AHEAD]))
            s = pending.pop(0)
            base = r * pad_seg
            win = slice(base + i * ATTN_BQ, base + i * ATTN_BQ + ATTN_BK)
            m = jnp.max(s, axis=0, keepdims=True)
            p = jnp.exp2(s - m)
            den = jnp.sum(p, axis=0, keepdims=True)
            pn = (p * (1.0 / den)).astype(BF16)
            u = lax.dot_general(pn, vp_sc[win], tn, preferred_element_type=F32)
            rows = pl.ds(r + i * ATTN_BQ * dil, ATTN_BQ, stride=dil)
            og_sc[g, rows, :] = jnp.where(head0, u[:ATTN_BQ], u[ATTN_BQ:])
            lt_sc[blk:blk + 1, :] = m + jnp.log2(den)

        ltt = lt_sc[...].T
        for r in range(dil):
            for i in range(nb):
                c = r * nb + i
                rows = pl.ds(r + i * ATTN_BQ * dil, ATTN_BQ, stride=dil)
                lg_sc[g, rows, :] = jnp.where(head0, ltt[:ATTN_BQ, c:c + 1], ltt[ATTN_BQ:, c:c + 1])

    l0, l1, l2 = lg_sc[0], lg_sc[1], lg_sc[2]
    m = jnp.maximum(jnp.maximum(l0, l1), l2)
    w0, w1, w2 = jnp.exp2(l0 - m), jnp.exp2(l1 - m), jnp.exp2(l2 - m)
    o = (w0 * og_sc[0] + w1 * og_sc[1] + w2 * og_sc[2]) / (w0 + w1 + w2)
    o_ref[...] = o.astype(o_ref.dtype)


def _attn_core(qkv, bias_tabs):
    in_specs = []
    for (_, dil) in DILATED_GROUPS:
        seg = SEQ // dil
        in_specs.append(pl.BlockSpec((None, dil, 2, seg, LANES), lambda b, p: (b, 0, 0, 0, p)))
        in_specs.append(pl.BlockSpec((None, dil, seg, LANES), lambda b, p: (b, 0, 0, p)))
        in_specs.append(pl.BlockSpec((None, dil, seg, LANES), lambda b, p: (b, 0, 0, p)))
    in_specs.append(pl.BlockSpec((N_ATTN_GROUPS, None, ATTN_BK, 2 * ATTN_BQ), lambda b, p: (0, p, 0, 0)))
    return pl.pallas_call(
        _attn_body,
        grid=(BATCH, N_PAIRS),
        in_specs=in_specs,
        out_specs=pl.BlockSpec((None, SEQ, LANES), lambda b, p: (b, 0, p)),
        out_shape=jax.ShapeDtypeStruct((BATCH, SEQ, D_MODEL), BF16),
        scratch_shapes=[pltpu.VMEM((ATTN_PAD_ROWS, LANES), BF16),
                        pltpu.VMEM((ATTN_PAD_ROWS, LANES), BF16),
                        pltpu.VMEM((LANES, LANES), F32),
                        pltpu.VMEM((N_ATTN_GROUPS, SEQ, LANES), F32),
                        pltpu.VMEM((N_ATTN_GROUPS, SEQ, LANES), F32)],
        compiler_params=_params(("parallel", "parallel"), 40 << 20),
        name="attn_core",
    )(*qkv, bias_tabs)


def _attn_layer(h, w_qkv_all, q_gain, k_gain, rel_bias, layer):
    qkv = []
    for gi, (window, dil) in enumerate(DILATED_GROUPS):
        assert (window // 2) // dil == ATTN_SIDE and (SEQ // dil) % ATTN_BQ == 0
        qkv.extend(_qkv_group(h, w_qkv_all, layer, gi, dil, q_gain[gi], k_gain[gi]))
    return _attn_core(qkv, _attn_bias_tables(rel_bias)).reshape(N_TOKENS, D_MODEL)


def kernel(x, norm_mix_g, norm_ffn_g, fnet_w_out, s5_lambda_re, s5_lambda_im, s5_log_dt, s5_b_re, s5_b_im, s5_c_re, s5_c_im, s5_d, s5_w_glu, attn_w_qkv, attn_q_gain, attn_k_gain, attn_w_o, rel_bias, ffn_w_gate_up, ffn_w_down):
    w_gate_up, w_down, w_qkv = ffn_w_gate_up, ffn_w_down, attn_w_qkv
    w_glu, w_o = _cast_bf16(s5_w_glu), _cast_bf16(attn_w_o)
    counts = [0, 0, 0]
    h_next = None
    for i in range(DEPTH):
        kind = i % 3
        j = counts[kind]
        counts[kind] += 1
        proj = None
        if kind == 0:
            x = _fnet_layer(x, norm_mix_g[i], fnet_w_out[j])
        elif kind == 1:
            h = h_next if h_next is not None else _s5_norm(x, norm_mix_g[i])
            x = _s5_layer(x, h, s5_lambda_re[j], s5_lambda_im[j], s5_log_dt[j], s5_b_re[j],
                          s5_b_im[j], s5_c_re[j], s5_c_im[j], s5_d[j], w_glu, j)
        else:
            assert h_next is not None, "an attention layer takes its normalised input from the previous FFN call"
            h = h_next.reshape(BATCH, SEQ, D_MODEL)
            proj = (_attn_layer(h, w_qkv, attn_q_gain[j], attn_k_gain[j], rel_bias, j), w_o, j)
        next_kind = (i + 1) % 3 if i + 1 < DEPTH else 0
        out = _ffn(x.reshape(N_TOKENS, D_MODEL), norm_ffn_g[i], w_gate_up, w_down, i, proj=proj,
                   norm_out_g=norm_mix_g[i + 1] if next_kind else None, norm_out_chunks=next_kind == 1)
        x, h_next = out if next_kind else (out, None)
        x = x.reshape(BATCH, SEQ, D_MODEL)
    return x
```

```python
import functools
import math

import numpy as np
import jax
import jax.numpy as jnp
from jax import lax
from jax.experimental import pallas as pl
from jax.experimental.pallas import tpu as pltpu

F32 = jnp.float32
BF16 = jnp.bfloat16

D_MODEL = 1024
BATCH = 8
SEQ = 2048
DEPTH = 4
N_TOKENS = BATCH * SEQ
EPS = 1e-6
D_FF = 2816
FOURIER_GROUP = 128
S5_GROUP = 16
S5_GROUPS = 64
S5_STATE = 64
HEAD_DIM = 64
HEADS_PER_GROUP = 16
DILATED_GROUPS = ((128, 1), (512, 4), (2048, 16))
N_ATTN_GROUPS = 3
NUM_BUCKETS = 32
MAX_DISTANCE = 1024
ATTN_SIDE = 64

LANES = 128
VMEM_LIMIT_CAP = 60 * 1024 * 1024

S5_CHUNK = 16
S5_CHUNKS = SEQ // S5_CHUNK
S5_ROWS = BATCH * S5_CHUNKS
S5_TILE_GROUPS = LANES // S5_GROUP
S5_CK = S5_CHUNK * S5_GROUP

ATTN_BQ = 64
ATTN_BK = ATTN_BQ + 2 * ATTN_SIDE


def _params(sem, vmem_bytes):
    return pltpu.CompilerParams(dimension_semantics=sem,
                                vmem_limit_bytes=int(min(VMEM_LIMIT_CAP, vmem_bytes)))


def _rms(x, g):
    ms = jnp.mean(x * x, axis=-1, keepdims=True)
    return x * lax.rsqrt(ms + EPS) * g


def _sigmoid(x):
    return 1.0 / (1.0 + jnp.exp(-x))


def _resident(shape):
    nd = len(shape)
    return pl.BlockSpec(shape, lambda *_: (0,) * nd, pipeline_mode=pl.Buffered(1))


def _cast_body(w_ref, o_ref):
    o_ref[...] = w_ref[...].astype(BF16)


def _cast_bf16(w, rows=256):
    n_layers, n_rows, n_cols = w.shape
    block = pl.BlockSpec((None, rows, n_cols), lambda l, i: (l, i, 0))
    return pl.pallas_call(
        _cast_body,
        grid=(n_layers, n_rows // rows),
        in_specs=[block],
        out_specs=block,
        out_shape=jax.ShapeDtypeStruct(w.shape, BF16),
        compiler_params=_params(("parallel", "parallel"), 4 * rows * n_cols * 6 + (4 << 20)),
        name="cast_bf16",
    )(w)


FFN_CHUNK = 256


def _ffn_body(*refs, has_proj, has_norm_out):
    refs = list(refs)
    x_ref, g_ref, wgu_ref, wd_ref = refs[:4]
    a_ref, wp_ref = refs[4:6] if has_proj else (None, None)
    gn_ref = refs[4 + 2 * has_proj] if has_norm_out else None
    o_ref = refs[4 + 2 * has_proj + has_norm_out]
    x = x_ref[...]
    if has_proj:
        x = x + jnp.dot(a_ref[...], wp_ref[...], preferred_element_type=F32)
    h = _rms(x, g_ref[...]).astype(BF16)
    acc = x
    for c in range(0, D_FF, FFN_CHUNK):
        w = min(FFN_CHUNK, D_FF - c)
        gate = jnp.dot(h, wgu_ref[:, c:c + w].astype(BF16), preferred_element_type=F32)
        up = jnp.dot(h, wgu_ref[:, D_FF + c:D_FF + c + w].astype(BF16), preferred_element_type=F32)
        a = (gate * _sigmoid(gate) * up).astype(BF16)
        acc = acc + jnp.dot(a, wd_ref[c:c + w, :].astype(BF16), preferred_element_type=F32)
    o_ref[...] = acc
    if has_norm_out:
        hn_ref = refs[-1]
        hn_ref[...] = _rms(acc, gn_ref[...]).reshape(hn_ref.shape).astype(hn_ref.dtype)


def _layer_block(shape, layer):
    return pl.BlockSpec((None,) + shape, lambda *_: (layer,) + (0,) * len(shape), pipeline_mode=pl.Buffered(1))


def _ffn(x2, g, wgu_all, wd_all, layer, proj=None, norm_out_g=None, norm_out_chunks=False, tm=512):
    m = x2.shape[0]
    tile = pl.BlockSpec((tm, D_MODEL), lambda i: (i, 0))
    in_specs = [tile, _resident((1, D_MODEL)), _layer_block((D_MODEL, 2 * D_FF), layer),
                _layer_block((D_FF, D_MODEL), layer)]
    args = [x2, g.reshape(1, D_MODEL), wgu_all, wd_all]
    out_specs, out_shape = [tile], [jax.ShapeDtypeStruct((m, D_MODEL), F32)]
    if proj is not None:
        in_specs += [tile, _layer_block((D_MODEL, D_MODEL), proj[2])]
        args += [proj[0], proj[1]]
    if norm_out_g is not None:
        tiles_per_seq = SEQ // tm
        in_specs.append(_resident((1, D_MODEL)))
        args.append(norm_out_g.reshape(1, D_MODEL))
        if norm_out_chunks:
            out_specs.append(pl.BlockSpec((tm // S5_CHUNK, None, S5_CHUNK, D_MODEL),
                                          lambda i: (i % tiles_per_seq, i // tiles_per_seq, 0, 0)))
            out_shape.append(jax.ShapeDtypeStruct((S5_CHUNKS, BATCH, S5_CHUNK, D_MODEL), F32))
        else:
            out_specs.append(tile)
            out_shape.append(jax.ShapeDtypeStruct((m, D_MODEL), BF16))
    vmem = 3 * D_MODEL * D_FF * 4 + 10 * tm * D_MODEL * 4 + (10 << 20)
    outs = pl.pallas_call(
        functools.partial(_ffn_body, has_proj=proj is not None, has_norm_out=norm_out_g is not None),
        grid=(m // tm,),
        in_specs=in_specs,
        out_specs=out_specs,
        out_shape=out_shape,
        compiler_params=_params(("parallel",), vmem),
        name="ffn",
    )(*args)
    return outs if norm_out_g is not None else outs[0]


def _fnet_weight_body(cc_ref, sc_ref, w_ref, o_ref):
    w = w_ref[...]
    o_ref[:, :D_MODEL] = jnp.dot(cc_ref[...], w, preferred_element_type=F32,
                                 precision=lax.Precision.HIGHEST).astype(BF16)
    o_ref[:, D_MODEL:] = jnp.dot(sc_ref[...], w, preferred_element_type=F32,
                                 precision=lax.Precision.HIGHEST).astype(BF16)


def _fnet_weights(w_out):
    n = np.arange(FOURIER_GROUP)
    ang = 2.0 * np.pi * ((n[:, None] * n[None, :]) % FOURIER_GROUP) / FOURIER_GROUP
    cc = jnp.asarray(np.cos(ang) / math.sqrt(FOURIER_GROUP), F32)
    sc = jnp.asarray(np.sin(ang) / math.sqrt(FOURIER_GROUP), F32)
    ng = D_MODEL // FOURIER_GROUP
    return pl.pallas_call(
        _fnet_weight_body,
        grid=(ng,),
        in_specs=[_resident((FOURIER_GROUP, FOURIER_GROUP)),
                  _resident((FOURIER_GROUP, FOURIER_GROUP)),
                  pl.BlockSpec((FOURIER_GROUP, D_MODEL), lambda i: (i, 0))],
        out_specs=pl.BlockSpec((FOURIER_GROUP, 2 * D_MODEL), lambda i: (i, 0)),
        out_shape=jax.ShapeDtypeStruct((D_MODEL, 2 * D_MODEL), BF16),
        compiler_params=_params(("parallel",), 16 << 20),
        name="fnet_weights",
    )(cc, sc, w_out)


FN_HALF = SEQ // 2
FN_BLK = 256
FN_NB = FN_HALF // FN_BLK


def _fnet_tables():
    k = np.arange(FN_HALF)[:, None]
    n = np.arange(FN_HALF)[None, :]
    ang = 2.0 * np.pi * ((k * n) % SEQ) / SEQ
    scale = 1.0 / math.sqrt(SEQ)
    i = np.arange(FN_BLK)
    rev = (i[None, :] == FN_BLK - i[:, None]).astype(np.float32)
    return (jnp.asarray(np.cos(ang) * scale, BF16), jnp.asarray(np.sin(ang) * scale, BF16),
            jnp.asarray(rev, BF16))


def _fnet_body(x_ref, g_ref, cs_ref, ss_ref, w_ref, rev_ref, o_ref, e_sc, d_sc, zc_sc, zs_sc):
    g = g_ref[...]
    scale = 1.0 / math.sqrt(SEQ)
    row = lax.broadcasted_iota(jnp.int32, (FN_BLK, 1), 0)
    first = row == 0
    sign = jnp.where((row & 1) == 0, 1.0, -1.0)
    rev = rev_ref[...]
    blk = lambda a: pl.ds(a * FN_BLK, FN_BLK)
    mirror = lambda a: pl.ds(SEQ - (a + 1) * FN_BLK, FN_BLK)

    alt = jnp.zeros((1, D_MODEL), F32)
    carry = jnp.zeros((1, D_MODEL), F32)
    for a in range(FN_NB):
        lo = _rms(x_ref[blk(a), :], g)
        hi = _rms(x_ref[mirror(a), :], g)
        alt = alt + jnp.sum(sign * lo, axis=0, keepdims=True) + jnp.sum(sign * hi, axis=0, keepdims=True)
        r = jnp.dot(rev, hi.astype(BF16), preferred_element_type=F32)
        r = jnp.where(first, carry, r)
        e_sc[blk(a), :] = (lo + r).astype(BF16)
        d_sc[blk(a), :] = (lo - r).astype(BF16)
        carry = hi[0:1, :]
    h_nyq = carry * scale

    for c in range(FN_NB):
        zc = jnp.dot(cs_ref[blk(c), :], e_sc[...], preferred_element_type=F32) + sign * h_nyq
        zs = jnp.dot(ss_ref[blk(c), :], d_sc[...], preferred_element_type=F32)
        zc_sc[blk(c), :] = zc.astype(BF16)
        zs_sc[blk(c), :] = zs.astype(BF16)

    wc = w_ref[:, :D_MODEL]
    ws = w_ref[:, D_MODEL:]
    z_nyq = jnp.broadcast_to(alt * scale, (8, D_MODEL)).astype(BF16)
    carry = jnp.dot(z_nyq, wc, preferred_element_type=F32)[0:1, :]
    def products(c):
        return (jnp.dot(zc_sc[blk(c), :], wc, preferred_element_type=F32),
                jnp.dot(zs_sc[blk(c), :], ws, preferred_element_type=F32))

    ahead = products(FN_NB - 1)
    for c in reversed(range(FN_NB)):
        p, q = ahead
        if c > 0:
            ahead = products(c - 1)
        o_ref[blk(c), :] = x_ref[blk(c), :] + (p - q)
        m = p + q
        m_hi = m.astype(BF16)
        m_lo = (m - m_hi.astype(F32)).astype(BF16)
        r = (jnp.dot(rev, m_hi, preferred_element_type=F32) + jnp.dot(rev, m_lo, preferred_element_type=F32))
        r = jnp.where(first, carry, r)
        o_ref[mirror(c), :] = x_ref[mirror(c), :] + r
        carry = m[0:1, :]


def _fnet_layer(x, g, w_out):
    wcs = _fnet_weights(w_out)
    cs, ss, rev = _fnet_tables()
    seq_block = pl.BlockSpec((None, SEQ, D_MODEL), lambda b: (b, 0, 0))
    half = pltpu.VMEM((FN_HALF, D_MODEL), BF16)
    vmem = 4 * SEQ * D_MODEL * 4 + 4 * FN_HALF * D_MODEL * 2 + 4 * FN_HALF * D_MODEL * 2 + (12 << 20)
    return pl.pallas_call(
        _fnet_body,
        grid=(BATCH,),
        in_specs=[seq_block, _resident((1, D_MODEL)), _resident((FN_HALF, FN_HALF)),
                  _resident((FN_HALF, FN_HALF)), _resident((D_MODEL, 2 * D_MODEL)),
                  _resident((FN_BLK, FN_BLK))],
        out_specs=seq_block,
        out_shape=jax.ShapeDtypeStruct((BATCH, SEQ, D_MODEL), F32),
        scratch_shapes=[half, half, half, half],
        compiler_params=_params(("parallel",), vmem),
        name="fnet_mix",
    )(x, g.reshape(1, D_MODEL), cs, ss, wcs, rev)


def _s5_prep_body(p_ref, bt_ref, ct_ref, strip_ref, s_ref, wc_ref, coef_ref, lag_sc):
    for gi in range(S5_TILE_GROUPS):
        _s5_prep_group(p_ref.at[gi], bt_ref.at[gi], ct_ref.at[gi], strip_ref.at[gi], s_ref.at[gi], wc_ref.at[gi],
                       coef_ref.at[gi], lag_sc.at[gi])


def _s5_prep_group(p_ref, bt_ref, ct_ref, strip_ref, s_ref, wc_ref, coef_ref, lag_sc):
    L = S5_CHUNK
    lane = lax.broadcasted_iota(jnp.int32, (1, LANES), 1)
    fwd = lane < S5_STATE
    l_re, l_im = p_ref[0:1, :], p_ref[1:2, :]
    dt = jnp.exp(p_ref[2:3, :])
    mag = jnp.exp(l_re * dt)
    lb_re, lb_im = mag * jnp.cos(l_im * dt), mag * jnp.sin(l_im * dt)
    n_re, n_im = lb_re - 1.0, lb_im
    den = l_re * l_re + l_im * l_im
    f_re, f_im = (n_re * l_re + n_im * l_im) / den, (n_im * l_re - n_re * l_im) / den
    b_re, b_im = bt_ref[0], bt_ref[1]
    bb_re, bb_im = f_re * b_re - f_im * b_im, f_re * b_im + f_im * b_re
    c_re, c_im = ct_ref[0], ct_ref[1]
    zero = jnp.zeros((1, LANES), F32)
    pw = [(jnp.ones((1, LANES), F32), zero)]
    for _ in range(L):
        pr, pi = pw[-1]
        pw.append((pr * lb_re - pi * lb_im, pr * lb_im + pi * lb_re))

    def mix(f, bk):
        return jnp.where(fwd, f[0], bk[0]), jnp.where(fwd, f[1], bk[1])

    def times(p, m_re, m_im):
        return p[0] * m_re - p[1] * m_im, p[0] * m_im + p[1] * m_re

    for t in range(L):
        rows = slice(t * S5_GROUP, (t + 1) * S5_GROUP)
        re, im = times(mix(pw[L - 1 - t], pw[t]), bb_re, bb_im)
        s_ref[rows, :LANES] = re.astype(BF16)
        s_ref[rows, LANES:] = im.astype(BF16)
        re, im = times(mix(pw[t + 1], pw[L - t]), c_re, c_im)
        wc_ref[rows, :LANES] = re.astype(BF16)
        wc_ref[rows, LANES:] = (-im).astype(BF16)
    none = (zero, zero)
    for j in range(2 * L):
        rows = slice(j * S5_GROUP, (j + 1) * S5_GROUP)
        f = pw[j - (L - 1)] if L - 1 <= j <= 2 * L - 2 else none
        bk = pw[L - 1 - j] if j <= L - 1 else none
        re, im = times(mix(f, bk), c_re, c_im)
        lag_sc[rows, :LANES] = re.astype(BF16)
        lag_sc[rows, LANES:] = (-im).astype(BF16)
    b_cat = jnp.concatenate([bb_re, bb_im], axis=1)
    b_hi = b_cat.astype(BF16)
    b_lo = (b_cat - b_hi.astype(F32)).astype(BF16)
    nt = (((1,), (1,)), ((), ()))
    lag = lag_sc[...]
    strip_ref[...] = (lax.dot_general(b_hi, lag, nt, preferred_element_type=F32)
                      + lax.dot_general(b_lo, lag, nt, preferred_element_type=F32))
    coef_ref[0:1, :] = pw[L][0]
    coef_ref[1:2, :] = pw[L][1]


def _s5_matrices(lam_re, lam_im, log_dt, b_re, b_im, c_re, c_im):
    def lanes(z):
        return jnp.concatenate([z[0], z[1]], axis=-1).astype(F32)
    params = jnp.stack([lanes(lam_re), lanes(lam_im), jnp.repeat(log_dt.T.astype(F32), S5_STATE, axis=1)], axis=1)
    params = jnp.pad(params, ((0, 0), (0, 5), (0, 0)))
    bt = jnp.stack([lanes(b_re.transpose(0, 1, 3, 2)), lanes(b_im.transpose(0, 1, 3, 2))], axis=1)
    ct = jnp.stack([lanes(c_re), lanes(c_im)], axis=1)
    tg = S5_TILE_GROUPS
    mat = pl.BlockSpec((tg, 2, S5_GROUP, LANES), lambda g: (g, 0, 0, 0))
    sq = pl.BlockSpec((tg, S5_CK, S5_CK), lambda g: (g, 0, 0))
    return pl.pallas_call(
        _s5_prep_body,
        grid=(S5_GROUPS // tg,),
        in_specs=[pl.BlockSpec((tg, 8, LANES), lambda g: (g, 0, 0)), mat, mat],
        out_specs=[pl.BlockSpec((tg, S5_GROUP, 2 * S5_CK), lambda g: (g, 0, 0)), sq, sq,
                   pl.BlockSpec((tg, 2, LANES), lambda g: (g, 0, 0))],
        out_shape=[jax.ShapeDtypeStruct((S5_GROUPS, S5_GROUP, 2 * S5_CK), F32),
                   jax.ShapeDtypeStruct((S5_GROUPS, S5_CK, S5_CK), BF16),
                   jax.ShapeDtypeStruct((S5_GROUPS, S5_CK, S5_CK), BF16),
                   jax.ShapeDtypeStruct((S5_GROUPS, 2, LANES), F32)],
        scratch_shapes=[pltpu.VMEM((S5_TILE_GROUPS, 2 * S5_CK, S5_CK), BF16)],
        compiler_params=_params(("parallel",), 24 << 20),
        name="s5_prep",
    )(params, bt, ct)


def _s5_norm_body(x_ref, g_ref, o_ref):
    h = _rms(x_ref[...], g_ref[...])
    o_ref[...] = h.reshape(o_ref.shape)


def _s5_norm(x, g, tm=512):
    cr = tm // S5_CHUNK
    return pl.pallas_call(
        _s5_norm_body,
        grid=(BATCH, SEQ // tm),
        in_specs=[pl.BlockSpec((None, tm, D_MODEL), lambda b, i: (b, i, 0)),
                  _resident((1, D_MODEL))],
        out_specs=pl.BlockSpec((cr, None, S5_CHUNK, D_MODEL), lambda b, i: (i, b, 0, 0)),
        out_shape=jax.ShapeDtypeStruct((S5_CHUNKS, BATCH, S5_CHUNK, D_MODEL), F32),
        compiler_params=_params(("parallel", "parallel"), 32 << 20),
        name="s5_norm",
    )(x, g.reshape(1, D_MODEL))


def _block_transpose8(a, lane_block):
    a = list(a)
    for d in (4, 2, 1):
        take_lo = (lane_block & d) == 0
        nxt = list(a)
        for i in range(8):
            if i & d:
                continue
            lo, hi = a[i], a[i + d]
            if 2 * S5_GROUP * d == LANES:
                both = pltpu.roll(jnp.where(take_lo, hi, lo), S5_GROUP * d, 1)
                nxt[i] = jnp.where(take_lo, lo, both)
                nxt[i + d] = jnp.where(take_lo, both, hi)
            else:
                nxt[i] = jnp.where(take_lo, lo, pltpu.roll(hi, S5_GROUP * d, 1))
                nxt[i + d] = jnp.where(take_lo, pltpu.roll(lo, LANES - S5_GROUP * d, 1), hi)
        a = nxt
    return a


def _gelu_tanh(y):
    return 0.5 * y * (1.0 + jnp.tanh(math.sqrt(2.0 / math.pi) * (y + 0.044715 * (y * y * y))))


S5_RB = 128
S5_STAGE = 4


def _s5_body(h_ref, strip_ref, ws_ref, wc_ref, coef_ref, d_ref, o_ref, x_sc, y_sc, toep_sc, stage_sc):
    lane = lax.broadcasted_iota(jnp.int32, (1, LANES), 1)
    lane_block = lane // S5_GROUP
    fwd_lanes = lane < S5_STATE
    n_rb = S5_ROWS // S5_RB
    tok_rb = S5_RB * S5_CHUNK

    def relayout_in(i, carry):
        base = pl.multiple_of(i * tok_rb, tok_rb)
        rows = pl.multiple_of(i * S5_RB, S5_RB)
        for half in range(2):
            a = [pltpu.bitcast(h_ref[pl.ds(base + half * 8 + t, S5_RB, stride=S5_CHUNK), :].astype(BF16), jnp.uint32)
                 for t in range(8)]
            xt = _block_transpose8(a, lane_block)
            for gi in range(S5_TILE_GROUPS):
                x_sc[gi, pl.ds(rows, S5_RB), half * LANES:(half + 1) * LANES] = pltpu.bitcast(xt[gi], BF16)
        return carry

    lax.fori_loop(0, n_rb, relayout_in, 0)

    nt = (((1,), (1,)), ((), ()))
    for gi in range(S5_TILE_GROUPS):
        y_sc[gi] = jnp.dot(x_sc[gi], ws_ref[gi], preferred_element_type=F32)
        strip = strip_ref[gi]
        for t in range(S5_CHUNK):
            off = (S5_CHUNK - 1 - t) * S5_GROUP
            win = strip if off == 0 else pltpu.roll(strip, strip.shape[1] - off, 1)
            toep_sc[gi, t * S5_GROUP:(t + 1) * S5_GROUP, :] = win[:, :S5_CK].astype(BF16)

    coef = [(coef_ref[gi, 0:1, :], coef_ref[gi, 1:2, :]) for gi in range(S5_TILE_GROUPS)]

    def scan_step(k, st):
        rf = pl.ds(pl.multiple_of(k * BATCH, BATCH), BATCH)
        rb = pl.ds(pl.multiple_of((S5_CHUNKS - 1 - k) * BATCH, BATCH), BATCH)
        s_in = [(jnp.where(fwd_lanes, y_sc[gi, rf, :LANES], y_sc[gi, rb, :LANES]),
                 jnp.where(fwd_lanes, y_sc[gi, rf, LANES:], y_sc[gi, rb, LANES:])) for gi in range(S5_TILE_GROUPS)]
        nxt = []
        for gi in range(S5_TILE_GROUPS):
            st_re, st_im = st[2 * gi], st[2 * gi + 1]
            a_re, a_im = coef[gi]
            s_re, s_im = s_in[gi]
            y_sc[gi, rf, :S5_STATE] = st_re[:, :S5_STATE]
            y_sc[gi, rb, S5_STATE:LANES] = st_re[:, S5_STATE:]
            y_sc[gi, rf, LANES:LANES + S5_STATE] = st_im[:, :S5_STATE]
            y_sc[gi, rb, LANES + S5_STATE:] = st_im[:, S5_STATE:]
            nxt.append(a_re * st_re - a_im * st_im + s_re)
            nxt.append(a_re * st_im + a_im * st_re + s_im)
        return tuple(nxt)

    zero = jnp.zeros((BATCH, LANES), F32)
    lax.fori_loop(0, S5_CHUNKS, scan_step, (zero,) * (2 * S5_TILE_GROUPS))

    for gi in range(S5_TILE_GROUPS):
        h_in = y_sc[gi].astype(BF16)
        y_sc[gi] = (jnp.dot(x_sc[gi], toep_sc[gi], preferred_element_type=F32)
                    + lax.dot_general(h_in, wc_ref[gi], nt, preferred_element_type=F32))

    d_skip = d_ref[...]

    def relayout_out(i, carry):
        base = pl.multiple_of(i * tok_rb, tok_rb)
        rows = pl.multiple_of(i * S5_RB, S5_RB)
        for half in range(2):
            yg = [y_sc[gi, pl.ds(rows, S5_RB), half * LANES:(half + 1) * LANES] for gi in range(S5_TILE_GROUPS)]
            yt = _block_transpose8(yg, lane_block)
            for tl in range(8):
                t = half * 8 + tl
                tok = pl.ds(base + t, S5_RB, stride=S5_CHUNK)
                act = _gelu_tanh(yt[tl] + d_skip * h_ref[tok, :])
                stage_sc[t % S5_STAGE, pl.ds(t // S5_STAGE, S5_RB, stride=S5_STAGE), :] = act
        for u in range(S5_STAGE):
            o_ref[pl.ds(base + u, S5_RB * S5_STAGE, stride=S5_STAGE), :] = stage_sc[u]
        return carry

    lax.fori_loop(0, n_rb, relayout_out, 0)


def _s5_core(h2, strip, s_in, wc_t, coef, d_skip):
    n_tiles = S5_GROUPS // S5_TILE_GROUPS
    tok_block = pl.BlockSpec((N_TOKENS, LANES), lambda j: (0, j))
    vmem = (4 * N_TOKENS * LANES * 4 + S5_TILE_GROUPS * S5_ROWS * S5_CK * (2 + 4)
            + 4 * S5_ROWS * LANES * 4 + 2 * S5_TILE_GROUPS * S5_CK * 3 * S5_CK * 2 + (12 << 20))
    return pl.pallas_call(
        _s5_body,
        grid=(n_tiles,),
        in_specs=[tok_block,
                  pl.BlockSpec((S5_TILE_GROUPS, S5_GROUP, 2 * S5_CK), lambda j: (j, 0, 0)),
                  pl.BlockSpec((S5_TILE_GROUPS, S5_CK, S5_CK), lambda j: (j, 0, 0)),
                  pl.BlockSpec((S5_TILE_GROUPS, S5_CK, S5_CK), lambda j: (j, 0, 0)),
                  pl.BlockSpec((S5_TILE_GROUPS, 2, LANES), lambda j: (j, 0, 0)),
                  pl.BlockSpec((1, LANES), lambda j: (0, j))],
        out_specs=tok_block,
        out_shape=jax.ShapeDtypeStruct((N_TOKENS, D_MODEL), F32),
        scratch_shapes=[pltpu.VMEM((S5_TILE_GROUPS, S5_ROWS, S5_CK), BF16),
                        pltpu.VMEM((S5_TILE_GROUPS, S5_ROWS, S5_CK), F32),
                        pltpu.VMEM((S5_TILE_GROUPS, S5_CK, S5_CK), BF16),
                        pltpu.VMEM((S5_STAGE, S5_RB * S5_STAGE, LANES), F32)],
        compiler_params=_params(("arbitrary",), vmem),
        name="s5_core",
    )(h2, strip, s_in, wc_t, coef, d_skip.reshape(1, D_MODEL))


GLU_CHUNK = 256


def _s5_glu_body(x_ref, a_ref, w_ref, o_ref):
    a = a_ref[...].reshape(x_ref.shape).astype(BF16)
    for c in range(0, D_MODEL, GLU_CHUNK):
        val = jnp.dot(a, w_ref[:, c:c + GLU_CHUNK], preferred_element_type=F32)
        gate = jnp.dot(a, w_ref[:, D_MODEL + c:D_MODEL + c + GLU_CHUNK], preferred_element_type=F32)
        o_ref[:, c:c + GLU_CHUNK] = x_ref[:, c:c + GLU_CHUNK] + val * _sigmoid(gate)


def _s5_glu(x, act, w_glu_all, layer, tm=1024):
    cr = tm // S5_CHUNK
    vmem = 2 * D_MODEL * D_MODEL * 2 + 6 * tm * D_MODEL * 4 + 2 * tm * 2 * D_MODEL * 4 + (4 << 20)
    return pl.pallas_call(
        _s5_glu_body,
        grid=(BATCH, SEQ // tm),
        in_specs=[pl.BlockSpec((None, tm, D_MODEL), lambda b, i: (b, i, 0)),
                  pl.BlockSpec((cr, None, S5_CHUNK, D_MODEL), lambda b, i: (i, b, 0, 0)),
                  _layer_block((D_MODEL, 2 * D_MODEL), layer)],
        out_specs=pl.BlockSpec((None, tm, D_MODEL), lambda b, i: (b, i, 0)),
        out_shape=jax.ShapeDtypeStruct((BATCH, SEQ, D_MODEL), F32),
        compiler_params=_params(("parallel", "parallel"), vmem),
        name="s5_glu",
    )(x, act, w_glu_all)


def _s5_layer(x, h, lam_re, lam_im, log_dt, b_re, b_im, c_re, c_im, d_skip, w_glu_all, layer):
    strip, s_in, wc_t, coef = _s5_matrices(lam_re, lam_im, log_dt, b_re, b_im, c_re, c_im)
    act = _s5_core(h.reshape(N_TOKENS, D_MODEL), strip, s_in, wc_t, coef, d_skip)
    act = act.reshape(S5_CHUNKS, BATCH, S5_CHUNK, D_MODEL)
    return _s5_glu(x, act, w_glu_all, layer)


def _t5_bucket(rel):
    half = NUM_BUCKETS // 2
    max_exact = half // 2
    n = np.abs(rel)
    sign = (rel > 0).astype(np.int32) * half
    large = max_exact + (np.log(np.maximum(n, 1) / max_exact) / math.log(MAX_DISTANCE / max_exact)
                         * (half - max_exact)).astype(np.int32)
    large = np.minimum(large, half - 1)
    return (sign + np.where(n < max_exact, n, large)).astype(np.int32)


LOG2E = 1.4426950408889634
N_PAIRS = HEADS_PER_GROUP // 2
ATTN_PAD_ROWS = SEQ + 2 * ATTN_SIDE * max(d for _, d in DILATED_GROUPS)
QKV_GROUP_WIDTH = 3 * D_MODEL
QKV_SLAB = 4 * LANES
QKV_TILE_ROWS = 512
QKV_REGROUP_STRIDE = 4


def _attn_bias_tables(rel_bias):
    n_off = ATTN_BK + ATTN_BQ - 1
    offs = np.arange(n_off) - (ATTN_BQ - 1) - ATTN_SIDE
    strips = []
    for gi, (_, dil) in enumerate(DILATED_GROUPS):
        onehot = jnp.asarray(_t5_bucket(offs * dil)[:, None] == np.arange(NUM_BUCKETS), F32)
        f = jnp.dot(onehot, rel_bias[:, gi * HEADS_PER_GROUP:(gi + 1) * HEADS_PER_GROUP].astype(F32),
                    precision=lax.Precision.HIGHEST).T
        strips.append(jnp.pad(f[:, ::-1], ((0, 0), (0, BIAS_STRIP - n_off))))
    strips = jnp.stack(strips).reshape(N_ATTN_GROUPS, HEADS_PER_GROUP, 1, BIAS_STRIP)
    return pl.pallas_call(
        _attn_bias_body,
        grid=(N_ATTN_GROUPS, N_PAIRS),
        in_specs=[pl.BlockSpec((None, 2, 1, BIAS_STRIP), lambda g, p: (g, p, 0, 0))],
        out_specs=pl.BlockSpec((None, None, ATTN_BK, 2 * ATTN_BQ), lambda g, p: (g, p, 0, 0)),
        out_shape=jax.ShapeDtypeStruct((N_ATTN_GROUPS, N_PAIRS, ATTN_BK, 2 * ATTN_BQ), F32),
        compiler_params=_params(("parallel", "parallel"), 16 << 20),
        name="attn_bias",
    )(strips)


BIAS_STRIP = 256


def _attn_bias_body(f_ref, o_ref):
    krow = lax.broadcasted_iota(jnp.int32, (ATTN_BK, 1), 0)
    lane = lax.broadcasted_iota(jnp.int32, (1, LANES), 1)
    head0 = lane < ATTN_BQ
    band = jnp.abs(krow - ATTN_SIDE - (lane & (ATTN_BQ - 1))) <= ATTN_SIDE
    halves = []
    for a in range(2):
        x = jnp.broadcast_to(f_ref[a], (ATTN_BK, BIAS_STRIP))
        shift = (a * ATTN_BQ - (ATTN_BK - 1)) % BIAS_STRIP
        halves.append(pltpu.roll(x, shift, 1, stride=1, stride_axis=0)[:, :LANES])
    o_ref[...] = jnp.where(band, jnp.where(head0, halves[0], halves[1]) * LOG2E, -1e30)


def _qkv_body(h_ref, w_ref, qg_ref, kg_ref, q_out, k_out, v_out, h_sc, *, n_res):
    rows = h_ref.shape[0] // n_res
    h = h_ref[...]
    if n_res > 1:
        tm = h_ref.shape[0]
        cols = []
        for c in range(D_MODEL // LANES):
            part = h[:, c * LANES:(c + 1) * LANES].astype(F32)
            done, left = 1, n_res
            residue = [0]
            while left > 1:
                step = min(left, QKV_REGROUP_STRIDE)
                h_sc[c] = part
                seg = tm // done
                part = jnp.concatenate(
                    [h_sc[c, pl.ds(g * seg + r, seg // step, stride=step), :] for g in range(done) for r in range(step)],
                    axis=0)
                residue = [residue[g] + done * r for g in range(done) for r in range(step)]
                done, left = done * step, left // step
            cols.append(part)
        h = jnp.concatenate(cols, axis=1).astype(BF16)
    else:
        residue = [0]
    lane = lax.broadcasted_iota(jnp.int32, (1, LANES), 1)
    head0 = lane < HEAD_DIM

    def head_norm(t, gain):
        sq = t * t
        s0 = jnp.sum(jnp.where(head0, sq, 0.0), axis=-1, keepdims=True)
        s1 = jnp.sum(jnp.where(head0, 0.0, sq), axis=-1, keepdims=True)
        ms = jnp.where(head0, s0, s1) * (1.0 / HEAD_DIM)
        return t * lax.rsqrt(ms + EPS) * gain

    for c in range(QKV_GROUP_WIDTH // QKV_SLAB):
        z = jnp.dot(h, w_ref[:, c * QKV_SLAB:(c + 1) * QKV_SLAB].astype(BF16), preferred_element_type=F32)
        for half in range(QKV_SLAB // LANES):
            section, lo = divmod(c * QKV_SLAB + half * LANES, D_MODEL)
            zz = z[:, half * LANES:(half + 1) * LANES]
            if section == 0:
                zz = head_norm(zz, qg_ref[...]) * (HEAD_DIM ** -0.5 * LOG2E)
                qa = jnp.where(head0, zz, 0.0).astype(BF16)
                qb = jnp.where(head0, 0.0, zz).astype(BF16)
            elif section == 1:
                zz = head_norm(zz, kg_ref[...]).astype(BF16)
            else:
                zz = zz.astype(BF16)
            for g, r in enumerate(residue):
                piece = slice(g * rows, (g + 1) * rows)
                if section == 0:
                    q_out[r, 0, :, lo:lo + LANES] = qa[piece]
                    q_out[r, 1, :, lo:lo + LANES] = qb[piece]
                elif section == 1:
                    k_out[r, :, lo:lo + LANES] = zz[piece]
                else:
                    v_out[r, :, lo:lo + LANES] = zz[piece]


def _qkv_group(h, w_qkv_all, layer, gi, dil, q_gain, k_gain):
    seg = SEQ // dil
    tm = QKV_TILE_ROWS
    rows = tm // dil
    assert tm % dil == 0 and rows % 16 == 0
    gain2 = lambda gn: jnp.tile(gn.astype(F32), 2).reshape(1, LANES)
    kv_spec = pl.BlockSpec((None, dil, rows, D_MODEL), lambda b, i: (b, 0, i, 0))
    kv_shape = jax.ShapeDtypeStruct((BATCH, dil, seg, D_MODEL), BF16)
    vmem = (D_MODEL * QKV_GROUP_WIDTH * 4 + 3 * tm * D_MODEL * 4 + 2 * 4 * tm * D_MODEL * 2 + (12 << 20))
    return pl.pallas_call(
        functools.partial(_qkv_body, n_res=dil),
        grid=(BATCH, SEQ // tm),
        in_specs=[pl.BlockSpec((None, tm, D_MODEL), lambda b, i: (b, i, 0)),
                  pl.BlockSpec((None, D_MODEL, QKV_GROUP_WIDTH), lambda b, i: (layer, 0, gi),
                               pipeline_mode=pl.Buffered(1)),
                  _resident((1, LANES)), _resident((1, LANES))],
        out_specs=[pl.BlockSpec((None, dil, 2, rows, D_MODEL), lambda b, i: (b, 0, 0, i, 0)),
                   kv_spec, kv_spec],
        out_shape=[jax.ShapeDtypeStruct((BATCH, dil, 2, seg, D_MODEL), BF16), kv_shape, kv_shape],
        scratch_shapes=[pltpu.VMEM((D_MODEL // LANES, tm, LANES), F32)],
        compiler_params=_params(("parallel", "parallel"), vmem),
        name=f"qkv_dil{dil}",
    )(h, w_qkv_all, gain2(q_gain), gain2(k_gain))


ATTN_AHEAD = 5


def _attn_body(q0, k0, v0, q1, k1, v1, q2, k2, v2, bias_ref, o_ref, kp_sc, vp_sc, lt_sc, og_sc, lg_sc):
    lane = lax.broadcasted_iota(jnp.int32, (1, LANES), 1)
    head0 = lane < HEAD_DIM
    krow = lax.broadcasted_iota(jnp.int32, (ATTN_BK, 1), 0)
    zpad = jnp.zeros((ATTN_SIDE, LANES), BF16)
    tn = (((0,), (0,)), ((), ()))

    qkv_refs = ((q0, k0, v0), (q1, k1, v1), (q2, k2, v2))
    for g, (_, dil) in enumerate(DILATED_GROUPS):
        q_ref, k_ref, v_ref = qkv_refs[g]
        seg = SEQ // dil
        nb = seg // ATTN_BQ
        pad_seg = seg + 2 * ATTN_SIDE
        lt_sc[...] = jnp.zeros(lt_sc.shape, F32)
        for r in range(dil):
            base = r * pad_seg
            for src, dst in ((k_ref, kp_sc), (v_ref, vp_sc)):
                dst[base:base + ATTN_SIDE] = zpad
                dst[base + ATTN_SIDE + seg:base + pad_seg] = zpad
                dst[base + ATTN_SIDE:base + ATTN_SIDE + seg] = src[r]

        def scores(r, i):
            base = r * pad_seg
            win = slice(base + i * ATTN_BQ, base + i * ATTN_BQ + ATTN_BK)
            qrows = slice(i * ATTN_BQ, (i + 1) * ATTN_BQ)
            qm = jnp.concatenate([q_ref[r, 0, qrows, :], q_ref[r, 1, qrows, :]], axis=0)
            s = lax.dot_general(kp_sc[win], qm, (((1,), (1,)), ((), ())), preferred_element_type=F32)
            s = s + bias_ref[g]
            if i == 0 or i == nb - 1:
                ok = None
                if i == 0:
                    ok = krow >= ATTN_SIDE
                if i == nb - 1:
                    ok_hi = krow < ATTN_BK - ATTN_SIDE
                    ok = ok_hi if ok is None else jnp.logical_and(ok, ok_hi)
                s = jnp.where(ok, s, -1e30)
            return s

        blocks = [(r, i) for r in range(dil) for i in range(nb)]
        pending = [scores(*blocks[j]) for j in range(min(ATTN_AHEAD, len(blocks)))]
        for blk, (r, i) in enumerate(blocks):
            if blk + ATTN_AHEAD < len(blocks):
                pending.append(scores(*blocks[blk + ATTN_AHEAD]))
            s = pending.pop(0)
            base = r * pad_seg
            win = slice(base + i * ATTN_BQ, base + i * ATTN_BQ + ATTN_BK)
            m = jnp.max(s, axis=0, keepdims=True)
            p = jnp.exp2(s - m)
            den = jnp.sum(p, axis=0, keepdims=True)
            pn = (p * (1.0 / den)).astype(BF16)
            u = lax.dot_general(pn, vp_sc[win], tn, preferred_element_type=F32)
            rows = pl.ds(r + i * ATTN_BQ * dil, ATTN_BQ, stride=dil)
            og_sc[g, rows, :] = jnp.where(head0, u[:ATTN_BQ], u[ATTN_BQ:])
            lt_sc[blk:blk + 1, :] = m + jnp.log2(den)

        ltt = lt_sc[...].T
        for r in range(dil):
            for i in range(nb):
                c = r * nb + i
                rows = pl.ds(r + i * ATTN_BQ * dil, ATTN_BQ, stride=dil)
                lg_sc[g, rows, :] = jnp.where(head0, ltt[:ATTN_BQ, c:c + 1], ltt[ATTN_BQ:, c:c + 1])

    l0, l1, l2 = lg_sc[0], lg_sc[1], lg_sc[2]
    m = jnp.maximum(jnp.maximum(l0, l1), l2)
    w0, w1, w2 = jnp.exp2(l0 - m), jnp.exp2(l1 - m), jnp.exp2(l2 - m)
    o = (w0 * og_sc[0] + w1 * og_sc[1] + w2 * og_sc[2]) / (w0 + w1 + w2)
    o_ref[...] = o.astype(o_ref.dtype)


def _attn_core(qkv, bias_tabs):
    in_specs = []
    for (_, dil) in DILATED_GROUPS:
        seg = SEQ // dil
        in_specs.append(pl.BlockSpec((None, dil, 2, seg, LANES), lambda b, p: (b, 0, 0, 0, p)))
        in_specs.append(pl.BlockSpec((None, dil, seg, LANES), lambda b, p: (b, 0, 0, p)))
        in_specs.append(pl.BlockSpec((None, dil, seg, LANES), lambda b, p: (b, 0, 0, p)))
    in_specs.append(pl.BlockSpec((N_ATTN_GROUPS, None, ATTN_BK, 2 * ATTN_BQ), lambda b, p: (0, p, 0, 0)))
    return pl.pallas_call(
        _attn_body,
        grid=(BATCH, N_PAIRS),
        in_specs=in_specs,
        out_specs=pl.BlockSpec((None, SEQ, LANES), lambda b, p: (b, 0, p)),
        out_shape=jax.ShapeDtypeStruct((BATCH, SEQ, D_MODEL), BF16),
        scratch_shapes=[pltpu.VMEM((ATTN_PAD_ROWS, LANES), BF16),
                        pltpu.VMEM((ATTN_PAD_ROWS, LANES), BF16),
                        pltpu.VMEM((LANES, LANES), F32),
                        pltpu.VMEM((N_ATTN_GROUPS, SEQ, LANES), F32),
                        pltpu.VMEM((N_ATTN_GROUPS, SEQ, LANES), F32)],
        compiler_params=_params(("parallel", "parallel"), 40 << 20),
        name="attn_core",
    )(*qkv, bias_tabs)


def _attn_layer(h, w_qkv_all, q_gain, k_gain, rel_bias, layer):
    qkv = []
    for gi, (window, dil) in enumerate(DILATED_GROUPS):
        assert (window // 2) // dil == ATTN_SIDE and (SEQ // dil) % ATTN_BQ == 0
        qkv.extend(_qkv_group(h, w_qkv_all, layer, gi, dil, q_gain[gi], k_gain[gi]))
    return _attn_core(qkv, _attn_bias_tables(rel_bias)).reshape(N_TOKENS, D_MODEL)


def kernel(x, norm_mix_g, norm_ffn_g, fnet_w_out, s5_lambda_re, s5_lambda_im, s5_log_dt, s5_b_re, s5_b_im, s5_c_re, s5_c_im, s5_d, s5_w_glu, attn_w_qkv, attn_q_gain, attn_k_gain, attn_w_o, rel_bias, ffn_w_gate_up, ffn_w_down):
    w_gate_up, w_down, w_qkv = ffn_w_gate_up, ffn_w_down, attn_w_qkv
    w_glu, w_o = _cast_bf16(s5_w_glu), _cast_bf16(attn_w_o)
    counts = [0, 0, 0]
    h_next = None
    for i in range(DEPTH):
        kind = i % 3
        j = counts[kind]
        counts[kind] += 1
        proj = None
        if kind == 0:
            x = _fnet_layer(x, norm_mix_g[i], fnet_w_out[j])
        elif kind == 1:
            h = h_next if h_next is not None else _s5_norm(x, norm_mix_g[i])
            x = _s5_layer(x, h, s5_lambda_re[j], s5_lambda_im[j], s5_log_dt[j], s5_b_re[j],
                          s5_b_im[j], s5_c_re[j], s5_c_im[j], s5_d[j], w_glu, j)
        else:
            assert h_next is not None, "an attention layer takes its normalised input from the previous FFN call"
            h = h_next.reshape(BATCH, SEQ, D_MODEL)
            proj = (_attn_layer(h, w_qkv, attn_q_gain[j], attn_k_gain[j], rel_bias, j), w_o, j)
        next_kind = (i + 1) % 3 if i + 1 < DEPTH else 0
        out = _ffn(x.reshape(N_TOKENS, D_MODEL), norm_ffn_g[i], w_gate_up, w_down, i, proj=proj,
                   norm_out_g=norm_mix_g[i + 1] if next_kind else None, norm_out_chunks=next_kind == 1)
        x, h_next = out if next_kind else (out, None)
        x = x.reshape(BATCH, SEQ, D_MODEL)
    return x
```

```python
import functools
import math

import numpy as np
import jax
import jax.numpy as jnp
from jax import lax
from jax.experimental import pallas as pl
from jax.experimental.pallas import tpu as pltpu

F32 = jnp.float32
BF16 = jnp.bfloat16

D_MODEL = 1024
BATCH = 8
SEQ = 2048
DEPTH = 4
N_TOKENS = BATCH * SEQ
EPS = 1e-6
D_FF = 2816
FOURIER_GROUP = 128
S5_GROUP = 16
S5_GROUPS = 64
S5_STATE = 64
HEAD_DIM = 64
HEADS_PER_GROUP = 16
DILATED_GROUPS = ((128, 1), (512, 4), (2048, 16))
N_ATTN_GROUPS = 3
NUM_BUCKETS = 32
MAX_DISTANCE = 1024
ATTN_SIDE = 64

LANES = 128
VMEM_LIMIT_CAP = 60 * 1024 * 1024

S5_CHUNK = 16
S5_CHUNKS = SEQ // S5_CHUNK
S5_ROWS = BATCH * S5_CHUNKS
S5_TILE_GROUPS = LANES // S5_GROUP
S5_CK = S5_CHUNK * S5_GROUP

ATTN_BQ = 64
ATTN_BK = ATTN_BQ + 2 * ATTN_SIDE


def _params(sem, vmem_bytes):
    return pltpu.CompilerParams(dimension_semantics=sem,
                                vmem_limit_bytes=int(min(VMEM_LIMIT_CAP, vmem_bytes)))


def _rms(x, g):
    ms = jnp.mean(x * x, axis=-1, keepdims=True)
    return x * lax.rsqrt(ms + EPS) * g


def _sigmoid(x):
    return 1.0 / (1.0 + jnp.exp(-x))


def _resident(shape):
    nd = len(shape)
    return pl.BlockSpec(shape, lambda *_: (0,) * nd, pipeline_mode=pl.Buffered(1))


def _cast_body(w_ref, o_ref):
    o_ref[...] = w_ref[...].astype(BF16)


def _cast_bf16(w, rows=256):
    n_layers, n_rows, n_cols = w.shape
    block = pl.BlockSpec((None, rows, n_cols), lambda l, i: (l, i, 0))
    return pl.pallas_call(
        _cast_body,
        grid=(n_layers, n_rows // rows),
        in_specs=[block],
        out_specs=block,
        out_shape=jax.ShapeDtypeStruct(w.shape, BF16),
        compiler_params=_params(("parallel", "parallel"), 4 * rows * n_cols * 6 + (4 << 20)),
        name="cast_bf16",
    )(w)


FFN_CHUNK = 256


def _ffn_body(*refs, has_proj, has_norm_out, layer):
    refs = list(refs)
    x_ref, g_ref, wgu_hbm, wd_hbm = refs[:4]
    a_ref, wp_ref = refs[4:6] if has_proj else (None, None)
    gn_ref = refs[4 + 2 * has_proj] if has_norm_out else None
    n_in = 4 + 2 * has_proj + has_norm_out
    o_ref = refs[n_in]
    hn_ref = refs[n_in + 1] if has_norm_out else None
    wgu_ref, wd_ref, sem = refs[-3:]
    chunks = list(range(0, D_FF, FFN_CHUNK))

    def copies(k):
        c = chunks[k]
        w = min(FFN_CHUNK, D_FF - c)
        return (pltpu.make_async_copy(wgu_hbm.at[layer, :, pl.ds(c, w)], wgu_ref.at[:, pl.ds(c, w)], sem.at[0, k]),
                pltpu.make_async_copy(wgu_hbm.at[layer, :, pl.ds(D_FF + c, w)], wgu_ref.at[:, pl.ds(D_FF + c, w)],
                                      sem.at[1, k]),
                pltpu.make_async_copy(wd_hbm.at[layer, pl.ds(c, w), :], wd_ref.at[pl.ds(c, w), :], sem.at[2, k]))

    def tile(first):
        x = x_ref[...]
        if has_proj:
            x = x + jnp.dot(a_ref[...], wp_ref[...], preferred_element_type=F32)
        h = _rms(x, g_ref[...]).astype(BF16)
        acc = x
        for k, c in enumerate(chunks):
            w = min(FFN_CHUNK, D_FF - c)
            if first:
                for cp in copies(k):
                    cp.wait()
            gate = jnp.dot(h, wgu_ref[:, c:c + w].astype(BF16), preferred_element_type=F32)
            up = jnp.dot(h, wgu_ref[:, D_FF + c:D_FF + c + w].astype(BF16), preferred_element_type=F32)
            a = (gate * _sigmoid(gate) * up).astype(BF16)
            acc = acc + jnp.dot(a, wd_ref[c:c + w, :].astype(BF16), preferred_element_type=F32)
        o_ref[...] = acc
        if has_norm_out:
            hn_ref[...] = _rms(acc, gn_ref[...]).reshape(hn_ref.shape).astype(hn_ref.dtype)

    @pl.when(pl.program_id(0) == 0)
    def _():
        for k in range(len(chunks)):
            for cp in copies(k):
                cp.start()
        tile(True)

    @pl.when(pl.program_id(0) != 0)
    def _():
        tile(False)


def _layer_block(shape, layer):
    return pl.BlockSpec((None,) + shape, lambda *_: (layer,) + (0,) * len(shape), pipeline_mode=pl.Buffered(1))


def _ffn(x2, g, wgu_all, wd_all, layer, proj=None, norm_out_g=None, norm_out_chunks=False, tm=512):
    m = x2.shape[0]
    tile = pl.BlockSpec((tm, D_MODEL), lambda i: (i, 0))
    in_specs = [tile, _resident((1, D_MODEL)), pl.BlockSpec(memory_space=pl.ANY), pl.BlockSpec(memory_space=pl.ANY)]
    args = [x2, g.reshape(1, D_MODEL), wgu_all, wd_all]
    out_specs, out_shape = [tile], [jax.ShapeDtypeStruct((m, D_MODEL), F32)]
    if proj is not None:
        in_specs += [tile, _layer_block((D_MODEL, D_MODEL), proj[2])]
        args += [proj[0], proj[1]]
    if norm_out_g is not None:
        tiles_per_seq = SEQ // tm
        in_specs.append(_resident((1, D_MODEL)))
        args.append(norm_out_g.reshape(1, D_MODEL))
        if norm_out_chunks:
            out_specs.append(pl.BlockSpec((tm // S5_CHUNK, None, S5_CHUNK, D_MODEL),
                                          lambda i: (i % tiles_per_seq, i // tiles_per_seq, 0, 0)))
            out_shape.append(jax.ShapeDtypeStruct((S5_CHUNKS, BATCH, S5_CHUNK, D_MODEL), F32))
        else:
            out_specs.append(tile)
            out_shape.append(jax.ShapeDtypeStruct((m, D_MODEL), BF16))
    vmem = 3 * D_MODEL * D_FF * 4 + 10 * tm * D_MODEL * 4 + (10 << 20)
    outs = pl.pallas_call(
        functools.partial(_ffn_body, has_proj=proj is not None, has_norm_out=norm_out_g is not None, layer=layer),
        grid=(m // tm,),
        in_specs=in_specs,
        out_specs=out_specs,
        out_shape=out_shape,
        scratch_shapes=[pltpu.VMEM((D_MODEL, 2 * D_FF), F32), pltpu.VMEM((D_FF, D_MODEL), F32),
                        pltpu.SemaphoreType.DMA((3, D_FF // FFN_CHUNK))],
        compiler_params=_params(("arbitrary",), vmem),
        name="ffn",
    )(*args)
    return outs if norm_out_g is not None else outs[0]


def _fnet_weight_body(cc_ref, sc_ref, w_ref, o_ref):
    w = w_ref[...]
    o_ref[:, :D_MODEL] = jnp.dot(cc_ref[...], w, preferred_element_type=F32,
                                 precision=lax.Precision.HIGHEST).astype(BF16)
    o_ref[:, D_MODEL:] = jnp.dot(sc_ref[...], w, preferred_element_type=F32,
                                 precision=lax.Precision.HIGHEST).astype(BF16)


def _fnet_weights(w_out):
    n = np.arange(FOURIER_GROUP)
    ang = 2.0 * np.pi * ((n[:, None] * n[None, :]) % FOURIER_GROUP) / FOURIER_GROUP
    cc = jnp.asarray(np.cos(ang) / math.sqrt(FOURIER_GROUP), F32)
    sc = jnp.asarray(np.sin(ang) / math.sqrt(FOURIER_GROUP), F32)
    ng = D_MODEL // FOURIER_GROUP
    return pl.pallas_call(
        _fnet_weight_body,
        grid=(ng,),
        in_specs=[_resident((FOURIER_GROUP, FOURIER_GROUP)),
                  _resident((FOURIER_GROUP, FOURIER_GROUP)),
                  pl.BlockSpec((FOURIER_GROUP, D_MODEL), lambda i: (i, 0))],
        out_specs=pl.BlockSpec((FOURIER_GROUP, 2 * D_MODEL), lambda i: (i, 0)),
        out_shape=jax.ShapeDtypeStruct((D_MODEL, 2 * D_MODEL), BF16),
        compiler_params=_params(("parallel",), 16 << 20),
        name="fnet_weights",
    )(cc, sc, w_out)


FN_HALF = SEQ // 2
FN_BLK = 256
FN_NB = FN_HALF // FN_BLK


def _fnet_tables():
    k = np.arange(FN_HALF)[:, None]
    n = np.arange(FN_HALF)[None, :]
    ang = 2.0 * np.pi * ((k * n) % SEQ) / SEQ
    scale = 1.0 / math.sqrt(SEQ)
    i = np.arange(FN_BLK)
    rev = (i[None, :] == FN_BLK - i[:, None]).astype(np.float32)
    return (jnp.asarray(np.cos(ang) * scale, BF16), jnp.asarray(np.sin(ang) * scale, BF16),
            jnp.asarray(rev, BF16))


def _fnet_body(x_ref, g_ref, cs_ref, ss_ref, w_ref, rev_ref, o_ref, e_sc, d_sc, zc_sc, zs_sc):
    g = g_ref[...]
    scale = 1.0 / math.sqrt(SEQ)
    row = lax.broadcasted_iota(jnp.int32, (FN_BLK, 1), 0)
    first = row == 0
    sign = jnp.where((row & 1) == 0, 1.0, -1.0)
    rev = rev_ref[...]
    blk = lambda a: pl.ds(a * FN_BLK, FN_BLK)
    mirror = lambda a: pl.ds(SEQ - (a + 1) * FN_BLK, FN_BLK)

    alt = jnp.zeros((1, D_MODEL), F32)
    carry = jnp.zeros((1, D_MODEL), F32)
    for a in range(FN_NB):
        lo = _rms(x_ref[blk(a), :], g)
        hi = _rms(x_ref[mirror(a), :], g)
        alt = alt + jnp.sum(sign * lo, axis=0, keepdims=True) + jnp.sum(sign * hi, axis=0, keepdims=True)
        r = jnp.dot(rev, hi.astype(BF16), preferred_element_type=F32)
        r = jnp.where(first, carry, r)
        e_sc[blk(a), :] = (lo + r).astype(BF16)
        d_sc[blk(a), :] = (lo - r).astype(BF16)
        carry = hi[0:1, :]
    h_nyq = carry * scale

    for c in range(FN_NB):
        zc = jnp.dot(cs_ref[blk(c), :], e_sc[...], preferred_element_type=F32) + sign * h_nyq
        zs = jnp.dot(ss_ref[blk(c), :], d_sc[...], preferred_element_type=F32)
        zc_sc[blk(c), :] = zc.astype(BF16)
        zs_sc[blk(c), :] = zs.astype(BF16)

    wc = w_ref[:, :D_MODEL]
    ws = w_ref[:, D_MODEL:]
    z_nyq = jnp.broadcast_to(alt * scale, (8, D_MODEL)).astype(BF16)
    carry = jnp.dot(z_nyq, wc, preferred_element_type=F32)[0:1, :]
    def products(c):
        return (jnp.dot(zc_sc[blk(c), :], wc, preferred_element_type=F32),
                jnp.dot(zs_sc[blk(c), :], ws, preferred_element_type=F32))

    ahead = products(FN_NB - 1)
    for c in reversed(range(FN_NB)):
        p, q = ahead
        if c > 0:
            ahead = products(c - 1)
        o_ref[blk(c), :] = x_ref[blk(c), :] + (p - q)
        m = p + q
        m_hi = m.astype(BF16)
        m_lo = (m - m_hi.astype(F32)).astype(BF16)
        r = (jnp.dot(rev, m_hi, preferred_element_type=F32) + jnp.dot(rev, m_lo, preferred_element_type=F32))
        r = jnp.where(first, carry, r)
        o_ref[mirror(c), :] = x_ref[mirror(c), :] + r
        carry = m[0:1, :]


def _fnet_layer(x, g, w_out):
    wcs = _fnet_weights(w_out)
    cs, ss, rev = _fnet_tables()
    seq_block = pl.BlockSpec((None, SEQ, D_MODEL), lambda b: (b, 0, 0))
    half = pltpu.VMEM((FN_HALF, D_MODEL), BF16)
    vmem = 4 * SEQ * D_MODEL * 4 + 4 * FN_HALF * D_MODEL * 2 + 4 * FN_HALF * D_MODEL * 2 + (12 << 20)
    return pl.pallas_call(
        _fnet_body,
        grid=(BATCH,),
        in_specs=[seq_block, _resident((1, D_MODEL)), _resident((FN_HALF, FN_HALF)),
                  _resident((FN_HALF, FN_HALF)), _resident((D_MODEL, 2 * D_MODEL)),
                  _resident((FN_BLK, FN_BLK))],
        out_specs=seq_block,
        out_shape=jax.ShapeDtypeStruct((BATCH, SEQ, D_MODEL), F32),
        scratch_shapes=[half, half, half, half],
        compiler_params=_params(("parallel",), vmem),
        name="fnet_mix",
    )(x, g.reshape(1, D_MODEL), cs, ss, wcs, rev)


def _s5_prep_body(p_ref, bt_ref, ct_ref, strip_ref, s_ref, wc_ref, coef_ref, lag_sc):
    for gi in range(S5_TILE_GROUPS):
        _s5_prep_group(p_ref.at[gi], bt_ref.at[gi], ct_ref.at[gi], strip_ref.at[gi], s_ref.at[gi], wc_ref.at[gi],
                       coef_ref.at[gi], lag_sc.at[gi])


def _s5_prep_group(p_ref, bt_ref, ct_ref, strip_ref, s_ref, wc_ref, coef_ref, lag_sc):
    L = S5_CHUNK
    lane = lax.broadcasted_iota(jnp.int32, (1, LANES), 1)
    fwd = lane < S5_STATE
    l_re, l_im = p_ref[0:1, :], p_ref[1:2, :]
    dt = jnp.exp(p_ref[2:3, :])
    mag = jnp.exp(l_re * dt)
    lb_re, lb_im = mag * jnp.cos(l_im * dt), mag * jnp.sin(l_im * dt)
    n_re, n_im = lb_re - 1.0, lb_im
    den = l_re * l_re + l_im * l_im
    f_re, f_im = (n_re * l_re + n_im * l_im) / den, (n_im * l_re - n_re * l_im) / den
    b_re, b_im = bt_ref[0], bt_ref[1]
    bb_re, bb_im = f_re * b_re - f_im * b_im, f_re * b_im + f_im * b_re
    c_re, c_im = ct_ref[0], ct_ref[1]
    zero = jnp.zeros((1, LANES), F32)
    pw = [(jnp.ones((1, LANES), F32), zero)]
    for _ in range(L):
        pr, pi = pw[-1]
        pw.append((pr * lb_re - pi * lb_im, pr * lb_im + pi * lb_re))

    def mix(f, bk):
        return jnp.where(fwd, f[0], bk[0]), jnp.where(fwd, f[1], bk[1])

    def times(p, m_re, m_im):
        return p[0] * m_re - p[1] * m_im, p[0] * m_im + p[1] * m_re

    for t in range(L):
        rows = slice(t * S5_GROUP, (t + 1) * S5_GROUP)
        re, im = times(mix(pw[L - 1 - t], pw[t]), bb_re, bb_im)
        s_ref[rows, :LANES] = re.astype(BF16)
        s_ref[rows, LANES:] = im.astype(BF16)
        re, im = times(mix(pw[t + 1], pw[L - t]), c_re, c_im)
        wc_ref[rows, :LANES] = re.astype(BF16)
        wc_ref[rows, LANES:] = (-im).astype(BF16)
    none = (zero, zero)
    for j in range(2 * L):
        rows = slice(j * S5_GROUP, (j + 1) * S5_GROUP)
        f = pw[j - (L - 1)] if L - 1 <= j <= 2 * L - 2 else none
        bk = pw[L - 1 - j] if j <= L - 1 else none
        re, im = times(mix(f, bk), c_re, c_im)
        lag_sc[rows, :LANES] = re.astype(BF16)
        lag_sc[rows, LANES:] = (-im).astype(BF16)
    b_cat = jnp.concatenate([bb_re, bb_im], axis=1)
    b_hi = b_cat.astype(BF16)
    b_lo = (b_cat - b_hi.astype(F32)).astype(BF16)
    nt = (((1,), (1,)), ((), ()))
    lag = lag_sc[...]
    strip_ref[...] = (lax.dot_general(b_hi, lag, nt, preferred_element_type=F32)
                      + lax.dot_general(b_lo, lag, nt, preferred_element_type=F32))
    coef_ref[0:1, :] = pw[L][0]
    coef_ref[1:2, :] = pw[L][1]


def _s5_matrices(lam_re, lam_im, log_dt, b_re, b_im, c_re, c_im):
    def lanes(z):
        return jnp.concatenate([z[0], z[1]], axis=-1).astype(F32)
    params = jnp.stack([lanes(lam_re), lanes(lam_im), jnp.repeat(log_dt.T.astype(F32), S5_STATE, axis=1)], axis=1)
    params = jnp.pad(params, ((0, 0), (0, 5), (0, 0)))
    bt = jnp.stack([lanes(b_re.transpose(0, 1, 3, 2)), lanes(b_im.transpose(0, 1, 3, 2))], axis=1)
    ct = jnp.stack([lanes(c_re), lanes(c_im)], axis=1)
    tg = S5_TILE_GROUPS
    mat = pl.BlockSpec((tg, 2, S5_GROUP, LANES), lambda g: (g, 0, 0, 0))
    sq = pl.BlockSpec((tg, S5_CK, S5_CK), lambda g: (g, 0, 0))
    return pl.pallas_call(
        _s5_prep_body,
        grid=(S5_GROUPS // tg,),
        in_specs=[pl.BlockSpec((tg, 8, LANES), lambda g: (g, 0, 0)), mat, mat],
        out_specs=[pl.BlockSpec((tg, S5_GROUP, 2 * S5_CK), lambda g: (g, 0, 0)), sq, sq,
                   pl.BlockSpec((tg, 2, LANES), lambda g: (g, 0, 0))],
        out_shape=[jax.ShapeDtypeStruct((S5_GROUPS, S5_GROUP, 2 * S5_CK), F32),
                   jax.ShapeDtypeStruct((S5_GROUPS, S5_CK, S5_CK), BF16),
                   jax.ShapeDtypeStruct((S5_GROUPS, S5_CK, S5_CK), BF16),
                   jax.ShapeDtypeStruct((S5_GROUPS, 2, LANES), F32)],
        scratch_shapes=[pltpu.VMEM((S5_TILE_GROUPS, 2 * S5_CK, S5_CK), BF16)],
        compiler_params=_params(("parallel",), 24 << 20),
        name="s5_prep",
    )(params, bt, ct)


def _s5_norm_body(x_ref, g_ref, o_ref):
    h = _rms(x_ref[...], g_ref[...])
    o_ref[...] = h.reshape(o_ref.shape)


def _s5_norm(x, g, tm=512):
    cr = tm // S5_CHUNK
    return pl.pallas_call(
        _s5_norm_body,
        grid=(BATCH, SEQ // tm),
        in_specs=[pl.BlockSpec((None, tm, D_MODEL), lambda b, i: (b, i, 0)),
                  _resident((1, D_MODEL))],
        out_specs=pl.BlockSpec((cr, None, S5_CHUNK, D_MODEL), lambda b, i: (i, b, 0, 0)),
        out_shape=jax.ShapeDtypeStruct((S5_CHUNKS, BATCH, S5_CHUNK, D_MODEL), F32),
        compiler_params=_params(("parallel", "parallel"), 32 << 20),
        name="s5_norm",
    )(x, g.reshape(1, D_MODEL))


def _block_transpose8(a, lane_block):
    a = list(a)
    for d in (4, 2, 1):
        take_lo = (lane_block & d) == 0
        nxt = list(a)
        for i in range(8):
            if i & d:
                continue
            lo, hi = a[i], a[i + d]
            if 2 * S5_GROUP * d == LANES:
                both = pltpu.roll(jnp.where(take_lo, hi, lo), S5_GROUP * d, 1)
                nxt[i] = jnp.where(take_lo, lo, both)
                nxt[i + d] = jnp.where(take_lo, both, hi)
            else:
                nxt[i] = jnp.where(take_lo, lo, pltpu.roll(hi, S5_GROUP * d, 1))
                nxt[i + d] = jnp.where(take_lo, pltpu.roll(lo, LANES - S5_GROUP * d, 1), hi)
        a = nxt
    return a


def _gelu_tanh(y):
    return 0.5 * y * (1.0 + jnp.tanh(math.sqrt(2.0 / math.pi) * (y + 0.044715 * (y * y * y))))


S5_RB = 128
S5_STAGE = 4


def _s5_body(h_ref, strip_ref, ws_ref, wc_ref, coef_ref, d_ref, o_ref, x_sc, y_sc, toep_sc, stage_sc):
    lane = lax.broadcasted_iota(jnp.int32, (1, LANES), 1)
    lane_block = lane // S5_GROUP
    fwd_lanes = lane < S5_STATE
    n_rb = S5_ROWS // S5_RB
    tok_rb = S5_RB * S5_CHUNK

    def relayout_in(i, carry):
        base = pl.multiple_of(i * tok_rb, tok_rb)
        rows = pl.multiple_of(i * S5_RB, S5_RB)
        for half in range(2):
            a = [pltpu.bitcast(h_ref[pl.ds(base + half * 8 + t, S5_RB, stride=S5_CHUNK), :].astype(BF16), jnp.uint32)
                 for t in range(8)]
            xt = _block_transpose8(a, lane_block)
            for gi in range(S5_TILE_GROUPS):
                x_sc[gi, pl.ds(rows, S5_RB), half * LANES:(half + 1) * LANES] = pltpu.bitcast(xt[gi], BF16)
        return carry

    lax.fori_loop(0, n_rb, relayout_in, 0)

    nt = (((1,), (1,)), ((), ()))
    for gi in range(S5_TILE_GROUPS):
        y_sc[gi] = jnp.dot(x_sc[gi], ws_ref[gi], preferred_element_type=F32)
        strip = strip_ref[gi]
        for t in range(S5_CHUNK):
            off = (S5_CHUNK - 1 - t) * S5_GROUP
            win = strip if off == 0 else pltpu.roll(strip, strip.shape[1] - off, 1)
            toep_sc[gi, t * S5_GROUP:(t + 1) * S5_GROUP, :] = win[:, :S5_CK].astype(BF16)

    coef = [(coef_ref[gi, 0:1, :], coef_ref[gi, 1:2, :]) for gi in range(S5_TILE_GROUPS)]

    def scan_step(k, st):
        rf = pl.ds(pl.multiple_of(k * BATCH, BATCH), BATCH)
        rb = pl.ds(pl.multiple_of((S5_CHUNKS - 1 - k) * BATCH, BATCH), BATCH)
        s_in = [(jnp.where(fwd_lanes, y_sc[gi, rf, :LANES], y_sc[gi, rb, :LANES]),
                 jnp.where(fwd_lanes, y_sc[gi, rf, LANES:], y_sc[gi, rb, LANES:])) for gi in range(S5_TILE_GROUPS)]
        nxt = []
        for gi in range(S5_TILE_GROUPS):
            st_re, st_im = st[2 * gi], st[2 * gi + 1]
            a_re, a_im = coef[gi]
            s_re, s_im = s_in[gi]
            y_sc[gi, rf, :S5_STATE] = st_re[:, :S5_STATE]
            y_sc[gi, rb, S5_STATE:LANES] = st_re[:, S5_STATE:]
            y_sc[gi, rf, LANES:LANES + S5_STATE] = st_im[:, :S5_STATE]
            y_sc[gi, rb, LANES + S5_STATE:] = st_im[:, S5_STATE:]
            nxt.append(a_re * st_re - a_im * st_im + s_re)
            nxt.append(a_re * st_im + a_im * st_re + s_im)
        return tuple(nxt)

    zero = jnp.zeros((BATCH, LANES), F32)
    lax.fori_loop(0, S5_CHUNKS, scan_step, (zero,) * (2 * S5_TILE_GROUPS))

    for gi in range(S5_TILE_GROUPS):
        h_in = y_sc[gi].astype(BF16)
        y_sc[gi] = (jnp.dot(x_sc[gi], toep_sc[gi], preferred_element_type=F32)
                    + lax.dot_general(h_in, wc_ref[gi], nt, preferred_element_type=F32))

    d_skip = d_ref[...]

    def relayout_out(i, carry):
        base = pl.multiple_of(i * tok_rb, tok_rb)
        rows = pl.multiple_of(i * S5_RB, S5_RB)
        for half in range(2):
            yg = [y_sc[gi, pl.ds(rows, S5_RB), half * LANES:(half + 1) * LANES] for gi in range(S5_TILE_GROUPS)]
            yt = _block_transpose8(yg, lane_block)
            for tl in range(8):
                t = half * 8 + tl
                tok = pl.ds(base + t, S5_RB, stride=S5_CHUNK)
                act = _gelu_tanh(yt[tl] + d_skip * h_ref[tok, :])
                stage_sc[t % S5_STAGE, pl.ds(t // S5_STAGE, S5_RB, stride=S5_STAGE), :] = act
        for u in range(S5_STAGE):
            o_ref[pl.ds(base + u, S5_RB * S5_STAGE, stride=S5_STAGE), :] = stage_sc[u]
        return carry

    lax.fori_loop(0, n_rb, relayout_out, 0)


def _s5_core(h2, strip, s_in, wc_t, coef, d_skip):
    n_tiles = S5_GROUPS // S5_TILE_GROUPS
    tok_block = pl.BlockSpec((N_TOKENS, LANES), lambda j: (0, j))
    vmem = (4 * N_TOKENS * LANES * 4 + S5_TILE_GROUPS * S5_ROWS * S5_CK * (2 + 4)
            + 4 * S5_ROWS * LANES * 4 + 2 * S5_TILE_GROUPS * S5_CK * 3 * S5_CK * 2 + (12 << 20))
    return pl.pallas_call(
        _s5_body,
        grid=(n_tiles,),
        in_specs=[tok_block,
                  pl.BlockSpec((S5_TILE_GROUPS, S5_GROUP, 2 * S5_CK), lambda j: (j, 0, 0)),
                  pl.BlockSpec((S5_TILE_GROUPS, S5_CK, S5_CK), lambda j: (j, 0, 0)),
                  pl.BlockSpec((S5_TILE_GROUPS, S5_CK, S5_CK), lambda j: (j, 0, 0)),
                  pl.BlockSpec((S5_TILE_GROUPS, 2, LANES), lambda j: (j, 0, 0)),
                  pl.BlockSpec((1, LANES), lambda j: (0, j))],
        out_specs=tok_block,
        out_shape=jax.ShapeDtypeStruct((N_TOKENS, D_MODEL), F32),
        scratch_shapes=[pltpu.VMEM((S5_TILE_GROUPS, S5_ROWS, S5_CK), BF16),
                        pltpu.VMEM((S5_TILE_GROUPS, S5_ROWS, S5_CK), F32),
                        pltpu.VMEM((S5_TILE_GROUPS, S5_CK, S5_CK), BF16),
                        pltpu.VMEM((S5_STAGE, S5_RB * S5_STAGE, LANES), F32)],
        compiler_params=_params(("arbitrary",), vmem),
        name="s5_core",
    )(h2, strip, s_in, wc_t, coef, d_skip.reshape(1, D_MODEL))


GLU_CHUNK = 256


def _s5_glu_body(x_ref, a_ref, w_ref, o_ref):
    a = a_ref[...].reshape(x_ref.shape).astype(BF16)
    for c in range(0, D_MODEL, GLU_CHUNK):
        val = jnp.dot(a, w_ref[:, c:c + GLU_CHUNK], preferred_element_type=F32)
        gate = jnp.dot(a, w_ref[:, D_MODEL + c:D_MODEL + c + GLU_CHUNK], preferred_element_type=F32)
        o_ref[:, c:c + GLU_CHUNK] = x_ref[:, c:c + GLU_CHUNK] + val * _sigmoid(gate)


def _s5_glu(x, act, w_glu_all, layer, tm=1024):
    cr = tm // S5_CHUNK
    vmem = 2 * D_MODEL * D_MODEL * 2 + 6 * tm * D_MODEL * 4 + 2 * tm * 2 * D_MODEL * 4 + (4 << 20)
    return pl.pallas_call(
        _s5_glu_body,
        grid=(BATCH, SEQ // tm),
        in_specs=[pl.BlockSpec((None, tm, D_MODEL), lambda b, i: (b, i, 0)),
                  pl.BlockSpec((cr, None, S5_CHUNK, D_MODEL), lambda b, i: (i, b, 0, 0)),
                  _layer_block((D_MODEL, 2 * D_MODEL), layer)],
        out_specs=pl.BlockSpec((None, tm, D_MODEL), lambda b, i: (b, i, 0)),
        out_shape=jax.ShapeDtypeStruct((BATCH, SEQ, D_MODEL), F32),
        compiler_params=_params(("parallel", "parallel"), vmem),
        name="s5_glu",
    )(x, act, w_glu_all)


def _s5_layer(x, h, lam_re, lam_im, log_dt, b_re, b_im, c_re, c_im, d_skip, w_glu_all, layer):
    strip, s_in, wc_t, coef = _s5_matrices(lam_re, lam_im, log_dt, b_re, b_im, c_re, c_im)
    act = _s5_core(h.reshape(N_TOKENS, D_MODEL), strip, s_in, wc_t, coef, d_skip)
    act = act.reshape(S5_CHUNKS, BATCH, S5_CHUNK, D_MODEL)
    return _s5_glu(x, act, w_glu_all, layer)


def _t5_bucket(rel):
    half = NUM_BUCKETS // 2
    max_exact = half // 2
    n = np.abs(rel)
    sign = (rel > 0).astype(np.int32) * half
    large = max_exact + (np.log(np.maximum(n, 1) / max_exact) / math.log(MAX_DISTANCE / max_exact)
                         * (half - max_exact)).astype(np.int32)
    large = np.minimum(large, half - 1)
    return (sign + np.where(n < max_exact, n, large)).astype(np.int32)


LOG2E = 1.4426950408889634
N_PAIRS = HEADS_PER_GROUP // 2
ATTN_PAD_ROWS = SEQ + 2 * ATTN_SIDE * max(d for _, d in DILATED_GROUPS)
QKV_GROUP_WIDTH = 3 * D_MODEL
QKV_SLAB = 4 * LANES
QKV_TILE_ROWS = 512
QKV_REGROUP_STRIDE = 4


def _attn_bias_tables(rel_bias):
    n_off = ATTN_BK + ATTN_BQ - 1
    offs = np.arange(n_off) - (ATTN_BQ - 1) - ATTN_SIDE
    strips = []
    for gi, (_, dil) in enumerate(DILATED_GROUPS):
        onehot = jnp.asarray(_t5_bucket(offs * dil)[:, None] == np.arange(NUM_BUCKETS), F32)
        f = jnp.dot(onehot, rel_bias[:, gi * HEADS_PER_GROUP:(gi + 1) * HEADS_PER_GROUP].astype(F32),
                    precision=lax.Precision.HIGHEST).T
        strips.append(jnp.pad(f[:, ::-1], ((0, 0), (0, BIAS_STRIP - n_off))))
    strips = jnp.stack(strips).reshape(N_ATTN_GROUPS, HEADS_PER_GROUP, 1, BIAS_STRIP)
    return pl.pallas_call(
        _attn_bias_body,
        grid=(N_ATTN_GROUPS, N_PAIRS),
        in_specs=[pl.BlockSpec((None, 2, 1, BIAS_STRIP), lambda g, p: (g, p, 0, 0))],
        out_specs=pl.BlockSpec((None, None, ATTN_BK, 2 * ATTN_BQ), lambda g, p: (g, p, 0, 0)),
        out_shape=jax.ShapeDtypeStruct((N_ATTN_GROUPS, N_PAIRS, ATTN_BK, 2 * ATTN_BQ), F32),
        compiler_params=_params(("parallel", "parallel"), 16 << 20),
        name="attn_bias",
    )(strips)


BIAS_STRIP = 256


def _attn_bias_body(f_ref, o_ref):
    krow = lax.broadcasted_iota(jnp.int32, (ATTN_BK, 1), 0)
    lane = lax.broadcasted_iota(jnp.int32, (1, LANES), 1)
    head0 = lane < ATTN_BQ
    band = jnp.abs(krow - ATTN_SIDE - (lane & (ATTN_BQ - 1))) <= ATTN_SIDE
    halves = []
    for a in range(2):
        x = jnp.broadcast_to(f_ref[a], (ATTN_BK, BIAS_STRIP))
        shift = (a * ATTN_BQ - (ATTN_BK - 1)) % BIAS_STRIP
        halves.append(pltpu.roll(x, shift, 1, stride=1, stride_axis=0)[:, :LANES])
    o_ref[...] = jnp.where(band, jnp.where(head0, halves[0], halves[1]) * LOG2E, -1e30)


def _qkv_body(h_ref, w_ref, qg_ref, kg_ref, q_out, k_out, v_out, h_sc, *, n_res):
    rows = h_ref.shape[0] // n_res
    h = h_ref[...]
    if n_res > 1:
        tm = h_ref.shape[0]
        cols = []
        for c in range(D_MODEL // LANES):
            part = h[:, c * LANES:(c + 1) * LANES].astype(F32)
            done, left = 1, n_res
            residue = [0]
            while left > 1:
                step = min(left, QKV_REGROUP_STRIDE)
                h_sc[c] = part
                seg = tm // done
                part = jnp.concatenate(
                    [h_sc[c, pl.ds(g * seg + r, seg // step, stride=step), :] for g in range(done) for r in range(step)],
                    axis=0)
                residue = [residue[g] + done * r for g in range(done) for r in range(step)]
                done, left = done * step, left // step
            cols.append(part)
        h = jnp.concatenate(cols, axis=1).astype(BF16)
    else:
        residue = [0]
    lane = lax.broadcasted_iota(jnp.int32, (1, LANES), 1)
    head0 = lane < HEAD_DIM

    def head_norm(t, gain):
        sq = t * t
        s0 = jnp.sum(jnp.where(head0, sq, 0.0), axis=-1, keepdims=True)
        s1 = jnp.sum(jnp.where(head0, 0.0, sq), axis=-1, keepdims=True)
        ms = jnp.where(head0, s0, s1) * (1.0 / HEAD_DIM)
        return t * lax.rsqrt(ms + EPS) * gain

    for c in range(QKV_GROUP_WIDTH // QKV_SLAB):
        z = jnp.dot(h, w_ref[:, c * QKV_SLAB:(c + 1) * QKV_SLAB].astype(BF16), preferred_element_type=F32)
        for half in range(QKV_SLAB // LANES):
            section, lo = divmod(c * QKV_SLAB + half * LANES, D_MODEL)
            zz = z[:, half * LANES:(half + 1) * LANES]
            if section == 0:
                zz = head_norm(zz, qg_ref[...]) * (HEAD_DIM ** -0.5 * LOG2E)
                qa = jnp.where(head0, zz, 0.0).astype(BF16)
                qb = jnp.where(head0, 0.0, zz).astype(BF16)
            elif section == 1:
                zz = head_norm(zz, kg_ref[...]).astype(BF16)
            else:
                zz = zz.astype(BF16)
            for g, r in enumerate(residue):
                piece = slice(g * rows, (g + 1) * rows)
                if section == 0:
                    q_out[r, 0, :, lo:lo + LANES] = qa[piece]
                    q_out[r, 1, :, lo:lo + LANES] = qb[piece]
                elif section == 1:
                    k_out[r, :, lo:lo + LANES] = zz[piece]
                else:
                    v_out[r, :, lo:lo + LANES] = zz[piece]


def _qkv_group(h, w_qkv_all, layer, gi, dil, q_gain, k_gain):
    seg = SEQ // dil
    tm = QKV_TILE_ROWS
    rows = tm // dil
    assert tm % dil == 0 and rows % 16 == 0
    gain2 = lambda gn: jnp.tile(gn.astype(F32), 2).reshape(1, LANES)
    kv_spec = pl.BlockSpec((None, dil, rows, D_MODEL), lambda b, i: (b, 0, i, 0))
    kv_shape = jax.ShapeDtypeStruct((BATCH, dil, seg, D_MODEL), BF16)
    vmem = (D_MODEL * QKV_GROUP_WIDTH * 4 + 3 * tm * D_MODEL * 4 + 2 * 4 * tm * D_MODEL * 2 + (12 << 20))
    return pl.pallas_call(
        functools.partial(_qkv_body, n_res=dil),
        grid=(BATCH, SEQ // tm),
        in_specs=[pl.BlockSpec((None, tm, D_MODEL), lambda b, i: (b, i, 0)),
                  pl.BlockSpec((None, D_MODEL, QKV_GROUP_WIDTH), lambda b, i: (layer, 0, gi),
                               pipeline_mode=pl.Buffered(1)),
                  _resident((1, LANES)), _resident((1, LANES))],
        out_specs=[pl.BlockSpec((None, dil, 2, rows, D_MODEL), lambda b, i: (b, 0, 0, i, 0)),
                   kv_spec, kv_spec],
        out_shape=[jax.ShapeDtypeStruct((BATCH, dil, 2, seg, D_MODEL), BF16), kv_shape, kv_shape],
        scratch_shapes=[pltpu.VMEM((D_MODEL // LANES, tm, LANES), F32)],
        compiler_params=_params(("parallel", "parallel"), vmem),
        name=f"qkv_dil{dil}",
    )(h, w_qkv_all, gain2(q_gain), gain2(k_gain))


ATTN_AHEAD = 5


def _attn_body(q0, k0, v0, q1, k1, v1, q2, k2, v2, bias_ref, o_ref, kp_sc, vp_sc, lt_sc, og_sc, lg_sc):
    lane = lax.broadcasted_iota(jnp.int32, (1, LANES), 1)
    head0 = lane < HEAD_DIM
    krow = lax.broadcasted_iota(jnp.int32, (ATTN_BK, 1), 0)
    zpad = jnp.zeros((ATTN_SIDE, LANES), BF16)
    tn = (((0,), (0,)), ((), ()))

    qkv_refs = ((q0, k0, v0), (q1, k1, v1), (q2, k2, v2))
    for g, (_, dil) in enumerate(DILATED_GROUPS):
        q_ref, k_ref, v_ref = qkv_refs[g]
        seg = SEQ // dil
        nb = seg // ATTN_BQ
        pad_seg = seg + 2 * ATTN_SIDE
        lt_sc[...] = jnp.zeros(lt_sc.shape, F32)
        for r in range(dil):
            base = r * pad_seg
            for src, dst in ((k_ref, kp_sc), (v_ref, vp_sc)):
                dst[base:base + ATTN_SIDE] = zpad
                dst[base + ATTN_SIDE + seg:base + pad_seg] = zpad
                dst[base + ATTN_SIDE:base + ATTN_SIDE + seg] = src[r]

        def scores(r, i):
            base = r * pad_seg
            win = slice(base + i * ATTN_BQ, base + i * ATTN_BQ + ATTN_BK)
            qrows = slice(i * ATTN_BQ, (i + 1) * ATTN_BQ)
            qm = jnp.concatenate([q_ref[r, 0, qrows, :], q_ref[r, 1, qrows, :]], axis=0)
            s = lax.dot_general(kp_sc[win], qm, (((1,), (1,)), ((), ())), preferred_element_type=F32)
            s = s + bias_ref[g]
            if i == 0 or i == nb - 1:
                ok = None
                if i == 0:
                    ok = krow >= ATTN_SIDE
                if i == nb - 1:
                    ok_hi = krow < ATTN_BK - ATTN_SIDE
                    ok = ok_hi if ok is None else jnp.logical_and(ok, ok_hi)
                s = jnp.where(ok, s, -1e30)
            return s

        blocks = [(r, i) for r in range(dil) for i in range(nb)]
        pending = [scores(*blocks[j]) for j in range(min(ATTN_AHEAD, len(blocks)))]
        for blk, (r, i) in enumerate(blocks):
            if blk + ATTN_AHEAD < len(blocks):
                pending.append(scores(*blocks[blk + ATTN_AHEAD]))
            s = pending.pop(0)
            base = r * pad_seg
            win = slice(base + i * ATTN_BQ, base + i * ATTN_BQ + ATTN_BK)
            m = jnp.max(s, axis=0, keepdims=True)
            p = jnp.exp2(s - m)
            den = jnp.sum(p, axis=0, keepdims=True)
            pn = (p * (1.0 / den)).astype(BF16)
            u = lax.dot_general(pn, vp_sc[win], tn, preferred_element_type=F32)
            rows = pl.ds(r + i * ATTN_BQ * dil, ATTN_BQ, stride=dil)
            og_sc[g, rows, :] = jnp.where(head0, u[:ATTN_BQ], u[ATTN_BQ:])
            lt_sc[blk:blk + 1, :] = m + jnp.log2(den)

        ltt = lt_sc[...].T
        for r in range(dil):
            for i in range(nb):
                c = r * nb + i
                rows = pl.ds(r + i * ATTN_BQ * dil, ATTN_BQ, stride=dil)
                lg_sc[g, rows, :] = jnp.where(head0, ltt[:ATTN_BQ, c:c + 1], ltt[ATTN_BQ:, c:c + 1])

    l0, l1, l2 = lg_sc[0], lg_sc[1], lg_sc[2]
    m = jnp.maximum(jnp.maximum(l0, l1), l2)
    w0, w1, w2 = jnp.exp2(l0 - m), jnp.exp2(l1 - m), jnp.exp2(l2 - m)
    o = (w0 * og_sc[0] + w1 * og_sc[1] + w2 * og_sc[2]) / (w0 + w1 + w2)
    o_ref[...] = o.astype(o_ref.dtype)


def _attn_core(qkv, bias_tabs):
    in_specs = []
    for (_, dil) in DILATED_GROUPS:
        seg = SEQ // dil
        in_specs.append(pl.BlockSpec((None, dil, 2, seg, LANES), lambda b, p: (b, 0, 0, 0, p)))
        in_specs.append(pl.BlockSpec((None, dil, seg, LANES), lambda b, p: (b, 0, 0, p)))
        in_specs.append(pl.BlockSpec((None, dil, seg, LANES), lambda b, p: (b, 0, 0, p)))
    in_specs.append(pl.BlockSpec((N_ATTN_GROUPS, None, ATTN_BK, 2 * ATTN_BQ), lambda b, p: (0, p, 0, 0)))
    return pl.pallas_call(
        _attn_body,
        grid=(BATCH, N_PAIRS),
        in_specs=in_specs,
        out_specs=pl.BlockSpec((None, SEQ, LANES), lambda b, p: (b, 0, p)),
        out_shape=jax.ShapeDtypeStruct((BATCH, SEQ, D_MODEL), BF16),
        scratch_shapes=[pltpu.VMEM((ATTN_PAD_ROWS, LANES), BF16),
                        pltpu.VMEM((ATTN_PAD_ROWS, LANES), BF16),
                        pltpu.VMEM((LANES, LANES), F32),
                        pltpu.VMEM((N_ATTN_GROUPS, SEQ, LANES), F32),
                        pltpu.VMEM((N_ATTN_GROUPS, SEQ, LANES), F32)],
        compiler_params=_params(("parallel", "parallel"), 40 << 20),
        name="attn_core",
    )(*qkv, bias_tabs)


def _attn_layer(h, w_qkv_all, q_gain, k_gain, rel_bias, layer):
    qkv = []
    for gi, (window, dil) in enumerate(DILATED_GROUPS):
        assert (window // 2) // dil == ATTN_SIDE and (SEQ // dil) % ATTN_BQ == 0
        qkv.extend(_qkv_group(h, w_qkv_all, layer, gi, dil, q_gain[gi], k_gain[gi]))
    return _attn_core(qkv, _attn_bias_tables(rel_bias)).reshape(N_TOKENS, D_MODEL)


def kernel(x, norm_mix_g, norm_ffn_g, fnet_w_out, s5_lambda_re, s5_lambda_im, s5_log_dt, s5_b_re, s5_b_im, s5_c_re, s5_c_im, s5_d, s5_w_glu, attn_w_qkv, attn_q_gain, attn_k_gain, attn_w_o, rel_bias, ffn_w_gate_up, ffn_w_down):
    w_gate_up, w_down, w_qkv = ffn_w_gate_up, ffn_w_down, attn_w_qkv
    w_glu, w_o = _cast_bf16(s5_w_glu), _cast_bf16(attn_w_o)
    counts = [0, 0, 0]
    h_next = None
    for i in range(DEPTH):
        kind = i % 3
        j = counts[kind]
        counts[kind] += 1
        proj = None
        if kind == 0:
            x = _fnet_layer(x, norm_mix_g[i], fnet_w_out[j])
        elif kind == 1:
            h = h_next if h_next is not None else _s5_norm(x, norm_mix_g[i])
            x = _s5_layer(x, h, s5_lambda_re[j], s5_lambda_im[j], s5_log_dt[j], s5_b_re[j],
                          s5_b_im[j], s5_c_re[j], s5_c_im[j], s5_d[j], w_glu, j)
        else:
            assert h_next is not None, "an attention layer takes its normalised input from the previous FFN call"
            h = h_next.reshape(BATCH, SEQ, D_MODEL)
            proj = (_attn_layer(h, w_qkv, attn_q_gain[j], attn_k_gain[j], rel_bias, j), w_o, j)
        next_kind = (i + 1) % 3 if i + 1 < DEPTH else 0
        out = _ffn(x.reshape(N_TOKENS, D_MODEL), norm_ffn_g[i], w_gate_up, w_down, i, proj=proj,
                   norm_out_g=norm_mix_g[i + 1] if next_kind else None, norm_out_chunks=next_kind == 1)
        x, h_next = out if next_kind else (out, None)
        x = x.reshape(BATCH, SEQ, D_MODEL)
    return x
```
